```python
import math
import jax, jax.numpy as jnp
from jax import lax
import numpy as np

D_MODEL = 2048
BATCH = 4
SEQ = 2048
DEPTH = 2
DEC_BATCH = 32
DEC_SEQ = 8
PAST_LEN = 8192
PAGE_SIZE = 128

CONV_DIM = D_MODEL // 2
CONV_WIDTH = 31
NSA_HEADS = D_MODEL // 128
NSA_KV_HEADS = NSA_HEADS // 4
NSA_GQA = NSA_HEADS // NSA_KV_HEADS
NSA_HEAD_DIM = 64
NSA_BLOCK = 64
NSA_TOPK = 8
NSA_WINDOW = 512
NSA_QBLOCK = 128
NSA_KV_KINDS = 6
SSM_INNER = D_MODEL
SSM_HEAD_DIM = 64
SSM_HEADS = SSM_INNER // SSM_HEAD_DIM
SSM_GROUPS = 4
SSM_HPG = SSM_HEADS // SSM_GROUPS
SSM_STATE = 128
SSM_CONV = 4
SSM_CHUNK = 128
SSM_CONV_DIM = SSM_INNER + 2 * SSM_GROUPS * SSM_STATE
REL_BUCKETS = 32
REL_MAX_DIST = 128
MOE_GROUPS = 4
MOE_EXPERTS_PER_GROUP = 8
MOE_EXPERTS = MOE_GROUPS * MOE_EXPERTS_PER_GROUP
MOE_TOPK = 2
MOE_FF = 512
N_BRANCH = 3
IN_SPLITS = (2 * CONV_DIM, NSA_HEADS * NSA_HEAD_DIM, NSA_KV_KINDS * NSA_KV_HEADS * NSA_HEAD_DIM,
             3 * NSA_HEADS, SSM_INNER, SSM_CONV_DIM, SSM_HEADS, N_BRANCH * D_MODEL)
D_IN = sum(IN_SPLITS)
EPS = 1e-6
NEG = -1e30
FORCE = 1e4

kernel_name = 'hybrid_conv_nsa_ssd_hmoe_step'


def rmsnorm(x, g):
    x32 = x.astype(jnp.float32)
    y = x32 * lax.rsqrt(jnp.mean(x32 * x32, axis=-1, keepdims=True) + EPS)
    return (y * g.astype(jnp.float32)).astype(x.dtype)


def layernorm(x, g, b):
    x32 = x.astype(jnp.float32)
    xc = x32 - jnp.mean(x32, axis=-1, keepdims=True)
    y = xc * lax.rsqrt(jnp.mean(xc * xc, axis=-1, keepdims=True) + EPS)
    return (y * g.astype(jnp.float32) + b.astype(jnp.float32)).astype(x.dtype)


def causal_dwconv(x_ext, w, b):
    y = lax.conv_general_dilated(x_ext, w[:, None, :].astype(x_ext.dtype), (1,), 'VALID',
                                 dimension_numbers=('NWC', 'WIO', 'NWC'),
                                 feature_group_count=x_ext.shape[-1])
    return y + b.astype(y.dtype)


def rel_bucket(dist):
    n = jnp.maximum(dist, 0)
    exact = REL_BUCKETS // 2
    nf = jnp.maximum(n, 1).astype(jnp.float32)
    large = exact + (jnp.log(nf / exact) / math.log(REL_MAX_DIST / exact)
                     * (REL_BUCKETS - exact)).astype(jnp.int32)
    return jnp.where(n < exact, n, jnp.minimum(large, REL_BUCKETS - 1))


def nsa_attention(q, kv_past, kv_rows, win_all, gates, rel_bias, win_pos0):
    B_, T_, G, R, Dh = q.shape
    q_pos0 = kv_past.shape[1]
    S = q_pos0 + T_
    n_blk = -(-S // NSA_BLOCK)
    s_pad = n_blk * NSA_BLOCK
    kv_pad = jnp.concatenate([kv_past, kv_rows,
                              jnp.zeros((B_, s_pad - S) + kv_rows.shape[2:], kv_rows.dtype)], axis=1)
    cmp = kv_pad[:, :, :2].reshape(B_, n_blk, NSA_BLOCK, 2, G, Dh).mean(axis=2)
    kc, vc = cmp[:, :, 0], cmp[:, :, 1]
    win_pad = jnp.pad(win_all, ((0, 0), (NSA_WINDOW, 0), (0, 0), (0, 0), (0, 0)))
    blk_idx = jnp.arange(n_blk)
    blk_end = blk_idx * NSA_BLOCK + (NSA_BLOCK - 1)
    qc_len = math.gcd(T_, NSA_QBLOCK)
    n_chunks = T_ // qc_len
    n_sel = min(NSA_TOPK, n_blk)
    band = NSA_WINDOW + qc_len
    bias_tab = rel_bias.reshape(REL_BUCKETS, G, R).astype(jnp.float32)
    bi = jnp.arange(B_)[:, None, None, None]
    gi = jnp.arange(G)[None, :, None, None]
    scale = Dh ** -0.5

    def attend_block(c):
        t0 = c * qc_len
        qb = lax.dynamic_slice_in_dim(q, t0, qc_len, axis=1)
        gb = lax.dynamic_slice_in_dim(gates, t0, qc_len, axis=1)
        qp = q_pos0 + t0 + jnp.arange(qc_len)
        d_c = qp[:, None] - blk_end[None, :]
        ok_c = d_c >= 0
        s_c = (jnp.einsum('btgrd,bngd->bgrtn', qb, kc).astype(jnp.float32) * scale
               + jnp.transpose(bias_tab[rel_bucket(d_c)], (2, 3, 0, 1)))
        p_c = jax.nn.softmax(jnp.where(ok_c, s_c, NEG), axis=-1) * ok_c
        o_c = jnp.einsum('bgrtn,bngd->btgrd', p_c.astype(vc.dtype), vc)
        cur = (qp // NSA_BLOCK)[:, None]
        imp = jnp.where(blk_idx[None, :] == cur, FORCE, jnp.where(ok_c, p_c.sum(axis=2), -1.0))
        _, sel = lax.top_k(imp, n_sel)
        tok = (sel[..., None] * NSA_BLOCK + jnp.arange(NSA_BLOCK)).reshape(B_, G, qc_len, n_sel * NSA_BLOCK)
        k_s = kv_pad[bi, tok, 2, gi]
        v_s = kv_pad[bi, tok, 3, gi]
        d_s = qp[None, None, :, None] - tok
        s_s = (jnp.einsum('btgrd,bgtkd->bgrtk', qb, k_s).astype(jnp.float32) * scale
               + jnp.moveaxis(bias_tab[rel_bucket(d_s), gi], -1, 2))
        p_s = jax.nn.softmax(jnp.where((d_s >= 0)[:, :, None], s_s, NEG), axis=-1)
        o_s = jnp.einsum('bgrtk,bgtkd->btgrd', p_s.astype(v_s.dtype), v_s)
        i0 = q_pos0 - win_pos0 + t0
        kv_w = lax.dynamic_slice_in_dim(win_pad, i0, band, axis=1)
        kp = win_pos0 + i0 - NSA_WINDOW + jnp.arange(band)
        d_w = qp[:, None] - kp[None, :]
        ok_w = (d_w >= 0) & (d_w <= NSA_WINDOW) & (kp >= win_pos0)[None, :]
        s_w = (jnp.einsum('btgrd,bkgd->bgrtk', qb, kv_w[:, :, 0]).astype(jnp.float32) * scale
               + jnp.transpose(bias_tab[rel_bucket(d_w)], (2, 3, 0, 1)))
        p_w = jax.nn.softmax(jnp.where(ok_w, s_w, NEG), axis=-1)
        o_w = jnp.einsum('bgrtk,bkgd->btgrd', p_w.astype(kv_w.dtype), kv_w[:, :, 1])
        return (gb[:, :, 0, ..., None] * o_c + gb[:, :, 1, ..., None] * o_s
                + gb[:, :, 2, ..., None] * o_w)

    out = lax.map(attend_block, jnp.arange(n_chunks))
    return jnp.moveaxis(out, 0, 1).reshape(B_, T_, G, R, Dh)


def ssd(x, dt, A, bm, cm, h0):
    B_, T_ = x.shape[:2]
    Q = math.gcd(T_, SSM_CHUNK)
    nc = T_ // Q

    def chunks(a):
        return jnp.moveaxis(a.astype(jnp.float32).reshape((B_, nc, Q) + a.shape[2:]), 1, 0)

    causal = jnp.tril(jnp.ones((Q, Q), bool))[None, :, :, None, None]
    A32 = A.astype(jnp.float32)

    def step(h, inp):
        xc, dtc, bc, cc = inp
        acum = jnp.cumsum(dtc * A32, axis=1)
        decay = jnp.exp(jnp.where(causal, acum[:, :, None] - acum[:, None, :], NEG))
        cb = jnp.einsum('bign,bjgn->bgij', cc, bc)
        w = jnp.einsum('bgij,bijge->bijge', cb, decay) * dtc[:, None]
        y = (jnp.einsum('bijge,bjgep->bigep', w, xc)
             + jnp.einsum('bign,bgepn->bigep', cc, h) * jnp.exp(acum)[..., None])
        w_end = jnp.exp(acum[:, -1:] - acum) * dtc
        h = (jnp.exp(acum[:, -1])[..., None, None] * h
             + jnp.einsum('bjgn,bjgep->bgepn', bc, xc * w_end[..., None]))
        return h, y

    hT, ys = lax.scan(step, h0.astype(jnp.float32), (chunks(x), chunks(dt), chunks(bm), chunks(cm)))
    y = jnp.moveaxis(ys, 0, 1).reshape(x.shape)
    return y.astype(x.dtype), hT.astype(h0.dtype)


def moe_ffn(h, l, P):
    B_, T_, D_ = h.shape
    z = h.reshape(-1, D_)
    n = z.shape[0]
    g_prob = jax.nn.softmax((z @ P['w_router_group'][l] + P['b_router_group'][l]).astype(jnp.float32), axis=-1)
    g_w, g_idx = lax.top_k(g_prob, 1)
    e_all = jnp.einsum('nd,gde->nge', z, P['w_router_expert'][l]) + P['b_router_expert'][l]
    e_logits = e_all[jnp.arange(n), g_idx[:, 0]].astype(jnp.float32)
    e_w, e_idx = lax.top_k(jax.nn.softmax(e_logits, axis=-1), MOE_TOPK)
    e_w = e_w / jnp.sum(e_w, axis=-1, keepdims=True)
    ids = g_idx * MOE_EXPERTS_PER_GROUP + e_idx
    comb = jnp.sum(jax.nn.one_hot(ids, MOE_EXPERTS, dtype=jnp.float32) * (g_w * e_w)[..., None], axis=1)
    a = jnp.einsum('nd,edf->nef', z, P['w_gate'][l])
    u = jnp.einsum('nd,edf->nef', z, P['w_up'][l])
    hid = jax.nn.silu(a) * u * comb[:, :, None].astype(z.dtype)
    return jnp.einsum('nef,efd->nd', hid, P['w_down'][l]).reshape(B_, T_, D_)


def token_mixers(h, l, P, hist_a, hist_s, ssm_h0, kv_past, win_buf, past_len):
    B_, T_, _ = h.shape
    cuts, acc = [], 0
    for s in IN_SPLITS[:-1]:
        acc += s
        cuts.append(acc)
    u_in, q, kv, g_nsa, z, xbc, dt_raw, g_mix = jnp.split(h @ P['w_in'][l], cuts, axis=-1)

    u = u_in[..., :CONV_DIM] * jax.nn.sigmoid(u_in[..., CONV_DIM:])
    u_ext = jnp.concatenate([hist_a, u], axis=1)
    c = causal_dwconv(u_ext, P['conv_a_w'][l], P['conv_a_b'][l])
    c = jax.nn.silu(layernorm(c, P['ln_a_g'][l], P['ln_a_b'][l]))
    out_a = c @ P['w_a_out'][l]
    new_hist_a = u_ext[:, u_ext.shape[1] - (CONV_WIDTH - 1):]

    q = q.reshape(B_, T_, NSA_KV_HEADS, NSA_GQA, NSA_HEAD_DIM)
    kv = kv.reshape(B_, T_, NSA_KV_KINDS, NSA_KV_HEADS, NSA_HEAD_DIM)
    kv_rows, win_rows = kv[:, :, :4], kv[:, :, 4:]
    win_all = jnp.concatenate([win_buf, win_rows], axis=1)
    L = win_buf.shape[1]
    gates = jax.nn.sigmoid(g_nsa).reshape(B_, T_, 3, NSA_KV_HEADS, NSA_GQA)
    o = nsa_attention(q, kv_past, kv_rows, win_all, gates, P['rel_bias'], past_len - L)
    out_b = o.reshape(B_, T_, NSA_HEADS * NSA_HEAD_DIM) @ P['w_nsa_out'][l]
    keep = min(NSA_WINDOW, L + T_)
    new_win = win_all[:, L + T_ - keep:]

    x_ext = jnp.concatenate([hist_s, xbc], axis=1)
    xc = jax.nn.silu(causal_dwconv(x_ext, P['conv_s_w'][l], P['conv_s_b'][l]))
    new_hist_s = x_ext[:, x_ext.shape[1] - (SSM_CONV - 1):]
    xs, bm, cm = jnp.split(xc, [SSM_INNER, SSM_INNER + SSM_GROUPS * SSM_STATE], axis=-1)
    dt = jax.nn.softplus(dt_raw.astype(jnp.float32) + P['dt_bias'][l].astype(jnp.float32))
    A = -jnp.exp(P['a_log'][l].astype(jnp.float32))
    y, h_new = ssd(xs.reshape(B_, T_, SSM_GROUPS, SSM_HPG, SSM_HEAD_DIM),
                   dt.reshape(B_, T_, SSM_GROUPS, SSM_HPG), A.reshape(SSM_GROUPS, SSM_HPG),
                   bm.reshape(B_, T_, SSM_GROUPS, SSM_STATE), cm.reshape(B_, T_, SSM_GROUPS, SSM_STATE),
                   ssm_h0.reshape(B_, SSM_GROUPS, SSM_HPG, SSM_HEAD_DIM, SSM_STATE))
    y = (y.reshape(B_, T_, SSM_HEADS, SSM_HEAD_DIM)
         + xs.reshape(B_, T_, SSM_HEADS, SSM_HEAD_DIM) * P['d_skip'][l][:, None].astype(xs.dtype))
    y = y.reshape(B_, T_, SSM_INNER) * jax.nn.silu(z)
    y = rmsnorm(y.reshape(B_, T_, SSM_GROUPS, SSM_INNER // SSM_GROUPS),
                P['ssm_norm_g'][l].reshape(SSM_GROUPS, SSM_INNER // SSM_GROUPS)).reshape(B_, T_, SSM_INNER)
    out_c = y @ P['w_ssm_out'][l]
    new_ssm = h_new.reshape(B_, SSM_HEADS, SSM_HEAD_DIM, SSM_STATE)

    g = jax.nn.sigmoid(g_mix).reshape(B_, T_, N_BRANCH, D_MODEL)
    mixed = g[:, :, 0] * out_a + g[:, :, 1] * out_b + g[:, :, 2] * out_c
    return mixed @ P['w_out'][l], (kv_rows, new_win, new_hist_a, new_hist_s, new_ssm)


def run_trunk(x, P, past_len, kv_past_of, win_bufs, hist_as, hist_ss, ssm_h0s):
    kv_new, win_new, ha_new, hs_new, ssm_new = [], [], [], [], []
    for l in range(DEPTH):
        mix, st = token_mixers(rmsnorm(x, P['norm_mix_g'][l]), l, P, hist_as[l], hist_ss[l],
                               ssm_h0s[l], kv_past_of(l), win_bufs[l], past_len)
        x = x + mix
        x = x + moe_ffn(rmsnorm(x, P['norm_ffn_g'][l]), l, P)
        kv_new.append(st[0]); win_new.append(st[1]); ha_new.append(st[2])
        hs_new.append(st[3]); ssm_new.append(st[4])
    return (rmsnorm(x, P['final_norm_g']), jnp.stack(kv_new, axis=1), jnp.stack(win_new),
            jnp.stack(ha_new), jnp.stack(hs_new), jnp.stack(ssm_new))


def setup_inputs(seed: int = 0) -> dict:
    key = jax.random.key(seed)
    ks = jax.random.split(key, 40)

    def nrm(i, shape, scale=1.0):
        return scale * jax.random.normal(ks[i], shape, jnp.float32)

    n_pages = PAST_LEN // PAGE_SIZE
    n_used = DEC_BATCH * n_pages
    n_pool = n_used + max(1, n_used // 4)
    win_len = min(NSA_WINDOW, PAST_LEN)
    page_table = jax.random.permutation(ks[0], n_pool)[:n_used].reshape(DEC_BATCH, n_pages).astype(jnp.int32)
    dt0 = jnp.exp(jax.random.uniform(ks[1], (DEPTH, SSM_HEADS), jnp.float32, math.log(1e-3), math.log(1e-1)))
    a0 = jax.random.uniform(ks[2], (DEPTH, SSM_HEADS), jnp.float32, 1.0, 16.0)
    return {
        'x_prompt': nrm(3, (BATCH, SEQ, D_MODEL)),
        'x_sample': nrm(4, (DEC_BATCH, DEC_SEQ, D_MODEL)),
        'cache_nsa_kv': nrm(5, (n_pool, DEPTH, PAGE_SIZE, 4, NSA_KV_HEADS, NSA_HEAD_DIM)),
        'cache_nsa_win': nrm(6, (DEPTH, DEC_BATCH, win_len, 2, NSA_KV_HEADS, NSA_HEAD_DIM)),
        'state_conv_a': nrm(7, (DEPTH, DEC_BATCH, CONV_WIDTH - 1, CONV_DIM), 0.5),
        'state_conv_ssm': nrm(8, (DEPTH, DEC_BATCH, SSM_CONV - 1, SSM_CONV_DIM)),
        'state_ssm': nrm(9, (DEPTH, DEC_BATCH, SSM_HEADS, SSM_HEAD_DIM, SSM_STATE), 0.1),
        'page_table': page_table,
        'rel_bias': nrm(10, (REL_BUCKETS, NSA_HEADS), 0.5),
        'norm_mix_g': 1.0 + nrm(11, (DEPTH, D_MODEL), 0.01),
        'w_in': nrm(12, (DEPTH, D_MODEL, D_IN), D_MODEL ** -0.5),
        'conv_a_w': nrm(13, (DEPTH, CONV_WIDTH, CONV_DIM), CONV_WIDTH ** -0.5),
        'conv_a_b': nrm(14, (DEPTH, CONV_DIM), 0.01),
        'ln_a_g': 1.0 + nrm(15, (DEPTH, CONV_DIM), 0.01),
        'ln_a_b': nrm(16, (DEPTH, CONV_DIM), 0.01),
        'w_a_out': nrm(17, (DEPTH, CONV_DIM, D_MODEL), CONV_DIM ** -0.5),
        'w_nsa_out': nrm(18, (DEPTH, NSA_HEADS * NSA_HEAD_DIM, D_MODEL), (NSA_HEADS * NSA_HEAD_DIM) ** -0.5),
        'conv_s_w': nrm(19, (DEPTH, SSM_CONV, SSM_CONV_DIM), SSM_CONV ** -0.5),
        'conv_s_b': nrm(20, (DEPTH, SSM_CONV_DIM), 0.01),
        'dt_bias': dt0 + jnp.log(-jnp.expm1(-dt0)),
        'a_log': jnp.log(a0),
        'd_skip': 1.0 + nrm(21, (DEPTH, SSM_HEADS), 0.01),
        'ssm_norm_g': 1.0 + nrm(22, (DEPTH, SSM_INNER), 0.01),
        'w_ssm_out': nrm(23, (DEPTH, SSM_INNER, D_MODEL), SSM_INNER ** -0.5),
        'w_out': nrm(24, (DEPTH, D_MODEL, D_MODEL), D_MODEL ** -0.5),
        'norm_ffn_g': 1.0 + nrm(25, (DEPTH, D_MODEL), 0.01),
        'w_router_group': nrm(26, (DEPTH, D_MODEL, MOE_GROUPS), D_MODEL ** -0.5),
        'b_router_group': nrm(27, (DEPTH, MOE_GROUPS), 0.01),
        'w_router_expert': nrm(28, (DEPTH, MOE_GROUPS, D_MODEL, MOE_EXPERTS_PER_GROUP), D_MODEL ** -0.5),
        'b_router_expert': nrm(29, (DEPTH, MOE_GROUPS, MOE_EXPERTS_PER_GROUP), 0.01),
        'w_gate': nrm(30, (DEPTH, MOE_EXPERTS, D_MODEL, MOE_FF), D_MODEL ** -0.5),
        'w_up': nrm(31, (DEPTH, MOE_EXPERTS, D_MODEL, MOE_FF), D_MODEL ** -0.5),
        'w_down': nrm(32, (DEPTH, MOE_EXPERTS, MOE_FF, D_MODEL), MOE_FF ** -0.5),
        'final_norm_g': 1.0 + nrm(33, (D_MODEL,), 0.01),
    }


def reference(x_prompt, x_sample, cache_nsa_kv, cache_nsa_win, state_conv_a, state_conv_ssm, state_ssm,
              page_table, rel_bias, norm_mix_g, w_in, conv_a_w, conv_a_b, ln_a_g, ln_a_b, w_a_out,
              w_nsa_out, conv_s_w, conv_s_b, dt_bias, a_log, d_skip, ssm_norm_g, w_ssm_out, w_out,
              norm_ffn_g, w_router_group, b_router_group, w_router_expert, b_router_expert,
              w_gate, w_up, w_down, final_norm_g):
    P = dict(rel_bias=rel_bias, norm_mix_g=norm_mix_g, w_in=w_in, conv_a_w=conv_a_w, conv_a_b=conv_a_b,
             ln_a_g=ln_a_g, ln_a_b=ln_a_b, w_a_out=w_a_out, w_nsa_out=w_nsa_out, conv_s_w=conv_s_w,
             conv_s_b=conv_s_b, dt_bias=dt_bias, a_log=a_log, d_skip=d_skip, ssm_norm_g=ssm_norm_g,
             w_ssm_out=w_ssm_out, w_out=w_out, norm_ffn_g=norm_ffn_g, w_router_group=w_router_group,
             b_router_group=b_router_group, w_router_expert=w_router_expert,
             b_router_expert=b_router_expert, w_gate=w_gate, w_up=w_up, w_down=w_down,
             final_norm_g=final_norm_g)
    kvh, hd = NSA_KV_HEADS, NSA_HEAD_DIM
    bp, dtp = x_prompt.shape[0], x_prompt.dtype
    zero_kv = jnp.zeros((bp, 0, 4, kvh, hd), dtp)
    y_prompt, p_kv, p_win, p_conv_a, p_conv_s, p_ssm = run_trunk(
        x_prompt, P, 0, lambda l: zero_kv,
        jnp.zeros((DEPTH, bp, 0, 2, kvh, hd), dtp),
        jnp.zeros((DEPTH, bp, CONV_WIDTH - 1, CONV_DIM), dtp),
        jnp.zeros((DEPTH, bp, SSM_CONV - 1, SSM_CONV_DIM), dtp),
        jnp.zeros((DEPTH, bp, SSM_HEADS, SSM_HEAD_DIM, SSM_STATE), dtp))
    past_len = page_table.shape[1] * cache_nsa_kv.shape[2]

    def gather_past(l):
        return cache_nsa_kv[page_table, l].reshape(page_table.shape[0], past_len, 4, kvh, hd)

    y_sample, s_kv, s_win, s_conv_a, s_conv_s, s_ssm = run_trunk(
        x_sample, P, past_len, gather_past, cache_nsa_win, state_conv_a, state_conv_ssm, state_ssm)
    return (y_prompt, y_sample, p_kv, s_kv, p_win, s_win, p_conv_a, s_conv_a, p_conv_s, s_conv_s, p_ssm, s_ssm)
```

```python
import functools
import math

import numpy as np
import jax
import jax.numpy as jnp
from jax import lax
from jax.experimental import pallas as pl
from jax.experimental.pallas import tpu as pltpu

HEAD_DIM = 64
KV_GROUPS = 4
GQA = 4
N_HEADS = KV_GROUPS * GQA
KV_KINDS = 6
BLK = 64
TOPK = 8
WINDOW = 512
QCHUNK = 128
REL_BUCKETS = 32
REL_MAX_DIST = 128
CONV_A_W = 31
SSM_CONV_W = 4
SSM_HEAD_DIM = 64
SSM_GROUPS = 4
SSM_STATE = 128
SSM_CHUNK = 128
MOE_GROUPS = 4
MOE_EPG = 8
MOE_TOPK = 2
EPS = 1e-6
NEG = -1e30
FORCE = 1e4

LANES = 128
SUBLANES = 8
BF16_ROWS = 16
VMEM_LIMIT = 56 * 1024 * 1024

HIGHEST = lax.Precision.HIGHEST
F32 = jnp.float32
BF16 = jnp.bfloat16


def _params(sem, limit=VMEM_LIMIT):
    return pltpu.CompilerParams(dimension_semantics=sem, vmem_limit_bytes=limit)


def _tile(n, target, mult):
    best = None
    for t in range(mult, min(n, target) + 1, mult):
        if n % t == 0:
            best = t
    return best if best is not None else n


def _dot_nt(a, b):
    return lax.dot_general(a, b, (((1,), (1,)), ((), ())), preferred_element_type=F32)


def _dot_tn(a, b):
    return lax.dot_general(a, b, (((0,), (0,)), ((), ())), preferred_element_type=F32)


def _silu(x):
    return x * jax.nn.sigmoid(x)


def _softplus(x):
    return jnp.maximum(x, 0.0) + jnp.log(1.0 + jnp.exp(-jnp.abs(x)))


def _rms_kernel(x_ref, g_ref, o_ref):
    x = x_ref[...]
    ms = jnp.mean(x * x, axis=-1, keepdims=True)
    o_ref[...] = (x * lax.rsqrt(ms + EPS) * g_ref[...]).astype(o_ref.dtype)


def rmsnorm(x, g, out_dtype):
    n, d = x.shape
    tm = _tile(n, 1024, BF16_ROWS)
    return pl.pallas_call(
        _rms_kernel,
        grid=(n // tm,),
        in_specs=[pl.BlockSpec((tm, d), lambda i: (i, 0)), pl.BlockSpec((1, d), lambda i: (0, 0))],
        out_specs=pl.BlockSpec((tm, d), lambda i: (i, 0)),
        out_shape=jax.ShapeDtypeStruct((n, d), out_dtype),
        compiler_params=_params(("parallel",)),
        name="rmsnorm",
    )(x, g.reshape(1, d))


def _mm_kernel(a_ref, w_ref, *o_refs):
    acc = jnp.dot(a_ref[...], w_ref[...], preferred_element_type=F32)
    for o in o_refs:
        o[...] = acc.astype(o.dtype)


def matmul(a, w, out_dtypes):
    n, k = a.shape
    c = w.shape[1]
    tm = _tile(n, 768, BF16_ROWS)
    tn = _tile(c, 1024, LANES)
    outs = pl.pallas_call(
        _mm_kernel,
        grid=(n // tm, c // tn),
        in_specs=[pl.BlockSpec((tm, k), lambda i, j: (i, 0)), pl.BlockSpec((k, tn), lambda i, j: (0, j))],
        out_specs=[pl.BlockSpec((tm, tn), lambda i, j: (i, j)) for _ in out_dtypes],
        out_shape=[jax.ShapeDtypeStruct((n, c), dt) for dt in out_dtypes],
        compiler_params=_params(("parallel", "parallel")),
        name="proj",
    )(a, w)
    return outs


HIST_A = 32
HIST_S = 8


def _conva_kernel(u_ref, hist_ref, w_ref, b_ref, lg_ref, lb_ref, c_ref, hout_ref, ext_ref, conv_ref, *, tt, ch):
    t = pl.program_id(1)

    @pl.when(t == 0)
    def _():
        ext_ref[0:HIST_A, :] = hist_ref[...]

    uin = u_ref[...]
    ext_ref[HIST_A:HIST_A + tt, :] = uin[:, :ch] * jax.nn.sigmoid(uin[:, ch:])
    off = HIST_A - (CONV_A_W - 1)
    rb = min(tt, 128)

    def lane_block(cb, carry):
        lo = pl.multiple_of(cb * LANES, LANES)
        for r in range(tt // rb):
            acc = jnp.broadcast_to(b_ref[:, pl.ds(lo, LANES)], (rb, LANES))
            for k in range(CONV_A_W):
                acc = acc + w_ref[k:k + 1, pl.ds(lo, LANES)] * ext_ref[r * rb + off + k:r * rb + off + k + rb, pl.ds(lo, LANES)]
            conv_ref[r * rb:(r + 1) * rb, pl.ds(lo, LANES)] = acc
        return carry

    lax.fori_loop(0, ch // LANES, lane_block, 0)
    cv = conv_ref[...]
    mu = jnp.mean(cv, axis=-1, keepdims=True)
    xc = cv - mu
    var = jnp.mean(xc * xc, axis=-1, keepdims=True)
    y = xc * lax.rsqrt(var + EPS) * lg_ref[...] + lb_ref[...]
    c_ref[...] = _silu(y).astype(c_ref.dtype)
    tail = ext_ref[tt:tt + HIST_A, :]
    hout_ref[...] = tail
    ext_ref[0:HIST_A, :] = tail


def conv_a_branch(u_in, hist, w, b, lg, lb, batch, t_len, out_dtype):
    ch = w.shape[1]
    tt = _tile(t_len, 256, SUBLANES)
    nt = t_len // tt
    kern = functools.partial(_conva_kernel, tt=tt, ch=ch)
    vec = lambda: pl.BlockSpec((1, ch), lambda bi, ti: (0, 0))
    return pl.pallas_call(
        kern,
        grid=(batch, nt),
        in_specs=[pl.BlockSpec((tt, 2 * ch), lambda bi, ti: (bi * nt + ti, 0)),
                  pl.BlockSpec((None, HIST_A, ch), lambda bi, ti: (bi, 0, 0)),
                  pl.BlockSpec((CONV_A_W, ch), lambda bi, ti: (0, 0)), vec(), vec(), vec()],
        out_specs=[pl.BlockSpec((tt, ch), lambda bi, ti: (bi * nt + ti, 0)),
                   pl.BlockSpec((None, HIST_A, ch), lambda bi, ti: (bi, 0, 0))],
        out_shape=[jax.ShapeDtypeStruct((batch * t_len, ch), out_dtype),
                   jax.ShapeDtypeStruct((batch, HIST_A, ch), F32)],
        scratch_shapes=[pltpu.VMEM((tt + HIST_A, ch), F32), pltpu.VMEM((tt, ch), F32)],
        compiler_params=_params(("parallel", "arbitrary")),
        name="conv_a",
    )(u_in, hist, w, b.reshape(1, ch), lg.reshape(1, ch), lb.reshape(1, ch))


def _ssd_kernel(xbc_ref, z_ref, dt_ref, dtt_ref, hist_ref, h0_ref, cw_ref, cb_ref, dtb_ref, dtbt_ref,
                a_ref, at_ref, dvec_ref, gn_ref, y_ref, hout_ref, sout_ref, ext_ref, h_ref, yacc_ref,
                *, q, nc, inner, heads):
    c = pl.program_id(1)

    @pl.when(c == 0)
    def _():
        ext_ref[0:HIST_S, :] = hist_ref[...]
        h_ref[...] = h0_ref[...]

    ext_ref[HIST_S:HIST_S + q, :] = xbc_ref[...]
    off = HIST_S - (SSM_CONV_W - 1)
    acc = jnp.broadcast_to(cb_ref[...], (q, cb_ref.shape[1]))
    for k in range(SSM_CONV_W):
        acc = acc + cw_ref[k:k + 1, :] * ext_ref[off + k:off + k + q, :]
    xc = _silu(acc)
    tail = ext_ref[q:q + HIST_S, :]
    hout_ref[...] = tail
    ext_ref[0:HIST_S, :] = tail

    gw = SSM_GROUPS * SSM_STATE
    xs = xc[:, :inner]
    bm = xc[:, inner:inner + gw]
    cm = xc[:, inner + gw:inner + 2 * gw]

    dt = _softplus(dt_ref[...] + dtb_ref[...])
    dtt = _softplus(dtt_ref[...] + dtbt_ref[...])
    da = dt * a_ref[...]
    dat = dtt * at_ref[...]
    ri = lax.broadcasted_iota(jnp.int32, (q, q), 0)
    ci = lax.broadcasted_iota(jnp.int32, (q, q), 1)
    causal = ri >= ci
    tri = causal.astype(F32)
    trit = (ri <= ci).astype(F32)
    acum = jnp.dot(tri, da, precision=HIGHEST, preferred_element_type=F32)
    acumt = jnp.dot(dat, trit, precision=HIGHEST, preferred_element_type=F32)
    alast = acum[q - 1:q, :]
    hpg = heads // SSM_GROUPS
    for g in range(SSM_GROUPS):
        bg = bm[:, g * SSM_STATE:(g + 1) * SSM_STATE].astype(BF16)
        cg = cm[:, g * SSM_STATE:(g + 1) * SSM_STATE].astype(BF16)
        cbm = _dot_nt(cg, bg)
        for e in range(hpg):
            hh = g * hpg + e
            a_col = acum[:, hh:hh + 1]
            a_row = acumt[hh:hh + 1, :]
            decay = jnp.exp(jnp.where(causal, a_col - a_row, NEG))
            wm = cbm * decay * dtt[hh:hh + 1, :]
            xh = xs[:, hh * SSM_HEAD_DIM:(hh + 1) * SSM_HEAD_DIM]
            hprev = h_ref[hh]
            y_h = (jnp.dot(wm.astype(BF16), xh.astype(BF16), preferred_element_type=F32)
                   + _dot_nt(cg, hprev.astype(BF16)) * jnp.exp(a_col))
            yacc_ref[:, hh * SSM_HEAD_DIM:(hh + 1) * SSM_HEAD_DIM] = y_h
            al = alast[:, hh:hh + 1]
            w_end = jnp.exp(al - a_col) * dt[:, hh:hh + 1]
            h_ref[hh] = jnp.exp(al) * hprev + _dot_tn((xh * w_end).astype(BF16), bg)

    y = (yacc_ref[...] + xs * dvec_ref[...]) * _silu(z_ref[...])
    gsz = inner // SSM_GROUPS
    for g in range(SSM_GROUPS):
        v = y[:, g * gsz:(g + 1) * gsz]
        ms = jnp.mean(v * v, axis=-1, keepdims=True)
        y_ref[:, g * gsz:(g + 1) * gsz] = (v * lax.rsqrt(ms + EPS) * gn_ref[:, g * gsz:(g + 1) * gsz]).astype(y_ref.dtype)

    @pl.when(c == nc - 1)
    def _():
        sout_ref[...] = h_ref[...]


def ssd_branch(xbc, z, small, hist, h0, cw, cb, dt_bias, a_log, d_skip, norm_g, batch, t_len, out_dtype):
    conv_dim = xbc.shape[1]
    inner = z.shape[1]
    heads = dt_bias.shape[0]
    q = math.gcd(t_len, SSM_CHUNK)
    nc = t_len // q
    pad = LANES - heads
    dtb = jnp.pad(dt_bias, (0, pad)).reshape(1, LANES)
    a_neg = jnp.pad(-jnp.exp(a_log), (0, pad)).reshape(1, LANES)
    dvec = jnp.repeat(d_skip, SSM_HEAD_DIM).reshape(1, inner)
    small_t = small.reshape(batch, t_len, LANES).transpose(0, 2, 1)
    kern = functools.partial(_ssd_kernel, q=q, nc=nc, inner=inner, heads=heads)
    row = lambda w: pl.BlockSpec((1, w), lambda bi, ci: (0, 0))
    col = lambda: pl.BlockSpec((LANES, 1), lambda bi, ci: (0, 0))
    return pl.pallas_call(
        kern,
        grid=(batch, nc),
        in_specs=[pl.BlockSpec((q, conv_dim), lambda bi, ci: (bi * nc + ci, 0)),
                  pl.BlockSpec((q, inner), lambda bi, ci: (bi * nc + ci, 0)),
                  pl.BlockSpec((q, LANES), lambda bi, ci: (bi * nc + ci, 0)),
                  pl.BlockSpec((None, LANES, q), lambda bi, ci: (bi, 0, ci)),
                  pl.BlockSpec((None, HIST_S, conv_dim), lambda bi, ci: (bi, 0, 0)),
                  pl.BlockSpec((None, heads, SSM_HEAD_DIM, SSM_STATE), lambda bi, ci: (bi, 0, 0, 0)),
                  pl.BlockSpec((SSM_CONV_W, conv_dim), lambda bi, ci: (0, 0)), row(conv_dim),
                  row(LANES), col(), row(LANES), col(), row(inner), row(inner)],
        out_specs=[pl.BlockSpec((q, inner), lambda bi, ci: (bi * nc + ci, 0)),
                   pl.BlockSpec((None, HIST_S, conv_dim), lambda bi, ci: (bi, 0, 0)),
                   pl.BlockSpec((None, heads, SSM_HEAD_DIM, SSM_STATE), lambda bi, ci: (bi, 0, 0, 0))],
        out_shape=[jax.ShapeDtypeStruct((batch * t_len, inner), out_dtype),
                   jax.ShapeDtypeStruct((batch, HIST_S, conv_dim), F32),
                   jax.ShapeDtypeStruct((batch, heads, SSM_HEAD_DIM, SSM_STATE), F32)],
        scratch_shapes=[pltpu.VMEM((q + HIST_S, conv_dim), F32),
                        pltpu.VMEM((heads, SSM_HEAD_DIM, SSM_STATE), F32),
                        pltpu.VMEM((q, inner), F32)],
        compiler_params=_params(("parallel", "arbitrary")),
        name="ssd",
    )(xbc, z, small, small_t, hist, h0, cw, cb.reshape(1, conv_dim), dtb, dtb.reshape(LANES, 1),
      a_neg, a_neg.reshape(LANES, 1), dvec, norm_g.reshape(1, inner))


def _bucket_np(d):
    n = np.maximum(d, 0)
    exact = REL_BUCKETS // 2
    nf = np.maximum(n, 1).astype(np.float32)
    large = exact + (np.log(nf / np.float32(exact)) / np.float32(math.log(REL_MAX_DIST / exact))
                     * np.float32(REL_BUCKETS - exact)).astype(np.int32)
    return np.where(n < exact, n, np.minimum(large, REL_BUCKETS - 1))


def _near_distances():
    i = np.arange(QCHUNK)[:, None]
    j = np.arange(LANES)[None, :]
    diag = i - j
    prev = QCHUNK + i - j
    band = np.where(j < 4, i + (BLK + 1) - BLK * j, 0)
    return np.stack([diag, prev, band])


def _expand_kernel(tab_ref, oh_ref, o_ref):
    o_ref[...] = jnp.dot(tab_ref[...], oh_ref[...], precision=HIGHEST, preferred_element_type=F32)


def expand_bias(rel_bias, dist):
    flat = dist.reshape(-1)
    m = flat.shape[0]
    assert m % LANES == 0
    onehot = (np.arange(REL_BUCKETS)[:, None] == _bucket_np(flat)[None, :]).astype(np.float32)
    tm = _tile(m, 8192, LANES)
    heads = rel_bias.shape[1]
    out = pl.pallas_call(
        _expand_kernel,
        grid=(m // tm,),
        in_specs=[pl.BlockSpec((heads, REL_BUCKETS), lambda i: (0, 0)), pl.BlockSpec((REL_BUCKETS, tm), lambda i: (0, i))],
        out_specs=pl.BlockSpec((heads, tm), lambda i: (0, i)),
        out_shape=jax.ShapeDtypeStruct((heads, m), F32),
        compiler_params=_params(("parallel",)),
        name="bias_expand",
    )(rel_bias.T, jnp.asarray(onehot))
    return out.reshape((heads,) + dist.shape)


def _attend(qg, k, v, bias, mask, m_ref, l_ref, acc_ref, first):
    s = _dot_nt(qg, k) + bias
    if mask is not None:
        s = jnp.where(mask, s, NEG)
    smax = jnp.max(s, axis=-1, keepdims=True)
    if first:
        p = jnp.exp(s - smax)
        m_ref[...] = smax
        l_ref[...] = jnp.sum(p, axis=-1, keepdims=True)
        acc_ref[...] = jnp.dot(p.astype(BF16), v, preferred_element_type=F32)
    else:
        m_old = m_ref[...]
        m_new = jnp.maximum(m_old, smax)
        alpha = jnp.exp(m_old - m_new)
        p = jnp.exp(s - m_new)
        m_ref[...] = m_new
        l_ref[...] = alpha * l_ref[...] + jnp.sum(p, axis=-1, keepdims=True)
        acc_ref[...] = alpha * acc_ref[...] + jnp.dot(p.astype(BF16), v, preferred_element_type=F32)


def _select_blocks(imp, lane):
    sel = jnp.zeros(imp.shape, F32)
    big = jnp.int32(1 << 30)
    v = imp
    for _ in range(TOPK):
        mx = jnp.max(v, axis=-1, keepdims=True)
        idx = jnp.min(jnp.where(v == mx, lane, big), axis=-1, keepdims=True)
        hit = lane == idx
        sel = jnp.where(hit, 1.0, sel)
        v = jnp.where(hit, -3e38, v)
    return sel


def _rep4(x):
    return jnp.concatenate([x, x, x, x], axis=0)


def _nsa_prompt_kernel(q_ref, kv_ref, cmp_ref, gn_ref, near_ref, cst_ref, e_ref, o_ref,
                       kc_ref, sel_ref, m_ref, l_ref, acc_ref, *, t_len, nb):
    c = pl.program_id(1)
    qc = QCHUNK
    rows = GQA * qc
    kw = KV_GROUPS * HEAD_DIM

    @pl.when(c == 0)
    def _():
        kc_ref[...] = jnp.zeros(kc_ref.shape, F32)
        kc_ref[0:nb, :] = jnp.sum(cmp_ref[...].reshape(nb, BLK, 2 * kw), axis=1) * (1.0 / BLK)

    lane = lax.broadcasted_iota(jnp.int32, (rows, LANES), 1)
    qi = lax.broadcasted_iota(jnp.int32, (rows, LANES), 0) & (qc - 1)
    qpos = c * qc + qi
    okc = (qpos - BLK * lane - (BLK - 1)) >= 0
    causal = qi >= lane
    edge = lane >= qi
    lane1 = lax.broadcasted_iota(jnp.int32, (qc, LANES), 1)
    qi1 = lax.broadcasted_iota(jnp.int32, (qc, LANES), 0)
    okc1 = (c * qc + qi1 - BLK * lane1 - (BLK - 1)) >= 0
    sg = jax.nn.sigmoid(gn_ref[...])
    gate0 = SSM_GROUPS * 8

    def ktile(kt, kind, g):
        r0 = pl.multiple_of(kt * qc, qc)
        return kv_ref[pl.ds(r0, qc), kind * kw + g * HEAD_DIM:kind * kw + (g + 1) * HEAD_DIM]

    for g in range(KV_GROUPS):
        qg = jnp.concatenate([q_ref[:, (g * GQA + r) * HEAD_DIM:(g * GQA + r + 1) * HEAD_DIM] for r in range(GQA)], axis=0)
        cst = cst_ref[g]
        kcg = kc_ref[:, g * HEAD_DIM:(g + 1) * HEAD_DIM].astype(BF16)
        vcg = kc_ref[:, kw + g * HEAD_DIM:kw + (g + 1) * HEAD_DIM].astype(BF16)
        shift = lax.rem(2 * c - 2 + LANES, LANES)
        band = pltpu.roll(near_ref[g, 2], shift, axis=1)
        bias_c = jnp.where(lane < 2 * c - 2, cst, band)
        s = jnp.where(okc, _dot_nt(qg, kcg) + bias_c, NEG)
        e = jnp.exp(s - jnp.max(s, axis=-1, keepdims=True))
        p = jnp.where(okc, e / jnp.sum(e, axis=-1, keepdims=True), 0.0)
        o_c = jnp.dot(p.astype(BF16), vcg, preferred_element_type=F32)
        psum = p[0:qc] + p[qc:2 * qc] + p[2 * qc:3 * qc] + p[3 * qc:4 * qc]
        cur = 2 * c + (qi1 >= BLK).astype(jnp.int32)
        imp = jnp.where(lane1 == cur, FORCE, jnp.where(okc1, psum, -1.0))
        imp = jnp.where(lane1 < nb, imp, -3e38)
        selm = _select_blocks(imp, lane1)
        sel_ref[...] = jnp.dot(selm.astype(BF16), e_ref[...], preferred_element_type=F32)

        def seltile(kt):
            return _rep4(sel_ref[:, pl.ds(pl.multiple_of(kt * qc, qc), qc)]) > 0.5

        _attend(qg, ktile(c, 2, g), ktile(c, 3, g), near_ref[g, 0], seltile(c) & causal, m_ref, l_ref, acc_ref, True)

        @pl.when(c >= 1)
        def _():
            _attend(qg, ktile(c - 1, 2, g), ktile(c - 1, 3, g), near_ref[g, 1], seltile(c - 1), m_ref, l_ref, acc_ref, False)

        def far(kt, carry):
            _attend(qg, ktile(kt, 2, g), ktile(kt, 3, g), cst, seltile(kt), m_ref, l_ref, acc_ref, False)
            return carry

        lax.fori_loop(0, jnp.maximum(c - 1, 0), far, 0)
        o_s = acc_ref[...] / l_ref[...]
        _attend(qg, ktile(c, 4, g), ktile(c, 5, g), near_ref[g, 0], causal, m_ref, l_ref, acc_ref, True)

        @pl.when(c >= 1)
        def _():
            _attend(qg, ktile(c - 1, 4, g), ktile(c - 1, 5, g), near_ref[g, 1], None, m_ref, l_ref, acc_ref, False)

        for back in range(2, WINDOW // qc):
            @pl.when(c >= back)
            def _(back=back):
                _attend(qg, ktile(c - back, 4, g), ktile(c - back, 5, g), cst, None, m_ref, l_ref, acc_ref, False)

        @pl.when(c >= WINDOW // qc)
        def _():
            kt = c - WINDOW // qc
            _attend(qg, ktile(kt, 4, g), ktile(kt, 5, g), cst, edge, m_ref, l_ref, acc_ref, False)

        o_w = acc_ref[...] / l_ref[...]

        def gcol(br):
            c0 = gate0 + br * N_HEADS + g * GQA
            return jnp.concatenate([sg[:, c0 + r:c0 + r + 1] for r in range(GQA)], axis=0)

        tot = gcol(0) * o_c + gcol(1) * o_s + gcol(2) * o_w
        for r in range(GQA):
            o_ref[:, (g * GQA + r) * HEAD_DIM:(g * GQA + r + 1) * HEAD_DIM] = tot[r * qc:(r + 1) * qc].astype(o_ref.dtype)


def nsa_prompt(q, kvb, kv, small, near, cst, batch, t_len):
    assert t_len % QCHUNK == 0 and t_len // BLK <= LANES
    nch = t_len // QCHUNK
    nb = t_len // BLK
    rows = GQA * QCHUNK
    hd = N_HEADS * HEAD_DIM
    kvw = kvb.shape[1]
    emat = (np.arange(LANES)[:, None] == (np.arange(t_len)[None, :] // BLK)).astype(np.float32)
    kern = functools.partial(_nsa_prompt_kernel, t_len=t_len, nb=nb)
    return pl.pallas_call(
        kern,
        grid=(batch, nch),
        in_specs=[pl.BlockSpec((QCHUNK, hd), lambda b, c: (b * nch + c, 0)),
                  pl.BlockSpec((t_len, kvw), lambda b, c: (b, 0)),
                  pl.BlockSpec((t_len, 2 * KV_GROUPS * HEAD_DIM), lambda b, c: (b, 0)),
                  pl.BlockSpec((QCHUNK, LANES), lambda b, c: (b * nch + c, 0)),
                  pl.BlockSpec((KV_GROUPS, 3, rows, LANES), lambda b, c: (0, 0, 0, 0)),
                  pl.BlockSpec((KV_GROUPS, rows, 1), lambda b, c: (0, 0, 0)),
                  pl.BlockSpec((LANES, t_len), lambda b, c: (0, 0))],
        out_specs=pl.BlockSpec((QCHUNK, hd), lambda b, c: (b * nch + c, 0)),
        out_shape=jax.ShapeDtypeStruct((batch * t_len, hd), BF16),
        scratch_shapes=[pltpu.VMEM((LANES, 2 * KV_GROUPS * HEAD_DIM), F32),
                        pltpu.VMEM((QCHUNK, t_len), F32),
                        pltpu.VMEM((rows, 1), F32), pltpu.VMEM((rows, 1), F32), pltpu.VMEM((rows, HEAD_DIM), F32)],
        compiler_params=_params(("parallel", "arbitrary")),
        name="nsa_prompt",
    )(q, kvb, kv, small, near, cst, jnp.asarray(emat, BF16))


def _page_specs(pps, n_steps, half, layer):
    def spec(k):
        return pl.BlockSpec((None, None, None, BLK * 2, 2 * KV_GROUPS * HEAD_DIM),
                            lambda b, j, pt: (pt[b, j * pps + k], layer, 0, 0, half))
    return [spec(k) for k in range(pps)]


def _means_kernel(pt_ref, *refs):
    pages, o_ref = refs[:-1], refs[-1]
    w = o_ref.shape[-1]
    parts = [jnp.sum(pg[...].reshape(2, BLK, w), axis=1) * (1.0 / BLK) for pg in pages]
    o_ref[...] = jnp.concatenate(parts, axis=0)


def sample_block_means(cache5, page_table, layer, pps):
    batch, n_pages = page_table.shape
    n_steps = n_pages // pps
    w = 2 * KV_GROUPS * HEAD_DIM
    return pl.pallas_call(
        _means_kernel,
        grid_spec=pltpu.PrefetchScalarGridSpec(
            num_scalar_prefetch=1, grid=(batch, n_steps),
            in_specs=_page_specs(pps, n_steps, 0, layer),
            out_specs=pl.BlockSpec((None, 2 * pps, w), lambda b, j, pt: (b, j, 0))),
        out_shape=jax.ShapeDtypeStruct((batch, 2 * n_pages, w), F32),
        compiler_params=_params(("parallel", "arbitrary")),
        name="cmp_means",
    )(page_table, *([cache5] * pps))


def _nsa_sample_kernel(pt_ref, q_ref, kvn_ref, gn_ref, kc_ref, win_ref, near_ref, wtab_ref, cst_ref, e_ref,
                       *refs, pps, n_steps, past, ts, nbl):
    pages = refs[:pps]
    o_ref = refs[pps]
    sel_ref, m_ref, l_ref, acc_ref, oc_ref, kn_ref = refs[pps + 1:]
    j = pl.program_id(1)
    rows = GQA * ts
    kw = KV_GROUPS * HEAD_DIM
    nb_past = past // BLK
    span = pps * 2 * BLK
    lane = lax.broadcasted_iota(jnp.int32, (rows, LANES), 1)
    ti = lax.broadcasted_iota(jnp.int32, (rows, LANES), 0) & (ts - 1)

    def qgroup(g):
        return jnp.concatenate([q_ref[:, (g * GQA + r) * HEAD_DIM:(g * GQA + r + 1) * HEAD_DIM] for r in range(GQA)],
                               axis=0).astype(BF16)

    @pl.when(j == 0)
    def _():
        kn_ref[...] = jnp.zeros(kn_ref.shape, F32)
        kn_ref[0:ts, :] = kvn_ref[:, 2 * kw:6 * kw]
        lane_b = lax.broadcasted_iota(jnp.int32, (rows, nbl), 1)
        t_b = lax.broadcasted_iota(jnp.int32, (rows, nbl), 0) & (ts - 1)
        okc = (past + t_b - BLK * lane_b - (BLK - 1)) >= 0
        lane1 = lax.broadcasted_iota(jnp.int32, (ts, nbl), 1)
        okc1 = (past + lax.broadcasted_iota(jnp.int32, (ts, nbl), 0) - BLK * lane1 - (BLK - 1)) >= 0
        for g in range(KV_GROUPS):
            qg = qgroup(g)
            cst = cst_ref[g]
            band = near_ref[g, 2]
            kcg = kc_ref[:, g * HEAD_DIM:(g + 1) * HEAD_DIM].astype(BF16)
            vcg = kc_ref[:, kw + g * HEAD_DIM:kw + (g + 1) * HEAD_DIM].astype(BF16)
            bias_c = jnp.where(lane_b == nb_past - 2, band[:, 0:1], jnp.where(lane_b == nb_past - 1, band[:, 1:2], cst))
            s = jnp.where(okc, _dot_nt(qg, kcg) + bias_c, NEG)
            e = jnp.exp(s - jnp.max(s, axis=-1, keepdims=True))
            p = jnp.where(okc, e / jnp.sum(e, axis=-1, keepdims=True), 0.0)
            oc_ref[g] = jnp.dot(p.astype(BF16), vcg, preferred_element_type=F32)
            psum = p[0:ts] + p[ts:2 * ts] + p[2 * ts:3 * ts] + p[3 * ts:4 * ts]
            imp = jnp.where(lane1 == nb_past, FORCE, jnp.where(okc1, psum, -1.0))
            imp = jnp.where(lane1 <= nb_past, imp, -3e38)
            selm = _select_blocks(imp, lane1)
            selm16 = jnp.concatenate([selm, jnp.zeros_like(selm)], axis=0).astype(BF16)
            sel_ref[g] = jnp.dot(selm16, e_ref[...], preferred_element_type=F32)[0:ts]
            kn = kn_ref[:, g * HEAD_DIM:(g + 1) * HEAD_DIM].astype(BF16)
            vn = kn_ref[:, kw + g * HEAD_DIM:kw + (g + 1) * HEAD_DIM].astype(BF16)
            _attend(qg, kn, vn, near_ref[g, 0], lane <= ti, m_ref.at[g], l_ref.at[g], acc_ref.at[g], True)

    is_last = j == n_steps - 1
    for g in range(KV_GROUPS):
        qg = qgroup(g)
        cst = cst_ref[g]
        parts = []
        for k in range(pps):
            kp = pages[k][:, g * HEAD_DIM:(g + 1) * HEAD_DIM].astype(BF16)
            sk = _dot_nt(qg, kp)
            if k == pps - 1:
                sk = sk + jnp.where(is_last, near_ref[g, 1], cst)
            else:
                sk = sk + cst
            parts.append(sk)
        s = jnp.concatenate(parts, axis=1)
        msk = _rep4(sel_ref[g, :, pl.ds(pl.multiple_of(j * span, span), span)]) > 0.5
        s = jnp.where(msk, s, NEG)
        m_old = m_ref[g]
        m_new = jnp.maximum(m_old, jnp.max(s, axis=-1, keepdims=True))
        alpha = jnp.exp(m_old - m_new)
        p = jnp.exp(s - m_new)
        pv = jnp.zeros((rows, HEAD_DIM), F32)
        for k in range(pps):
            vp = pages[k][:, kw + g * HEAD_DIM:kw + (g + 1) * HEAD_DIM].astype(BF16)
            pv = pv + jnp.dot(p[:, k * 2 * BLK:(k + 1) * 2 * BLK].astype(BF16), vp, preferred_element_type=F32)
        m_ref[g] = m_new
        l_ref[g] = alpha * l_ref[g] + jnp.sum(p, axis=-1, keepdims=True)
        acc_ref[g] = alpha * acc_ref[g] + pv

    @pl.when(is_last)
    def _():
        sg = jax.nn.sigmoid(gn_ref[...])
        gate0 = SSM_GROUPS * 8
        wl = win_ref.shape[0]
        lane_w = lax.broadcasted_iota(jnp.int32, (rows, wl + LANES), 1)
        t_w = lax.broadcasted_iota(jnp.int32, (rows, wl + LANES), 0) & (ts - 1)
        okw = jnp.where(lane_w < wl, lane_w - t_w, t_w - (lane_w - wl)) >= 0
        for g in range(KV_GROUPS):
            qg = qgroup(g)
            o_s = acc_ref[g] / l_ref[g]
            kwin = win_ref[:, g * HEAD_DIM:(g + 1) * HEAD_DIM].astype(BF16)
            vwin = win_ref[:, kw + g * HEAD_DIM:kw + (g + 1) * HEAD_DIM].astype(BF16)
            kn = kn_ref[:, 2 * kw + g * HEAD_DIM:2 * kw + (g + 1) * HEAD_DIM].astype(BF16)
            vn = kn_ref[:, 3 * kw + g * HEAD_DIM:3 * kw + (g + 1) * HEAD_DIM].astype(BF16)
            s = jnp.concatenate([_dot_nt(qg, kwin), _dot_nt(qg, kn)], axis=1) + wtab_ref[g]
            s = jnp.where(okw, s, NEG)
            e = jnp.exp(s - jnp.max(s, axis=-1, keepdims=True))
            p = e / jnp.sum(e, axis=-1, keepdims=True)
            o_w = (jnp.dot(p[:, :wl].astype(BF16), vwin, preferred_element_type=F32)
                   + jnp.dot(p[:, wl:].astype(BF16), vn, preferred_element_type=F32))

            def gcol(br):
                c0 = gate0 + br * N_HEADS + g * GQA
                return jnp.concatenate([sg[:, c0 + r:c0 + r + 1] for r in range(GQA)], axis=0)

            tot = gcol(0) * oc_ref[g] + gcol(1) * o_s + gcol(2) * o_w
            for r in range(GQA):
                o_ref[:, (g * GQA + r) * HEAD_DIM:(g * GQA + r + 1) * HEAD_DIM] = tot[r * ts:(r + 1) * ts]


def nsa_sample(q3, kvn3, small3, kcvc, win4, near_s, wtab, cst_s, cache5, page_table, layer, pps):
    batch, ts, hd = q3.shape
    n_pages = page_table.shape[1]
    n_steps = n_pages // pps
    past = n_pages * 2 * BLK
    nbl = kcvc.shape[1]
    wl = win4.shape[2]
    rows = GQA * ts
    kw = KV_GROUPS * HEAD_DIM
    assert ts == SUBLANES and past % QCHUNK == 0 and wl % LANES == 0
    emat = (np.arange(nbl)[:, None] == (np.arange(past)[None, :] // BLK)).astype(np.float32)
    kern = functools.partial(_nsa_sample_kernel, pps=pps, n_steps=n_steps, past=past, ts=ts, nbl=nbl)
    c3 = lambda shp: pl.BlockSpec(shp, lambda b, j, pt: (0,) * len(shp))
    return pl.pallas_call(
        kern,
        grid_spec=pltpu.PrefetchScalarGridSpec(
            num_scalar_prefetch=1, grid=(batch, n_steps),
            in_specs=[pl.BlockSpec((None, ts, hd), lambda b, j, pt: (b, 0, 0)),
                      pl.BlockSpec((None, ts, kvn3.shape[2]), lambda b, j, pt: (b, 0, 0)),
                      pl.BlockSpec((None, ts, LANES), lambda b, j, pt: (b, 0, 0)),
                      pl.BlockSpec((None, nbl, 2 * kw), lambda b, j, pt: (b, 0, 0)),
                      pl.BlockSpec((None, None, wl, 2 * kw), lambda b, j, pt: (layer, b, 0, 0)),
                      c3((KV_GROUPS, 3, rows, LANES)), c3((KV_GROUPS, rows, wl + LANES)), c3((KV_GROUPS, rows, 1)),
                      c3((nbl, past))] + _page_specs(pps, n_steps, 1, layer),
            out_specs=pl.BlockSpec((None, ts, hd), lambda b, j, pt: (b, 0, 0)),
            scratch_shapes=[pltpu.VMEM((KV_GROUPS, ts, past), F32),
                            pltpu.VMEM((KV_GROUPS, rows, 1), F32), pltpu.VMEM((KV_GROUPS, rows, 1), F32),
                            pltpu.VMEM((KV_GROUPS, rows, HEAD_DIM), F32), pltpu.VMEM((KV_GROUPS, rows, HEAD_DIM), F32),
                            pltpu.VMEM((LANES, 4 * kw), F32)]),
        out_shape=jax.ShapeDtypeStruct((batch, ts, hd), F32),
        compiler_params=_params(("parallel", "arbitrary")),
        name="nsa_sample",
    )(page_table, q3, kvn3, small3, kcvc, win4, near_s, wtab, cst_s, jnp.asarray(emat, BF16), *([cache5] * pps))


def _merge_kernel(c_ref, o_ref, y_ref, g0_ref, g1_ref, g2_ref, wa_ref, wn_ref, ws_ref, out_ref):
    oa = jnp.dot(c_ref[...], wa_ref[...], preferred_element_type=F32)
    ob = jnp.dot(o_ref[...], wn_ref[...], preferred_element_type=F32)
    oc = jnp.dot(y_ref[...], ws_ref[...], preferred_element_type=F32)
    mixed = (jax.nn.sigmoid(g0_ref[...]) * oa + jax.nn.sigmoid(g1_ref[...]) * ob + jax.nn.sigmoid(g2_ref[...]) * oc)
    out_ref[...] = mixed.astype(out_ref.dtype)


def merge_branches(c, o, y, gmix, wa, wn, ws):
    n = c.shape[0]
    d = wa.shape[1]
    tm = _tile(n, 768, BF16_ROWS)
    tn = _tile(d, 512, LANES)
    nj = d // tn
    act = lambda a: pl.BlockSpec((tm, a.shape[1]), lambda i, j: (i, 0))
    wsp = lambda w: pl.BlockSpec((w.shape[0], tn), lambda i, j: (0, j))
    gsp = lambda br: pl.BlockSpec((tm, tn), lambda i, j: (i, br * nj + j))
    return pl.pallas_call(
        _merge_kernel,
        grid=(n // tm, nj),
        in_specs=[act(c), act(o), act(y), gsp(0), gsp(1), gsp(2), wsp(wa), wsp(wn), wsp(ws)],
        out_specs=pl.BlockSpec((tm, tn), lambda i, j: (i, j)),
        out_shape=jax.ShapeDtypeStruct((n, d), BF16),
        compiler_params=_params(("parallel", "parallel")),
        name="merge",
    )(c, o, y, gmix, gmix, gmix, wa, wn, ws)


def _outproj_kernel(a_ref, w_ref, x_ref, o_ref):
    o_ref[...] = x_ref[...] + jnp.dot(a_ref[...], w_ref[...], preferred_element_type=F32)


def out_projection(a, w, x):
    n, k = a.shape
    d = w.shape[1]
    tm = _tile(n, 768, BF16_ROWS)
    tn = _tile(d, 1024, LANES)
    return pl.pallas_call(
        _outproj_kernel,
        grid=(n // tm, d // tn),
        in_specs=[pl.BlockSpec((tm, k), lambda i, j: (i, 0)), pl.BlockSpec((k, tn), lambda i, j: (0, j)),
                  pl.BlockSpec((tm, tn), lambda i, j: (i, j))],
        out_specs=pl.BlockSpec((tm, tn), lambda i, j: (i, j)),
        out_shape=jax.ShapeDtypeStruct((n, d), F32),
        compiler_params=_params(("parallel", "parallel")),
        name="out_proj",
    )(a, w, x)


def _router_kernel(x_ref, g_ref, w_ref, b_ref, h_ref, r_ref):
    x = x_ref[...]
    ms = jnp.mean(x * x, axis=-1, keepdims=True)
    h = x * lax.rsqrt(ms + EPS) * g_ref[...]
    h_ref[...] = h.astype(h_ref.dtype)
    logits = jnp.dot(h.astype(BF16), w_ref[...], preferred_element_type=F32) + b_ref[...]
    lane = lax.broadcasted_iota(jnp.int32, logits.shape, 1)
    big = jnp.int32(1 << 30)
    gl = jnp.where(lane < MOE_GROUPS, logits, NEG)
    gmax = jnp.max(gl, axis=-1, keepdims=True)
    gidx = jnp.min(jnp.where(gl == gmax, lane, big), axis=-1, keepdims=True)
    g_w = 1.0 / jnp.sum(jnp.exp(gl - gmax), axis=-1, keepdims=True)
    eid = lane - MOE_GROUPS
    in_grp = (eid >= 0) & (eid < MOE_GROUPS * MOE_EPG) & ((eid >> 3) == gidx)
    el = jnp.where(in_grp, logits, NEG)
    m1 = jnp.max(el, axis=-1, keepdims=True)
    i1 = jnp.min(jnp.where(el == m1, lane, big), axis=-1, keepdims=True)
    el2 = jnp.where(lane == i1, NEG, el)
    m2 = jnp.max(el2, axis=-1, keepdims=True)
    i2 = jnp.min(jnp.where(el2 == m2, lane, big), axis=-1, keepdims=True)
    p2 = jnp.exp(m2 - m1)
    w1 = g_w / (1.0 + p2)
    w2 = g_w * p2 / (1.0 + p2)
    r_ref[...] = jnp.where(lane == 0, (i1 - MOE_GROUPS).astype(F32),
                           jnp.where(lane == 1, (i2 - MOE_GROUPS).astype(F32),
                                     jnp.where(lane == 2, w1, jnp.where(lane == 3, w2, 0.0))))


def moe_router(x, g, wr, br):
    n, d = x.shape
    tm = _tile(n, 256, BF16_ROWS)
    return pl.pallas_call(
        _router_kernel,
        grid=(n // tm,),
        in_specs=[pl.BlockSpec((tm, d), lambda i: (i, 0)), pl.BlockSpec((1, d), lambda i: (0, 0)),
                  pl.BlockSpec((d, LANES), lambda i: (0, 0)), pl.BlockSpec((1, LANES), lambda i: (0, 0))],
        out_specs=[pl.BlockSpec((tm, d), lambda i: (i, 0)), pl.BlockSpec((tm, LANES), lambda i: (i, 0))],
        out_shape=[jax.ShapeDtypeStruct((n, d), BF16), jax.ShapeDtypeStruct((n, LANES), F32)],
        compiler_params=_params(("parallel",)),
        name="moe_router",
    )(x, g.reshape(1, d), wr, br)


def _expert_kernel(te_ref, nu_ref, x_ref, rw_ref, wg_ref, wu_ref, wd_ref, y_ref, wgb_ref, wub_ref, wdb_ref):
    t = pl.program_id(0)
    fresh = jnp.logical_or(t == 0, te_ref[t] != te_ref[jnp.maximum(t - 1, 0)])

    @pl.when(jnp.logical_and(fresh, t < nu_ref[0]))
    def _():
        wgb_ref[...] = wg_ref[...].astype(BF16)
        wub_ref[...] = wu_ref[...].astype(BF16)
        wdb_ref[...] = wd_ref[...].astype(BF16)

    @pl.when(t < nu_ref[0])
    def _():
        x = x_ref[...]
        a = jnp.dot(x, wgb_ref[...], preferred_element_type=F32)
        u = jnp.dot(x, wub_ref[...], preferred_element_type=F32)
        hid = (_silu(a) * u * rw_ref[...]).astype(BF16)
        y_ref[...] = jnp.dot(hid, wdb_ref[...], preferred_element_type=F32)


def moe_experts(xs, row_w, tile_e, n_used, wg, wu, wd, tme):
    r, d = xs.shape
    ff = wg.shape[2]
    nt = r // tme
    return pl.pallas_call(
        _expert_kernel,
        grid_spec=pltpu.PrefetchScalarGridSpec(
            num_scalar_prefetch=2, grid=(nt,),
            in_specs=[pl.BlockSpec((tme, d), lambda t, te, nu: (t, 0)),
                      pl.BlockSpec((tme, 1), lambda t, te, nu: (t, 0)),
                      pl.BlockSpec((None, d, ff), lambda t, te, nu: (te[t], 0, 0)),
                      pl.BlockSpec((None, d, ff), lambda t, te, nu: (te[t], 0, 0)),
                      pl.BlockSpec((None, ff, d), lambda t, te, nu: (te[t], 0, 0))],
            out_specs=pl.BlockSpec((tme, d), lambda t, te, nu: (t, 0)),
            scratch_shapes=[pltpu.VMEM((d, ff), BF16), pltpu.VMEM((d, ff), BF16), pltpu.VMEM((ff, d), BF16)]),
        out_shape=jax.ShapeDtypeStruct((r, d), F32),
        compiler_params=_params(("arbitrary",)),
        name="moe_experts",
    )(tile_e, n_used, xs, row_w, wg, wu, wd)


def moe_layer(x, g, wrg, brg, wre, bre, wg, wu, wd):
    n, d = x.shape
    n_exp = wg.shape[0]
    wr = jnp.concatenate([wrg, wre.transpose(1, 0, 2).reshape(d, n_exp)], axis=1)
    wr = jnp.pad(wr, ((0, 0), (0, LANES - wr.shape[1])))
    br = jnp.pad(jnp.concatenate([brg, bre.reshape(-1)]), (0, LANES - MOE_GROUPS - n_exp)).reshape(1, LANES)
    h, route = moe_router(x, g, wr.astype(BF16), br)
    tme = 256
    e_flat = route[:, 0:MOE_TOPK].astype(jnp.int32).T.reshape(-1)
    w_flat = route[:, MOE_TOPK:2 * MOE_TOPK].T.reshape(-1)
    na = e_flat.shape[0]
    nt = -(-na // tme) + n_exp
    order = jnp.argsort(e_flat, stable=True)
    counts = jnp.sum(e_flat[:, None] == jnp.arange(n_exp)[None, :], axis=0).astype(jnp.int32)
    pc = ((counts + tme - 1) // tme) * tme
    pend = jnp.cumsum(pc)
    pstart = pend - pc
    ustart = jnp.cumsum(counts) - counts
    se = e_flat[order]
    dest = pstart[se] + jnp.arange(na, dtype=jnp.int32) - ustart[se]
    row_tok = jnp.zeros((nt * tme,), jnp.int32).at[dest].set((order % n).astype(jnp.int32))
    row_w = jnp.zeros((nt * tme,), F32).at[dest].set(w_flat[order])
    dest_flat = jnp.zeros((na,), jnp.int32).at[order].set(dest)
    tile_e = jnp.minimum(jnp.searchsorted(pend, jnp.arange(nt, dtype=jnp.int32) * tme, side="right"), n_exp - 1).astype(jnp.int32)
    n_used = (pend[-1] // tme).astype(jnp.int32).reshape(1)
    xs = jnp.take(h, row_tok, axis=0)
    ys = moe_experts(xs, row_w.reshape(-1, 1), tile_e, n_used, wg, wu, wd, tme)
    return x + jnp.take(ys, dest_flat[:n], axis=0) + jnp.take(ys, dest_flat[n:], axis=0)


def kernel(x_prompt, x_sample, cache_nsa_kv, cache_nsa_win, state_conv_a, state_conv_ssm, state_ssm, page_table,
           rel_bias, norm_mix_g, w_in, conv_a_w, conv_a_b, ln_a_g, ln_a_b, w_a_out, w_nsa_out, conv_s_w, conv_s_b,
           dt_bias, a_log, d_skip, ssm_norm_g, w_ssm_out, w_out, norm_ffn_g, w_router_group, b_router_group,
           w_router_expert, b_router_expert, w_gate, w_up, w_down, final_norm_g):
    bp, tp, d = x_prompt.shape
    bs, ts, _ = x_sample.shape
    depth = w_in.shape[0]
    n_p, n_s = bp * tp, bs * ts
    ch = conv_a_w.shape[2]
    conv_dim = conv_s_w.shape[2]
    heads_ssm = dt_bias.shape[1]
    inner = heads_ssm * SSM_HEAD_DIM
    hd = N_HEADS * HEAD_DIM
    kvw = KV_KINDS * KV_GROUPS * HEAD_DIM
    n_pages = page_table.shape[1]
    page = cache_nsa_kv.shape[2]
    past = n_pages * page
    wl = cache_nsa_win.shape[2]
    assert page == 2 * BLK and wl == min(WINDOW, past) and tp >= WINDOW
    pps = _tile(n_pages, 16, 1)

    splits = (2 * ch, hd, kvw, 3 * N_HEADS, inner, conv_dim, heads_ssm, 3 * d)
    offs = np.concatenate([[0], np.cumsum(splits)])
    o_u, o_q, o_kv, o_gn, o_z, o_x, o_dt, o_gm = [int(v) for v in offs[:-1]]

    near_d = _near_distances()
    tj = np.arange(wl + LANES)[None, :]
    win_d = np.where(tj < wl + ts, wl + np.arange(ts)[:, None] - tj, 0)
    near_f = expand_bias(rel_bias, near_d)
    wtab_f = expand_bias(rel_bias, win_d)
    near_g = near_f.reshape(KV_GROUPS, GQA, 3, QCHUNK, LANES).transpose(0, 2, 1, 3, 4)
    near_p = near_g.reshape(KV_GROUPS, 3, GQA * QCHUNK, LANES)
    near_s = near_g[:, :, :, :ts].reshape(KV_GROUPS, 3, GQA * ts, LANES)
    wtab = wtab_f.reshape(KV_GROUPS, GQA * ts, wl + LANES)
    far = rel_bias[REL_BUCKETS - 1].reshape(KV_GROUPS, GQA, 1)
    cst_p = jnp.repeat(far, QCHUNK, axis=1).reshape(KV_GROUPS, GQA * QCHUNK, 1)
    cst_s = jnp.repeat(far, ts, axis=1).reshape(KV_GROUPS, GQA * ts, 1)

    cache5 = cache_nsa_kv.reshape(cache_nsa_kv.shape[0], depth, 1, page, 4 * KV_GROUPS * HEAD_DIM)
    win4 = cache_nsa_win.reshape(depth, bs, wl, 2 * KV_GROUPS * HEAD_DIM)
    zeros_hist_a = jnp.zeros((bp, HIST_A, ch), F32)
    zeros_hist_s = jnp.zeros((bp, HIST_S, conv_dim), F32)
    zeros_ssm = jnp.zeros((bp, heads_ssm, SSM_HEAD_DIM, SSM_STATE), F32)

    x = jnp.concatenate([x_prompt.reshape(n_p, d), x_sample.reshape(n_s, d)], axis=0)
    kv_p, kv_s, win_p, win_s, ha_p, ha_s, hs_p, hs_s, st_p, st_s = ([] for _ in range(10))
    for l in range(depth):
        wl_in = w_in[l]
        seg = lambda a, b: wl_in[:, a:b].astype(BF16)
        h = rmsnorm(x, norm_mix_g[l], BF16)
        (u_in,) = matmul(h, seg(o_u, o_q), [F32])
        (q,) = matmul(h, (wl_in[:, o_q:o_kv] * (HEAD_DIM ** -0.5)).astype(BF16), [BF16])
        kv, kvb = matmul(h, seg(o_kv, o_gn), [F32, BF16])
        w_small = jnp.concatenate([wl_in[:, o_dt:o_gm], wl_in[:, o_gn:o_z],
                                   jnp.zeros((d, LANES - heads_ssm - 3 * N_HEADS), F32)], axis=1).astype(BF16)
        (small,) = matmul(h, w_small, [F32])
        (z,) = matmul(h, seg(o_z, o_x), [F32])
        (xbc,) = matmul(h, seg(o_x, o_dt), [F32])
        (gmix,) = matmul(h, seg(o_gm, o_gm + 3 * d), [F32])

        hist_a = jnp.pad(state_conv_a[l], ((0, 0), (HIST_A - (CONV_A_W - 1), 0), (0, 0)))
        ca_args = (conv_a_w[l], conv_a_b[l], ln_a_g[l], ln_a_b[l])
        c_p, ha_new_p = conv_a_branch(u_in[:n_p], zeros_hist_a, *ca_args, bp, tp, BF16)
        c_s, ha_new_s = conv_a_branch(u_in[n_p:], hist_a, *ca_args, bs, ts, F32)
        hist_s = jnp.pad(state_conv_ssm[l], ((0, 0), (HIST_S - (SSM_CONV_W - 1), 0), (0, 0)))
        ss_args = (conv_s_w[l], conv_s_b[l], dt_bias[l], a_log[l], d_skip[l], ssm_norm_g[l])
        y_p, hs_new_p, st_new_p = ssd_branch(xbc[:n_p], z[:n_p], small[:n_p], zeros_hist_s, zeros_ssm, *ss_args, bp, tp, BF16)
        y_s, hs_new_s, st_new_s = ssd_branch(xbc[n_p:], z[n_p:], small[n_p:], hist_s, state_ssm[l], *ss_args, bs, ts, F32)
        o_p = nsa_prompt(q, kvb, kv, small, near_p, cst_p, bp, tp)
        means = sample_block_means(cache5, page_table, l, pps)
        nbl = -(-(past // BLK + 1) // LANES) * LANES
        kcvc = jnp.pad(means, ((0, 0), (0, nbl - means.shape[1]), (0, 0)))
        o_s = nsa_sample(q[n_p:].astype(F32).reshape(bs, ts, hd), kv[n_p:].reshape(bs, ts, kvw),
                         small[n_p:].reshape(bs, ts, LANES), kcvc, win4, near_s, wtab, cst_s, cache5, page_table, l, pps)

        c_all = jnp.concatenate([c_p, c_s.astype(BF16)], axis=0)
        o_all = jnp.concatenate([o_p[:n_p], o_s.reshape(n_s, hd).astype(BF16)], axis=0)
        y_all = jnp.concatenate([y_p, y_s.astype(BF16)], axis=0)
        mixed = merge_branches(c_all, o_all, y_all, gmix, w_a_out[l].astype(BF16), w_nsa_out[l].astype(BF16),
                               w_ssm_out[l].astype(BF16))
        x = out_projection(mixed, w_out[l].astype(BF16), x)
        x = moe_layer(x, norm_ffn_g[l], w_router_group[l], b_router_group[l], w_router_expert[l], b_router_expert[l],
                      w_gate[l], w_up[l], w_down[l])

        kvr = kv[:, :4 * KV_GROUPS * HEAD_DIM]
        kv_p.append(kvr[:n_p].reshape(bp, tp, 4, KV_GROUPS, HEAD_DIM))
        kv_s.append(kvr[n_p:].reshape(bs, ts, 4, KV_GROUPS, HEAD_DIM))
        wr = kv[:, 4 * KV_GROUPS * HEAD_DIM:]
        win_p.append(wr[:n_p].reshape(bp, tp, 2, KV_GROUPS, HEAD_DIM)[:, tp - WINDOW:])
        win_all = jnp.concatenate([cache_nsa_win[l], wr[n_p:].reshape(bs, ts, 2, KV_GROUPS, HEAD_DIM)], axis=1)
        keep = min(WINDOW, wl + ts)
        win_s.append(win_all[:, wl + ts - keep:])
        ha_p.append(ha_new_p[:, HIST_A - (CONV_A_W - 1):])
        ha_s.append(ha_new_s[:, HIST_A - (CONV_A_W - 1):])
        hs_p.append(hs_new_p[:, HIST_S - (SSM_CONV_W - 1):])
        hs_s.append(hs_new_s[:, HIST_S - (SSM_CONV_W - 1):])
        st_p.append(st_new_p)
        st_s.append(st_new_s)

    y = rmsnorm(x, final_norm_g, F32)
    return (y[:n_p].reshape(bp, tp, d), y[n_p:].reshape(bs, ts, d),
            jnp.stack(kv_p, axis=1), jnp.stack(kv_s, axis=1), jnp.stack(win_p), jnp.stack(win_s),
            jnp.stack(ha_p), jnp.stack(ha_s), jnp.stack(hs_p), jnp.stack(hs_s), jnp.stack(st_p), jnp.stack(st_s))
```

```python
import functools
import math

import numpy as np
import jax
import jax.numpy as jnp
from jax import lax
from jax.experimental import pallas as pl
from jax.experimental.pallas import tpu as pltpu

HEAD_DIM = 64
KV_GROUPS = 4
GQA = 4
N_HEADS = KV_GROUPS * GQA
KV_KINDS = 6
BLK = 64
TOPK = 8
WINDOW = 512
QCHUNK = 128
REL_BUCKETS = 32
REL_MAX_DIST = 128
CONV_A_W = 31
SSM_CONV_W = 4
SSM_HEAD_DIM = 64
SSM_GROUPS = 4
SSM_STATE = 128
SSM_CHUNK = 128
MOE_GROUPS = 4
MOE_EPG = 8
MOE_TOPK = 2
EPS = 1e-6
NEG = -1e30
FORCE = 1e4

LANES = 128
SUBLANES = 8
BF16_ROWS = 16
VMEM_LIMIT = 56 * 1024 * 1024

DT_COLS = 32
GATE_COL0 = DT_COLS
SUPER = 4

HIGHEST = lax.Precision.HIGHEST
F32 = jnp.float32
BF16 = jnp.bfloat16


def _params(sem, limit=VMEM_LIMIT):
    return pltpu.CompilerParams(dimension_semantics=sem, vmem_limit_bytes=limit)


def _tile(n, target, mult):
    best = None
    for t in range(mult, min(n, target) + 1, mult):
        if n % t == 0:
            best = t
    return best if best is not None else n


def _dot_nt(a, b):
    return lax.dot_general(a, b, (((1,), (1,)), ((), ())), preferred_element_type=F32)


def _dot_tn(a, b):
    return lax.dot_general(a, b, (((0,), (0,)), ((), ())), preferred_element_type=F32)


def _silu(x):
    return x * jax.nn.sigmoid(x)


def _softplus(x):
    return jnp.maximum(x, 0.0) + jnp.log(1.0 + jnp.exp(-jnp.abs(x)))


def _rms_kernel(x_ref, g_ref, o_ref):
    x = x_ref[...]
    ms = jnp.mean(x * x, axis=-1, keepdims=True)
    o_ref[...] = (x * lax.rsqrt(ms + EPS) * g_ref[...]).astype(o_ref.dtype)


def rmsnorm(x, g, out_dtype):
    n, d = x.shape
    tm = _tile(n, 1024, BF16_ROWS)
    return pl.pallas_call(
        _rms_kernel,
        grid=(n // tm,),
        in_specs=[pl.BlockSpec((tm, d), lambda i: (i, 0)), pl.BlockSpec((1, d), lambda i: (0, 0))],
        out_specs=pl.BlockSpec((tm, d), lambda i: (i, 0)),
        out_shape=jax.ShapeDtypeStruct((n, d), out_dtype),
        compiler_params=_params(("parallel",)),
        name="rmsnorm",
    )(x, g.reshape(1, d))


def _mm_kernel(a_ref, w_ref, *o_refs):
    acc = jnp.dot(a_ref[...], w_ref[...], preferred_element_type=F32)
    for o in o_refs:
        o[...] = acc.astype(o.dtype)


def matmul(a, w, out_dtypes):
    n, k = a.shape
    c = w.shape[1]
    tm = _tile(n, 768, BF16_ROWS)
    tn = _tile(c, 1024, LANES)
    outs = pl.pallas_call(
        _mm_kernel,
        grid=(n // tm, c // tn),
        in_specs=[pl.BlockSpec((tm, k), lambda i, j: (i, 0)), pl.BlockSpec((k, tn), lambda i, j: (0, j))],
        out_specs=[pl.BlockSpec((tm, tn), lambda i, j: (i, j)) for _ in out_dtypes],
        out_shape=[jax.ShapeDtypeStruct((n, c), dt) for dt in out_dtypes],
        compiler_params=_params(("parallel", "parallel")),
        name="proj",
    )(a, w)
    return outs


HIST_A = 32
HIST_S = 8


def _round_bf16(x):
    return x.astype(BF16).astype(F32)


def _push_tail(tail_ref, new, n_new, keep):
    if n_new >= keep:
        tail_ref[...] = new[n_new - keep:n_new]
    else:
        tail_ref[...] = jnp.concatenate([tail_ref[n_new:keep], new], axis=0)


def _conva_kernel(u_ref, hist_ref, w_ref, b_ref, lg_ref, lb_ref, c_ref, hout_ref, ext_ref, conv_ref, tail_ref, *, tt, ch):
    t = pl.program_id(1)

    @pl.when(t == 0)
    def _():
        tail_ref[...] = hist_ref[...]
        ext_ref[0:HIST_A, :] = _round_bf16(hist_ref[...])

    uin = u_ref[...]
    u = uin[:, :ch] * jax.nn.sigmoid(uin[:, ch:])
    ext_ref[HIST_A:HIST_A + tt, :] = _round_bf16(u)
    _push_tail(tail_ref, u, tt, HIST_A)
    hout_ref[...] = tail_ref[...]
    off = HIST_A - (CONV_A_W - 1)
    rb = min(tt, 128)

    def lane_block(cb, carry):
        lo = pl.multiple_of(cb * LANES, LANES)
        for r in range(tt // rb):
            acc = jnp.broadcast_to(b_ref[:, pl.ds(lo, LANES)], (rb, LANES))
            for k in range(CONV_A_W):
                acc = acc + w_ref[k:k + 1, pl.ds(lo, LANES)] * ext_ref[r * rb + off + k:r * rb + off + k + rb, pl.ds(lo, LANES)]
            conv_ref[r * rb:(r + 1) * rb, pl.ds(lo, LANES)] = acc
        return carry

    lax.fori_loop(0, ch // LANES, lane_block, 0)
    cv = conv_ref[...]
    mu = jnp.mean(cv, axis=-1, keepdims=True)
    xc = cv - mu
    var = jnp.mean(xc * xc, axis=-1, keepdims=True)
    y = xc * lax.rsqrt(var + EPS) * lg_ref[...] + lb_ref[...]
    c_ref[...] = _silu(y).astype(c_ref.dtype)
    ext_ref[0:HIST_A, :] = ext_ref[tt:tt + HIST_A, :]


def conv_a_branch(u_in, row_off, hist, w, b, lg, lb, batch, t_len, out_dtype):
    ch = w.shape[1]
    tt = _tile(t_len, 256, SUBLANES)
    nt = t_len // tt
    assert row_off % tt == 0
    r0 = row_off // tt
    kern = functools.partial(_conva_kernel, tt=tt, ch=ch)
    vec = lambda: pl.BlockSpec((1, ch), lambda bi, ti: (0, 0))
    return pl.pallas_call(
        kern,
        grid=(batch, nt),
        in_specs=[pl.BlockSpec((tt, 2 * ch), lambda bi, ti: (r0 + bi * nt + ti, 0)),
                  pl.BlockSpec((None, HIST_A, ch), lambda bi, ti: (bi, 0, 0)),
                  pl.BlockSpec((CONV_A_W, ch), lambda bi, ti: (0, 0)), vec(), vec(), vec()],
        out_specs=[pl.BlockSpec((tt, ch), lambda bi, ti: (bi * nt + ti, 0)),
                   pl.BlockSpec((None, HIST_A, ch), lambda bi, ti: (bi, 0, 0))],
        out_shape=[jax.ShapeDtypeStruct((batch * t_len, ch), out_dtype),
                   jax.ShapeDtypeStruct((batch, HIST_A, ch), F32)],
        scratch_shapes=[pltpu.VMEM((tt + HIST_A, ch), F32), pltpu.VMEM((tt, ch), F32), pltpu.VMEM((HIST_A, ch), F32)],
        compiler_params=_params(("parallel", "arbitrary")),
        name="conv_a",
    )(u_in, hist, _round_bf16(w), b.reshape(1, ch), lg.reshape(1, ch), lb.reshape(1, ch))


def _ssd_kernel(xbc_ref, z_ref, dt_ref, dtt_ref, hist_ref, h0_ref, cw_ref, cb_ref, dtb_ref, dtbt_ref,
                a_ref, at_ref, dvec_ref, gn_ref, y_ref, hout_ref, sout_ref, ext_ref, h_ref, yacc_ref, tail_ref,
                *, q, nc, inner, heads):
    c = pl.program_id(1)

    @pl.when(c == 0)
    def _():
        tail_ref[...] = hist_ref[...]
        ext_ref[0:HIST_S, :] = _round_bf16(hist_ref[...])
        h_ref[...] = h0_ref[...]

    xbc = xbc_ref[...]
    ext_ref[HIST_S:HIST_S + q, :] = _round_bf16(xbc)
    _push_tail(tail_ref, xbc, q, HIST_S)
    hout_ref[...] = tail_ref[...]
    off = HIST_S - (SSM_CONV_W - 1)
    acc = jnp.broadcast_to(cb_ref[...], (q, cb_ref.shape[1]))
    for k in range(SSM_CONV_W):
        acc = acc + cw_ref[k:k + 1, :] * ext_ref[off + k:off + k + q, :]
    xc = _silu(acc)
    ext_ref[0:HIST_S, :] = ext_ref[q:q + HIST_S, :]

    gw = SSM_GROUPS * SSM_STATE
    xs = xc[:, :inner]
    bm = xc[:, inner:inner + gw]
    cm = xc[:, inner + gw:inner + 2 * gw]

    dt = _softplus(dt_ref[...] + dtb_ref[...])
    dtt = _softplus(dtt_ref[...] + dtbt_ref[...])
    da = dt * a_ref[...]
    dat = dtt * at_ref[...]
    ri = lax.broadcasted_iota(jnp.int32, (q, q), 0)
    ci = lax.broadcasted_iota(jnp.int32, (q, q), 1)
    causal = ri >= ci
    tri = causal.astype(F32)
    trit = (ri <= ci).astype(F32)
    acum = jnp.dot(tri, da, precision=HIGHEST, preferred_element_type=F32)
    acumt = jnp.dot(dat, trit, precision=HIGHEST, preferred_element_type=F32)
    alast = acum[q - 1:q, :]
    hpg = heads // SSM_GROUPS
    for g in range(SSM_GROUPS):
        bg = bm[:, g * SSM_STATE:(g + 1) * SSM_STATE].astype(BF16)
        cg = cm[:, g * SSM_STATE:(g + 1) * SSM_STATE].astype(BF16)
        cbm = _dot_nt(cg, bg)
        for e in range(hpg):
            hh = g * hpg + e
            a_col = acum[:, hh:hh + 1]
            a_row = acumt[hh:hh + 1, :]
            decay = jnp.exp(jnp.where(causal, a_col - a_row, NEG))
            wm = cbm * decay * dtt[hh:hh + 1, :]
            xh = xs[:, hh * SSM_HEAD_DIM:(hh + 1) * SSM_HEAD_DIM]
            hprev = h_ref[hh]
            y_h = (jnp.dot(wm.astype(BF16), xh.astype(BF16), preferred_element_type=F32)
                   + _dot_nt(cg, hprev.astype(BF16)) * jnp.exp(a_col))
            yacc_ref[:, hh * SSM_HEAD_DIM:(hh + 1) * SSM_HEAD_DIM] = y_h
            al = alast[:, hh:hh + 1]
            w_end = jnp.exp(al - a_col) * dt[:, hh:hh + 1]
            h_ref[hh] = jnp.exp(al) * hprev + _dot_tn((xh * w_end).astype(BF16), bg)

    y = (yacc_ref[...] + xs * dvec_ref[...]) * _silu(z_ref[...])
    gsz = inner // SSM_GROUPS
    for g in range(SSM_GROUPS):
        v = y[:, g * gsz:(g + 1) * gsz]
        ms = jnp.mean(v * v, axis=-1, keepdims=True)
        y_ref[:, g * gsz:(g + 1) * gsz] = (v * lax.rsqrt(ms + EPS) * gn_ref[:, g * gsz:(g + 1) * gsz]).astype(y_ref.dtype)

    @pl.when(c == nc - 1)
    def _():
        sout_ref[...] = h_ref[...]


def ssd_branch(xbc, z, small, small_t, row_off, hist, h0, cw, cb, dt_bias, a_log, d_skip, norm_g, batch, t_len, out_dtype):
    conv_dim = xbc.shape[1]
    inner = z.shape[1]
    heads = dt_bias.shape[0]
    assert heads == DT_COLS
    q = math.gcd(t_len, SSM_CHUNK)
    nc = t_len // q
    assert row_off % q == 0
    r0 = row_off // q
    pad = LANES - heads
    dtb = jnp.pad(dt_bias, (0, pad)).reshape(1, LANES)
    a_neg = jnp.pad(-jnp.exp(a_log), (0, pad)).reshape(1, LANES)
    dvec = jnp.repeat(d_skip, SSM_HEAD_DIM).reshape(1, inner)
    kern = functools.partial(_ssd_kernel, q=q, nc=nc, inner=inner, heads=heads)
    row = lambda w: pl.BlockSpec((1, w), lambda bi, ci: (0, 0))
    col = lambda: pl.BlockSpec((LANES, 1), lambda bi, ci: (0, 0))
    tok = lambda w: pl.BlockSpec((q, w), lambda bi, ci: (r0 + bi * nc + ci, 0))
    return pl.pallas_call(
        kern,
        grid=(batch, nc),
        in_specs=[tok(conv_dim), tok(inner), tok(LANES),
                  pl.BlockSpec((None, LANES, q), lambda bi, ci: (bi, 0, ci)),
                  pl.BlockSpec((None, HIST_S, conv_dim), lambda bi, ci: (bi, 0, 0)),
                  pl.BlockSpec((None, heads, SSM_HEAD_DIM, SSM_STATE), lambda bi, ci: (bi, 0, 0, 0)),
                  pl.BlockSpec((SSM_CONV_W, conv_dim), lambda bi, ci: (0, 0)), row(conv_dim),
                  row(LANES), col(), row(LANES), col(), row(inner), row(inner)],
        out_specs=[pl.BlockSpec((q, inner), lambda bi, ci: (bi * nc + ci, 0)),
                   pl.BlockSpec((None, HIST_S, conv_dim), lambda bi, ci: (bi, 0, 0)),
                   pl.BlockSpec((None, heads, SSM_HEAD_DIM, SSM_STATE), lambda bi, ci: (bi, 0, 0, 0))],
        out_shape=[jax.ShapeDtypeStruct((batch * t_len, inner), out_dtype),
                   jax.ShapeDtypeStruct((batch, HIST_S, conv_dim), F32),
                   jax.ShapeDtypeStruct((batch, heads, SSM_HEAD_DIM, SSM_STATE), F32)],
        scratch_shapes=[pltpu.VMEM((q + HIST_S, conv_dim), F32),
                        pltpu.VMEM((heads, SSM_HEAD_DIM, SSM_STATE), F32),
                        pltpu.VMEM((q, inner), F32), pltpu.VMEM((HIST_S, conv_dim), F32)],
        compiler_params=_params(("parallel", "arbitrary")),
        name="ssd",
    )(xbc, z, small, small_t, hist, h0, _round_bf16(cw), cb.reshape(1, conv_dim), dtb, dtb.reshape(LANES, 1),
      a_neg, a_neg.reshape(LANES, 1), dvec, norm_g.reshape(1, inner))


def _bucket_np(d):
    n = np.maximum(d, 0)
    exact = REL_BUCKETS // 2
    nf = np.maximum(n, 1).astype(np.float32)
    large = exact + (np.log(nf / np.float32(exact)) / np.float32(math.log(REL_MAX_DIST / exact))
                     * np.float32(REL_BUCKETS - exact)).astype(np.int32)
    return np.where(n < exact, n, np.minimum(large, REL_BUCKETS - 1))


def _near_distances():
    i = np.arange(QCHUNK)[:, None]
    j = np.arange(LANES)[None, :]
    diag = i - j
    prev = QCHUNK + i - j
    band = np.where(j < 4, i + (BLK + 1) - BLK * j, 0)
    return np.stack([diag, prev, band])


def _expand_kernel(tab_ref, oh_ref, o_ref):
    o_ref[...] = jnp.dot(tab_ref[...], oh_ref[...], precision=HIGHEST, preferred_element_type=F32)


def expand_bias(rel_bias, dist):
    flat = dist.reshape(-1)
    m = flat.shape[0]
    assert m % LANES == 0
    onehot = (np.arange(REL_BUCKETS)[:, None] == _bucket_np(flat)[None, :]).astype(np.float32)
    tm = _tile(m, 8192, LANES)
    heads = rel_bias.shape[1]
    out = pl.pallas_call(
        _expand_kernel,
        grid=(m // tm,),
        in_specs=[pl.BlockSpec((heads, REL_BUCKETS), lambda i: (0, 0)), pl.BlockSpec((REL_BUCKETS, tm), lambda i: (0, i))],
        out_specs=pl.BlockSpec((heads, tm), lambda i: (0, i)),
        out_shape=jax.ShapeDtypeStruct((heads, m), F32),
        compiler_params=_params(("parallel",)),
        name="bias_expand",
    )(rel_bias.T, jnp.asarray(onehot))
    return out.reshape((heads,) + dist.shape)


def _select_blocks(imp, lane):
    sel = jnp.zeros(imp.shape, F32)
    big = jnp.int32(1 << 30)
    v = imp
    for _ in range(TOPK):
        mx = jnp.max(v, axis=-1, keepdims=True)
        idx = jnp.min(jnp.where(v == mx, lane, big), axis=-1, keepdims=True)
        hit = lane == idx
        sel = jnp.where(hit, 1.0, sel)
        v = jnp.where(hit, -3e38, v)
    return sel


def _rep4(x):
    return jnp.concatenate([x, x, x, x], axis=0)


def _qgroup(q_ref, g):
    return jnp.concatenate([q_ref[:, (g * GQA + r) * HEAD_DIM:(g * GQA + r + 1) * HEAD_DIM] for r in range(GQA)], axis=0)


def _gate_col(sg, br, g):
    c0 = GATE_COL0 + br * N_HEADS + g * GQA
    return jnp.concatenate([sg[:, c0 + r:c0 + r + 1] for r in range(GQA)], axis=0)


def _finish(acc):
    return acc[:, :HEAD_DIM] / acc[:, HEAD_DIM:HEAD_DIM + 1]


def _rows_means_kernel(x_ref, o_ref):
    nb, w = o_ref.shape
    o_ref[...] = jnp.sum(x_ref[...].reshape(nb, BLK, w), axis=1) * (1.0 / BLK)


def prompt_block_means(kv, batch, t_len):
    w = 2 * KV_GROUPS * HEAD_DIM
    nb = t_len // BLK
    return pl.pallas_call(
        _rows_means_kernel,
        grid=(batch,),
        in_specs=[pl.BlockSpec((t_len, w), lambda b: (b, 0))],
        out_specs=pl.BlockSpec((None, nb, w), lambda b: (b, 0, 0)),
        out_shape=jax.ShapeDtypeStruct((batch, nb, w), F32),
        compiler_params=_params(("parallel",)),
        name="cmp_means_prompt",
    )(kv)


def _page_specs(pps, half, layer):
    def spec(k):
        return pl.BlockSpec((None, None, None, BLK * 2, 2 * KV_GROUPS * HEAD_DIM),
                            lambda b, j, pt: (pt[b, j * pps + k], layer, 0, 0, half))
    return [spec(k) for k in range(pps)]


def _page_means_kernel(pt_ref, *refs):
    pages, o_ref = refs[:-1], refs[-1]
    w = o_ref.shape[-1]
    parts = [jnp.sum(pg[...].reshape(2, BLK, w), axis=1) * (1.0 / BLK) for pg in pages]
    o_ref[...] = jnp.concatenate(parts, axis=0)


def sample_block_means(cache5, page_table, layer, pps):
    batch, n_pages = page_table.shape
    n_steps = n_pages // pps
    w = 2 * KV_GROUPS * HEAD_DIM
    return pl.pallas_call(
        _page_means_kernel,
        grid_spec=pltpu.PrefetchScalarGridSpec(
            num_scalar_prefetch=1, grid=(batch, n_steps),
            in_specs=_page_specs(pps, 0, layer),
            out_specs=pl.BlockSpec((None, 2 * pps, w), lambda b, j, pt: (b, j, 0))),
        out_shape=jax.ShapeDtypeStruct((batch, 2 * n_pages, w), F32),
        compiler_params=_params(("parallel", "arbitrary")),
        name="cmp_means",
    )(page_table, *([cache5] * pps))


def _nsa_prompt_kernel(q_ref, kk_ref, vv_ref, kc_ref, gn_ref, tab_ref, band_ref, cst_ref, e_ref, o_ref,
                       s_ref, sel_ref, oc_ref, mrun_ref, acc_ref, *, nb):
    c = pl.program_id(1)
    qc = QCHUNK
    rows = GQA * qc
    kw = KV_GROUPS * HEAD_DIM
    span = SUPER * qc
    lane = lax.broadcasted_iota(jnp.int32, (rows, LANES), 1)
    qi = lax.broadcasted_iota(jnp.int32, (rows, LANES), 0) & (qc - 1)
    okc = (c * qc + qi - BLK * lane - (BLK - 1)) >= 0
    lane1 = lax.broadcasted_iota(jnp.int32, (qc, LANES), 1)
    qi1 = lax.broadcasted_iota(jnp.int32, (qc, LANES), 0)
    okc1 = (c * qc + qi1 - BLK * lane1 - (BLK - 1)) >= 0
    cur = 2 * c + (qi1 >= BLK).astype(jnp.int32)

    for g in range(KV_GROUPS):
        qg = _qgroup(q_ref, g)
        cst = cst_ref[g]
        kcg = kc_ref[:, g * HEAD_DIM:(g + 1) * HEAD_DIM].astype(BF16)
        vcg = kc_ref[:, kw + g * HEAD_DIM:kw + (g + 1) * HEAD_DIM].astype(BF16)
        shift = lax.rem(2 * c - 2 + LANES, LANES)
        band = pltpu.roll(band_ref[g], shift, axis=1)
        bias_c = jnp.where(lane < 2 * c - 2, cst, band)
        s = jnp.where(okc, _dot_nt(qg, kcg) + bias_c, NEG)
        e = jnp.exp(s - jnp.max(s, axis=-1, keepdims=True))
        p = jnp.where(okc, e / jnp.sum(e, axis=-1, keepdims=True), 0.0)
        oc_ref[g] = jnp.dot(p.astype(BF16), vcg, preferred_element_type=F32)
        psum = p[0:qc] + p[qc:2 * qc] + p[2 * qc:3 * qc] + p[3 * qc:4 * qc]
        imp = jnp.where(lane1 == cur, FORCE, jnp.where(okc1, psum, -1.0))
        imp = jnp.where(lane1 < nb, imp, -3e38)
        selm = jnp.where(lane1 <= cur, _select_blocks(imp, lane1), 0.0)
        sel_ref[g] = (jnp.dot(selm.astype(BF16), e_ref[...], preferred_element_type=F32) - 1.0) * (-NEG)

    sg = jax.nn.sigmoid(gn_ref[...])
    n_super = c // SUPER + 1
    w_tiles = WINDOW // qc + 1
    st0 = jnp.maximum(c - (w_tiles - 1), 0)
    for g in range(KV_GROUPS):
        qg = _qgroup(q_ref, g)
        mrun_ref[...] = jnp.full((rows, LANES), NEG, F32)

        def scores(st, carry):
            k = kk_ref[pl.ds(pl.multiple_of(st * span, span), span), g * HEAD_DIM:(g + 1) * HEAD_DIM]
            s = _dot_nt(qg, k)
            mrun = mrun_ref[...]
            for j in range(SUPER):
                kt = st * SUPER + j
                idx = jnp.where(kt == c, 0, jnp.where(kt == c - 1, 1, 2))
                off = pl.multiple_of(kt * qc, qc)
                sj = s[:, j * qc:(j + 1) * qc] + tab_ref[g, idx] + _rep4(sel_ref[g, :, pl.ds(off, qc)])
                s_ref[:, pl.ds(off, qc)] = sj
                mrun = jnp.maximum(mrun, sj)
            mrun_ref[...] = mrun
            return carry

        lax.fori_loop(0, n_super, scores, 0)
        m = jnp.max(mrun_ref[...], axis=-1, keepdims=True)
        mrun_ref[...] = jnp.zeros((rows, LANES), F32)

        def exps(st, carry):
            off = pl.multiple_of(st * span, span)
            e = jnp.exp(s_ref[:, pl.ds(off, span)] - m)
            s_ref[:, pl.ds(off, span)] = e
            lrun = mrun_ref[...]
            for j in range(SUPER):
                lrun = lrun + e[:, j * qc:(j + 1) * qc]
            mrun_ref[...] = lrun
            return carry

        lax.fori_loop(0, n_super, exps, 0)
        denom = jnp.sum(mrun_ref[...], axis=-1, keepdims=True)
        acc_ref[...] = jnp.zeros((rows, HEAD_DIM), F32)

        def values(st, carry):
            off = pl.multiple_of(st * span, span)
            p = (s_ref[:, pl.ds(off, span)] / denom).astype(BF16)
            acc_ref[...] += jnp.dot(p, vv_ref[pl.ds(off, span), g * HEAD_DIM:(g + 1) * HEAD_DIM], preferred_element_type=F32)
            return carry

        lax.fori_loop(0, n_super, values, 0)
        o_s = acc_ref[...]
        r0 = pl.multiple_of(st0 * qc, qc)
        s = _dot_nt(qg, kk_ref[pl.ds(r0, w_tiles * qc), kw + g * HEAD_DIM:kw + (g + 1) * HEAD_DIM])
        parts = []
        for w in range(w_tiles):
            kt = st0 + w
            idx = jnp.where(kt > c, 4, jnp.where(kt == c, 0, jnp.where(kt == c - 1, 1,
                                                                         jnp.where(kt == c - (w_tiles - 1), 3, 2))))
            parts.append(s[:, w * qc:(w + 1) * qc] + tab_ref[g, idx])
        sw = jnp.concatenate(parts, axis=1)
        ew = jnp.exp(sw - jnp.max(sw, axis=-1, keepdims=True))
        pw = (ew / jnp.sum(ew, axis=-1, keepdims=True)).astype(BF16)
        o_w = jnp.dot(pw, vv_ref[pl.ds(r0, w_tiles * qc), kw + g * HEAD_DIM:kw + (g + 1) * HEAD_DIM],
                      preferred_element_type=F32)
        tot = _gate_col(sg, 0, g) * oc_ref[g] + _gate_col(sg, 1, g) * o_s + _gate_col(sg, 2, g) * o_w
        for r in range(GQA):
            o_ref[:, (g * GQA + r) * HEAD_DIM:(g * GQA + r + 1) * HEAD_DIM] = tot[r * qc:(r + 1) * qc].astype(o_ref.dtype)


def nsa_prompt(q, kk, vv, kc, small, tabs, band, cst, batch, t_len):
    assert t_len % (SUPER * QCHUNK) == 0 and t_len // BLK <= LANES and t_len >= WINDOW + QCHUNK
    nch = t_len // QCHUNK
    nb = t_len // BLK
    rows = GQA * QCHUNK
    hd = N_HEADS * HEAD_DIM
    emat = (np.arange(LANES)[:, None] == (np.arange(t_len)[None, :] // BLK)).astype(np.float32)
    kern = functools.partial(_nsa_prompt_kernel, nb=nb)
    full = lambda shp: pl.BlockSpec(shp, lambda b, c: (0,) * len(shp))
    return pl.pallas_call(
        kern,
        grid=(batch, nch),
        in_specs=[pl.BlockSpec((QCHUNK, hd), lambda b, c: (b * nch + c, 0)),
                  pl.BlockSpec((t_len, kk.shape[1]), lambda b, c: (b, 0)),
                  pl.BlockSpec((t_len, vv.shape[1]), lambda b, c: (b, 0)),
                  pl.BlockSpec((None, LANES, kc.shape[2]), lambda b, c: (b, 0, 0)),
                  pl.BlockSpec((QCHUNK, LANES), lambda b, c: (b * nch + c, 0)),
                  full((KV_GROUPS, 5, rows, LANES)), full((KV_GROUPS, rows, LANES)), full((KV_GROUPS, rows, 1)),
                  full((LANES, t_len))],
        out_specs=pl.BlockSpec((QCHUNK, hd), lambda b, c: (b * nch + c, 0)),
        out_shape=jax.ShapeDtypeStruct((batch * t_len, hd), BF16),
        scratch_shapes=[pltpu.VMEM((rows, t_len), F32),
                        pltpu.VMEM((KV_GROUPS, QCHUNK, t_len), F32),
                        pltpu.VMEM((KV_GROUPS, rows, HEAD_DIM), F32),
                        pltpu.VMEM((rows, LANES), F32), pltpu.VMEM((rows, HEAD_DIM), F32)],
        compiler_params=_params(("parallel", "arbitrary")),
        name="nsa_prompt",
    )(q, kk, vv, kc, small, tabs, band, cst, jnp.asarray(emat, BF16))


def _nsa_sample_kernel(pt_ref, q_ref, kvn_ref, gn_ref, kc_ref, win_ref, near_ref, wtab_ref, cst_ref, e_ref,
                       *refs, pps, n_steps, past, ts, nbl):
    pages = refs[:pps]
    o_ref = refs[pps]
    s_ref, sel_ref, oc_ref, vbuf_ref, kn_ref = refs[pps + 1:]
    j = pl.program_id(1)
    rows = GQA * ts
    kw = KV_GROUPS * HEAD_DIM
    nb_past = past // BLK
    pg = 2 * BLK
    lane = lax.broadcasted_iota(jnp.int32, (rows, LANES), 1)
    ti = lax.broadcasted_iota(jnp.int32, (rows, LANES), 0) & (ts - 1)

    def qgroup(g):
        return _qgroup(q_ref, g).astype(BF16)

    @pl.when(j == 0)
    def _():
        kn_ref[...] = jnp.zeros(kn_ref.shape, F32)
        kn_ref[0:ts, :] = kvn_ref[:, 2 * kw:6 * kw]
        lane_b = lax.broadcasted_iota(jnp.int32, (rows, nbl), 1)
        t_b = lax.broadcasted_iota(jnp.int32, (rows, nbl), 0) & (ts - 1)
        okc = (past + t_b - BLK * lane_b - (BLK - 1)) >= 0
        lane1 = lax.broadcasted_iota(jnp.int32, (ts, nbl), 1)
        okc1 = (past + lax.broadcasted_iota(jnp.int32, (ts, nbl), 0) - BLK * lane1 - (BLK - 1)) >= 0
        for g in range(KV_GROUPS):
            qg = qgroup(g)
            cst = cst_ref[g]
            band = near_ref[g, 2]
            kcg = kc_ref[:, g * HEAD_DIM:(g + 1) * HEAD_DIM].astype(BF16)
            vcg = kc_ref[:, kw + g * HEAD_DIM:kw + (g + 1) * HEAD_DIM].astype(BF16)
            bias_c = jnp.where(lane_b == nb_past - 2, band[:, 0:1], jnp.where(lane_b == nb_past - 1, band[:, 1:2], cst))
            s = jnp.where(okc, _dot_nt(qg, kcg) + bias_c, NEG)
            e = jnp.exp(s - jnp.max(s, axis=-1, keepdims=True))
            p = jnp.where(okc, e / jnp.sum(e, axis=-1, keepdims=True), 0.0)
            oc_ref[g] = jnp.dot(p.astype(BF16), vcg, preferred_element_type=F32)
            psum = p[0:ts] + p[ts:2 * ts] + p[2 * ts:3 * ts] + p[3 * ts:4 * ts]
            imp = jnp.where(lane1 == nb_past, FORCE, jnp.where(okc1, psum, -1.0))
            imp = jnp.where(lane1 <= nb_past, imp, -3e38)
            selm = _select_blocks(imp, lane1)
            selm16 = jnp.concatenate([selm, jnp.zeros_like(selm)], axis=0).astype(BF16)
            sel_ref[g] = (jnp.dot(selm16, e_ref[...], preferred_element_type=F32)[0:ts] - 1.0) * (-NEG)

    is_last = j == n_steps - 1
    base = j * (pps * pg)
    qgs = [qgroup(g) for g in range(KV_GROUPS)]
    for k in range(pps):
        off = pl.multiple_of(base + k * pg, pg)
        vbuf_ref[pl.ds(off, pg), :] = pages[k][:, kw:2 * kw].astype(BF16)
        for g in range(KV_GROUPS):
            sk = _dot_nt(qgs[g], pages[k][:, g * HEAD_DIM:(g + 1) * HEAD_DIM].astype(BF16))
            bias = jnp.where(is_last, near_ref[g, 1], cst_ref[g]) if k == pps - 1 else cst_ref[g]
            s_ref[g, :, pl.ds(off, pg)] = sk + bias + _rep4(sel_ref[g, :, pl.ds(off, pg)])

    @pl.when(is_last)
    def _():
        sg = jax.nn.sigmoid(gn_ref[...])
        wl = win_ref.shape[0]
        lane_w = lax.broadcasted_iota(jnp.int32, (rows, wl + LANES), 1)
        t_w = lax.broadcasted_iota(jnp.int32, (rows, wl + LANES), 0) & (ts - 1)
        okw = jnp.where(lane_w < wl, lane_w - t_w, t_w - (lane_w - wl)) >= 0
        for g in range(KV_GROUPS):
            qg = qgroup(g)
            kn = kn_ref[:, g * HEAD_DIM:(g + 1) * HEAD_DIM].astype(BF16)
            vn = kn_ref[:, kw + g * HEAD_DIM:kw + (g + 1) * HEAD_DIM].astype(BF16)
            s_ref[g, :, past:past + LANES] = jnp.where(lane <= ti, _dot_nt(qg, kn) + near_ref[g, 0], NEG)
            s = s_ref[g]
            e = jnp.exp(s - jnp.max(s, axis=-1, keepdims=True))
            p = e / jnp.sum(e, axis=-1, keepdims=True)
            o_s = (jnp.dot(p[:, :past].astype(BF16), vbuf_ref[:, g * HEAD_DIM:(g + 1) * HEAD_DIM], preferred_element_type=F32)
                   + jnp.dot(p[:, past:].astype(BF16), vn, preferred_element_type=F32))
            kwin = win_ref[:, g * HEAD_DIM:(g + 1) * HEAD_DIM].astype(BF16)
            vwin = win_ref[:, kw + g * HEAD_DIM:kw + (g + 1) * HEAD_DIM].astype(BF16)
            knw = kn_ref[:, 2 * kw + g * HEAD_DIM:2 * kw + (g + 1) * HEAD_DIM].astype(BF16)
            vnw = kn_ref[:, 3 * kw + g * HEAD_DIM:3 * kw + (g + 1) * HEAD_DIM].astype(BF16)
            s = jnp.concatenate([_dot_nt(qg, kwin), _dot_nt(qg, knw)], axis=1) + wtab_ref[g]
            s = jnp.where(okw, s, NEG)
            e = jnp.exp(s - jnp.max(s, axis=-1, keepdims=True))
            p = e / jnp.sum(e, axis=-1, keepdims=True)
            o_w = (jnp.dot(p[:, :wl].astype(BF16), vwin, preferred_element_type=F32)
                   + jnp.dot(p[:, wl:].astype(BF16), vnw, preferred_element_type=F32))
            tot = _gate_col(sg, 0, g) * oc_ref[g] + _gate_col(sg, 1, g) * o_s + _gate_col(sg, 2, g) * o_w
            for r in range(GQA):
                o_ref[:, (g * GQA + r) * HEAD_DIM:(g * GQA + r + 1) * HEAD_DIM] = tot[r * ts:(r + 1) * ts]


def nsa_sample(q3, kvn3, small3, kcvc, win4, near_s, wtab, cst_s, cache5, page_table, layer, pps):
    batch, ts, hd = q3.shape
    n_pages = page_table.shape[1]
    n_steps = n_pages // pps
    past = n_pages * 2 * BLK
    nbl = kcvc.shape[1]
    wl = win4.shape[2]
    rows = GQA * ts
    kw = KV_GROUPS * HEAD_DIM
    assert ts == SUBLANES and past % QCHUNK == 0 and wl % LANES == 0
    emat = (np.arange(nbl)[:, None] == (np.arange(past)[None, :] // BLK)).astype(np.float32)
    kern = functools.partial(_nsa_sample_kernel, pps=pps, n_steps=n_steps, past=past, ts=ts, nbl=nbl)
    c3 = lambda shp: pl.BlockSpec(shp, lambda b, j, pt: (0,) * len(shp))
    return pl.pallas_call(
        kern,
        grid_spec=pltpu.PrefetchScalarGridSpec(
            num_scalar_prefetch=1, grid=(batch, n_steps),
            in_specs=[pl.BlockSpec((None, ts, hd), lambda b, j, pt: (b, 0, 0)),
                      pl.BlockSpec((None, ts, kvn3.shape[2]), lambda b, j, pt: (b, 0, 0)),
                      pl.BlockSpec((None, ts, LANES), lambda b, j, pt: (b, 0, 0)),
                      pl.BlockSpec((None, nbl, 2 * kw), lambda b, j, pt: (b, 0, 0)),
                      pl.BlockSpec((None, None, wl, 2 * kw), lambda b, j, pt: (layer, b, 0, 0)),
                      c3((KV_GROUPS, 3, rows, LANES)), c3((KV_GROUPS, rows, wl + LANES)), c3((KV_GROUPS, rows, 1)),
                      c3((nbl, past))] + _page_specs(pps, 1, layer),
            out_specs=pl.BlockSpec((None, ts, hd), lambda b, j, pt: (b, 0, 0)),
            scratch_shapes=[pltpu.VMEM((KV_GROUPS, rows, past + LANES), F32),
                            pltpu.VMEM((KV_GROUPS, ts, past), F32),
                            pltpu.VMEM((KV_GROUPS, rows, HEAD_DIM), F32),
                            pltpu.VMEM((past, kw), BF16),
                            pltpu.VMEM((LANES, 4 * kw), F32)]),
        out_shape=jax.ShapeDtypeStruct((batch, ts, hd), F32),
        compiler_params=_params(("parallel", "arbitrary")),
        name="nsa_sample",
    )(page_table, q3, kvn3, small3, kcvc, win4, near_s, wtab, cst_s, jnp.asarray(emat, BF16), *([cache5] * pps))


def _merge_kernel(c_ref, o_ref, y_ref, g0_ref, g1_ref, g2_ref, wa_ref, wn_ref, ws_ref, out_ref):
    oa = jnp.dot(c_ref[...], wa_ref[...], preferred_element_type=F32)
    ob = jnp.dot(o_ref[...], wn_ref[...], preferred_element_type=F32)
    oc = jnp.dot(y_ref[...], ws_ref[...], preferred_element_type=F32)
    mixed = (jax.nn.sigmoid(g0_ref[...]) * oa + jax.nn.sigmoid(g1_ref[...]) * ob + jax.nn.sigmoid(g2_ref[...]) * oc)
    out_ref[...] = mixed.astype(out_ref.dtype)


def merge_branches(c, o, y, gmix, wa, wn, ws):
    n = c.shape[0]
    d = wa.shape[1]
    tm = _tile(n, 768, BF16_ROWS)
    tn = _tile(d, 512, LANES)
    nj = d // tn
    act = lambda a: pl.BlockSpec((tm, a.shape[1]), lambda i, j: (i, 0))
    wsp = lambda w: pl.BlockSpec((w.shape[0], tn), lambda i, j: (0, j))
    gsp = lambda br: pl.BlockSpec((tm, tn), lambda i, j: (i, br * nj + j))
    return pl.pallas_call(
        _merge_kernel,
        grid=(n // tm, nj),
        in_specs=[act(c), act(o), act(y), gsp(0), gsp(1), gsp(2), wsp(wa), wsp(wn), wsp(ws)],
        out_specs=pl.BlockSpec((tm, tn), lambda i, j: (i, j)),
        out_shape=jax.ShapeDtypeStruct((n, d), BF16),
        compiler_params=_params(("parallel", "parallel")),
        name="merge",
    )(c, o, y, gmix, gmix, gmix, wa, wn, ws)


def _outproj_kernel(a_ref, w_ref, x_ref, o_ref):
    o_ref[...] = x_ref[...] + jnp.dot(a_ref[...], w_ref[...], preferred_element_type=F32)


def out_projection(a, w, x):
    n, k = a.shape
    d = w.shape[1]
    tm = _tile(n, 768, BF16_ROWS)
    tn = _tile(d, 1024, LANES)
    return pl.pallas_call(
        _outproj_kernel,
        grid=(n // tm, d // tn),
        in_specs=[pl.BlockSpec((tm, k), lambda i, j: (i, 0)), pl.BlockSpec((k, tn), lambda i, j: (0, j)),
                  pl.BlockSpec((tm, tn), lambda i, j: (i, j))],
        out_specs=pl.BlockSpec((tm, tn), lambda i, j: (i, j)),
        out_shape=jax.ShapeDtypeStruct((n, d), F32),
        compiler_params=_params(("parallel", "parallel")),
        name="out_proj",
    )(a, w, x)


def _router_kernel(x_ref, g_ref, w_ref, b_ref, h_ref, r_ref):
    x = x_ref[...]
    ms = jnp.mean(x * x, axis=-1, keepdims=True)
    h = (x * lax.rsqrt(ms + EPS) * g_ref[...]).astype(BF16)
    h_ref[...] = h
    logits = jnp.dot(h, w_ref[...], preferred_element_type=F32) + b_ref[...]
    lane = lax.broadcasted_iota(jnp.int32, logits.shape, 1)
    big = jnp.int32(1 << 30)
    gl = jnp.where(lane < MOE_GROUPS, logits, NEG)
    gmax = jnp.max(gl, axis=-1, keepdims=True)
    gidx = jnp.min(jnp.where(gl == gmax, lane, big), axis=-1, keepdims=True)
    g_w = 1.0 / jnp.sum(jnp.exp(gl - gmax), axis=-1, keepdims=True)
    eid = lane - MOE_GROUPS
    in_grp = (eid >= 0) & (eid < MOE_GROUPS * MOE_EPG) & ((eid >> 3) == gidx)
    el = jnp.where(in_grp, logits, NEG)
    m1 = jnp.max(el, axis=-1, keepdims=True)
    i1 = jnp.min(jnp.where(el == m1, lane, big), axis=-1, keepdims=True)
    el2 = jnp.where(lane == i1, NEG, el)
    m2 = jnp.max(el2, axis=-1, keepdims=True)
    i2 = jnp.min(jnp.where(el2 == m2, lane, big), axis=-1, keepdims=True)
    p2 = jnp.exp(m2 - m1)
    w1 = g_w / (1.0 + p2)
    w2 = g_w * p2 / (1.0 + p2)
    r_ref[...] = jnp.where(lane == 0, (i1 - MOE_GROUPS).astype(F32),
                           jnp.where(lane == 1, (i2 - MOE_GROUPS).astype(F32),
                                     jnp.where(lane == 2, w1, jnp.where(lane == 3, w2, 0.0))))


def moe_router(x, g, wr, br):
    n, d = x.shape
    tm = _tile(n, 256, BF16_ROWS)
    return pl.pallas_call(
        _router_kernel,
        grid=(n // tm,),
        in_specs=[pl.BlockSpec((tm, d), lambda i: (i, 0)), pl.BlockSpec((1, d), lambda i: (0, 0)),
                  pl.BlockSpec((d, LANES), lambda i: (0, 0)), pl.BlockSpec((1, LANES), lambda i: (0, 0))],
        out_specs=[pl.BlockSpec((tm, d), lambda i: (i, 0)), pl.BlockSpec((tm, LANES), lambda i: (i, 0))],
        out_shape=[jax.ShapeDtypeStruct((n, d), BF16), jax.ShapeDtypeStruct((n, LANES), F32)],
        compiler_params=_params(("parallel",)),
        name="moe_router",
    )(x, g.reshape(1, d), wr, br)


def _expert_kernel(te_ref, nu_ref, x_ref, rw_ref, wg_ref, wu_ref, wd_ref, y_ref, wgb_ref, wub_ref, wdb_ref):
    t = pl.program_id(0)
    fresh = jnp.logical_or(t == 0, te_ref[t] != te_ref[jnp.maximum(t - 1, 0)])

    @pl.when(jnp.logical_and(fresh, t < nu_ref[0]))
    def _():
        wgb_ref[...] = wg_ref[...].astype(BF16)
        wub_ref[...] = wu_ref[...].astype(BF16)
        wdb_ref[...] = wd_ref[...].astype(BF16)

    @pl.when(t < nu_ref[0])
    def _():
        x = x_ref[...]
        a = jnp.dot(x, wgb_ref[...], preferred_element_type=F32)
        u = jnp.dot(x, wub_ref[...], preferred_element_type=F32)
        hid = (_silu(a) * u * rw_ref[...]).astype(BF16)
        y_ref[...] = jnp.dot(hid, wdb_ref[...], preferred_element_type=F32)


def moe_experts(xs, row_w, tile_e, n_used, wg, wu, wd, layer, tme):
    r, d = xs.shape
    ff = wg.shape[3]
    nt = r // tme
    return pl.pallas_call(
        _expert_kernel,
        grid_spec=pltpu.PrefetchScalarGridSpec(
            num_scalar_prefetch=2, grid=(nt,),
            in_specs=[pl.BlockSpec((tme, d), lambda t, te, nu: (t, 0)),
                      pl.BlockSpec((tme, 1), lambda t, te, nu: (t, 0)),
                      pl.BlockSpec((None, None, d, ff), lambda t, te, nu: (layer, te[t], 0, 0)),
                      pl.BlockSpec((None, None, d, ff), lambda t, te, nu: (layer, te[t], 0, 0)),
                      pl.BlockSpec((None, None, ff, d), lambda t, te, nu: (layer, te[t], 0, 0))],
            out_specs=pl.BlockSpec((tme, d), lambda t, te, nu: (t, 0)),
            scratch_shapes=[pltpu.VMEM((d, ff), BF16), pltpu.VMEM((d, ff), BF16), pltpu.VMEM((ff, d), BF16)]),
        out_shape=jax.ShapeDtypeStruct((r, d), F32),
        compiler_params=_params(("arbitrary",)),
        name="moe_experts",
    )(tile_e, n_used, xs, row_w, wg, wu, wd)


def moe_layer(x, g, wrg, brg, wre, bre, wg, wu, wd, layer):
    n, d = x.shape
    n_exp = wg.shape[1]
    wr = jnp.concatenate([wrg, wre.transpose(1, 0, 2).reshape(d, n_exp)], axis=1)
    wr = jnp.pad(wr, ((0, 0), (0, LANES - wr.shape[1])))
    br = jnp.pad(jnp.concatenate([brg, bre.reshape(-1)]), (0, LANES - MOE_GROUPS - n_exp)).reshape(1, LANES)
    h, route = moe_router(x, g, wr.astype(BF16), br)
    tme = 256
    e_flat = route[:, 0:MOE_TOPK].astype(jnp.int32).T.reshape(-1)
    w_flat = route[:, MOE_TOPK:2 * MOE_TOPK].T.reshape(-1)
    na = e_flat.shape[0]
    nt = -(-na // tme) + n_exp
    order = jnp.argsort(e_flat, stable=True)
    counts = jnp.sum(e_flat[:, None] == jnp.arange(n_exp)[None, :], axis=0).astype(jnp.int32)
    pc = ((counts + tme - 1) // tme) * tme
    pend = jnp.cumsum(pc)
    pstart = pend - pc
    ustart = jnp.cumsum(counts) - counts
    se = e_flat[order]
    dest = pstart[se] + jnp.arange(na, dtype=jnp.int32) - ustart[se]
    row_tok = jnp.zeros((nt * tme,), jnp.int32).at[dest].set((order % n).astype(jnp.int32))
    row_w = jnp.zeros((nt * tme,), F32).at[dest].set(w_flat[order])
    dest_flat = jnp.zeros((na,), jnp.int32).at[order].set(dest)
    tile_e = jnp.minimum(jnp.searchsorted(pend, jnp.arange(nt, dtype=jnp.int32) * tme, side="right"), n_exp - 1).astype(jnp.int32)
    n_used = (pend[-1] // tme).astype(jnp.int32).reshape(1)
    xs = jnp.take(h, row_tok, axis=0)
    ys = moe_experts(xs, row_w.reshape(-1, 1), tile_e, n_used, wg, wu, wd, layer, tme)
    return x + jnp.take(ys, dest_flat[:n], axis=0) + jnp.take(ys, dest_flat[n:], axis=0)


def kernel(x_prompt, x_sample, cache_nsa_kv, cache_nsa_win, state_conv_a, state_conv_ssm, state_ssm, page_table,
           rel_bias, norm_mix_g, w_in, conv_a_w, conv_a_b, ln_a_g, ln_a_b, w_a_out, w_nsa_out, conv_s_w, conv_s_b,
           dt_bias, a_log, d_skip, ssm_norm_g, w_ssm_out, w_out, norm_ffn_g, w_router_group, b_router_group,
           w_router_expert, b_router_expert, w_gate, w_up, w_down, final_norm_g):
    bp, tp, d = x_prompt.shape
    bs, ts, _ = x_sample.shape
    depth = w_in.shape[0]
    n_p, n_s = bp * tp, bs * ts
    ch = conv_a_w.shape[2]
    conv_dim = conv_s_w.shape[2]
    heads_ssm = dt_bias.shape[1]
    inner = heads_ssm * SSM_HEAD_DIM
    hd = N_HEADS * HEAD_DIM
    kw = KV_GROUPS * HEAD_DIM
    kvw = KV_KINDS * kw
    n_pages = page_table.shape[1]
    page = cache_nsa_kv.shape[2]
    past = n_pages * page
    wl = cache_nsa_win.shape[2]
    assert page == 2 * BLK and wl == min(WINDOW, past) and tp >= WINDOW
    pps = _tile(n_pages, 16, 1)
    rows_p = GQA * QCHUNK

    splits = (2 * ch, hd, kvw, 3 * N_HEADS, inner, conv_dim, heads_ssm, 3 * d)
    offs = np.concatenate([[0], np.cumsum(splits)])
    o_u, o_q, o_kv, o_gn, o_z, o_x, o_dt, o_gm = [int(v) for v in offs[:-1]]

    near_d = _near_distances()
    tj = np.arange(wl + LANES)[None, :]
    win_d = np.where(tj < wl + ts, wl + np.arange(ts)[:, None] - tj, 0)
    near_f = expand_bias(rel_bias, near_d)
    wtab_f = expand_bias(rel_bias, win_d)
    near_g = near_f.reshape(KV_GROUPS, GQA, 3, QCHUNK, LANES).transpose(0, 2, 1, 3, 4)
    near_p = near_g.reshape(KV_GROUPS, 3, rows_p, LANES)
    near_s = near_g[:, :, :, :ts].reshape(KV_GROUPS, 3, GQA * ts, LANES)
    wtab = wtab_f.reshape(KV_GROUPS, GQA * ts, wl + LANES)
    far = rel_bias[REL_BUCKETS - 1].reshape(KV_GROUPS, GQA, 1)
    cst_p = jnp.repeat(far, QCHUNK, axis=1).reshape(KV_GROUPS, rows_p, 1)
    cst_s = jnp.repeat(far, ts, axis=1).reshape(KV_GROUPS, GQA * ts, 1)
    ii = (np.arange(rows_p) % QCHUNK)[:, None]
    jj = np.arange(LANES)[None, :]
    cst_t = jnp.broadcast_to(cst_p, (KV_GROUPS, rows_p, LANES))
    tabs_p = jnp.stack([jnp.where(jj <= ii, near_p[:, 0], NEG), near_p[:, 1], cst_t,
                        jnp.where(jj >= ii, cst_t, NEG), jnp.full((KV_GROUPS, rows_p, LANES), NEG, F32)], axis=1)

    cache5 = cache_nsa_kv.reshape(cache_nsa_kv.shape[0], depth, 1, page, 4 * kw)
    win4 = cache_nsa_win.reshape(depth, bs, wl, 2 * kw)
    zeros_hist_a = jnp.zeros((bp, HIST_A, ch), F32)
    zeros_hist_s = jnp.zeros((bp, HIST_S, conv_dim), F32)
    zeros_ssm = jnp.zeros((bp, heads_ssm, SSM_HEAD_DIM, SSM_STATE), F32)
    nbl = -(-(past // BLK + 1) // LANES) * LANES

    x = jnp.concatenate([x_prompt.reshape(n_p, d), x_sample.reshape(n_s, d)], axis=0)
    kv_p, kv_s, win_p, win_s, ha_p, ha_s, hs_p, hs_s, st_p, st_s = ([] for _ in range(10))
    for l in range(depth):
        wl_in = w_in[l]
        seg = lambda a, b: wl_in[:, a:b].astype(BF16)
        h = rmsnorm(x, norm_mix_g[l], BF16)
        (u_in,) = matmul(h, seg(o_u, o_q), [F32])
        (q,) = matmul(h, (wl_in[:, o_q:o_kv] * (HEAD_DIM ** -0.5)).astype(BF16), [BF16])
        kv, kvb = matmul(h, seg(o_kv, o_gn), [F32, BF16])
        w_small = jnp.concatenate([wl_in[:, o_dt:o_gm], wl_in[:, o_gn:o_z],
                                   jnp.zeros((d, LANES - heads_ssm - 3 * N_HEADS), F32)], axis=1).astype(BF16)
        (small,) = matmul(h, w_small, [F32])
        (z,) = matmul(h, seg(o_z, o_x), [F32])
        (xbc,) = matmul(h, seg(o_x, o_dt), [F32])
        (gmix,) = matmul(h, seg(o_gm, o_gm + 3 * d), [F32])

        hist_a = jnp.pad(state_conv_a[l], ((0, 0), (HIST_A - (CONV_A_W - 1), 0), (0, 0)))
        ca_args = (conv_a_w[l], conv_a_b[l], ln_a_g[l], ln_a_b[l])
        c_p, ha_new_p = conv_a_branch(u_in, 0, zeros_hist_a, *ca_args, bp, tp, BF16)
        c_s, ha_new_s = conv_a_branch(u_in, n_p, hist_a, *ca_args, bs, ts, F32)
        hist_s = jnp.pad(state_conv_ssm[l], ((0, 0), (HIST_S - (SSM_CONV_W - 1), 0), (0, 0)))
        ss_args = (conv_s_w[l], conv_s_b[l], dt_bias[l], a_log[l], d_skip[l], ssm_norm_g[l])
        small_tp = small[:n_p].reshape(bp, tp, LANES).transpose(0, 2, 1)
        small_ts = small[n_p:].reshape(bs, ts, LANES).transpose(0, 2, 1)
        y_p, hs_new_p, st_new_p = ssd_branch(xbc, z, small, small_tp, 0, zeros_hist_s, zeros_ssm, *ss_args, bp, tp, BF16)
        y_s, hs_new_s, st_new_s = ssd_branch(xbc, z, small, small_ts, n_p, hist_s, state_ssm[l], *ss_args, bs, ts, F32)
        kc_p = prompt_block_means(kv, bp, tp)
        kc_p = jnp.pad(kc_p, ((0, 0), (0, LANES - kc_p.shape[1]), (0, 0)))
        kk = jnp.concatenate([kvb[:n_p, 2 * kw:3 * kw], kvb[:n_p, 4 * kw:5 * kw]], axis=1)
        vv = jnp.concatenate([kvb[:n_p, 3 * kw:4 * kw], kvb[:n_p, 5 * kw:6 * kw]], axis=1)
        o_p = nsa_prompt(q, kk, vv, kc_p, small, tabs_p, near_p[:, 2], cst_p, bp, tp)
        means = sample_block_means(cache5, page_table, l, pps)
        kcvc = jnp.pad(means, ((0, 0), (0, nbl - means.shape[1]), (0, 0)))
        o_s = nsa_sample(q[n_p:].astype(F32).reshape(bs, ts, hd), kv[n_p:].reshape(bs, ts, kvw),
                         small[n_p:].reshape(bs, ts, LANES), kcvc, win4, near_s, wtab, cst_s, cache5, page_table, l, pps)

        c_all = jnp.concatenate([c_p, c_s.astype(BF16)], axis=0)
        o_all = jnp.concatenate([o_p, o_s.reshape(n_s, hd).astype(BF16)], axis=0)
        y_all = jnp.concatenate([y_p, y_s.astype(BF16)], axis=0)
        mixed = merge_branches(c_all, o_all, y_all, gmix, w_a_out[l].astype(BF16), w_nsa_out[l].astype(BF16),
                               w_ssm_out[l].astype(BF16))
        x = out_projection(mixed, w_out[l].astype(BF16), x)
        x = moe_layer(x, norm_ffn_g[l], w_router_group[l], b_router_group[l], w_router_expert[l], b_router_expert[l],
                      w_gate, w_up, w_down, l)

        kvr = kv[:, :4 * kw]
        kv_p.append(kvr[:n_p].reshape(bp, tp, 4, KV_GROUPS, HEAD_DIM))
        kv_s.append(kvr[n_p:].reshape(bs, ts, 4, KV_GROUPS, HEAD_DIM))
        wr = kv[:, 4 * kw:]
        win_p.append(wr[:n_p].reshape(bp, tp, 2, KV_GROUPS, HEAD_DIM)[:, tp - WINDOW:])
        win_all = jnp.concatenate([cache_nsa_win[l], wr[n_p:].reshape(bs, ts, 2, KV_GROUPS, HEAD_DIM)], axis=1)
        keep = min(WINDOW, wl + ts)
        win_s.append(win_all[:, wl + ts - keep:])
        ha_p.append(ha_new_p[:, HIST_A - (CONV_A_W - 1):])
        ha_s.append(ha_new_s[:, HIST_A - (CONV_A_W - 1):])
        hs_p.append(hs_new_p[:, HIST_S - (SSM_CONV_W - 1):])
        hs_s.append(hs_new_s[:, HIST_S - (SSM_CONV_W - 1):])
        st_p.append(st_new_p)
        st_s.append(st_new_s)

    y = rmsnorm(x, final_norm_g, F32)
    return (y[:n_p].reshape(bp, tp, d), y[n_p:].reshape(bs, ts, d),
            jnp.stack(kv_p, axis=1), jnp.stack(kv_s, axis=1), jnp.stack(win_p), jnp.stack(win_s),
            jnp.stack(ha_p), jnp.stack(ha_s), jnp.stack(hs_p), jnp.stack(hs_s), jnp.stack(st_p), jnp.stack(st_s))
```

```python
import functools
import math

import numpy as np
import jax
import jax.numpy as jnp
from jax import lax
from jax.experimental import pallas as pl
from jax.experimental.pallas import tpu as pltpu

HEAD_DIM = 64
KV_GROUPS = 4
GQA = 4
N_HEADS = KV_GROUPS * GQA
KV_KINDS = 6
BLK = 64
TOPK = 8
WINDOW = 512
QCHUNK = 128
REL_BUCKETS = 32
REL_MAX_DIST = 128
CONV_A_W = 31
SSM_CONV_W = 4
SSM_HEAD_DIM = 64
SSM_GROUPS = 4
SSM_STATE = 128
SSM_CHUNK = 128
MOE_GROUPS = 4
MOE_EPG = 8
MOE_TOPK = 2
EPS = 1e-6
NEG = -1e30
FORCE = 1e4

LANES = 128
SUBLANES = 8
BF16_ROWS = 16
VMEM_LIMIT = 56 * 1024 * 1024

DT_COLS = 32
GATE_COL0 = DT_COLS
SUPER = 4

HIGHEST = lax.Precision.HIGHEST
F32 = jnp.float32
BF16 = jnp.bfloat16


def _params(sem, limit=VMEM_LIMIT):
    return pltpu.CompilerParams(dimension_semantics=sem, vmem_limit_bytes=limit)


def _tile(n, target, mult):
    best = None
    for t in range(mult, min(n, target) + 1, mult):
        if n % t == 0:
            best = t
    return best if best is not None else n


def _dot_nt(a, b):
    return lax.dot_general(a, b, (((1,), (1,)), ((), ())), preferred_element_type=F32)


def _dot_tn(a, b):
    return lax.dot_general(a, b, (((0,), (0,)), ((), ())), preferred_element_type=F32)


def _silu(x):
    return x * jax.nn.sigmoid(x)


def _softplus(x):
    return jnp.maximum(x, 0.0) + jnp.log(1.0 + jnp.exp(-jnp.abs(x)))


def _rms_kernel(x_ref, g_ref, o_ref):
    x = x_ref[...]
    ms = jnp.mean(x * x, axis=-1, keepdims=True)
    o_ref[...] = (x * lax.rsqrt(ms + EPS) * g_ref[...]).astype(o_ref.dtype)


def rmsnorm(x, g, out_dtype):
    n, d = x.shape
    tm = _tile(n, 1024, BF16_ROWS)
    return pl.pallas_call(
        _rms_kernel,
        grid=(n // tm,),
        in_specs=[pl.BlockSpec((tm, d), lambda i: (i, 0)), pl.BlockSpec((1, d), lambda i: (0, 0))],
        out_specs=pl.BlockSpec((tm, d), lambda i: (i, 0)),
        out_shape=jax.ShapeDtypeStruct((n, d), out_dtype),
        compiler_params=_params(("parallel",)),
        name="rmsnorm",
    )(x, g.reshape(1, d))


def _mm_kernel(a_ref, w_ref, *o_refs):
    acc = _dot_nt(a_ref[...], w_ref[...])
    for o in o_refs:
        o[...] = acc.astype(o.dtype)


def matmul(a, wt, out_dtypes):
    n, k = a.shape
    c = wt.shape[0]
    tm = _tile(n, 768, BF16_ROWS)
    tn = _tile(c, 1024, LANES)
    outs = pl.pallas_call(
        _mm_kernel,
        grid=(n // tm, c // tn),
        in_specs=[pl.BlockSpec((tm, k), lambda i, j: (i, 0)), pl.BlockSpec((tn, k), lambda i, j: (j, 0))],
        out_specs=[pl.BlockSpec((tm, tn), lambda i, j: (i, j)) for _ in out_dtypes],
        out_shape=[jax.ShapeDtypeStruct((n, c), dt) for dt in out_dtypes],
        compiler_params=_params(("parallel", "parallel")),
        name="proj",
    )(a, wt)
    return outs


HIST_A = 32
HIST_S = 8


def _round_bf16(x):
    return x.astype(BF16).astype(F32)


def _push_tail(tail_ref, new, n_new, keep):
    if n_new >= keep:
        tail_ref[...] = new[n_new - keep:n_new]
    else:
        tail_ref[...] = jnp.concatenate([tail_ref[n_new:keep], new], axis=0)


def _conva_kernel(u_ref, hist_ref, w_ref, b_ref, lg_ref, lb_ref, c_ref, hout_ref, ext_ref, conv_ref, tail_ref, *, tt, ch):
    t = pl.program_id(1)

    @pl.when(t == 0)
    def _():
        tail_ref[...] = hist_ref[...]
        ext_ref[0:HIST_A, :] = _round_bf16(hist_ref[...])

    uin = u_ref[...]
    u = uin[:, :ch] * jax.nn.sigmoid(uin[:, ch:])
    ext_ref[HIST_A:HIST_A + tt, :] = _round_bf16(u)
    _push_tail(tail_ref, u, tt, HIST_A)
    hout_ref[...] = tail_ref[...]
    off = HIST_A - (CONV_A_W - 1)
    rb = min(tt, 128)

    def lane_block(cb, carry):
        lo = pl.multiple_of(cb * LANES, LANES)
        for r in range(tt // rb):
            acc = jnp.broadcast_to(b_ref[:, pl.ds(lo, LANES)], (rb, LANES))
            for k in range(CONV_A_W):
                acc = acc + w_ref[k:k + 1, pl.ds(lo, LANES)] * ext_ref[r * rb + off + k:r * rb + off + k + rb, pl.ds(lo, LANES)]
            conv_ref[r * rb:(r + 1) * rb, pl.ds(lo, LANES)] = acc
        return carry

    lax.fori_loop(0, ch // LANES, lane_block, 0)
    cv = conv_ref[...]
    mu = jnp.mean(cv, axis=-1, keepdims=True)
    xc = cv - mu
    var = jnp.mean(xc * xc, axis=-1, keepdims=True)
    y = xc * lax.rsqrt(var + EPS) * lg_ref[...] + lb_ref[...]
    c_ref[...] = _silu(y).astype(c_ref.dtype)
    ext_ref[0:HIST_A, :] = ext_ref[tt:tt + HIST_A, :]


def conv_a_branch(u_in, row_off, hist, w, b, lg, lb, batch, t_len, out_dtype):
    ch = w.shape[1]
    tt = _tile(t_len, 256, SUBLANES)
    nt = t_len // tt
    assert row_off % tt == 0
    r0 = row_off // tt
    kern = functools.partial(_conva_kernel, tt=tt, ch=ch)
    vec = lambda: pl.BlockSpec((1, ch), lambda bi, ti: (0, 0))
    return pl.pallas_call(
        kern,
        grid=(batch, nt),
        in_specs=[pl.BlockSpec((tt, 2 * ch), lambda bi, ti: (r0 + bi * nt + ti, 0)),
                  pl.BlockSpec((None, HIST_A, ch), lambda bi, ti: (bi, 0, 0)),
                  pl.BlockSpec((CONV_A_W, ch), lambda bi, ti: (0, 0)), vec(), vec(), vec()],
        out_specs=[pl.BlockSpec((tt, ch), lambda bi, ti: (bi * nt + ti, 0)),
                   pl.BlockSpec((None, HIST_A, ch), lambda bi, ti: (bi, 0, 0))],
        out_shape=[jax.ShapeDtypeStruct((batch * t_len, ch), out_dtype),
                   jax.ShapeDtypeStruct((batch, HIST_A, ch), F32)],
        scratch_shapes=[pltpu.VMEM((tt + HIST_A, ch), F32), pltpu.VMEM((tt, ch), F32), pltpu.VMEM((HIST_A, ch), F32)],
        compiler_params=_params(("parallel", "arbitrary")),
        name="conv_a",
    )(u_in, hist, _round_bf16(w), b.reshape(1, ch), lg.reshape(1, ch), lb.reshape(1, ch))


def _ssd_kernel(xbc_ref, z_ref, dt_ref, dtt_ref, hist_ref, h0_ref, cw_ref, cb_ref, dtb_ref, dtbt_ref,
                a_ref, at_ref, dvec_ref, gn_ref, y_ref, hout_ref, sout_ref, ext_ref, h_ref, yacc_ref, tail_ref,
                *, q, nc, inner, heads):
    c = pl.program_id(1)

    @pl.when(c == 0)
    def _():
        tail_ref[...] = hist_ref[...]
        ext_ref[0:HIST_S, :] = _round_bf16(hist_ref[...])
        h_ref[...] = h0_ref[...]

    xbc = xbc_ref[...]
    ext_ref[HIST_S:HIST_S + q, :] = _round_bf16(xbc)
    _push_tail(tail_ref, xbc, q, HIST_S)
    hout_ref[...] = tail_ref[...]
    off = HIST_S - (SSM_CONV_W - 1)
    acc = jnp.broadcast_to(cb_ref[...], (q, cb_ref.shape[1]))
    for k in range(SSM_CONV_W):
        acc = acc + cw_ref[k:k + 1, :] * ext_ref[off + k:off + k + q, :]
    xc = _silu(acc)
    ext_ref[0:HIST_S, :] = ext_ref[q:q + HIST_S, :]

    gw = SSM_GROUPS * SSM_STATE
    xs = xc[:, :inner]
    bm = xc[:, inner:inner + gw]
    cm = xc[:, inner + gw:inner + 2 * gw]

    dt = _softplus(dt_ref[...] + dtb_ref[...])
    dtt = _softplus(dtt_ref[...] + dtbt_ref[...])
    da = dt * a_ref[...]
    dat = dtt * at_ref[...]
    ri = lax.broadcasted_iota(jnp.int32, (q, q), 0)
    ci = lax.broadcasted_iota(jnp.int32, (q, q), 1)
    causal = ri >= ci
    tri = causal.astype(F32)
    trit = (ri <= ci).astype(F32)
    acum = jnp.dot(tri, da, precision=HIGHEST, preferred_element_type=F32)
    acumt = jnp.dot(dat, trit, precision=HIGHEST, preferred_element_type=F32)
    alast = acum[q - 1:q, :]
    hpg = heads // SSM_GROUPS
    for g in range(SSM_GROUPS):
        bg = bm[:, g * SSM_STATE:(g + 1) * SSM_STATE].astype(BF16)
        cg = cm[:, g * SSM_STATE:(g + 1) * SSM_STATE].astype(BF16)
        cbm = _dot_nt(cg, bg)
        for e in range(hpg):
            hh = g * hpg + e
            a_col = acum[:, hh:hh + 1]
            a_row = acumt[hh:hh + 1, :]
            decay = jnp.exp(jnp.where(causal, a_col - a_row, NEG))
            wm = cbm * decay * dtt[hh:hh + 1, :]
            xh = xs[:, hh * SSM_HEAD_DIM:(hh + 1) * SSM_HEAD_DIM]
            hprev = h_ref[hh]
            y_h = (jnp.dot(wm.astype(BF16), xh.astype(BF16), preferred_element_type=F32)
                   + _dot_nt(cg, hprev.astype(BF16)) * jnp.exp(a_col))
            yacc_ref[:, hh * SSM_HEAD_DIM:(hh + 1) * SSM_HEAD_DIM] = y_h
            al = alast[:, hh:hh + 1]
            w_end = jnp.exp(al - a_col) * dt[:, hh:hh + 1]
            h_ref[hh] = jnp.exp(al) * hprev + _dot_tn((xh * w_end).astype(BF16), bg)

    y = (yacc_ref[...] + xs * dvec_ref[...]) * _silu(z_ref[...])
    gsz = inner // SSM_GROUPS
    for g in range(SSM_GROUPS):
        v = y[:, g * gsz:(g + 1) * gsz]
        ms = jnp.mean(v * v, axis=-1, keepdims=True)
        y_ref[:, g * gsz:(g + 1) * gsz] = (v * lax.rsqrt(ms + EPS) * gn_ref[:, g * gsz:(g + 1) * gsz]).astype(y_ref.dtype)

    @pl.when(c == nc - 1)
    def _():
        sout_ref[...] = h_ref[...]


def ssd_branch(xbc, z, small, small_t, row_off, hist, h0, cw, cb, dt_bias, a_log, d_skip, norm_g, batch, t_len, out_dtype):
    conv_dim = xbc.shape[1]
    inner = z.shape[1]
    heads = dt_bias.shape[0]
    assert heads == DT_COLS
    q = math.gcd(t_len, SSM_CHUNK)
    nc = t_len // q
    assert row_off % q == 0
    r0 = row_off // q
    pad = LANES - heads
    dtb = jnp.pad(dt_bias, (0, pad)).reshape(1, LANES)
    a_neg = jnp.pad(-jnp.exp(a_log), (0, pad)).reshape(1, LANES)
    dvec = jnp.repeat(d_skip, SSM_HEAD_DIM).reshape(1, inner)
    kern = functools.partial(_ssd_kernel, q=q, nc=nc, inner=inner, heads=heads)
    row = lambda w: pl.BlockSpec((1, w), lambda bi, ci: (0, 0))
    col = lambda: pl.BlockSpec((LANES, 1), lambda bi, ci: (0, 0))
    tok = lambda w: pl.BlockSpec((q, w), lambda bi, ci: (r0 + bi * nc + ci, 0))
    return pl.pallas_call(
        kern,
        grid=(batch, nc),
        in_specs=[tok(conv_dim), tok(inner), tok(LANES),
                  pl.BlockSpec((None, LANES, q), lambda bi, ci: (bi, 0, ci)),
                  pl.BlockSpec((None, HIST_S, conv_dim), lambda bi, ci: (bi, 0, 0)),
                  pl.BlockSpec((None, heads, SSM_HEAD_DIM, SSM_STATE), lambda bi, ci: (bi, 0, 0, 0)),
                  pl.BlockSpec((SSM_CONV_W, conv_dim), lambda bi, ci: (0, 0)), row(conv_dim),
                  row(LANES), col(), row(LANES), col(), row(inner), row(inner)],
        out_specs=[pl.BlockSpec((q, inner), lambda bi, ci: (bi * nc + ci, 0)),
                   pl.BlockSpec((None, HIST_S, conv_dim), lambda bi, ci: (bi, 0, 0)),
                   pl.BlockSpec((None, heads, SSM_HEAD_DIM, SSM_STATE), lambda bi, ci: (bi, 0, 0, 0))],
        out_shape=[jax.ShapeDtypeStruct((batch * t_len, inner), out_dtype),
                   jax.ShapeDtypeStruct((batch, HIST_S, conv_dim), F32),
                   jax.ShapeDtypeStruct((batch, heads, SSM_HEAD_DIM, SSM_STATE), F32)],
        scratch_shapes=[pltpu.VMEM((q + HIST_S, conv_dim), F32),
                        pltpu.VMEM((heads, SSM_HEAD_DIM, SSM_STATE), F32),
                        pltpu.VMEM((q, inner), F32), pltpu.VMEM((HIST_S, conv_dim), F32)],
        compiler_params=_params(("parallel", "arbitrary")),
        name="ssd",
    )(xbc, z, small, small_t, hist, h0, _round_bf16(cw), cb.reshape(1, conv_dim), dtb, dtb.reshape(LANES, 1),
      a_neg, a_neg.reshape(LANES, 1), dvec, norm_g.reshape(1, inner))


def _bucket_np(d):
    n = np.maximum(d, 0)
    exact = REL_BUCKETS // 2
    nf = np.maximum(n, 1).astype(np.float32)
    large = exact + (np.log(nf / np.float32(exact)) / np.float32(math.log(REL_MAX_DIST / exact))
                     * np.float32(REL_BUCKETS - exact)).astype(np.int32)
    return np.where(n < exact, n, np.minimum(large, REL_BUCKETS - 1))


def _near_distances():
    i = np.arange(QCHUNK)[:, None]
    j = np.arange(LANES)[None, :]
    diag = i - j
    prev = QCHUNK + i - j
    band = np.where(j < 4, i + (BLK + 1) - BLK * j, 0)
    return np.stack([diag, prev, band])


def _expand_kernel(tab_ref, oh_ref, o_ref):
    o_ref[...] = jnp.dot(tab_ref[...], oh_ref[...], precision=HIGHEST, preferred_element_type=F32)


def expand_bias(rel_bias, dist):
    flat = dist.reshape(-1)
    m = flat.shape[0]
    assert m % LANES == 0
    onehot = (np.arange(REL_BUCKETS)[:, None] == _bucket_np(flat)[None, :]).astype(np.float32)
    tm = _tile(m, 8192, LANES)
    heads = rel_bias.shape[1]
    out = pl.pallas_call(
        _expand_kernel,
        grid=(m // tm,),
        in_specs=[pl.BlockSpec((heads, REL_BUCKETS), lambda i: (0, 0)), pl.BlockSpec((REL_BUCKETS, tm), lambda i: (0, i))],
        out_specs=pl.BlockSpec((heads, tm), lambda i: (0, i)),
        out_shape=jax.ShapeDtypeStruct((heads, m), F32),
        compiler_params=_params(("parallel",)),
        name="bias_expand",
    )(rel_bias.T, jnp.asarray(onehot))
    return out.reshape((heads,) + dist.shape)


def _select_blocks(imp, lane):
    sel = jnp.zeros(imp.shape, F32)
    big = jnp.int32(1 << 30)
    v = imp
    for _ in range(TOPK):
        mx = jnp.max(v, axis=-1, keepdims=True)
        idx = jnp.min(jnp.where(v == mx, lane, big), axis=-1, keepdims=True)
        hit = lane == idx
        sel = jnp.where(hit, 1.0, sel)
        v = jnp.where(hit, -3e38, v)
    return sel


def _rep4(x):
    return jnp.concatenate([x, x, x, x], axis=0)


def _qgroup(q_ref, g):
    return jnp.concatenate([q_ref[:, (g * GQA + r) * HEAD_DIM:(g * GQA + r + 1) * HEAD_DIM] for r in range(GQA)], axis=0)


def _gate_col(sg, br, g):
    c0 = GATE_COL0 + br * N_HEADS + g * GQA
    return jnp.concatenate([sg[:, c0 + r:c0 + r + 1] for r in range(GQA)], axis=0)


def _rows_means_kernel(x_ref, o_ref):
    nb, w = o_ref.shape
    o_ref[...] = jnp.sum(x_ref[...].reshape(nb, BLK, w), axis=1) * (1.0 / BLK)


def prompt_block_means(kv, batch, t_len):
    w = 2 * KV_GROUPS * HEAD_DIM
    nb = t_len // BLK
    return pl.pallas_call(
        _rows_means_kernel,
        grid=(batch,),
        in_specs=[pl.BlockSpec((t_len, w), lambda b: (b, 0))],
        out_specs=pl.BlockSpec((None, nb, w), lambda b: (b, 0, 0)),
        out_shape=jax.ShapeDtypeStruct((batch, nb, w), F32),
        compiler_params=_params(("parallel",)),
        name="cmp_means_prompt",
    )(kv)


def _page_specs(pps, half, layer):
    def spec(k):
        return pl.BlockSpec((None, None, 2, KV_GROUPS, HEAD_DIM, 2 * BLK),
                            lambda b, j, pt: (pt[b, j * pps + k], layer, half, 0, 0, 0))
    return [spec(k) for k in range(pps)]


def _page_means_kernel(pt_ref, *refs, pps, nbl):
    pages, o_ref = refs[:-1], refs[-1]
    j = pl.program_id(1)

    @pl.when(j == 0)
    def _():
        o_ref[...] = jnp.zeros(o_ref.shape, F32)

    span = pps * 2 * BLK
    row_blk = lax.broadcasted_iota(jnp.int32, (span, nbl), 0) // BLK + j * (2 * pps)
    avg = jnp.where(row_blk == lax.broadcasted_iota(jnp.int32, (span, nbl), 1), 1.0 / BLK, 0.0)
    for kind in range(2):
        for g in range(KV_GROUPS):
            cat = jnp.concatenate([pg[kind, g] for pg in pages], axis=1)
            o_ref[kind, g] += jnp.dot(cat, avg, precision=HIGHEST, preferred_element_type=F32)


def sample_block_means(cache_t, page_table, layer, pps, nbl):
    batch, n_pages = page_table.shape
    n_steps = n_pages // pps
    kern = functools.partial(_page_means_kernel, pps=pps, nbl=nbl)
    return pl.pallas_call(
        kern,
        grid_spec=pltpu.PrefetchScalarGridSpec(
            num_scalar_prefetch=1, grid=(batch, n_steps),
            in_specs=_page_specs(pps, 0, layer),
            out_specs=pl.BlockSpec((None, 2, KV_GROUPS, HEAD_DIM, nbl), lambda b, j, pt: (b, 0, 0, 0, 0))),
        out_shape=jax.ShapeDtypeStruct((batch, 2, KV_GROUPS, HEAD_DIM, nbl), F32),
        compiler_params=_params(("parallel", "arbitrary")),
        name="cmp_means",
    )(page_table, *([cache_t] * pps))


def _nsa_prompt_kernel(q_ref, kk_ref, vv_ref, kc_ref, gn_ref, tab_ref, band_ref, cst_ref, e_ref, o_ref,
                       s_ref, sel_ref, oc_ref, mrun_ref, acc_ref, *, nb):
    c = pl.program_id(1)
    qc = QCHUNK
    rows = GQA * qc
    kw = KV_GROUPS * HEAD_DIM
    span = SUPER * qc
    lane = lax.broadcasted_iota(jnp.int32, (rows, LANES), 1)
    qi = lax.broadcasted_iota(jnp.int32, (rows, LANES), 0) & (qc - 1)
    okc = (c * qc + qi - BLK * lane - (BLK - 1)) >= 0
    lane1 = lax.broadcasted_iota(jnp.int32, (qc, LANES), 1)
    qi1 = lax.broadcasted_iota(jnp.int32, (qc, LANES), 0)
    okc1 = (c * qc + qi1 - BLK * lane1 - (BLK - 1)) >= 0
    cur = 2 * c + (qi1 >= BLK).astype(jnp.int32)

    for g in range(KV_GROUPS):
        qg = _qgroup(q_ref, g)
        cst = cst_ref[g]
        kcg = kc_ref[:, g * HEAD_DIM:(g + 1) * HEAD_DIM].astype(BF16)
        vcg = kc_ref[:, kw + g * HEAD_DIM:kw + (g + 1) * HEAD_DIM].astype(BF16)
        shift = lax.rem(2 * c - 2 + LANES, LANES)
        band = pltpu.roll(band_ref[g], shift, axis=1)
        bias_c = jnp.where(lane < 2 * c - 2, cst, band)
        s = jnp.where(okc, _dot_nt(qg, kcg) + bias_c, NEG)
        e = jnp.exp(s - jnp.max(s, axis=-1, keepdims=True))
        p = jnp.where(okc, e / jnp.sum(e, axis=-1, keepdims=True), 0.0)
        oc_ref[g] = jnp.dot(p.astype(BF16), vcg, preferred_element_type=F32)
        psum = p[0:qc] + p[qc:2 * qc] + p[2 * qc:3 * qc] + p[3 * qc:4 * qc]
        imp = jnp.where(lane1 == cur, FORCE, jnp.where(okc1, psum, -1.0))
        imp = jnp.where(lane1 < nb, imp, -3e38)
        selm = jnp.where(lane1 <= cur, _select_blocks(imp, lane1), 0.0)
        sel_ref[g] = (jnp.dot(selm.astype(BF16), e_ref[...], preferred_element_type=F32) - 1.0) * (-NEG)

    sg = jax.nn.sigmoid(gn_ref[...])
    n_super = c // SUPER + 1
    w_tiles = WINDOW // qc + 1
    st0 = jnp.maximum(c - (w_tiles - 1), 0)
    for g in range(KV_GROUPS):
        qg = _qgroup(q_ref, g)
        mrun_ref[...] = jnp.full((rows, LANES), NEG, F32)

        def scores(st, carry):
            k = kk_ref[pl.ds(pl.multiple_of(st * span, span), span), g * HEAD_DIM:(g + 1) * HEAD_DIM]
            s = _dot_nt(qg, k)
            mrun = mrun_ref[...]
            for j in range(SUPER):
                kt = st * SUPER + j
                idx = jnp.where(kt == c, 0, jnp.where(kt == c - 1, 1, 2))
                off = pl.multiple_of(kt * qc, qc)
                sj = s[:, j * qc:(j + 1) * qc] + tab_ref[g, idx] + _rep4(sel_ref[g, :, pl.ds(off, qc)])
                s_ref[:, pl.ds(off, qc)] = sj
                mrun = jnp.maximum(mrun, sj)
            mrun_ref[...] = mrun
            return carry

        lax.fori_loop(0, n_super, scores, 0)
        m = jnp.max(mrun_ref[...], axis=-1, keepdims=True)
        mrun_ref[...] = jnp.zeros((rows, LANES), F32)

        def exps(st, carry):
            off = pl.multiple_of(st * span, span)
            e = jnp.exp(s_ref[:, pl.ds(off, span)] - m)
            s_ref[:, pl.ds(off, span)] = e
            lrun = mrun_ref[...]
            for j in range(SUPER):
                lrun = lrun + e[:, j * qc:(j + 1) * qc]
            mrun_ref[...] = lrun
            return carry

        lax.fori_loop(0, n_super, exps, 0)
        denom = jnp.sum(mrun_ref[...], axis=-1, keepdims=True)
        acc_ref[...] = jnp.zeros((rows, HEAD_DIM), F32)

        def values(st, carry):
            off = pl.multiple_of(st * span, span)
            p = (s_ref[:, pl.ds(off, span)] / denom).astype(BF16)
            acc_ref[...] += jnp.dot(p, vv_ref[pl.ds(off, span), g * HEAD_DIM:(g + 1) * HEAD_DIM], preferred_element_type=F32)
            return carry

        lax.fori_loop(0, n_super, values, 0)
        o_s = acc_ref[...]
        r0 = pl.multiple_of(st0 * qc, qc)
        s = _dot_nt(qg, kk_ref[pl.ds(r0, w_tiles * qc), kw + g * HEAD_DIM:kw + (g + 1) * HEAD_DIM])
        parts = []
        for w in range(w_tiles):
            kt = st0 + w
            idx = jnp.where(kt > c, 4, jnp.where(kt == c, 0, jnp.where(kt == c - 1, 1,
                                                                         jnp.where(kt == c - (w_tiles - 1), 3, 2))))
            parts.append(s[:, w * qc:(w + 1) * qc] + tab_ref[g, idx])
        sw = jnp.concatenate(parts, axis=1)
        ew = jnp.exp(sw - jnp.max(sw, axis=-1, keepdims=True))
        pw = (ew / jnp.sum(ew, axis=-1, keepdims=True)).astype(BF16)
        o_w = jnp.dot(pw, vv_ref[pl.ds(r0, w_tiles * qc), kw + g * HEAD_DIM:kw + (g + 1) * HEAD_DIM],
                      preferred_element_type=F32)
        tot = _gate_col(sg, 0, g) * oc_ref[g] + _gate_col(sg, 1, g) * o_s + _gate_col(sg, 2, g) * o_w
        for r in range(GQA):
            o_ref[:, (g * GQA + r) * HEAD_DIM:(g * GQA + r + 1) * HEAD_DIM] = tot[r * qc:(r + 1) * qc].astype(o_ref.dtype)


def nsa_prompt(q, kk, vv, kc, small, tabs, band, cst, batch, t_len):
    assert t_len % (SUPER * QCHUNK) == 0 and t_len // BLK <= LANES and t_len >= WINDOW + QCHUNK
    nch = t_len // QCHUNK
    nb = t_len // BLK
    rows = GQA * QCHUNK
    hd = N_HEADS * HEAD_DIM
    emat = (np.arange(LANES)[:, None] == (np.arange(t_len)[None, :] // BLK)).astype(np.float32)
    kern = functools.partial(_nsa_prompt_kernel, nb=nb)
    full = lambda shp: pl.BlockSpec(shp, lambda b, c: (0,) * len(shp))
    return pl.pallas_call(
        kern,
        grid=(batch, nch),
        in_specs=[pl.BlockSpec((QCHUNK, hd), lambda b, c: (b * nch + c, 0)),
                  pl.BlockSpec((t_len, kk.shape[1]), lambda b, c: (b, 0)),
                  pl.BlockSpec((t_len, vv.shape[1]), lambda b, c: (b, 0)),
                  pl.BlockSpec((None, LANES, kc.shape[2]), lambda b, c: (b, 0, 0)),
                  pl.BlockSpec((QCHUNK, LANES), lambda b, c: (b * nch + c, 0)),
                  full((KV_GROUPS, 5, rows, LANES)), full((KV_GROUPS, rows, LANES)), full((KV_GROUPS, rows, 1)),
                  full((LANES, t_len))],
        out_specs=pl.BlockSpec((QCHUNK, hd), lambda b, c: (b * nch + c, 0)),
        out_shape=jax.ShapeDtypeStruct((batch * t_len, hd), BF16),
        scratch_shapes=[pltpu.VMEM((rows, t_len), F32),
                        pltpu.VMEM((KV_GROUPS, QCHUNK, t_len), F32),
                        pltpu.VMEM((KV_GROUPS, rows, HEAD_DIM), F32),
                        pltpu.VMEM((rows, LANES), F32), pltpu.VMEM((rows, HEAD_DIM), F32)],
        compiler_params=_params(("parallel", "arbitrary")),
        name="nsa_prompt",
    )(q, kk, vv, kc, small, tabs, band, cst, jnp.asarray(emat, BF16))


def _nsa_sample_kernel(pt_ref, q_ref, kvn_ref, gn_ref, kc_ref, win_ref, near_ref, wtab_ref, cst_ref, e_ref,
                       *refs, pps, n_steps, past, ts, nbl):
    pages = refs[:pps]
    o_ref = refs[pps]
    s_ref, sel_ref, oc_ref, vbuf_ref, kn_ref = refs[pps + 1:]
    j = pl.program_id(1)
    rows = GQA * ts
    kw = KV_GROUPS * HEAD_DIM
    nb_past = past // BLK
    pg = 2 * BLK
    lane = lax.broadcasted_iota(jnp.int32, (rows, LANES), 1)
    ti = lax.broadcasted_iota(jnp.int32, (rows, LANES), 0) & (ts - 1)

    def qgroup(g):
        return _qgroup(q_ref, g).astype(BF16)

    @pl.when(j == 0)
    def _():
        kn_ref[...] = jnp.zeros(kn_ref.shape, F32)
        kn_ref[0:ts, :] = kvn_ref[:, 2 * kw:6 * kw]
        lane_b = lax.broadcasted_iota(jnp.int32, (rows, nbl), 1)
        t_b = lax.broadcasted_iota(jnp.int32, (rows, nbl), 0) & (ts - 1)
        okc = (past + t_b - BLK * lane_b - (BLK - 1)) >= 0
        lane1 = lax.broadcasted_iota(jnp.int32, (ts, nbl), 1)
        okc1 = (past + lax.broadcasted_iota(jnp.int32, (ts, nbl), 0) - BLK * lane1 - (BLK - 1)) >= 0
        for g in range(KV_GROUPS):
            qg = qgroup(g)
            cst = cst_ref[g]
            band = near_ref[g, 2]
            kcg = kc_ref[0, g].astype(BF16)
            vcg = kc_ref[1, g].astype(BF16)
            bias_c = jnp.where(lane_b == nb_past - 2, band[:, 0:1], jnp.where(lane_b == nb_past - 1, band[:, 1:2], cst))
            s = jnp.where(okc, jnp.dot(qg, kcg, preferred_element_type=F32) + bias_c, NEG)
            e = jnp.exp(s - jnp.max(s, axis=-1, keepdims=True))
            p = jnp.where(okc, e / jnp.sum(e, axis=-1, keepdims=True), 0.0)
            oc_ref[g] = _dot_nt(p.astype(BF16), vcg)
            psum = p[0:ts] + p[ts:2 * ts] + p[2 * ts:3 * ts] + p[3 * ts:4 * ts]
            imp = jnp.where(lane1 == nb_past, FORCE, jnp.where(okc1, psum, -1.0))
            imp = jnp.where(lane1 <= nb_past, imp, -3e38)
            selm = _select_blocks(imp, lane1)
            selm16 = jnp.concatenate([selm, jnp.zeros_like(selm)], axis=0).astype(BF16)
            sel_ref[g] = (jnp.dot(selm16, e_ref[...], preferred_element_type=F32)[0:ts] - 1.0) * (-NEG)

    is_last = j == n_steps - 1
    base = j * (pps * pg)
    qgs = [qgroup(g) for g in range(KV_GROUPS)]
    for k in range(pps):
        off = pl.multiple_of(base + k * pg, pg)
        for g in range(KV_GROUPS):
            vbuf_ref[g, :, pl.ds(off, pg)] = pages[k][1, g].astype(BF16)
            sk = jnp.dot(qgs[g], pages[k][0, g].astype(BF16), preferred_element_type=F32)
            bias = jnp.where(is_last, near_ref[g, 1], cst_ref[g]) if k == pps - 1 else cst_ref[g]
            s_ref[g, :, pl.ds(off, pg)] = sk + bias + _rep4(sel_ref[g, :, pl.ds(off, pg)])

    @pl.when(is_last)
    def _():
        sg = jax.nn.sigmoid(gn_ref[...])
        wl = win_ref.shape[-1]
        lane_w = lax.broadcasted_iota(jnp.int32, (rows, wl + LANES), 1)
        t_w = lax.broadcasted_iota(jnp.int32, (rows, wl + LANES), 0) & (ts - 1)
        okw = jnp.where(lane_w < wl, lane_w - t_w, t_w - (lane_w - wl)) >= 0
        for g in range(KV_GROUPS):
            qg = qgroup(g)
            kn = kn_ref[:, g * HEAD_DIM:(g + 1) * HEAD_DIM].astype(BF16)
            vn = kn_ref[:, kw + g * HEAD_DIM:kw + (g + 1) * HEAD_DIM].astype(BF16)
            s_ref[g, :, past:past + LANES] = jnp.where(lane <= ti, _dot_nt(qg, kn) + near_ref[g, 0], NEG)
            s = s_ref[g]
            e = jnp.exp(s - jnp.max(s, axis=-1, keepdims=True))
            p = e / jnp.sum(e, axis=-1, keepdims=True)
            o_s = (_dot_nt(p[:, :past].astype(BF16), vbuf_ref[g])
                   + jnp.dot(p[:, past:].astype(BF16), vn, preferred_element_type=F32))
            kwin = win_ref[0, g].astype(BF16)
            vwin = win_ref[1, g].astype(BF16)
            knw = kn_ref[:, 2 * kw + g * HEAD_DIM:2 * kw + (g + 1) * HEAD_DIM].astype(BF16)
            vnw = kn_ref[:, 3 * kw + g * HEAD_DIM:3 * kw + (g + 1) * HEAD_DIM].astype(BF16)
            s = jnp.concatenate([jnp.dot(qg, kwin, preferred_element_type=F32), _dot_nt(qg, knw)], axis=1) + wtab_ref[g]
            s = jnp.where(okw, s, NEG)
            e = jnp.exp(s - jnp.max(s, axis=-1, keepdims=True))
            p = e / jnp.sum(e, axis=-1, keepdims=True)
            o_w = (_dot_nt(p[:, :wl].astype(BF16), vwin)
                   + jnp.dot(p[:, wl:].astype(BF16), vnw, preferred_element_type=F32))
            tot = _gate_col(sg, 0, g) * oc_ref[g] + _gate_col(sg, 1, g) * o_s + _gate_col(sg, 2, g) * o_w
            for r in range(GQA):
                o_ref[:, (g * GQA + r) * HEAD_DIM:(g * GQA + r + 1) * HEAD_DIM] = tot[r * ts:(r + 1) * ts]


def nsa_sample(q3, kvn3, small3, kcvc, win_t, near_s, wtab, cst_s, cache_t, page_table, layer, pps):
    batch, ts, hd = q3.shape
    n_pages = page_table.shape[1]
    n_steps = n_pages // pps
    past = n_pages * 2 * BLK
    nbl = kcvc.shape[-1]
    wl = win_t.shape[-1]
    rows = GQA * ts
    kw = KV_GROUPS * HEAD_DIM
    assert ts == SUBLANES and past % QCHUNK == 0 and wl % LANES == 0
    emat = (np.arange(nbl)[:, None] == (np.arange(past)[None, :] // BLK)).astype(np.float32)
    kern = functools.partial(_nsa_sample_kernel, pps=pps, n_steps=n_steps, past=past, ts=ts, nbl=nbl)
    c3 = lambda shp: pl.BlockSpec(shp, lambda b, j, pt: (0,) * len(shp))
    return pl.pallas_call(
        kern,
        grid_spec=pltpu.PrefetchScalarGridSpec(
            num_scalar_prefetch=1, grid=(batch, n_steps),
            in_specs=[pl.BlockSpec((None, ts, hd), lambda b, j, pt: (b, 0, 0)),
                      pl.BlockSpec((None, ts, kvn3.shape[2]), lambda b, j, pt: (b, 0, 0)),
                      pl.BlockSpec((None, ts, LANES), lambda b, j, pt: (b, 0, 0)),
                      pl.BlockSpec((None, 2, KV_GROUPS, HEAD_DIM, nbl), lambda b, j, pt: (b, 0, 0, 0, 0)),
                      pl.BlockSpec((None, None, 2, KV_GROUPS, HEAD_DIM, wl), lambda b, j, pt: (layer, b, 0, 0, 0, 0)),
                      c3((KV_GROUPS, 3, rows, LANES)), c3((KV_GROUPS, rows, wl + LANES)), c3((KV_GROUPS, rows, 1)),
                      c3((nbl, past))] + _page_specs(pps, 1, layer),
            out_specs=pl.BlockSpec((None, ts, hd), lambda b, j, pt: (b, 0, 0)),
            scratch_shapes=[pltpu.VMEM((KV_GROUPS, rows, past + LANES), F32),
                            pltpu.VMEM((KV_GROUPS, ts, past), F32),
                            pltpu.VMEM((KV_GROUPS, rows, HEAD_DIM), F32),
                            pltpu.VMEM((KV_GROUPS, HEAD_DIM, past), BF16),
                            pltpu.VMEM((LANES, 4 * kw), F32)]),
        out_shape=jax.ShapeDtypeStruct((batch, ts, hd), F32),
        compiler_params=_params(("parallel", "arbitrary")),
        name="nsa_sample",
    )(page_table, q3, kvn3, small3, kcvc, win_t, near_s, wtab, cst_s, jnp.asarray(emat, BF16), *([cache_t] * pps))


def _merge_kernel(c_ref, o_ref, y_ref, g0_ref, g1_ref, g2_ref, wa_ref, wn_ref, ws_ref, out_ref):
    oa = jnp.dot(c_ref[...], wa_ref[...], preferred_element_type=F32)
    ob = jnp.dot(o_ref[...], wn_ref[...], preferred_element_type=F32)
    oc = jnp.dot(y_ref[...], ws_ref[...], preferred_element_type=F32)
    mixed = (jax.nn.sigmoid(g0_ref[...]) * oa + jax.nn.sigmoid(g1_ref[...]) * ob + jax.nn.sigmoid(g2_ref[...]) * oc)
    out_ref[...] = mixed.astype(out_ref.dtype)


def merge_branches(c, o, y, gmix, wa, wn, ws):
    n = c.shape[0]
    d = wa.shape[1]
    tm = _tile(n, 768, BF16_ROWS)
    tn = _tile(d, 512, LANES)
    nj = d // tn
    act = lambda a: pl.BlockSpec((tm, a.shape[1]), lambda i, j: (i, 0))
    wsp = lambda w: pl.BlockSpec((w.shape[0], tn), lambda i, j: (0, j))
    gsp = lambda br: pl.BlockSpec((tm, tn), lambda i, j: (i, br * nj + j))
    return pl.pallas_call(
        _merge_kernel,
        grid=(n // tm, nj),
        in_specs=[act(c), act(o), act(y), gsp(0), gsp(1), gsp(2), wsp(wa), wsp(wn), wsp(ws)],
        out_specs=pl.BlockSpec((tm, tn), lambda i, j: (i, j)),
        out_shape=jax.ShapeDtypeStruct((n, d), BF16),
        compiler_params=_params(("parallel", "parallel")),
        name="merge",
    )(c, o, y, gmix, gmix, gmix, wa, wn, ws)


def _outproj_kernel(a_ref, w_ref, x_ref, o_ref):
    o_ref[...] = x_ref[...] + jnp.dot(a_ref[...], w_ref[...], preferred_element_type=F32)


def out_projection(a, w, x):
    n, k = a.shape
    d = w.shape[1]
    tm = _tile(n, 768, BF16_ROWS)
    tn = _tile(d, 1024, LANES)
    return pl.pallas_call(
        _outproj_kernel,
        grid=(n // tm, d // tn),
        in_specs=[pl.BlockSpec((tm, k), lambda i, j: (i, 0)), pl.BlockSpec((k, tn), lambda i, j: (0, j)),
                  pl.BlockSpec((tm, tn), lambda i, j: (i, j))],
        out_specs=pl.BlockSpec((tm, tn), lambda i, j: (i, j)),
        out_shape=jax.ShapeDtypeStruct((n, d), F32),
        compiler_params=_params(("parallel", "parallel")),
        name="out_proj",
    )(a, w, x)


def _router_kernel(x_ref, g_ref, w_ref, b_ref, h_ref, r_ref):
    x = x_ref[...]
    ms = jnp.mean(x * x, axis=-1, keepdims=True)
    h = (x * lax.rsqrt(ms + EPS) * g_ref[...]).astype(BF16)
    h_ref[...] = h
    logits = jnp.dot(h, w_ref[...], preferred_element_type=F32) + b_ref[...]
    lane = lax.broadcasted_iota(jnp.int32, logits.shape, 1)
    big = jnp.int32(1 << 30)
    gl = jnp.where(lane < MOE_GROUPS, logits, NEG)
    gmax = jnp.max(gl, axis=-1, keepdims=True)
    gidx = jnp.min(jnp.where(gl == gmax, lane, big), axis=-1, keepdims=True)
    g_w = 1.0 / jnp.sum(jnp.exp(gl - gmax), axis=-1, keepdims=True)
    eid = lane - MOE_GROUPS
    in_grp = (eid >= 0) & (eid < MOE_GROUPS * MOE_EPG) & ((eid >> 3) == gidx)
    el = jnp.where(in_grp, logits, NEG)
    m1 = jnp.max(el, axis=-1, keepdims=True)
    i1 = jnp.min(jnp.where(el == m1, lane, big), axis=-1, keepdims=True)
    el2 = jnp.where(lane == i1, NEG, el)
    m2 = jnp.max(el2, axis=-1, keepdims=True)
    i2 = jnp.min(jnp.where(el2 == m2, lane, big), axis=-1, keepdims=True)
    p2 = jnp.exp(m2 - m1)
    w1 = g_w / (1.0 + p2)
    w2 = g_w * p2 / (1.0 + p2)
    r_ref[...] = jnp.where(lane == 0, (i1 - MOE_GROUPS).astype(F32),
                           jnp.where(lane == 1, (i2 - MOE_GROUPS).astype(F32),
                                     jnp.where(lane == 2, w1, jnp.where(lane == 3, w2, 0.0))))


def moe_router(x, g, wr, br):
    n, d = x.shape
    tm = _tile(n, 256, BF16_ROWS)
    return pl.pallas_call(
        _router_kernel,
        grid=(n // tm,),
        in_specs=[pl.BlockSpec((tm, d), lambda i: (i, 0)), pl.BlockSpec((1, d), lambda i: (0, 0)),
                  pl.BlockSpec((d, LANES), lambda i: (0, 0)), pl.BlockSpec((1, LANES), lambda i: (0, 0))],
        out_specs=[pl.BlockSpec((tm, d), lambda i: (i, 0)), pl.BlockSpec((tm, LANES), lambda i: (i, 0))],
        out_shape=[jax.ShapeDtypeStruct((n, d), BF16), jax.ShapeDtypeStruct((n, LANES), F32)],
        compiler_params=_params(("parallel",)),
        name="moe_router",
    )(x, g.reshape(1, d), wr, br)


def _expert_kernel(te_ref, nu_ref, x_ref, rw_ref, wg_ref, wu_ref, wd_ref, y_ref, wgb_ref, wub_ref, wdb_ref):
    t = pl.program_id(0)
    fresh = jnp.logical_or(t == 0, te_ref[t] != te_ref[jnp.maximum(t - 1, 0)])

    @pl.when(jnp.logical_and(fresh, t < nu_ref[0]))
    def _():
        wgb_ref[...] = wg_ref[...].astype(BF16)
        wub_ref[...] = wu_ref[...].astype(BF16)
        wdb_ref[...] = wd_ref[...].astype(BF16)

    @pl.when(t < nu_ref[0])
    def _():
        x = x_ref[...]
        a = jnp.dot(x, wgb_ref[...], preferred_element_type=F32)
        u = jnp.dot(x, wub_ref[...], preferred_element_type=F32)
        hid = (_silu(a) * u * rw_ref[...]).astype(BF16)
        y_ref[...] = jnp.dot(hid, wdb_ref[...], preferred_element_type=F32)


def moe_experts(xs, row_w, tile_e, n_used, wg, wu, wd, layer, tme):
    r, d = xs.shape
    ff = wg.shape[3]
    nt = r // tme
    return pl.pallas_call(
        _expert_kernel,
        grid_spec=pltpu.PrefetchScalarGridSpec(
            num_scalar_prefetch=2, grid=(nt,),
            in_specs=[pl.BlockSpec((tme, d), lambda t, te, nu: (t, 0)),
                      pl.BlockSpec((tme, 1), lambda t, te, nu: (t, 0)),
                      pl.BlockSpec((None, None, d, ff), lambda t, te, nu: (layer, te[t], 0, 0)),
                      pl.BlockSpec((None, None, d, ff), lambda t, te, nu: (layer, te[t], 0, 0)),
                      pl.BlockSpec((None, None, ff, d), lambda t, te, nu: (layer, te[t], 0, 0))],
            out_specs=pl.BlockSpec((tme, d), lambda t, te, nu: (t, 0)),
            scratch_shapes=[pltpu.VMEM((d, ff), BF16), pltpu.VMEM((d, ff), BF16), pltpu.VMEM((ff, d), BF16)]),
        out_shape=jax.ShapeDtypeStruct((r, d), F32),
        compiler_params=_params(("arbitrary",)),
        name="moe_experts",
    )(tile_e, n_used, xs, row_w, wg, wu, wd)


def moe_layer(x, g, wrg, brg, wre, bre, wg, wu, wd, layer):
    n, d = x.shape
    n_exp = wg.shape[1]
    wr = jnp.concatenate([wrg, wre.transpose(1, 0, 2).reshape(d, n_exp)], axis=1)
    wr = jnp.pad(wr, ((0, 0), (0, LANES - wr.shape[1])))
    br = jnp.pad(jnp.concatenate([brg, bre.reshape(-1)]), (0, LANES - MOE_GROUPS - n_exp)).reshape(1, LANES)
    h, route = moe_router(x, g, wr.astype(BF16), br)
    tme = 256
    e_flat = route[:, 0:MOE_TOPK].astype(jnp.int32).T.reshape(-1)
    w_flat = route[:, MOE_TOPK:2 * MOE_TOPK].T.reshape(-1)
    na = e_flat.shape[0]
    nt = -(-na // tme) + n_exp
    order = jnp.argsort(e_flat, stable=True)
    counts = jnp.sum(e_flat[:, None] == jnp.arange(n_exp)[None, :], axis=0).astype(jnp.int32)
    pc = ((counts + tme - 1) // tme) * tme
    pend = jnp.cumsum(pc)
    pstart = pend - pc
    ustart = jnp.cumsum(counts) - counts
    se = e_flat[order]
    dest = pstart[se] + jnp.arange(na, dtype=jnp.int32) - ustart[se]
    row_tok = jnp.zeros((nt * tme,), jnp.int32).at[dest].set((order % n).astype(jnp.int32))
    row_w = jnp.zeros((nt * tme,), F32).at[dest].set(w_flat[order])
    dest_flat = jnp.zeros((na,), jnp.int32).at[order].set(dest)
    tile_e = jnp.minimum(jnp.searchsorted(pend, jnp.arange(nt, dtype=jnp.int32) * tme, side="right"), n_exp - 1).astype(jnp.int32)
    n_used = (pend[-1] // tme).astype(jnp.int32).reshape(1)
    xs = jnp.take(h, row_tok, axis=0)
    ys = moe_experts(xs, row_w.reshape(-1, 1), tile_e, n_used, wg, wu, wd, layer, tme)
    return x + jnp.take(ys, dest_flat[:n], axis=0) + jnp.take(ys, dest_flat[n:], axis=0)


def kernel(x_prompt, x_sample, cache_nsa_kv, cache_nsa_win, state_conv_a, state_conv_ssm, state_ssm, page_table,
           rel_bias, norm_mix_g, w_in, conv_a_w, conv_a_b, ln_a_g, ln_a_b, w_a_out, w_nsa_out, conv_s_w, conv_s_b,
           dt_bias, a_log, d_skip, ssm_norm_g, w_ssm_out, w_out, norm_ffn_g, w_router_group, b_router_group,
           w_router_expert, b_router_expert, w_gate, w_up, w_down, final_norm_g):
    bp, tp, d = x_prompt.shape
    bs, ts, _ = x_sample.shape
    depth = w_in.shape[0]
    n_p, n_s = bp * tp, bs * ts
    ch = conv_a_w.shape[2]
    conv_dim = conv_s_w.shape[2]
    heads_ssm = dt_bias.shape[1]
    inner = heads_ssm * SSM_HEAD_DIM
    hd = N_HEADS * HEAD_DIM
    kw = KV_GROUPS * HEAD_DIM
    kvw = KV_KINDS * kw
    n_pages = page_table.shape[1]
    page = cache_nsa_kv.shape[2]
    past = n_pages * page
    wl = cache_nsa_win.shape[2]
    assert page == 2 * BLK and wl == min(WINDOW, past) and tp >= WINDOW
    pps = _tile(n_pages, 16, 1)
    rows_p = GQA * QCHUNK

    splits = (2 * ch, hd, kvw, 3 * N_HEADS, inner, conv_dim, heads_ssm, 3 * d)
    offs = np.concatenate([[0], np.cumsum(splits)])
    o_u, o_q, o_kv, o_gn, o_z, o_x, o_dt, o_gm = [int(v) for v in offs[:-1]]

    near_d = _near_distances()
    tj = np.arange(wl + LANES)[None, :]
    win_d = np.where(tj < wl + ts, wl + np.arange(ts)[:, None] - tj, 0)
    near_f = expand_bias(rel_bias, near_d)
    wtab_f = expand_bias(rel_bias, win_d)
    near_g = near_f.reshape(KV_GROUPS, GQA, 3, QCHUNK, LANES).transpose(0, 2, 1, 3, 4)
    near_p = near_g.reshape(KV_GROUPS, 3, rows_p, LANES)
    near_s = near_g[:, :, :, :ts].reshape(KV_GROUPS, 3, GQA * ts, LANES)
    wtab = wtab_f.reshape(KV_GROUPS, GQA * ts, wl + LANES)
    far = rel_bias[REL_BUCKETS - 1].reshape(KV_GROUPS, GQA, 1)
    cst_p = jnp.repeat(far, QCHUNK, axis=1).reshape(KV_GROUPS, rows_p, 1)
    cst_s = jnp.repeat(far, ts, axis=1).reshape(KV_GROUPS, GQA * ts, 1)
    ii = (np.arange(rows_p) % QCHUNK)[:, None]
    jj = np.arange(LANES)[None, :]
    cst_t = jnp.broadcast_to(cst_p, (KV_GROUPS, rows_p, LANES))
    tabs_p = jnp.stack([jnp.where(jj <= ii, near_p[:, 0], NEG), near_p[:, 1], cst_t,
                        jnp.where(jj >= ii, cst_t, NEG), jnp.full((KV_GROUPS, rows_p, LANES), NEG, F32)], axis=1)

    w_in_t = w_in.transpose(0, 2, 1)
    cache_t = cache_nsa_kv.transpose(0, 1, 3, 4, 5, 2)
    win_t = cache_nsa_win.transpose(0, 1, 3, 4, 5, 2)
    zeros_hist_a = jnp.zeros((bp, HIST_A, ch), F32)
    zeros_hist_s = jnp.zeros((bp, HIST_S, conv_dim), F32)
    zeros_ssm = jnp.zeros((bp, heads_ssm, SSM_HEAD_DIM, SSM_STATE), F32)
    nbl = -(-(past // BLK + 1) // LANES) * LANES

    x = jnp.concatenate([x_prompt.reshape(n_p, d), x_sample.reshape(n_s, d)], axis=0)
    kv_p, kv_s, win_p, win_s, ha_p, ha_s, hs_p, hs_s, st_p, st_s = ([] for _ in range(10))
    for l in range(depth):
        wl_in = w_in_t[l]
        seg = lambda a, b: wl_in[a:b].astype(BF16)
        h = rmsnorm(x, norm_mix_g[l], BF16)
        (u_in,) = matmul(h, seg(o_u, o_q), [F32])
        (q,) = matmul(h, (wl_in[o_q:o_kv] * (HEAD_DIM ** -0.5)).astype(BF16), [BF16])
        kv, kvb = matmul(h, seg(o_kv, o_gn), [F32, BF16])
        w_small = jnp.concatenate([wl_in[o_dt:o_gm], wl_in[o_gn:o_z],
                                   jnp.zeros((LANES - heads_ssm - 3 * N_HEADS, d), F32)], axis=0).astype(BF16)
        (small,) = matmul(h, w_small, [F32])
        (z,) = matmul(h, seg(o_z, o_x), [F32])
        (xbc,) = matmul(h, seg(o_x, o_dt), [F32])
        (gmix,) = matmul(h, seg(o_gm, o_gm + 3 * d), [F32])

        hist_a = jnp.pad(state_conv_a[l], ((0, 0), (HIST_A - (CONV_A_W - 1), 0), (0, 0)))
        ca_args = (conv_a_w[l], conv_a_b[l], ln_a_g[l], ln_a_b[l])
        c_p, ha_new_p = conv_a_branch(u_in, 0, zeros_hist_a, *ca_args, bp, tp, BF16)
        c_s, ha_new_s = conv_a_branch(u_in, n_p, hist_a, *ca_args, bs, ts, F32)
        hist_s = jnp.pad(state_conv_ssm[l], ((0, 0), (HIST_S - (SSM_CONV_W - 1), 0), (0, 0)))
        ss_args = (conv_s_w[l], conv_s_b[l], dt_bias[l], a_log[l], d_skip[l], ssm_norm_g[l])
        small_tp = small[:n_p].reshape(bp, tp, LANES).transpose(0, 2, 1)
        small_ts = small[n_p:].reshape(bs, ts, LANES).transpose(0, 2, 1)
        y_p, hs_new_p, st_new_p = ssd_branch(xbc, z, small, small_tp, 0, zeros_hist_s, zeros_ssm, *ss_args, bp, tp, BF16)
        y_s, hs_new_s, st_new_s = ssd_branch(xbc, z, small, small_ts, n_p, hist_s, state_ssm[l], *ss_args, bs, ts, F32)
        kc_p = prompt_block_means(kv, bp, tp)
        kc_p = jnp.pad(kc_p, ((0, 0), (0, LANES - kc_p.shape[1]), (0, 0)))
        kk = jnp.concatenate([kvb[:n_p, 2 * kw:3 * kw], kvb[:n_p, 4 * kw:5 * kw]], axis=1)
        vv = jnp.concatenate([kvb[:n_p, 3 * kw:4 * kw], kvb[:n_p, 5 * kw:6 * kw]], axis=1)
        o_p = nsa_prompt(q, kk, vv, kc_p, small, tabs_p, near_p[:, 2], cst_p, bp, tp)
        kcvc = sample_block_means(cache_t, page_table, l, pps, nbl)
        o_s = nsa_sample(q[n_p:].astype(F32).reshape(bs, ts, hd), kv[n_p:].reshape(bs, ts, kvw),
                         small[n_p:].reshape(bs, ts, LANES), kcvc, win_t, near_s, wtab, cst_s, cache_t, page_table, l, pps)

        c_all = jnp.concatenate([c_p, c_s.astype(BF16)], axis=0)
        o_all = jnp.concatenate([o_p, o_s.reshape(n_s, hd).astype(BF16)], axis=0)
        y_all = jnp.concatenate([y_p, y_s.astype(BF16)], axis=0)
        mixed = merge_branches(c_all, o_all, y_all, gmix, w_a_out[l].astype(BF16), w_nsa_out[l].astype(BF16),
                               w_ssm_out[l].astype(BF16))
        x = out_projection(mixed, w_out[l].astype(BF16), x)
        x = moe_layer(x, norm_ffn_g[l], w_router_group[l], b_router_group[l], w_router_expert[l], b_router_expert[l],
                      w_gate, w_up, w_down, l)

        kvr = kv[:, :4 * kw]
        kv_p.append(kvr[:n_p].reshape(bp, tp, 4, KV_GROUPS, HEAD_DIM))
        kv_s.append(kvr[n_p:].reshape(bs, ts, 4, KV_GROUPS, HEAD_DIM))
        wr = kv[:, 4 * kw:]
        win_p.append(wr[:n_p].reshape(bp, tp, 2, KV_GROUPS, HEAD_DIM)[:, tp - WINDOW:])
        win_all = jnp.concatenate([cache_nsa_win[l], wr[n_p:].reshape(bs, ts, 2, KV_GROUPS, HEAD_DIM)], axis=1)
        keep = min(WINDOW, wl + ts)
        win_s.append(win_all[:, wl + ts - keep:])
        ha_p.append(ha_new_p[:, HIST_A - (CONV_A_W - 1):])
        ha_s.append(ha_new_s[:, HIST_A - (CONV_A_W - 1):])
        hs_p.append(hs_new_p[:, HIST_S - (SSM_CONV_W - 1):])
        hs_s.append(hs_new_s[:, HIST_S - (SSM_CONV_W - 1):])
        st_p.append(st_new_p)
        st_s.append(st_new_s)

    y = rmsnorm(x, final_norm_g, F32)
    return (y[:n_p].reshape(bp, tp, d), y[n_p:].reshape(bs, ts, d),
            jnp.stack(kv_p, axis=1), jnp.stack(kv_s, axis=1), jnp.stack(win_p), jnp.stack(win_s),
            jnp.stack(ha_p), jnp.stack(ha_s), jnp.stack(hs_p), jnp.stack(hs_s), jnp.stack(st_p), jnp.stack(st_s))
```

```python
import functools
import math

import numpy as np
import jax
import jax.numpy as jnp
from jax import lax
from jax.experimental import pallas as pl
from jax.experimental.pallas import tpu as pltpu

HEAD_DIM = 64
KV_GROUPS = 4
GQA = 4
N_HEADS = KV_GROUPS * GQA
KV_KINDS = 6
BLK = 64
TOPK = 8
WINDOW = 512
QCHUNK = 128
REL_BUCKETS = 32
REL_MAX_DIST = 128
CONV_A_W = 31
SSM_CONV_W = 4
SSM_HEAD_DIM = 64
SSM_GROUPS = 4
SSM_STATE = 128
SSM_CHUNK = 128
MOE_GROUPS = 4
MOE_EPG = 8
MOE_TOPK = 2
EPS = 1e-6
NEG = -1e30
FORCE = 1e4

LANES = 128
SUBLANES = 8
BF16_ROWS = 16
VMEM_LIMIT = 56 * 1024 * 1024

DT_COLS = 32
GATE_COL0 = DT_COLS
SUPER = 4

HIGHEST = lax.Precision.HIGHEST
F32 = jnp.float32
BF16 = jnp.bfloat16


def _params(sem, limit=VMEM_LIMIT):
    return pltpu.CompilerParams(dimension_semantics=sem, vmem_limit_bytes=limit)


def _tile(n, target, mult):
    best = None
    for t in range(mult, min(n, target) + 1, mult):
        if n % t == 0:
            best = t
    return best if best is not None else n


def _dot_nt(a, b):
    return lax.dot_general(a, b, (((1,), (1,)), ((), ())), preferred_element_type=F32)


def _dot_tn(a, b):
    return lax.dot_general(a, b, (((0,), (0,)), ((), ())), preferred_element_type=F32)


def _silu(x):
    return x * jax.nn.sigmoid(x)


def _softplus(x):
    return jnp.maximum(x, 0.0) + jnp.log(1.0 + jnp.exp(-jnp.abs(x)))


def _rms_kernel(x_ref, g_ref, o_ref):
    x = x_ref[...]
    ms = jnp.mean(x * x, axis=-1, keepdims=True)
    o_ref[...] = (x * lax.rsqrt(ms + EPS) * g_ref[...]).astype(o_ref.dtype)


def rmsnorm(x, g, out_dtype):
    n, d = x.shape
    tm = _tile(n, 1024, BF16_ROWS)
    return pl.pallas_call(
        _rms_kernel,
        grid=(n // tm,),
        in_specs=[pl.BlockSpec((tm, d), lambda i: (i, 0)), pl.BlockSpec((1, d), lambda i: (0, 0))],
        out_specs=pl.BlockSpec((tm, d), lambda i: (i, 0)),
        out_shape=jax.ShapeDtypeStruct((n, d), out_dtype),
        compiler_params=_params(("parallel",)),
        name="rmsnorm",
    )(x, g.reshape(1, d))


def _mm_kernel(a_ref, w_ref, *refs, scale):
    o_refs, wb_ref = refs[:-1], refs[-1]

    @pl.when(pl.program_id(1) == 0)
    def _():
        w = w_ref[0]
        wb_ref[...] = (w if scale == 1.0 else w * scale).astype(BF16)

    acc = _dot_nt(a_ref[...], wb_ref[...])
    for o in o_refs:
        o[...] = acc.astype(o.dtype)


def matmul(a, wt, layer, row0, c, out_dtypes, scale=1.0):
    n, k = a.shape
    assert row0 % SUBLANES == 0
    tm = _tile(n, 768, BF16_ROWS)
    tn = _tile(c, 1024, SUBLANES)
    outs = pl.pallas_call(
        functools.partial(_mm_kernel, scale=scale),
        grid=(c // tn, n // tm),
        in_specs=[pl.BlockSpec((tm, k), lambda j, i: (i, 0)),
                  pl.BlockSpec((pl.Element(1), pl.Element(tn), pl.Element(k)),
                               lambda j, i: (layer, pl.multiple_of(row0 + j * tn, SUBLANES), 0))],
        out_specs=[pl.BlockSpec((tm, tn), lambda j, i: (i, j)) for _ in out_dtypes],
        out_shape=[jax.ShapeDtypeStruct((n, c), dt) for dt in out_dtypes],
        scratch_shapes=[pltpu.VMEM((tn, k), BF16)],
        compiler_params=_params(("parallel", "arbitrary")),
        name="proj",
    )(a, wt)
    return outs


HIST_A = 32
HIST_S = 8


def _round_bf16(x):
    return x.astype(BF16).astype(F32)


def _push_tail(tail_ref, new, n_new, keep):
    if n_new >= keep:
        tail_ref[...] = new[n_new - keep:n_new]
    else:
        tail_ref[...] = jnp.concatenate([tail_ref[n_new:keep], new], axis=0)


def _conva_kernel(u_ref, hist_ref, w_ref, b_ref, lg_ref, lb_ref, c_ref, hout_ref, ext_ref, conv_ref, tail_ref, *, tt, ch):
    t = pl.program_id(1)

    @pl.when(t == 0)
    def _():
        tail_ref[...] = hist_ref[...]
        ext_ref[0:HIST_A, :] = _round_bf16(hist_ref[...])

    uin = u_ref[...]
    u = uin[:, :ch] * jax.nn.sigmoid(uin[:, ch:])
    ext_ref[HIST_A:HIST_A + tt, :] = _round_bf16(u)
    _push_tail(tail_ref, u, tt, HIST_A)
    hout_ref[...] = tail_ref[...]
    off = HIST_A - (CONV_A_W - 1)
    rb = min(tt, 128)

    def lane_block(cb, carry):
        lo = pl.multiple_of(cb * LANES, LANES)
        for r in range(tt // rb):
            acc = jnp.broadcast_to(b_ref[:, pl.ds(lo, LANES)], (rb, LANES))
            for k in range(CONV_A_W):
                acc = acc + w_ref[k:k + 1, pl.ds(lo, LANES)] * ext_ref[r * rb + off + k:r * rb + off + k + rb, pl.ds(lo, LANES)]
            conv_ref[r * rb:(r + 1) * rb, pl.ds(lo, LANES)] = acc
        return carry

    lax.fori_loop(0, ch // LANES, lane_block, 0)
    cv = conv_ref[...]
    mu = jnp.mean(cv, axis=-1, keepdims=True)
    xc = cv - mu
    var = jnp.mean(xc * xc, axis=-1, keepdims=True)
    y = xc * lax.rsqrt(var + EPS) * lg_ref[...] + lb_ref[...]
    c_ref[...] = _silu(y).astype(c_ref.dtype)
    ext_ref[0:HIST_A, :] = ext_ref[tt:tt + HIST_A, :]


def conv_a_branch(u_in, row_off, hist, w, b, lg, lb, batch, t_len, out_dtype, n_out=None):
    ch = w.shape[1]
    tt = _tile(t_len, 256, SUBLANES)
    nt = t_len // tt
    assert row_off % tt == 0
    r0 = row_off // tt
    kern = functools.partial(_conva_kernel, tt=tt, ch=ch)
    vec = lambda: pl.BlockSpec((1, ch), lambda bi, ti: (0, 0))
    return pl.pallas_call(
        kern,
        grid=(batch, nt),
        in_specs=[pl.BlockSpec((tt, 2 * ch), lambda bi, ti: (r0 + bi * nt + ti, 0)),
                  pl.BlockSpec((None, HIST_A, ch), lambda bi, ti: (bi, 0, 0)),
                  pl.BlockSpec((CONV_A_W, ch), lambda bi, ti: (0, 0)), vec(), vec(), vec()],
        out_specs=[pl.BlockSpec((tt, ch), lambda bi, ti: (bi * nt + ti, 0)),
                   pl.BlockSpec((None, HIST_A, ch), lambda bi, ti: (bi, 0, 0))],
        out_shape=[jax.ShapeDtypeStruct((n_out or batch * t_len, ch), out_dtype),
                   jax.ShapeDtypeStruct((batch, HIST_A, ch), F32)],
        scratch_shapes=[pltpu.VMEM((tt + HIST_A, ch), F32), pltpu.VMEM((tt, ch), F32), pltpu.VMEM((HIST_A, ch), F32)],
        compiler_params=_params(("parallel", "arbitrary")),
        name="conv_a",
    )(u_in, hist, _round_bf16(w), b.reshape(1, ch), lg.reshape(1, ch), lb.reshape(1, ch))


def _ssd_kernel(xbc_ref, z_ref, dt_ref, dtt_ref, hist_ref, h0_ref, cw_ref, cb_ref, dtb_ref, dtbt_ref,
                a_ref, at_ref, dvec_ref, gn_ref, y_ref, hout_ref, sout_ref, ext_ref, h_ref, yacc_ref, tail_ref,
                *, q, nc, inner, heads):
    c = pl.program_id(1)

    @pl.when(c == 0)
    def _():
        tail_ref[...] = hist_ref[...]
        ext_ref[0:HIST_S, :] = _round_bf16(hist_ref[...])
        h_ref[...] = h0_ref[...]

    xbc = xbc_ref[...]
    ext_ref[HIST_S:HIST_S + q, :] = _round_bf16(xbc)
    _push_tail(tail_ref, xbc, q, HIST_S)
    hout_ref[...] = tail_ref[...]
    off = HIST_S - (SSM_CONV_W - 1)
    acc = jnp.broadcast_to(cb_ref[...], (q, cb_ref.shape[1]))
    for k in range(SSM_CONV_W):
        acc = acc + cw_ref[k:k + 1, :] * ext_ref[off + k:off + k + q, :]
    xc = _silu(acc)
    ext_ref[0:HIST_S, :] = ext_ref[q:q + HIST_S, :]

    gw = SSM_GROUPS * SSM_STATE
    xs = xc[:, :inner]
    bm = xc[:, inner:inner + gw]
    cm = xc[:, inner + gw:inner + 2 * gw]

    dt = _softplus(dt_ref[...] + dtb_ref[...])
    dtt = _softplus(dtt_ref[...] + dtbt_ref[...])
    da = dt * a_ref[...]
    dat = dtt * at_ref[...]
    ri = lax.broadcasted_iota(jnp.int32, (q, q), 0)
    ci = lax.broadcasted_iota(jnp.int32, (q, q), 1)
    causal = ri >= ci
    tri = causal.astype(F32)
    trit = (ri <= ci).astype(F32)
    acum = jnp.dot(tri, da, precision=HIGHEST, preferred_element_type=F32)
    acumt = jnp.dot(dat, trit, precision=HIGHEST, preferred_element_type=F32)
    alast = acum[q - 1:q, :]
    hpg = heads // SSM_GROUPS
    for g in range(SSM_GROUPS):
        bg = bm[:, g * SSM_STATE:(g + 1) * SSM_STATE].astype(BF16)
        cg = cm[:, g * SSM_STATE:(g + 1) * SSM_STATE].astype(BF16)
        cbm = _dot_nt(cg, bg)
        for e in range(hpg):
            hh = g * hpg + e
            a_col = acum[:, hh:hh + 1]
            a_row = acumt[hh:hh + 1, :]
            decay = jnp.exp(jnp.where(causal, a_col - a_row, NEG))
            wm = cbm * decay * dtt[hh:hh + 1, :]
            xh = xs[:, hh * SSM_HEAD_DIM:(hh + 1) * SSM_HEAD_DIM]
            hprev = h_ref[hh]
            y_h = (jnp.dot(wm.astype(BF16), xh.astype(BF16), preferred_element_type=F32)
                   + _dot_nt(cg, hprev.astype(BF16)) * jnp.exp(a_col))
            yacc_ref[:, hh * SSM_HEAD_DIM:(hh + 1) * SSM_HEAD_DIM] = y_h
            al = alast[:, hh:hh + 1]
            w_end = jnp.exp(al - a_col) * dt[:, hh:hh + 1]
            h_ref[hh] = jnp.exp(al) * hprev + _dot_tn((xh * w_end).astype(BF16), bg)

    y = (yacc_ref[...] + xs * dvec_ref[...]) * _silu(z_ref[...])
    gsz = inner // SSM_GROUPS
    for g in range(SSM_GROUPS):
        v = y[:, g * gsz:(g + 1) * gsz]
        ms = jnp.mean(v * v, axis=-1, keepdims=True)
        y_ref[:, g * gsz:(g + 1) * gsz] = (v * lax.rsqrt(ms + EPS) * gn_ref[:, g * gsz:(g + 1) * gsz]).astype(y_ref.dtype)

    @pl.when(c == nc - 1)
    def _():
        sout_ref[...] = h_ref[...]


def ssd_branch(xbc, z, small, small_t, row_off, hist, h0, cw, cb, dt_bias, a_log, d_skip, norm_g, batch, t_len, out_dtype,
               n_out=None):
    conv_dim = xbc.shape[1]
    inner = z.shape[1]
    heads = dt_bias.shape[0]
    assert heads == DT_COLS
    q = math.gcd(t_len, SSM_CHUNK)
    nc = t_len // q
    assert row_off % q == 0
    r0 = row_off // q
    pad = LANES - heads
    dtb = jnp.pad(dt_bias, (0, pad)).reshape(1, LANES)
    a_neg = jnp.pad(-jnp.exp(a_log), (0, pad)).reshape(1, LANES)
    dvec = jnp.repeat(d_skip, SSM_HEAD_DIM).reshape(1, inner)
    kern = functools.partial(_ssd_kernel, q=q, nc=nc, inner=inner, heads=heads)
    row = lambda w: pl.BlockSpec((1, w), lambda bi, ci: (0, 0))
    col = lambda: pl.BlockSpec((LANES, 1), lambda bi, ci: (0, 0))
    tok = lambda w: pl.BlockSpec((q, w), lambda bi, ci: (r0 + bi * nc + ci, 0))
    return pl.pallas_call(
        kern,
        grid=(batch, nc),
        in_specs=[tok(conv_dim), tok(inner), tok(LANES),
                  pl.BlockSpec((None, LANES, q), lambda bi, ci: (bi, 0, ci)),
                  pl.BlockSpec((None, HIST_S, conv_dim), lambda bi, ci: (bi, 0, 0)),
                  pl.BlockSpec((None, heads, SSM_HEAD_DIM, SSM_STATE), lambda bi, ci: (bi, 0, 0, 0)),
                  pl.BlockSpec((SSM_CONV_W, conv_dim), lambda bi, ci: (0, 0)), row(conv_dim),
                  row(LANES), col(), row(LANES), col(), row(inner), row(inner)],
        out_specs=[pl.BlockSpec((q, inner), lambda bi, ci: (bi * nc + ci, 0)),
                   pl.BlockSpec((None, HIST_S, conv_dim), lambda bi, ci: (bi, 0, 0)),
                   pl.BlockSpec((None, heads, SSM_HEAD_DIM, SSM_STATE), lambda bi, ci: (bi, 0, 0, 0))],
        out_shape=[jax.ShapeDtypeStruct((n_out or batch * t_len, inner), out_dtype),
                   jax.ShapeDtypeStruct((batch, HIST_S, conv_dim), F32),
                   jax.ShapeDtypeStruct((batch, heads, SSM_HEAD_DIM, SSM_STATE), F32)],
        scratch_shapes=[pltpu.VMEM((q + HIST_S, conv_dim), F32),
                        pltpu.VMEM((heads, SSM_HEAD_DIM, SSM_STATE), F32),
                        pltpu.VMEM((q, inner), F32), pltpu.VMEM((HIST_S, conv_dim), F32)],
        compiler_params=_params(("parallel", "arbitrary")),
        name="ssd",
    )(xbc, z, small, small_t, hist, h0, _round_bf16(cw), cb.reshape(1, conv_dim), dtb, dtb.reshape(LANES, 1),
      a_neg, a_neg.reshape(LANES, 1), dvec, norm_g.reshape(1, inner))


def _bucket_np(d):
    n = np.maximum(d, 0)
    exact = REL_BUCKETS // 2
    nf = np.maximum(n, 1).astype(np.float32)
    large = exact + (np.log(nf / np.float32(exact)) / np.float32(math.log(REL_MAX_DIST / exact))
                     * np.float32(REL_BUCKETS - exact)).astype(np.int32)
    return np.where(n < exact, n, np.minimum(large, REL_BUCKETS - 1))


def _near_distances():
    i = np.arange(QCHUNK)[:, None]
    j = np.arange(LANES)[None, :]
    diag = i - j
    prev = QCHUNK + i - j
    band = np.where(j < 4, i + (BLK + 1) - BLK * j, 0)
    return np.stack([diag, prev, band])


def _expand_kernel(tab_ref, oh_ref, o_ref):
    o_ref[...] = jnp.dot(tab_ref[...], oh_ref[...], precision=HIGHEST, preferred_element_type=F32)


def expand_bias(rel_bias, dist):
    flat = dist.reshape(-1)
    m = flat.shape[0]
    assert m % LANES == 0
    onehot = (np.arange(REL_BUCKETS)[:, None] == _bucket_np(flat)[None, :]).astype(np.float32)
    tm = _tile(m, 8192, LANES)
    heads = rel_bias.shape[1]
    out = pl.pallas_call(
        _expand_kernel,
        grid=(m // tm,),
        in_specs=[pl.BlockSpec((heads, REL_BUCKETS), lambda i: (0, 0)), pl.BlockSpec((REL_BUCKETS, tm), lambda i: (0, i))],
        out_specs=pl.BlockSpec((heads, tm), lambda i: (0, i)),
        out_shape=jax.ShapeDtypeStruct((heads, m), F32),
        compiler_params=_params(("parallel",)),
        name="bias_expand",
    )(rel_bias.T, jnp.asarray(onehot))
    return out.reshape((heads,) + dist.shape)


def _select_blocks(imp, lane):
    sel = jnp.zeros(imp.shape, F32)
    big = jnp.int32(1 << 30)
    v = imp
    for _ in range(TOPK):
        mx = jnp.max(v, axis=-1, keepdims=True)
        idx = jnp.min(jnp.where(v == mx, lane, big), axis=-1, keepdims=True)
        hit = lane == idx
        sel = jnp.where(hit, 1.0, sel)
        v = jnp.where(hit, -3e38, v)
    return sel


def _rep4(x):
    return jnp.concatenate([x, x, x, x], axis=0)


def _qgroup(q_ref, g):
    return jnp.concatenate([q_ref[:, (g * GQA + r) * HEAD_DIM:(g * GQA + r + 1) * HEAD_DIM] for r in range(GQA)], axis=0)


def _gate_col(sg, br, g):
    c0 = GATE_COL0 + br * N_HEADS + g * GQA
    return jnp.concatenate([sg[:, c0 + r:c0 + r + 1] for r in range(GQA)], axis=0)


def _rows_means_kernel(x_ref, o_ref):
    nb, w = o_ref.shape
    o_ref[...] = jnp.sum(x_ref[...].reshape(nb, BLK, w), axis=1) * (1.0 / BLK)


def prompt_block_means(kv, batch, t_len):
    w = 2 * KV_GROUPS * HEAD_DIM
    nb = t_len // BLK
    return pl.pallas_call(
        _rows_means_kernel,
        grid=(batch,),
        in_specs=[pl.BlockSpec((t_len, w), lambda b: (b, 0))],
        out_specs=pl.BlockSpec((None, nb, w), lambda b: (b, 0, 0)),
        out_shape=jax.ShapeDtypeStruct((batch, nb, w), F32),
        compiler_params=_params(("parallel",)),
        name="cmp_means_prompt",
    )(kv)


def _page_specs(pps, half, layer):
    def spec(k):
        return pl.BlockSpec((None, None, 2, KV_GROUPS, HEAD_DIM, 2 * BLK),
                            lambda b, j, pt: (pt[b, j * pps + k], layer, half, 0, 0, 0))
    return [spec(k) for k in range(pps)]


def _page_means_kernel(pt_ref, avg_ref, *refs):
    pages, o_ref = refs[:-1], refs[-1]
    j = pl.program_id(1)

    @pl.when(j == 0)
    def _():
        o_ref[...] = jnp.zeros(o_ref.shape, F32)

    avg = avg_ref[...]
    for kind in range(2):
        for g in range(KV_GROUPS):
            cat = jnp.concatenate([pg[kind, g] for pg in pages], axis=1)
            hi = cat.astype(BF16)
            lo = (cat - hi.astype(F32)).astype(BF16)
            o_ref[kind, g] += (jnp.dot(hi, avg, preferred_element_type=F32) + jnp.dot(lo, avg, preferred_element_type=F32))


def sample_block_means(cache_t, page_table, layer, pps, nbl):
    batch, n_pages = page_table.shape
    n_steps = n_pages // pps
    span = pps * 2 * BLK
    blk_of_row = np.arange(n_steps * span) // BLK
    avg = np.where(blk_of_row[:, None] == np.arange(nbl)[None, :], 1.0 / BLK, 0.0).reshape(n_steps, span, nbl)
    return pl.pallas_call(
        _page_means_kernel,
        grid_spec=pltpu.PrefetchScalarGridSpec(
            num_scalar_prefetch=1, grid=(batch, n_steps),
            in_specs=[pl.BlockSpec((None, span, nbl), lambda b, j, pt: (j, 0, 0))] + _page_specs(pps, 0, layer),
            out_specs=pl.BlockSpec((None, 2, KV_GROUPS, HEAD_DIM, nbl), lambda b, j, pt: (b, 0, 0, 0, 0))),
        out_shape=jax.ShapeDtypeStruct((batch, 2, KV_GROUPS, HEAD_DIM, nbl), F32),
        compiler_params=_params(("parallel", "arbitrary")),
        name="cmp_means",
    )(page_table, jnp.asarray(avg, BF16), *([cache_t] * pps))


def _nsa_prompt_kernel(q_ref, ksel_ref, vsel_ref, kwin_ref, vwin_ref, kc_ref, gn_ref, tab_ref, band_ref, cst_ref, e_ref, o_ref,
                       s_ref, sel_ref, oc_ref, mrun_ref, acc_ref, *, nb):
    c = pl.program_id(1)
    qc = QCHUNK
    rows = GQA * qc
    kw = KV_GROUPS * HEAD_DIM
    span = SUPER * qc
    lane = lax.broadcasted_iota(jnp.int32, (rows, LANES), 1)
    qi = lax.broadcasted_iota(jnp.int32, (rows, LANES), 0) & (qc - 1)
    okc = (c * qc + qi - BLK * lane - (BLK - 1)) >= 0
    lane1 = lax.broadcasted_iota(jnp.int32, (qc, LANES), 1)
    qi1 = lax.broadcasted_iota(jnp.int32, (qc, LANES), 0)
    okc1 = (c * qc + qi1 - BLK * lane1 - (BLK - 1)) >= 0
    cur = 2 * c + (qi1 >= BLK).astype(jnp.int32)

    for g in range(KV_GROUPS):
        qg = _qgroup(q_ref, g)
        cst = cst_ref[g]
        kcg = kc_ref[:, g * HEAD_DIM:(g + 1) * HEAD_DIM].astype(BF16)
        vcg = kc_ref[:, kw + g * HEAD_DIM:kw + (g + 1) * HEAD_DIM].astype(BF16)
        shift = lax.rem(2 * c - 2 + LANES, LANES)
        band = pltpu.roll(band_ref[g], shift, axis=1)
        bias_c = jnp.where(lane < 2 * c - 2, cst, band)
        s = jnp.where(okc, _dot_nt(qg, kcg) + bias_c, NEG)
        e = jnp.exp(s - jnp.max(s, axis=-1, keepdims=True))
        p = jnp.where(okc, e / jnp.sum(e, axis=-1, keepdims=True), 0.0)
        oc_ref[g] = jnp.dot(p.astype(BF16), vcg, preferred_element_type=F32)
        psum = p[0:qc] + p[qc:2 * qc] + p[2 * qc:3 * qc] + p[3 * qc:4 * qc]
        imp = jnp.where(lane1 == cur, FORCE, jnp.where(okc1, psum, -1.0))
        imp = jnp.where(lane1 < nb, imp, -3e38)
        selm = jnp.where(lane1 <= cur, _select_blocks(imp, lane1), 0.0)
        sel_ref[g] = (jnp.dot(selm.astype(BF16), e_ref[...], preferred_element_type=F32) - 1.0) * (-NEG)

    sg = jax.nn.sigmoid(gn_ref[...])
    n_super = c // SUPER + 1
    w_tiles = WINDOW // qc + 1
    st0 = jnp.maximum(c - (w_tiles - 1), 0)
    for g in range(KV_GROUPS):
        qg = _qgroup(q_ref, g)
        mrun_ref[...] = jnp.full((rows, LANES), NEG, F32)

        def scores(st, carry):
            k = ksel_ref[pl.ds(pl.multiple_of(st * span, span), span), g * HEAD_DIM:(g + 1) * HEAD_DIM]
            s = _dot_nt(qg, k)
            mrun = mrun_ref[...]
            for j in range(SUPER):
                kt = st * SUPER + j
                idx = jnp.where(kt == c, 0, jnp.where(kt == c - 1, 1, 2))
                off = pl.multiple_of(kt * qc, qc)
                sj = s[:, j * qc:(j + 1) * qc] + tab_ref[g, idx] + _rep4(sel_ref[g, :, pl.ds(off, qc)])
                s_ref[:, pl.ds(off, qc)] = sj
                mrun = jnp.maximum(mrun, sj)
            mrun_ref[...] = mrun
            return carry

        lax.fori_loop(0, n_super, scores, 0)
        m = jnp.max(mrun_ref[...], axis=-1, keepdims=True)
        mrun_ref[...] = jnp.zeros((rows, LANES), F32)

        def exps(st, carry):
            off = pl.multiple_of(st * span, span)
            e = jnp.exp(s_ref[:, pl.ds(off, span)] - m)
            s_ref[:, pl.ds(off, span)] = e
            lrun = mrun_ref[...]
            for j in range(SUPER):
                lrun = lrun + e[:, j * qc:(j + 1) * qc]
            mrun_ref[...] = lrun
            return carry

        lax.fori_loop(0, n_super, exps, 0)
        denom = jnp.sum(mrun_ref[...], axis=-1, keepdims=True)
        acc_ref[...] = jnp.zeros((rows, HEAD_DIM), F32)

        def values(st, carry):
            off = pl.multiple_of(st * span, span)
            p = (s_ref[:, pl.ds(off, span)] / denom).astype(BF16)
            acc_ref[...] += jnp.dot(p, vsel_ref[pl.ds(off, span), g * HEAD_DIM:(g + 1) * HEAD_DIM], preferred_element_type=F32)
            return carry

        lax.fori_loop(0, n_super, values, 0)
        o_s = acc_ref[...]
        r0 = pl.multiple_of(st0 * qc, qc)
        s = _dot_nt(qg, kwin_ref[pl.ds(r0, w_tiles * qc), g * HEAD_DIM:(g + 1) * HEAD_DIM])
        parts = []
        for w in range(w_tiles):
            kt = st0 + w
            idx = jnp.where(kt > c, 4, jnp.where(kt == c, 0, jnp.where(kt == c - 1, 1,
                                                                         jnp.where(kt == c - (w_tiles - 1), 3, 2))))
            parts.append(s[:, w * qc:(w + 1) * qc] + tab_ref[g, idx])
        sw = jnp.concatenate(parts, axis=1)
        ew = jnp.exp(sw - jnp.max(sw, axis=-1, keepdims=True))
        pw = (ew / jnp.sum(ew, axis=-1, keepdims=True)).astype(BF16)
        o_w = jnp.dot(pw, vwin_ref[pl.ds(r0, w_tiles * qc), g * HEAD_DIM:(g + 1) * HEAD_DIM],
                      preferred_element_type=F32)
        tot = _gate_col(sg, 0, g) * oc_ref[g] + _gate_col(sg, 1, g) * o_s + _gate_col(sg, 2, g) * o_w
        for r in range(GQA):
            o_ref[:, (g * GQA + r) * HEAD_DIM:(g * GQA + r + 1) * HEAD_DIM] = tot[r * qc:(r + 1) * qc].astype(o_ref.dtype)


def nsa_prompt(q, kvb, kc, small, tabs, band, cst, batch, t_len, n_out=None):
    assert t_len % (SUPER * QCHUNK) == 0 and t_len // BLK <= LANES and t_len >= WINDOW + QCHUNK
    nch = t_len // QCHUNK
    nb = t_len // BLK
    rows = GQA * QCHUNK
    hd = N_HEADS * HEAD_DIM
    emat = (np.arange(LANES)[:, None] == (np.arange(t_len)[None, :] // BLK)).astype(np.float32)
    kern = functools.partial(_nsa_prompt_kernel, nb=nb)
    full = lambda shp: pl.BlockSpec(shp, lambda b, c: (0,) * len(shp))
    kind = lambda k: pl.BlockSpec((t_len, KV_GROUPS * HEAD_DIM), lambda b, c: (b, k))
    return pl.pallas_call(
        kern,
        grid=(batch, nch),
        in_specs=[pl.BlockSpec((QCHUNK, hd), lambda b, c: (b * nch + c, 0)),
                  kind(2), kind(3), kind(4), kind(5),
                  pl.BlockSpec((None, LANES, kc.shape[2]), lambda b, c: (b, 0, 0)),
                  pl.BlockSpec((QCHUNK, LANES), lambda b, c: (b * nch + c, 0)),
                  full((KV_GROUPS, 5, rows, LANES)), full((KV_GROUPS, rows, LANES)), full((KV_GROUPS, rows, 1)),
                  full((LANES, t_len))],
        out_specs=pl.BlockSpec((QCHUNK, hd), lambda b, c: (b * nch + c, 0)),
        out_shape=jax.ShapeDtypeStruct((n_out or batch * t_len, hd), BF16),
        scratch_shapes=[pltpu.VMEM((rows, t_len), F32),
                        pltpu.VMEM((KV_GROUPS, QCHUNK, t_len), F32),
                        pltpu.VMEM((KV_GROUPS, rows, HEAD_DIM), F32),
                        pltpu.VMEM((rows, LANES), F32), pltpu.VMEM((rows, HEAD_DIM), F32)],
        compiler_params=_params(("parallel", "arbitrary")),
        name="nsa_prompt",
    )(q, kvb, kvb, kvb, kvb, kc, small, tabs, band, cst, jnp.asarray(emat, BF16))


def _nsa_sample_kernel(pt_ref, q_ref, kvn_ref, gn_ref, kc_ref, win_ref, near_ref, wtab_ref, cst_ref, e_ref,
                       *refs, pps, n_steps, past, ts, nbl):
    pages = refs[:pps]
    o_ref = refs[pps]
    s_ref, sel_ref, oc_ref, vbuf_ref, kn_ref = refs[pps + 1:]
    j = pl.program_id(1)
    rows = GQA * ts
    kw = KV_GROUPS * HEAD_DIM
    nb_past = past // BLK
    pg = 2 * BLK
    lane = lax.broadcasted_iota(jnp.int32, (rows, LANES), 1)
    ti = lax.broadcasted_iota(jnp.int32, (rows, LANES), 0) & (ts - 1)

    def qgroup(g):
        return _qgroup(q_ref, g).astype(BF16)

    @pl.when(j == 0)
    def _():
        kn_ref[...] = jnp.zeros(kn_ref.shape, F32)
        kn_ref[0:ts, :] = kvn_ref[:, 2 * kw:6 * kw]
        lane_b = lax.broadcasted_iota(jnp.int32, (rows, nbl), 1)
        t_b = lax.broadcasted_iota(jnp.int32, (rows, nbl), 0) & (ts - 1)
        okc = (past + t_b - BLK * lane_b - (BLK - 1)) >= 0
        lane1 = lax.broadcasted_iota(jnp.int32, (ts, nbl), 1)
        okc1 = (past + lax.broadcasted_iota(jnp.int32, (ts, nbl), 0) - BLK * lane1 - (BLK - 1)) >= 0
        for g in range(KV_GROUPS):
            qg = qgroup(g)
            cst = cst_ref[g]
            band = near_ref[g, 2]
            kcg = kc_ref[0, g].astype(BF16)
            vcg = kc_ref[1, g].astype(BF16)
            bias_c = jnp.where(lane_b == nb_past - 2, band[:, 0:1], jnp.where(lane_b == nb_past - 1, band[:, 1:2], cst))
            s = jnp.where(okc, jnp.dot(qg, kcg, preferred_element_type=F32) + bias_c, NEG)
            e = jnp.exp(s - jnp.max(s, axis=-1, keepdims=True))
            p = jnp.where(okc, e / jnp.sum(e, axis=-1, keepdims=True), 0.0)
            oc_ref[g] = _dot_nt(p.astype(BF16), vcg)
            psum = p[0:ts] + p[ts:2 * ts] + p[2 * ts:3 * ts] + p[3 * ts:4 * ts]
            imp = jnp.where(lane1 == nb_past, FORCE, jnp.where(okc1, psum, -1.0))
            imp = jnp.where(lane1 <= nb_past, imp, -3e38)
            selm = _select_blocks(imp, lane1)
            selm16 = jnp.concatenate([selm, jnp.zeros_like(selm)], axis=0).astype(BF16)
            sel_ref[g] = (jnp.dot(selm16, e_ref[...], preferred_element_type=F32)[0:ts] - 1.0) * (-NEG)

    is_last = j == n_steps - 1
    base = j * (pps * pg)
    qgs = [qgroup(g) for g in range(KV_GROUPS)]
    for k in range(pps):
        off = pl.multiple_of(base + k * pg, pg)
        for g in range(KV_GROUPS):
            vbuf_ref[g, :, pl.ds(off, pg)] = pages[k][1, g].astype(BF16)
            sk = jnp.dot(qgs[g], pages[k][0, g].astype(BF16), preferred_element_type=F32)
            bias = jnp.where(is_last, near_ref[g, 1], cst_ref[g]) if k == pps - 1 else cst_ref[g]
            s_ref[g, :, pl.ds(off, pg)] = sk + bias + _rep4(sel_ref[g, :, pl.ds(off, pg)])

    @pl.when(is_last)
    def _():
        sg = jax.nn.sigmoid(gn_ref[...])
        wl = win_ref.shape[-1]
        lane_w = lax.broadcasted_iota(jnp.int32, (rows, wl + LANES), 1)
        t_w = lax.broadcasted_iota(jnp.int32, (rows, wl + LANES), 0) & (ts - 1)
        okw = jnp.where(lane_w < wl, lane_w - t_w, t_w - (lane_w - wl)) >= 0
        for g in range(KV_GROUPS):
            qg = qgroup(g)
            kn = kn_ref[:, g * HEAD_DIM:(g + 1) * HEAD_DIM].astype(BF16)
            vn = kn_ref[:, kw + g * HEAD_DIM:kw + (g + 1) * HEAD_DIM].astype(BF16)
            s_ref[g, :, past:past + LANES] = jnp.where(lane <= ti, _dot_nt(qg, kn) + near_ref[g, 0], NEG)
            s = s_ref[g]
            e = jnp.exp(s - jnp.max(s, axis=-1, keepdims=True))
            p = e / jnp.sum(e, axis=-1, keepdims=True)
            o_s = (_dot_nt(p[:, :past].astype(BF16), vbuf_ref[g])
                   + jnp.dot(p[:, past:].astype(BF16), vn, preferred_element_type=F32))
            kwin = win_ref[0, g].astype(BF16)
            vwin = win_ref[1, g].astype(BF16)
            knw = kn_ref[:, 2 * kw + g * HEAD_DIM:2 * kw + (g + 1) * HEAD_DIM].astype(BF16)
            vnw = kn_ref[:, 3 * kw + g * HEAD_DIM:3 * kw + (g + 1) * HEAD_DIM].astype(BF16)
            s = jnp.concatenate([jnp.dot(qg, kwin, preferred_element_type=F32), _dot_nt(qg, knw)], axis=1) + wtab_ref[g]
            s = jnp.where(okw, s, NEG)
            e = jnp.exp(s - jnp.max(s, axis=-1, keepdims=True))
            p = e / jnp.sum(e, axis=-1, keepdims=True)
            o_w = (_dot_nt(p[:, :wl].astype(BF16), vwin)
                   + jnp.dot(p[:, wl:].astype(BF16), vnw, preferred_element_type=F32))
            tot = _gate_col(sg, 0, g) * oc_ref[g] + _gate_col(sg, 1, g) * o_s + _gate_col(sg, 2, g) * o_w
            for r in range(GQA):
                o_ref[:, (g * GQA + r) * HEAD_DIM:(g * GQA + r + 1) * HEAD_DIM] = tot[r * ts:(r + 1) * ts]


def nsa_sample(q3, kvn3, small3, kcvc, win_t, near_s, wtab, cst_s, cache_t, page_table, layer, pps):
    batch, ts, hd = q3.shape
    n_pages = page_table.shape[1]
    n_steps = n_pages // pps
    past = n_pages * 2 * BLK
    nbl = kcvc.shape[-1]
    wl = win_t.shape[-1]
    rows = GQA * ts
    kw = KV_GROUPS * HEAD_DIM
    assert ts == SUBLANES and past % QCHUNK == 0 and wl % LANES == 0
    emat = (np.arange(nbl)[:, None] == (np.arange(past)[None, :] // BLK)).astype(np.float32)
    kern = functools.partial(_nsa_sample_kernel, pps=pps, n_steps=n_steps, past=past, ts=ts, nbl=nbl)
    c3 = lambda shp: pl.BlockSpec(shp, lambda b, j, pt: (0,) * len(shp))
    return pl.pallas_call(
        kern,
        grid_spec=pltpu.PrefetchScalarGridSpec(
            num_scalar_prefetch=1, grid=(batch, n_steps),
            in_specs=[pl.BlockSpec((None, ts, hd), lambda b, j, pt: (b, 0, 0)),
                      pl.BlockSpec((None, ts, kvn3.shape[2]), lambda b, j, pt: (b, 0, 0)),
                      pl.BlockSpec((None, ts, LANES), lambda b, j, pt: (b, 0, 0)),
                      pl.BlockSpec((None, 2, KV_GROUPS, HEAD_DIM, nbl), lambda b, j, pt: (b, 0, 0, 0, 0)),
                      pl.BlockSpec((None, None, 2, KV_GROUPS, HEAD_DIM, wl), lambda b, j, pt: (layer, b, 0, 0, 0, 0)),
                      c3((KV_GROUPS, 3, rows, LANES)), c3((KV_GROUPS, rows, wl + LANES)), c3((KV_GROUPS, rows, 1)),
                      c3((nbl, past))] + _page_specs(pps, 1, layer),
            out_specs=pl.BlockSpec((None, ts, hd), lambda b, j, pt: (b, 0, 0)),
            scratch_shapes=[pltpu.VMEM((KV_GROUPS, rows, past + LANES), F32),
                            pltpu.VMEM((KV_GROUPS, ts, past), F32),
                            pltpu.VMEM((KV_GROUPS, rows, HEAD_DIM), F32),
                            pltpu.VMEM((KV_GROUPS, HEAD_DIM, past), BF16),
                            pltpu.VMEM((LANES, 4 * kw), F32)]),
        out_shape=jax.ShapeDtypeStruct((batch, ts, hd), F32),
        compiler_params=_params(("parallel", "arbitrary")),
        name="nsa_sample",
    )(page_table, q3, kvn3, small3, kcvc, win_t, near_s, wtab, cst_s, jnp.asarray(emat, BF16), *([cache_t] * pps))


def _merge_kernel(c_ref, o_ref, y_ref, g0_ref, g1_ref, g2_ref, wa_ref, wn_ref, ws_ref, out_ref):
    oa = jnp.dot(c_ref[...], wa_ref[...], preferred_element_type=F32)
    ob = jnp.dot(o_ref[...], wn_ref[...], preferred_element_type=F32)
    oc = jnp.dot(y_ref[...], ws_ref[...], preferred_element_type=F32)
    mixed = (jax.nn.sigmoid(g0_ref[...]) * oa + jax.nn.sigmoid(g1_ref[...]) * ob + jax.nn.sigmoid(g2_ref[...]) * oc)
    out_ref[...] = mixed.astype(out_ref.dtype)


def merge_branches(c, o, y, gmix, wa, wn, ws):
    n = c.shape[0]
    d = wa.shape[1]
    tm = _tile(n, 768, BF16_ROWS)
    tn = _tile(d, 512, LANES)
    nj = d // tn
    act = lambda a: pl.BlockSpec((tm, a.shape[1]), lambda i, j: (i, 0))
    wsp = lambda w: pl.BlockSpec((w.shape[0], tn), lambda i, j: (0, j))
    gsp = lambda br: pl.BlockSpec((tm, tn), lambda i, j: (i, br * nj + j))
    return pl.pallas_call(
        _merge_kernel,
        grid=(n // tm, nj),
        in_specs=[act(c), act(o), act(y), gsp(0), gsp(1), gsp(2), wsp(wa), wsp(wn), wsp(ws)],
        out_specs=pl.BlockSpec((tm, tn), lambda i, j: (i, j)),
        out_shape=jax.ShapeDtypeStruct((n, d), BF16),
        compiler_params=_params(("parallel", "parallel")),
        name="merge",
    )(c, o, y, gmix, gmix, gmix, wa, wn, ws)


def _outproj_kernel(a_ref, w_ref, x_ref, o_ref):
    o_ref[...] = x_ref[...] + jnp.dot(a_ref[...], w_ref[...], preferred_element_type=F32)


def out_projection(a, w, x):
    n, k = a.shape
    d = w.shape[1]
    tm = _tile(n, 768, BF16_ROWS)
    tn = _tile(d, 1024, LANES)
    return pl.pallas_call(
        _outproj_kernel,
        grid=(n // tm, d // tn),
        in_specs=[pl.BlockSpec((tm, k), lambda i, j: (i, 0)), pl.BlockSpec((k, tn), lambda i, j: (0, j)),
                  pl.BlockSpec((tm, tn), lambda i, j: (i, j))],
        out_specs=pl.BlockSpec((tm, tn), lambda i, j: (i, j)),
        out_shape=jax.ShapeDtypeStruct((n, d), F32),
        compiler_params=_params(("parallel", "parallel")),
        name="out_proj",
    )(a, w, x)


def _router_kernel(x_ref, g_ref, w_ref, b_ref, h_ref, r_ref):
    x = x_ref[...]
    ms = jnp.mean(x * x, axis=-1, keepdims=True)
    h = (x * lax.rsqrt(ms + EPS) * g_ref[...]).astype(BF16)
    h_ref[...] = h
    logits = jnp.dot(h, w_ref[...], preferred_element_type=F32) + b_ref[...]
    lane = lax.broadcasted_iota(jnp.int32, logits.shape, 1)
    big = jnp.int32(1 << 30)
    gl = jnp.where(lane < MOE_GROUPS, logits, NEG)
    gmax = jnp.max(gl, axis=-1, keepdims=True)
    gidx = jnp.min(jnp.where(gl == gmax, lane, big), axis=-1, keepdims=True)
    g_w = 1.0 / jnp.sum(jnp.exp(gl - gmax), axis=-1, keepdims=True)
    eid = lane - MOE_GROUPS
    in_grp = (eid >= 0) & (eid < MOE_GROUPS * MOE_EPG) & ((eid >> 3) == gidx)
    el = jnp.where(in_grp, logits, NEG)
    m1 = jnp.max(el, axis=-1, keepdims=True)
    i1 = jnp.min(jnp.where(el == m1, lane, big), axis=-1, keepdims=True)
    el2 = jnp.where(lane == i1, NEG, el)
    m2 = jnp.max(el2, axis=-1, keepdims=True)
    i2 = jnp.min(jnp.where(el2 == m2, lane, big), axis=-1, keepdims=True)
    p2 = jnp.exp(m2 - m1)
    w1 = g_w / (1.0 + p2)
    w2 = g_w * p2 / (1.0 + p2)
    r_ref[...] = jnp.where(lane == 0, (i1 - MOE_GROUPS).astype(F32),
                           jnp.where(lane == 1, (i2 - MOE_GROUPS).astype(F32),
                                     jnp.where(lane == 2, w1, jnp.where(lane == 3, w2, 0.0))))


def moe_router(x, g, wr, br):
    n, d = x.shape
    tm = _tile(n, 256, BF16_ROWS)
    return pl.pallas_call(
        _router_kernel,
        grid=(n // tm,),
        in_specs=[pl.BlockSpec((tm, d), lambda i: (i, 0)), pl.BlockSpec((1, d), lambda i: (0, 0)),
                  pl.BlockSpec((d, LANES), lambda i: (0, 0)), pl.BlockSpec((1, LANES), lambda i: (0, 0))],
        out_specs=[pl.BlockSpec((tm, d), lambda i: (i, 0)), pl.BlockSpec((tm, LANES), lambda i: (i, 0))],
        out_shape=[jax.ShapeDtypeStruct((n, d), BF16), jax.ShapeDtypeStruct((n, LANES), F32)],
        compiler_params=_params(("parallel",)),
        name="moe_router",
    )(x, g.reshape(1, d), wr, br)


def _expert_kernel(te_ref, nu_ref, x_ref, rw_ref, wg_ref, wu_ref, wd_ref, y_ref, wgb_ref, wub_ref, wdb_ref):
    t = pl.program_id(0)
    fresh = jnp.logical_or(t == 0, te_ref[t] != te_ref[jnp.maximum(t - 1, 0)])

    @pl.when(jnp.logical_and(fresh, t < nu_ref[0]))
    def _():
        wgb_ref[...] = wg_ref[...].astype(BF16)
        wub_ref[...] = wu_ref[...].astype(BF16)
        wdb_ref[...] = wd_ref[...].astype(BF16)

    @pl.when(t < nu_ref[0])
    def _():
        x = x_ref[...]
        a = jnp.dot(x, wgb_ref[...], preferred_element_type=F32)
        u = jnp.dot(x, wub_ref[...], preferred_element_type=F32)
        hid = (_silu(a) * u * rw_ref[...]).astype(BF16)
        y_ref[...] = jnp.dot(hid, wdb_ref[...], preferred_element_type=F32)


def moe_experts(xs, row_w, tile_e, n_used, wg, wu, wd, layer, tme):
    r, d = xs.shape
    ff = wg.shape[3]
    nt = r // tme
    return pl.pallas_call(
        _expert_kernel,
        grid_spec=pltpu.PrefetchScalarGridSpec(
            num_scalar_prefetch=2, grid=(nt,),
            in_specs=[pl.BlockSpec((tme, d), lambda t, te, nu: (t, 0)),
                      pl.BlockSpec((tme, 1), lambda t, te, nu: (t, 0)),
                      pl.BlockSpec((None, None, d, ff), lambda t, te, nu: (layer, te[t], 0, 0)),
                      pl.BlockSpec((None, None, d, ff), lambda t, te, nu: (layer, te[t], 0, 0)),
                      pl.BlockSpec((None, None, ff, d), lambda t, te, nu: (layer, te[t], 0, 0))],
            out_specs=pl.BlockSpec((tme, d), lambda t, te, nu: (t, 0)),
            scratch_shapes=[pltpu.VMEM((d, ff), BF16), pltpu.VMEM((d, ff), BF16), pltpu.VMEM((ff, d), BF16)]),
        out_shape=jax.ShapeDtypeStruct((r, d), F32),
        compiler_params=_params(("arbitrary",)),
        name="moe_experts",
    )(tile_e, n_used, xs, row_w, wg, wu, wd)


def moe_layer(x, g, wrg, brg, wre, bre, wg, wu, wd, layer):
    n, d = x.shape
    n_exp = wg.shape[1]
    wr = jnp.concatenate([wrg, wre.transpose(1, 0, 2).reshape(d, n_exp)], axis=1)
    wr = jnp.pad(wr, ((0, 0), (0, LANES - wr.shape[1])))
    br = jnp.pad(jnp.concatenate([brg, bre.reshape(-1)]), (0, LANES - MOE_GROUPS - n_exp)).reshape(1, LANES)
    h, route = moe_router(x, g, wr.astype(BF16), br)
    tme = 256
    e_flat = route[:, 0:MOE_TOPK].astype(jnp.int32).T.reshape(-1)
    w_flat = route[:, MOE_TOPK:2 * MOE_TOPK].T.reshape(-1)
    na = e_flat.shape[0]
    nt = -(-na // tme) + n_exp
    order = jnp.argsort(e_flat, stable=True)
    counts = jnp.sum(e_flat[:, None] == jnp.arange(n_exp)[None, :], axis=0).astype(jnp.int32)
    pc = ((counts + tme - 1) // tme) * tme
    pend = jnp.cumsum(pc)
    pstart = pend - pc
    ustart = jnp.cumsum(counts) - counts
    se = e_flat[order]
    dest = pstart[se] + jnp.arange(na, dtype=jnp.int32) - ustart[se]
    row_tok = jnp.zeros((nt * tme,), jnp.int32).at[dest].set((order % n).astype(jnp.int32))
    row_w = jnp.zeros((nt * tme,), F32).at[dest].set(w_flat[order])
    dest_flat = jnp.zeros((na,), jnp.int32).at[order].set(dest)
    tile_e = jnp.minimum(jnp.searchsorted(pend, jnp.arange(nt, dtype=jnp.int32) * tme, side="right"), n_exp - 1).astype(jnp.int32)
    n_used = (pend[-1] // tme).astype(jnp.int32).reshape(1)
    xs = jnp.take(h, row_tok, axis=0)
    ys = moe_experts(xs, row_w.reshape(-1, 1), tile_e, n_used, wg, wu, wd, layer, tme)
    return x + jnp.take(ys, dest_flat[:n], axis=0) + jnp.take(ys, dest_flat[n:], axis=0)


def kernel(x_prompt, x_sample, cache_nsa_kv, cache_nsa_win, state_conv_a, state_conv_ssm, state_ssm, page_table,
           rel_bias, norm_mix_g, w_in, conv_a_w, conv_a_b, ln_a_g, ln_a_b, w_a_out, w_nsa_out, conv_s_w, conv_s_b,
           dt_bias, a_log, d_skip, ssm_norm_g, w_ssm_out, w_out, norm_ffn_g, w_router_group, b_router_group,
           w_router_expert, b_router_expert, w_gate, w_up, w_down, final_norm_g):
    bp, tp, d = x_prompt.shape
    bs, ts, _ = x_sample.shape
    depth = w_in.shape[0]
    n_p, n_s = bp * tp, bs * ts
    ch = conv_a_w.shape[2]
    conv_dim = conv_s_w.shape[2]
    heads_ssm = dt_bias.shape[1]
    inner = heads_ssm * SSM_HEAD_DIM
    hd = N_HEADS * HEAD_DIM
    kw = KV_GROUPS * HEAD_DIM
    kvw = KV_KINDS * kw
    n_pages = page_table.shape[1]
    page = cache_nsa_kv.shape[2]
    past = n_pages * page
    wl = cache_nsa_win.shape[2]
    assert page == 2 * BLK and wl == min(WINDOW, past) and tp >= WINDOW
    pps = _tile(n_pages, 16, 1)
    rows_p = GQA * QCHUNK

    splits = (2 * ch, hd, kvw, 3 * N_HEADS, inner, conv_dim, heads_ssm, 3 * d)
    offs = np.concatenate([[0], np.cumsum(splits)])
    o_u, o_q, o_kv, o_gn, o_z, o_x, o_dt, o_gm = [int(v) for v in offs[:-1]]

    near_d = _near_distances()
    tj = np.arange(wl + LANES)[None, :]
    win_d = np.where(tj < wl + ts, wl + np.arange(ts)[:, None] - tj, 0)
    near_f = expand_bias(rel_bias, near_d)
    wtab_f = expand_bias(rel_bias, win_d)
    near_g = near_f.reshape(KV_GROUPS, GQA, 3, QCHUNK, LANES).transpose(0, 2, 1, 3, 4)
    near_p = near_g.reshape(KV_GROUPS, 3, rows_p, LANES)
    near_s = near_g[:, :, :, :ts].reshape(KV_GROUPS, 3, GQA * ts, LANES)
    wtab = wtab_f.reshape(KV_GROUPS, GQA * ts, wl + LANES)
    far = rel_bias[REL_BUCKETS - 1].reshape(KV_GROUPS, GQA, 1)
    cst_p = jnp.repeat(far, QCHUNK, axis=1).reshape(KV_GROUPS, rows_p, 1)
    cst_s = jnp.repeat(far, ts, axis=1).reshape(KV_GROUPS, GQA * ts, 1)
    ii = (np.arange(rows_p) % QCHUNK)[:, None]
    jj = np.arange(LANES)[None, :]
    cst_t = jnp.broadcast_to(cst_p, (KV_GROUPS, rows_p, LANES))
    tabs_p = jnp.stack([jnp.where(jj <= ii, near_p[:, 0], NEG), near_p[:, 1], cst_t,
                        jnp.where(jj >= ii, cst_t, NEG), jnp.full((KV_GROUPS, rows_p, LANES), NEG, F32)], axis=1)

    w_in_t = w_in.transpose(0, 2, 1)
    cache_t = cache_nsa_kv.transpose(0, 1, 3, 4, 5, 2)
    win_t = cache_nsa_win.transpose(0, 1, 3, 4, 5, 2)
    zeros_hist_a = jnp.zeros((bp, HIST_A, ch), F32)
    zeros_hist_s = jnp.zeros((bp, HIST_S, conv_dim), F32)
    zeros_ssm = jnp.zeros((bp, heads_ssm, SSM_HEAD_DIM, SSM_STATE), F32)
    nbl = -(-(past // BLK + 1) // LANES) * LANES

    x = jnp.concatenate([x_prompt.reshape(n_p, d), x_sample.reshape(n_s, d)], axis=0)
    kv_p, kv_s, win_p, win_s, ha_p, ha_s, hs_p, hs_s, st_p, st_s = ([] for _ in range(10))
    for l in range(depth):
        h = rmsnorm(x, norm_mix_g[l], BF16)
        (u_in,) = matmul(h, w_in_t, l, o_u, 2 * ch, [F32])
        (q,) = matmul(h, w_in_t, l, o_q, hd, [BF16], scale=HEAD_DIM ** -0.5)
        kv, kvb = matmul(h, w_in_t, l, o_kv, kvw, [F32, BF16])
        w_small = jnp.concatenate([w_in_t[l, o_dt:o_gm], w_in_t[l, o_gn:o_z],
                                   jnp.zeros((LANES - heads_ssm - 3 * N_HEADS, d), F32)], axis=0)
        (small,) = matmul(h, w_small[None], 0, 0, LANES, [F32])
        (z,) = matmul(h, w_in_t, l, o_z, inner, [F32])
        (xbc,) = matmul(h, w_in_t, l, o_x, conv_dim, [F32])
        (gmix,) = matmul(h, w_in_t, l, o_gm, 3 * d, [F32])

        hist_a = jnp.pad(state_conv_a[l], ((0, 0), (HIST_A - (CONV_A_W - 1), 0), (0, 0)))
        ca_args = (conv_a_w[l], conv_a_b[l], ln_a_g[l], ln_a_b[l])
        c_p, ha_new_p = conv_a_branch(u_in, 0, zeros_hist_a, *ca_args, bp, tp, BF16, n_out=n_p + n_s)
        c_s, ha_new_s = conv_a_branch(u_in, n_p, hist_a, *ca_args, bs, ts, F32)
        hist_s = jnp.pad(state_conv_ssm[l], ((0, 0), (HIST_S - (SSM_CONV_W - 1), 0), (0, 0)))
        ss_args = (conv_s_w[l], conv_s_b[l], dt_bias[l], a_log[l], d_skip[l], ssm_norm_g[l])
        small_tp = small[:n_p].reshape(bp, tp, LANES).transpose(0, 2, 1)
        small_ts = small[n_p:].reshape(bs, ts, LANES).transpose(0, 2, 1)
        y_p, hs_new_p, st_new_p = ssd_branch(xbc, z, small, small_tp, 0, zeros_hist_s, zeros_ssm, *ss_args, bp, tp, BF16,
                                             n_out=n_p + n_s)
        y_s, hs_new_s, st_new_s = ssd_branch(xbc, z, small, small_ts, n_p, hist_s, state_ssm[l], *ss_args, bs, ts, F32)
        kc_p = prompt_block_means(kv, bp, tp)
        kc_p = jnp.pad(kc_p, ((0, 0), (0, LANES - kc_p.shape[1]), (0, 0)))
        o_p = nsa_prompt(q, kvb, kc_p, small, tabs_p, near_p[:, 2], cst_p, bp, tp, n_out=n_p + n_s)
        kcvc = sample_block_means(cache_t, page_table, l, pps, nbl)
        o_s = nsa_sample(q[n_p:].astype(F32).reshape(bs, ts, hd), kv[n_p:].reshape(bs, ts, kvw),
                         small[n_p:].reshape(bs, ts, LANES), kcvc, win_t, near_s, wtab, cst_s, cache_t, page_table, l, pps)

        c_all = lax.dynamic_update_slice(c_p, c_s.astype(BF16), (n_p, 0))
        o_all = lax.dynamic_update_slice(o_p, o_s.reshape(n_s, hd).astype(BF16), (n_p, 0))
        y_all = lax.dynamic_update_slice(y_p, y_s.astype(BF16), (n_p, 0))
        mixed = merge_branches(c_all, o_all, y_all, gmix, w_a_out[l].astype(BF16), w_nsa_out[l].astype(BF16),
                               w_ssm_out[l].astype(BF16))
        x = out_projection(mixed, w_out[l].astype(BF16), x)
        x = moe_layer(x, norm_ffn_g[l], w_router_group[l], b_router_group[l], w_router_expert[l], b_router_expert[l],
                      w_gate, w_up, w_down, l)

        kvr = kv[:, :4 * kw]
        kv_p.append(kvr[:n_p].reshape(bp, tp, 4, KV_GROUPS, HEAD_DIM))
        kv_s.append(kvr[n_p:].reshape(bs, ts, 4, KV_GROUPS, HEAD_DIM))
        wr = kv[:, 4 * kw:]
        win_p.append(wr[:n_p].reshape(bp, tp, 2, KV_GROUPS, HEAD_DIM)[:, tp - WINDOW:])
        win_all = jnp.concatenate([cache_nsa_win[l], wr[n_p:].reshape(bs, ts, 2, KV_GROUPS, HEAD_DIM)], axis=1)
        keep = min(WINDOW, wl + ts)
        win_s.append(win_all[:, wl + ts - keep:])
        ha_p.append(ha_new_p[:, HIST_A - (CONV_A_W - 1):])
        ha_s.append(ha_new_s[:, HIST_A - (CONV_A_W - 1):])
        hs_p.append(hs_new_p[:, HIST_S - (SSM_CONV_W - 1):])
        hs_s.append(hs_new_s[:, HIST_S - (SSM_CONV_W - 1):])
        st_p.append(st_new_p)
        st_s.append(st_new_s)

    y = rmsnorm(x, final_norm_g, F32)
    return (y[:n_p].reshape(bp, tp, d), y[n_p:].reshape(bs, ts, d),
            jnp.stack(kv_p, axis=1), jnp.stack(kv_s, axis=1), jnp.stack(win_p), jnp.stack(win_s),
            jnp.stack(ha_p), jnp.stack(ha_s), jnp.stack(hs_p), jnp.stack(hs_s), jnp.stack(st_p), jnp.stack(st_s))
```

```python
import functools
import math

import numpy as np
import jax
import jax.numpy as jnp
from jax import lax
from jax.experimental import pallas as pl
from jax.experimental.pallas import tpu as pltpu

HEAD_DIM = 64
KV_GROUPS = 4
GQA = 4
N_HEADS = KV_GROUPS * GQA
KV_KINDS = 6
BLK = 64
TOPK = 8
WINDOW = 512
QCHUNK = 128
REL_BUCKETS = 32
REL_MAX_DIST = 128
CONV_A_W = 31
SSM_CONV_W = 4
SSM_HEAD_DIM = 64
SSM_GROUPS = 4
SSM_STATE = 128
SSM_CHUNK = 128
MOE_GROUPS = 4
MOE_EPG = 8
MOE_TOPK = 2
EPS = 1e-6
NEG = -1e30
FORCE = 1e4

LANES = 128
SUBLANES = 8
BF16_ROWS = 16
VMEM_LIMIT = 56 * 1024 * 1024

DT_COLS = 32
GATE_COL0 = DT_COLS
SUPER = 4

HIGHEST = lax.Precision.HIGHEST
F32 = jnp.float32
BF16 = jnp.bfloat16


def _params(sem, limit=VMEM_LIMIT):
    return pltpu.CompilerParams(dimension_semantics=sem, vmem_limit_bytes=limit)


def _tile(n, target, mult):
    best = None
    for t in range(mult, min(n, target) + 1, mult):
        if n % t == 0:
            best = t
    return best if best is not None else n


def _dot_nt(a, b):
    return lax.dot_general(a, b, (((1,), (1,)), ((), ())), preferred_element_type=F32)


def _dot_tn(a, b):
    return lax.dot_general(a, b, (((0,), (0,)), ((), ())), preferred_element_type=F32)


def _silu(x):
    return x * jax.nn.sigmoid(x)


def _softplus(x):
    return jnp.maximum(x, 0.0) + jnp.log(1.0 + jnp.exp(-jnp.abs(x)))


def _rms_kernel(x_ref, g_ref, o_ref):
    x = x_ref[...]
    ms = jnp.mean(x * x, axis=-1, keepdims=True)
    o_ref[...] = (x * lax.rsqrt(ms + EPS) * g_ref[...]).astype(o_ref.dtype)


def rmsnorm(x, g, out_dtype):
    n, d = x.shape
    tm = _tile(n, 1024, BF16_ROWS)
    return pl.pallas_call(
        _rms_kernel,
        grid=(n // tm,),
        in_specs=[pl.BlockSpec((tm, d), lambda i: (i, 0)), pl.BlockSpec((1, d), lambda i: (0, 0))],
        out_specs=pl.BlockSpec((tm, d), lambda i: (i, 0)),
        out_shape=jax.ShapeDtypeStruct((n, d), out_dtype),
        compiler_params=_params(("parallel",)),
        name="rmsnorm",
    )(x, g.reshape(1, d))


def _mm_kernel(a_ref, w_ref, *refs, scale):
    o_refs, wb_ref = refs[:-1], refs[-1]

    @pl.when(pl.program_id(1) == 0)
    def _():
        w = w_ref[0]
        wb_ref[...] = (w if scale == 1.0 else w * scale).astype(BF16)

    acc = _dot_nt(a_ref[...], wb_ref[...])
    for o in o_refs:
        o[...] = acc.astype(o.dtype)


def matmul(a, wt, layer, row0, c, out_dtypes, scale=1.0):
    n, k = a.shape
    assert row0 % SUBLANES == 0
    tm = _tile(n, 768, BF16_ROWS)
    tn = _tile(c, 1024, SUBLANES)
    outs = pl.pallas_call(
        functools.partial(_mm_kernel, scale=scale),
        grid=(c // tn, n // tm),
        in_specs=[pl.BlockSpec((tm, k), lambda j, i: (i, 0)),
                  pl.BlockSpec((pl.Element(1), pl.Element(tn), pl.Element(k)),
                               lambda j, i: (layer, pl.multiple_of(row0 + j * tn, SUBLANES), 0))],
        out_specs=[pl.BlockSpec((tm, tn), lambda j, i: (i, j)) for _ in out_dtypes],
        out_shape=[jax.ShapeDtypeStruct((n, c), dt) for dt in out_dtypes],
        scratch_shapes=[pltpu.VMEM((tn, k), BF16)],
        compiler_params=_params(("parallel", "arbitrary")),
        name="proj",
    )(a, wt)
    return outs


HIST_A = 32
HIST_S = 8


def _round_bf16(x):
    return x.astype(BF16).astype(F32)


def _push_tail(tail_ref, new, n_new, keep):
    if n_new >= keep:
        tail_ref[...] = new[n_new - keep:n_new]
    else:
        tail_ref[...] = jnp.concatenate([tail_ref[n_new:keep], new], axis=0)


def _conva_kernel(u_ref, hist_ref, w_ref, b_ref, lg_ref, lb_ref, c_ref, hout_ref, ext_ref, conv_ref, tail_ref, *, tt, ch):
    t = pl.program_id(1)

    @pl.when(t == 0)
    def _():
        tail_ref[...] = hist_ref[...]
        ext_ref[0:HIST_A, :] = _round_bf16(hist_ref[...])

    uin = u_ref[...]
    u = uin[:, :ch] * jax.nn.sigmoid(uin[:, ch:])
    ext_ref[HIST_A:HIST_A + tt, :] = _round_bf16(u)
    _push_tail(tail_ref, u, tt, HIST_A)
    hout_ref[...] = tail_ref[...]
    off = HIST_A - (CONV_A_W - 1)
    rb = min(tt, 128)

    def lane_block(cb, carry):
        lo = pl.multiple_of(cb * LANES, LANES)
        for r in range(tt // rb):
            acc = jnp.broadcast_to(b_ref[:, pl.ds(lo, LANES)], (rb, LANES))
            for k in range(CONV_A_W):
                acc = acc + w_ref[k:k + 1, pl.ds(lo, LANES)] * ext_ref[r * rb + off + k:r * rb + off + k + rb, pl.ds(lo, LANES)]
            conv_ref[r * rb:(r + 1) * rb, pl.ds(lo, LANES)] = acc
        return carry

    lax.fori_loop(0, ch // LANES, lane_block, 0)
    cv = conv_ref[...]
    mu = jnp.mean(cv, axis=-1, keepdims=True)
    xc = cv - mu
    var = jnp.mean(xc * xc, axis=-1, keepdims=True)
    y = xc * lax.rsqrt(var + EPS) * lg_ref[...] + lb_ref[...]
    c_ref[...] = _silu(y).astype(c_ref.dtype)
    ext_ref[0:HIST_A, :] = ext_ref[tt:tt + HIST_A, :]


def conv_a_branch(u_in, row_off, hist, w, b, lg, lb, batch, t_len, out_dtype, n_out=None):
    ch = w.shape[1]
    tt = _tile(t_len, 256, SUBLANES)
    nt = t_len // tt
    assert row_off % tt == 0
    r0 = row_off // tt
    kern = functools.partial(_conva_kernel, tt=tt, ch=ch)
    vec = lambda: pl.BlockSpec((1, ch), lambda bi, ti: (0, 0))
    return pl.pallas_call(
        kern,
        grid=(batch, nt),
        in_specs=[pl.BlockSpec((tt, 2 * ch), lambda bi, ti: (r0 + bi * nt + ti, 0)),
                  pl.BlockSpec((None, HIST_A, ch), lambda bi, ti: (bi, 0, 0)),
                  pl.BlockSpec((CONV_A_W, ch), lambda bi, ti: (0, 0)), vec(), vec(), vec()],
        out_specs=[pl.BlockSpec((tt, ch), lambda bi, ti: (bi * nt + ti, 0)),
                   pl.BlockSpec((None, HIST_A, ch), lambda bi, ti: (bi, 0, 0))],
        out_shape=[jax.ShapeDtypeStruct((n_out or batch * t_len, ch), out_dtype),
                   jax.ShapeDtypeStruct((batch, HIST_A, ch), F32)],
        scratch_shapes=[pltpu.VMEM((tt + HIST_A, ch), F32), pltpu.VMEM((tt, ch), F32), pltpu.VMEM((HIST_A, ch), F32)],
        compiler_params=_params(("parallel", "arbitrary")),
        name="conv_a",
    )(u_in, hist, _round_bf16(w), b.reshape(1, ch), lg.reshape(1, ch), lb.reshape(1, ch))


def _ssd_kernel(xbc_ref, z_ref, dt_ref, dtt_ref, hist_ref, h0_ref, cw_ref, cb_ref, dtb_ref, dtbt_ref,
                a_ref, at_ref, dvec_ref, gn_ref, y_ref, hout_ref, sout_ref, ext_ref, h_ref, yacc_ref, tail_ref,
                *, q, nc, inner, heads):
    c = pl.program_id(1)

    @pl.when(c == 0)
    def _():
        tail_ref[...] = hist_ref[...]
        ext_ref[0:HIST_S, :] = _round_bf16(hist_ref[...])
        h_ref[...] = h0_ref[...]

    xbc = xbc_ref[...]
    ext_ref[HIST_S:HIST_S + q, :] = _round_bf16(xbc)
    _push_tail(tail_ref, xbc, q, HIST_S)
    hout_ref[...] = tail_ref[...]
    off = HIST_S - (SSM_CONV_W - 1)
    acc = jnp.broadcast_to(cb_ref[...], (q, cb_ref.shape[1]))
    for k in range(SSM_CONV_W):
        acc = acc + cw_ref[k:k + 1, :] * ext_ref[off + k:off + k + q, :]
    xc = _silu(acc)
    ext_ref[0:HIST_S, :] = ext_ref[q:q + HIST_S, :]

    gw = SSM_GROUPS * SSM_STATE
    xs = xc[:, :inner]
    bm = xc[:, inner:inner + gw]
    cm = xc[:, inner + gw:inner + 2 * gw]

    dt = _softplus(dt_ref[...] + dtb_ref[...])
    dtt = _softplus(dtt_ref[...] + dtbt_ref[...])
    da = dt * a_ref[...]
    dat = dtt * at_ref[...]
    ri = lax.broadcasted_iota(jnp.int32, (q, q), 0)
    ci = lax.broadcasted_iota(jnp.int32, (q, q), 1)
    causal = ri >= ci
    tri = causal.astype(F32)
    trit = (ri <= ci).astype(F32)
    acum = jnp.dot(tri, da, precision=HIGHEST, preferred_element_type=F32)
    acumt = jnp.dot(dat, trit, precision=HIGHEST, preferred_element_type=F32)
    alast = acum[q - 1:q, :]
    hpg = heads // SSM_GROUPS
    for g in range(SSM_GROUPS):
        bg = bm[:, g * SSM_STATE:(g + 1) * SSM_STATE].astype(BF16)
        cg = cm[:, g * SSM_STATE:(g + 1) * SSM_STATE].astype(BF16)
        cbm = _dot_nt(cg, bg)
        for e in range(hpg):
            hh = g * hpg + e
            a_col = acum[:, hh:hh + 1]
            a_row = acumt[hh:hh + 1, :]
            decay = jnp.exp(jnp.where(causal, a_col - a_row, NEG))
            wm = cbm * decay * dtt[hh:hh + 1, :]
            xh = xs[:, hh * SSM_HEAD_DIM:(hh + 1) * SSM_HEAD_DIM]
            hprev = h_ref[hh]
            y_h = (jnp.dot(wm.astype(BF16), xh.astype(BF16), preferred_element_type=F32)
                   + _dot_nt(cg, hprev.astype(BF16)) * jnp.exp(a_col))
            yacc_ref[:, hh * SSM_HEAD_DIM:(hh + 1) * SSM_HEAD_DIM] = y_h
            al = alast[:, hh:hh + 1]
            w_end = jnp.exp(al - a_col) * dt[:, hh:hh + 1]
            h_ref[hh] = jnp.exp(al) * hprev + _dot_tn((xh * w_end).astype(BF16), bg)

    y = (yacc_ref[...] + xs * dvec_ref[...]) * _silu(z_ref[...])
    gsz = inner // SSM_GROUPS
    for g in range(SSM_GROUPS):
        v = y[:, g * gsz:(g + 1) * gsz]
        ms = jnp.mean(v * v, axis=-1, keepdims=True)
        y_ref[:, g * gsz:(g + 1) * gsz] = (v * lax.rsqrt(ms + EPS) * gn_ref[:, g * gsz:(g + 1) * gsz]).astype(y_ref.dtype)

    @pl.when(c == nc - 1)
    def _():
        sout_ref[...] = h_ref[...]


def ssd_branch(xbc, z, small, small_t, row_off, hist, h0, cw, cb, dt_bias, a_log, d_skip, norm_g, batch, t_len, out_dtype,
               n_out=None):
    conv_dim = xbc.shape[1]
    inner = z.shape[1]
    heads = dt_bias.shape[0]
    assert heads == DT_COLS
    q = math.gcd(t_len, SSM_CHUNK)
    nc = t_len // q
    assert row_off % q == 0
    r0 = row_off // q
    pad = LANES - heads
    dtb = jnp.pad(dt_bias, (0, pad)).reshape(1, LANES)
    a_neg = jnp.pad(-jnp.exp(a_log), (0, pad)).reshape(1, LANES)
    dvec = jnp.repeat(d_skip, SSM_HEAD_DIM).reshape(1, inner)
    kern = functools.partial(_ssd_kernel, q=q, nc=nc, inner=inner, heads=heads)
    row = lambda w: pl.BlockSpec((1, w), lambda bi, ci: (0, 0))
    col = lambda: pl.BlockSpec((LANES, 1), lambda bi, ci: (0, 0))
    tok = lambda w: pl.BlockSpec((q, w), lambda bi, ci: (r0 + bi * nc + ci, 0))
    return pl.pallas_call(
        kern,
        grid=(batch, nc),
        in_specs=[tok(conv_dim), tok(inner), tok(LANES),
                  pl.BlockSpec((None, LANES, q), lambda bi, ci: (bi, 0, ci)),
                  pl.BlockSpec((None, HIST_S, conv_dim), lambda bi, ci: (bi, 0, 0)),
                  pl.BlockSpec((None, heads, SSM_HEAD_DIM, SSM_STATE), lambda bi, ci: (bi, 0, 0, 0)),
                  pl.BlockSpec((SSM_CONV_W, conv_dim), lambda bi, ci: (0, 0)), row(conv_dim),
                  row(LANES), col(), row(LANES), col(), row(inner), row(inner)],
        out_specs=[pl.BlockSpec((q, inner), lambda bi, ci: (bi * nc + ci, 0)),
                   pl.BlockSpec((None, HIST_S, conv_dim), lambda bi, ci: (bi, 0, 0)),
                   pl.BlockSpec((None, heads, SSM_HEAD_DIM, SSM_STATE), lambda bi, ci: (bi, 0, 0, 0))],
        out_shape=[jax.ShapeDtypeStruct((n_out or batch * t_len, inner), out_dtype),
                   jax.ShapeDtypeStruct((batch, HIST_S, conv_dim), F32),
                   jax.ShapeDtypeStruct((batch, heads, SSM_HEAD_DIM, SSM_STATE), F32)],
        scratch_shapes=[pltpu.VMEM((q + HIST_S, conv_dim), F32),
                        pltpu.VMEM((heads, SSM_HEAD_DIM, SSM_STATE), F32),
                        pltpu.VMEM((q, inner), F32), pltpu.VMEM((HIST_S, conv_dim), F32)],
        compiler_params=_params(("parallel", "arbitrary")),
        name="ssd",
    )(xbc, z, small, small_t, hist, h0, _round_bf16(cw), cb.reshape(1, conv_dim), dtb, dtb.reshape(LANES, 1),
      a_neg, a_neg.reshape(LANES, 1), dvec, norm_g.reshape(1, inner))


def _bucket_np(d):
    n = np.maximum(d, 0)
    exact = REL_BUCKETS // 2
    nf = np.maximum(n, 1).astype(np.float32)
    large = exact + (np.log(nf / np.float32(exact)) / np.float32(math.log(REL_MAX_DIST / exact))
                     * np.float32(REL_BUCKETS - exact)).astype(np.int32)
    return np.where(n < exact, n, np.minimum(large, REL_BUCKETS - 1))


def _near_distances():
    i = np.arange(QCHUNK)[:, None]
    j = np.arange(LANES)[None, :]
    diag = i - j
    prev = QCHUNK + i - j
    band = np.where(j < 4, i + (BLK + 1) - BLK * j, 0)
    return np.stack([diag, prev, band])


def _expand_kernel(tab_ref, oh_ref, o_ref):
    o_ref[...] = jnp.dot(tab_ref[...], oh_ref[...], precision=HIGHEST, preferred_element_type=F32)


def expand_bias(rel_bias, dist):
    flat = dist.reshape(-1)
    m = flat.shape[0]
    assert m % LANES == 0
    onehot = (np.arange(REL_BUCKETS)[:, None] == _bucket_np(flat)[None, :]).astype(np.float32)
    tm = _tile(m, 8192, LANES)
    heads = rel_bias.shape[1]
    out = pl.pallas_call(
        _expand_kernel,
        grid=(m // tm,),
        in_specs=[pl.BlockSpec((heads, REL_BUCKETS), lambda i: (0, 0)), pl.BlockSpec((REL_BUCKETS, tm), lambda i: (0, i))],
        out_specs=pl.BlockSpec((heads, tm), lambda i: (0, i)),
        out_shape=jax.ShapeDtypeStruct((heads, m), F32),
        compiler_params=_params(("parallel",)),
        name="bias_expand",
    )(rel_bias.T, jnp.asarray(onehot))
    return out.reshape((heads,) + dist.shape)


def _select_blocks(imp, lane):
    sel = jnp.zeros(imp.shape, F32)
    big = jnp.int32(1 << 30)
    v = imp
    for _ in range(TOPK):
        mx = jnp.max(v, axis=-1, keepdims=True)
        idx = jnp.min(jnp.where(v == mx, lane, big), axis=-1, keepdims=True)
        hit = lane == idx
        sel = jnp.where(hit, 1.0, sel)
        v = jnp.where(hit, -3e38, v)
    return sel


def _rep4(x):
    return jnp.concatenate([x, x, x, x], axis=0)


def _qgroup(q_ref, g):
    return jnp.concatenate([q_ref[:, (g * GQA + r) * HEAD_DIM:(g * GQA + r + 1) * HEAD_DIM] for r in range(GQA)], axis=0)


def _gate_col(sg, br, g):
    c0 = GATE_COL0 + br * N_HEADS + g * GQA
    return jnp.concatenate([sg[:, c0 + r:c0 + r + 1] for r in range(GQA)], axis=0)


def _rows_means_kernel(x_ref, o_ref):
    nb, w = o_ref.shape
    o_ref[...] = jnp.sum(x_ref[...].reshape(nb, BLK, w), axis=1) * (1.0 / BLK)


def prompt_block_means(kv, batch, t_len):
    w = 2 * KV_GROUPS * HEAD_DIM
    nb = t_len // BLK
    return pl.pallas_call(
        _rows_means_kernel,
        grid=(batch,),
        in_specs=[pl.BlockSpec((t_len, w), lambda b: (b, 0))],
        out_specs=pl.BlockSpec((None, nb, w), lambda b: (b, 0, 0)),
        out_shape=jax.ShapeDtypeStruct((batch, nb, w), F32),
        compiler_params=_params(("parallel",)),
        name="cmp_means_prompt",
    )(kv)


def _page_specs(pps, layer):
    def spec(k):
        return pl.BlockSpec((None, None, 4, KV_GROUPS, HEAD_DIM, 2 * BLK),
                            lambda b, j, pt: (pt[b, j * pps + k], layer, 0, 0, 0, 0))
    return [spec(k) for k in range(pps)]


def _nsa_prompt_kernel(q_ref, ksel_ref, vsel_ref, kwin_ref, vwin_ref, kc_ref, gn_ref, tab_ref, band_ref, cst_ref, e_ref, o_ref,
                       s_ref, sel_ref, oc_ref, mrun_ref, acc_ref, *, nb):
    c = pl.program_id(1)
    qc = QCHUNK
    rows = GQA * qc
    kw = KV_GROUPS * HEAD_DIM
    span = SUPER * qc
    lane = lax.broadcasted_iota(jnp.int32, (rows, LANES), 1)
    qi = lax.broadcasted_iota(jnp.int32, (rows, LANES), 0) & (qc - 1)
    okc = (c * qc + qi - BLK * lane - (BLK - 1)) >= 0
    lane1 = lax.broadcasted_iota(jnp.int32, (qc, LANES), 1)
    qi1 = lax.broadcasted_iota(jnp.int32, (qc, LANES), 0)
    okc1 = (c * qc + qi1 - BLK * lane1 - (BLK - 1)) >= 0
    cur = 2 * c + (qi1 >= BLK).astype(jnp.int32)

    for g in range(KV_GROUPS):
        qg = _qgroup(q_ref, g)
        cst = cst_ref[g]
        kcg = kc_ref[:, g * HEAD_DIM:(g + 1) * HEAD_DIM].astype(BF16)
        vcg = kc_ref[:, kw + g * HEAD_DIM:kw + (g + 1) * HEAD_DIM].astype(BF16)
        shift = lax.rem(2 * c - 2 + LANES, LANES)
        band = pltpu.roll(band_ref[g], shift, axis=1)
        bias_c = jnp.where(lane < 2 * c - 2, cst, band)
        s = jnp.where(okc, _dot_nt(qg, kcg) + bias_c, NEG)
        e = jnp.exp(s - jnp.max(s, axis=-1, keepdims=True))
        p = jnp.where(okc, e / jnp.sum(e, axis=-1, keepdims=True), 0.0)
        oc_ref[g] = jnp.dot(p.astype(BF16), vcg, preferred_element_type=F32)
        psum = p[0:qc] + p[qc:2 * qc] + p[2 * qc:3 * qc] + p[3 * qc:4 * qc]
        imp = jnp.where(lane1 == cur, FORCE, jnp.where(okc1, psum, -1.0))
        imp = jnp.where(lane1 < nb, imp, -3e38)
        selm = jnp.where(lane1 <= cur, _select_blocks(imp, lane1), 0.0)
        sel_ref[g] = (jnp.dot(selm.astype(BF16), e_ref[...], preferred_element_type=F32) - 1.0) * (-NEG)

    sg = jax.nn.sigmoid(gn_ref[...])
    n_super = c // SUPER + 1
    w_tiles = WINDOW // qc + 1
    st0 = jnp.maximum(c - (w_tiles - 1), 0)
    for g in range(KV_GROUPS):
        qg = _qgroup(q_ref, g)
        mrun_ref[...] = jnp.full((rows, LANES), NEG, F32)

        def scores(st, carry):
            k = ksel_ref[pl.ds(pl.multiple_of(st * span, span), span), g * HEAD_DIM:(g + 1) * HEAD_DIM]
            s = _dot_nt(qg, k)
            mrun = mrun_ref[...]
            for j in range(SUPER):
                kt = st * SUPER + j
                idx = jnp.where(kt == c, 0, jnp.where(kt == c - 1, 1, 2))
                off = pl.multiple_of(kt * qc, qc)
                sj = s[:, j * qc:(j + 1) * qc] + tab_ref[g, idx] + _rep4(sel_ref[g, :, pl.ds(off, qc)])
                s_ref[:, pl.ds(off, qc)] = sj
                mrun = jnp.maximum(mrun, sj)
            mrun_ref[...] = mrun
            return carry

        lax.fori_loop(0, n_super, scores, 0)
        m = jnp.max(mrun_ref[...], axis=-1, keepdims=True)
        mrun_ref[...] = jnp.zeros((rows, LANES), F32)

        def exps(st, carry):
            off = pl.multiple_of(st * span, span)
            e = jnp.exp(s_ref[:, pl.ds(off, span)] - m)
            s_ref[:, pl.ds(off, span)] = e
            lrun = mrun_ref[...]
            for j in range(SUPER):
                lrun = lrun + e[:, j * qc:(j + 1) * qc]
            mrun_ref[...] = lrun
            return carry

        lax.fori_loop(0, n_super, exps, 0)
        denom = jnp.sum(mrun_ref[...], axis=-1, keepdims=True)
        acc_ref[...] = jnp.zeros((rows, HEAD_DIM), F32)

        def values(st, carry):
            off = pl.multiple_of(st * span, span)
            p = (s_ref[:, pl.ds(off, span)] / denom).astype(BF16)
            acc_ref[...] += jnp.dot(p, vsel_ref[pl.ds(off, span), g * HEAD_DIM:(g + 1) * HEAD_DIM], preferred_element_type=F32)
            return carry

        lax.fori_loop(0, n_super, values, 0)
        o_s = acc_ref[...]
        r0 = pl.multiple_of(st0 * qc, qc)
        s = _dot_nt(qg, kwin_ref[pl.ds(r0, w_tiles * qc), g * HEAD_DIM:(g + 1) * HEAD_DIM])
        parts = []
        for w in range(w_tiles):
            kt = st0 + w
            idx = jnp.where(kt > c, 4, jnp.where(kt == c, 0, jnp.where(kt == c - 1, 1,
                                                                         jnp.where(kt == c - (w_tiles - 1), 3, 2))))
            parts.append(s[:, w * qc:(w + 1) * qc] + tab_ref[g, idx])
        sw = jnp.concatenate(parts, axis=1)
        ew = jnp.exp(sw - jnp.max(sw, axis=-1, keepdims=True))
        pw = (ew / jnp.sum(ew, axis=-1, keepdims=True)).astype(BF16)
        o_w = jnp.dot(pw, vwin_ref[pl.ds(r0, w_tiles * qc), g * HEAD_DIM:(g + 1) * HEAD_DIM],
                      preferred_element_type=F32)
        tot = _gate_col(sg, 0, g) * oc_ref[g] + _gate_col(sg, 1, g) * o_s + _gate_col(sg, 2, g) * o_w
        for r in range(GQA):
            o_ref[:, (g * GQA + r) * HEAD_DIM:(g * GQA + r + 1) * HEAD_DIM] = tot[r * qc:(r + 1) * qc].astype(o_ref.dtype)


def nsa_prompt(q, kvb, kc, small, tabs, band, cst, batch, t_len, n_out=None):
    assert t_len % (SUPER * QCHUNK) == 0 and t_len // BLK <= LANES and t_len >= WINDOW + QCHUNK
    nch = t_len // QCHUNK
    nb = t_len // BLK
    rows = GQA * QCHUNK
    hd = N_HEADS * HEAD_DIM
    emat = (np.arange(LANES)[:, None] == (np.arange(t_len)[None, :] // BLK)).astype(np.float32)
    kern = functools.partial(_nsa_prompt_kernel, nb=nb)
    full = lambda shp: pl.BlockSpec(shp, lambda b, c: (0,) * len(shp))
    kind = lambda k: pl.BlockSpec((t_len, KV_GROUPS * HEAD_DIM), lambda b, c: (b, k))
    return pl.pallas_call(
        kern,
        grid=(batch, nch),
        in_specs=[pl.BlockSpec((QCHUNK, hd), lambda b, c: (b * nch + c, 0)),
                  kind(2), kind(3), kind(4), kind(5),
                  pl.BlockSpec((None, LANES, kc.shape[2]), lambda b, c: (b, 0, 0)),
                  pl.BlockSpec((QCHUNK, LANES), lambda b, c: (b * nch + c, 0)),
                  full((KV_GROUPS, 5, rows, LANES)), full((KV_GROUPS, rows, LANES)), full((KV_GROUPS, rows, 1)),
                  full((LANES, t_len))],
        out_specs=pl.BlockSpec((QCHUNK, hd), lambda b, c: (b * nch + c, 0)),
        out_shape=jax.ShapeDtypeStruct((n_out or batch * t_len, hd), BF16),
        scratch_shapes=[pltpu.VMEM((rows, t_len), F32),
                        pltpu.VMEM((KV_GROUPS, QCHUNK, t_len), F32),
                        pltpu.VMEM((KV_GROUPS, rows, HEAD_DIM), F32),
                        pltpu.VMEM((rows, LANES), F32), pltpu.VMEM((rows, HEAD_DIM), F32)],
        compiler_params=_params(("parallel", "arbitrary")),
        name="nsa_prompt",
    )(q, kvb, kvb, kvb, kvb, kc, small, tabs, band, cst, jnp.asarray(emat, BF16))


def _nsa_sample_kernel(pt_ref, q_ref, kvn_ref, gn_ref, avg_ref, win_ref, near_ref, wtab_ref, cst_ref, e_ref,
                       *refs, pps, n_steps, past, ts, nbl):
    pages = refs[:pps]
    o_ref = refs[pps]
    s_ref, kc_ref, vbuf_ref, kn_ref = refs[pps + 1:]
    j = pl.program_id(1)
    rows = GQA * ts
    kw = KV_GROUPS * HEAD_DIM
    nb_past = past // BLK
    pg = 2 * BLK
    lane = lax.broadcasted_iota(jnp.int32, (rows, LANES), 1)
    ti = lax.broadcasted_iota(jnp.int32, (rows, LANES), 0) & (ts - 1)

    def qgroup(g):
        return _qgroup(q_ref, g).astype(BF16)

    @pl.when(j == 0)
    def _():
        kn_ref[...] = jnp.zeros(kn_ref.shape, F32)
        kn_ref[0:ts, :] = kvn_ref[:, 2 * kw:6 * kw]
        kc_ref[...] = jnp.zeros(kc_ref.shape, F32)

    is_last = j == n_steps - 1
    base = j * (pps * pg)
    qgs = [qgroup(g) for g in range(KV_GROUPS)]
    avg = avg_ref[...]
    blk0 = pl.multiple_of(j * (2 * pps), 2 * pps)
    for kind in range(2):
        for g in range(KV_GROUPS):
            cat = jnp.concatenate([pgr[kind, g] for pgr in pages], axis=1)
            hi = cat.astype(BF16)
            lo = (cat - hi.astype(F32)).astype(BF16)
            kc_ref[kind, g, pl.ds(blk0, 2 * pps), :] = _dot_nt(avg, hi) + _dot_nt(avg, lo)
    for k in range(pps):
        off = pl.multiple_of(base + k * pg, pg)
        for g in range(KV_GROUPS):
            vbuf_ref[g, :, pl.ds(off, pg)] = pages[k][3, g].astype(BF16)
            sk = jnp.dot(qgs[g], pages[k][2, g].astype(BF16), preferred_element_type=F32)
            bias = jnp.where(is_last, near_ref[g, 1], cst_ref[g]) if k == pps - 1 else cst_ref[g]
            s_ref[g, :, pl.ds(off, pg)] = sk + bias

    @pl.when(is_last)
    def _():
        sg = jax.nn.sigmoid(gn_ref[...])
        wl = win_ref.shape[-1]
        lane_w = lax.broadcasted_iota(jnp.int32, (rows, wl + LANES), 1)
        t_w = lax.broadcasted_iota(jnp.int32, (rows, wl + LANES), 0) & (ts - 1)
        okw = jnp.where(lane_w < wl, lane_w - t_w, t_w - (lane_w - wl)) >= 0
        lane_b = lax.broadcasted_iota(jnp.int32, (rows, nbl), 1)
        t_b = lax.broadcasted_iota(jnp.int32, (rows, nbl), 0) & (ts - 1)
        okc = (past + t_b - BLK * lane_b - (BLK - 1)) >= 0
        lane1 = lax.broadcasted_iota(jnp.int32, (ts, nbl), 1)
        okc1 = (past + lax.broadcasted_iota(jnp.int32, (ts, nbl), 0) - BLK * lane1 - (BLK - 1)) >= 0
        for g in range(KV_GROUPS):
            qg = qgroup(g)
            cst = cst_ref[g]
            band = near_ref[g, 2]
            kcg = kc_ref[0, g].astype(BF16)
            vcg = kc_ref[1, g].astype(BF16)
            bias_c = jnp.where(lane_b == nb_past - 2, band[:, 0:1], jnp.where(lane_b == nb_past - 1, band[:, 1:2], cst))
            s = jnp.where(okc, _dot_nt(qg, kcg) + bias_c, NEG)
            e = jnp.exp(s - jnp.max(s, axis=-1, keepdims=True))
            p = jnp.where(okc, e / jnp.sum(e, axis=-1, keepdims=True), 0.0)
            o_c = jnp.dot(p.astype(BF16), vcg, preferred_element_type=F32)
            psum = p[0:ts] + p[ts:2 * ts] + p[2 * ts:3 * ts] + p[3 * ts:4 * ts]
            imp = jnp.where(lane1 == nb_past, FORCE, jnp.where(okc1, psum, -1.0))
            imp = jnp.where(lane1 <= nb_past, imp, -3e38)
            selm = _select_blocks(imp, lane1)
            selm16 = jnp.concatenate([selm, jnp.zeros_like(selm)], axis=0).astype(BF16)
            madd = (jnp.dot(selm16, e_ref[...], preferred_element_type=F32)[0:ts] - 1.0) * (-NEG)
            kn = kn_ref[:, g * HEAD_DIM:(g + 1) * HEAD_DIM].astype(BF16)
            vn = kn_ref[:, kw + g * HEAD_DIM:kw + (g + 1) * HEAD_DIM].astype(BF16)
            s_new = jnp.where(lane <= ti, _dot_nt(qg, kn) + near_ref[g, 0], NEG)
            s = jnp.concatenate([s_ref[g] + _rep4(madd), s_new], axis=1)
            e = jnp.exp(s - jnp.max(s, axis=-1, keepdims=True))
            p = e / jnp.sum(e, axis=-1, keepdims=True)
            o_s = (_dot_nt(p[:, :past].astype(BF16), vbuf_ref[g])
                   + jnp.dot(p[:, past:].astype(BF16), vn, preferred_element_type=F32))
            kwin = win_ref[0, g].astype(BF16)
            vwin = win_ref[1, g].astype(BF16)
            knw = kn_ref[:, 2 * kw + g * HEAD_DIM:2 * kw + (g + 1) * HEAD_DIM].astype(BF16)
            vnw = kn_ref[:, 3 * kw + g * HEAD_DIM:3 * kw + (g + 1) * HEAD_DIM].astype(BF16)
            s = jnp.concatenate([jnp.dot(qg, kwin, preferred_element_type=F32), _dot_nt(qg, knw)], axis=1) + wtab_ref[g]
            s = jnp.where(okw, s, NEG)
            e = jnp.exp(s - jnp.max(s, axis=-1, keepdims=True))
            p = e / jnp.sum(e, axis=-1, keepdims=True)
            o_w = (_dot_nt(p[:, :wl].astype(BF16), vwin)
                   + jnp.dot(p[:, wl:].astype(BF16), vnw, preferred_element_type=F32))
            tot = _gate_col(sg, 0, g) * o_c + _gate_col(sg, 1, g) * o_s + _gate_col(sg, 2, g) * o_w
            for r in range(GQA):
                o_ref[:, (g * GQA + r) * HEAD_DIM:(g * GQA + r + 1) * HEAD_DIM] = tot[r * ts:(r + 1) * ts]


def nsa_sample(q3, kvn3, small3, win_t, near_s, wtab, cst_s, cache_t, page_table, layer, pps, nbl):
    batch, ts, hd = q3.shape
    n_pages = page_table.shape[1]
    n_steps = n_pages // pps
    past = n_pages * 2 * BLK
    wl = win_t.shape[-1]
    avg = np.where(np.arange(2 * pps)[:, None] == np.arange(pps * 2 * BLK)[None, :] // BLK, 1.0 / BLK, 0.0)
    rows = GQA * ts
    kw = KV_GROUPS * HEAD_DIM
    assert ts == SUBLANES and past % QCHUNK == 0 and wl % LANES == 0
    emat = (np.arange(nbl)[:, None] == (np.arange(past)[None, :] // BLK)).astype(np.float32)
    kern = functools.partial(_nsa_sample_kernel, pps=pps, n_steps=n_steps, past=past, ts=ts, nbl=nbl)
    c3 = lambda shp: pl.BlockSpec(shp, lambda b, j, pt: (0,) * len(shp))
    return pl.pallas_call(
        kern,
        grid_spec=pltpu.PrefetchScalarGridSpec(
            num_scalar_prefetch=1, grid=(batch, n_steps),
            in_specs=[pl.BlockSpec((None, ts, hd), lambda b, j, pt: (b, 0, 0)),
                      pl.BlockSpec((None, ts, kvn3.shape[2]), lambda b, j, pt: (b, 0, 0)),
                      pl.BlockSpec((None, ts, LANES), lambda b, j, pt: (b, 0, 0)),
                      c3((2 * pps, pps * 2 * BLK)),
                      pl.BlockSpec((None, None, 2, KV_GROUPS, HEAD_DIM, wl), lambda b, j, pt: (layer, b, 0, 0, 0, 0)),
                      c3((KV_GROUPS, 3, rows, LANES)), c3((KV_GROUPS, rows, wl + LANES)), c3((KV_GROUPS, rows, 1)),
                      c3((nbl, past))] + _page_specs(pps, layer),
            out_specs=pl.BlockSpec((None, ts, hd), lambda b, j, pt: (b, 0, 0)),
            scratch_shapes=[pltpu.VMEM((KV_GROUPS, rows, past), F32),
                            pltpu.VMEM((2, KV_GROUPS, nbl, HEAD_DIM), F32),
                            pltpu.VMEM((KV_GROUPS, HEAD_DIM, past), BF16),
                            pltpu.VMEM((LANES, 4 * kw), F32)]),
        out_shape=jax.ShapeDtypeStruct((batch, ts, hd), F32),
        compiler_params=_params(("parallel", "arbitrary")),
        name="nsa_sample",
    )(page_table, q3, kvn3, small3, jnp.asarray(avg, BF16), win_t, near_s, wtab, cst_s, jnp.asarray(emat, BF16),
      *([cache_t] * pps))


def _merge_kernel(c_ref, o_ref, y_ref, g0_ref, g1_ref, g2_ref, wa_ref, wn_ref, ws_ref, out_ref):
    oa = jnp.dot(c_ref[...], wa_ref[...], preferred_element_type=F32)
    ob = jnp.dot(o_ref[...], wn_ref[...], preferred_element_type=F32)
    oc = jnp.dot(y_ref[...], ws_ref[...], preferred_element_type=F32)
    mixed = (jax.nn.sigmoid(g0_ref[...]) * oa + jax.nn.sigmoid(g1_ref[...]) * ob + jax.nn.sigmoid(g2_ref[...]) * oc)
    out_ref[...] = mixed.astype(out_ref.dtype)


def merge_branches(c, o, y, gmix, wa, wn, ws):
    n = c.shape[0]
    d = wa.shape[1]
    tm = _tile(n, 768, BF16_ROWS)
    tn = _tile(d, 512, LANES)
    nj = d // tn
    act = lambda a: pl.BlockSpec((tm, a.shape[1]), lambda i, j: (i, 0))
    wsp = lambda w: pl.BlockSpec((w.shape[0], tn), lambda i, j: (0, j))
    gsp = lambda br: pl.BlockSpec((tm, tn), lambda i, j: (i, br * nj + j))
    return pl.pallas_call(
        _merge_kernel,
        grid=(n // tm, nj),
        in_specs=[act(c), act(o), act(y), gsp(0), gsp(1), gsp(2), wsp(wa), wsp(wn), wsp(ws)],
        out_specs=pl.BlockSpec((tm, tn), lambda i, j: (i, j)),
        out_shape=jax.ShapeDtypeStruct((n, d), BF16),
        compiler_params=_params(("parallel", "parallel")),
        name="merge",
    )(c, o, y, gmix, gmix, gmix, wa, wn, ws)


def _outproj_kernel(a_ref, w_ref, x_ref, o_ref):
    o_ref[...] = x_ref[...] + jnp.dot(a_ref[...], w_ref[...], preferred_element_type=F32)


def out_projection(a, w, x):
    n, k = a.shape
    d = w.shape[1]
    tm = _tile(n, 768, BF16_ROWS)
    tn = _tile(d, 1024, LANES)
    return pl.pallas_call(
        _outproj_kernel,
        grid=(n // tm, d // tn),
        in_specs=[pl.BlockSpec((tm, k), lambda i, j: (i, 0)), pl.BlockSpec((k, tn), lambda i, j: (0, j)),
                  pl.BlockSpec((tm, tn), lambda i, j: (i, j))],
        out_specs=pl.BlockSpec((tm, tn), lambda i, j: (i, j)),
        out_shape=jax.ShapeDtypeStruct((n, d), F32),
        compiler_params=_params(("parallel", "parallel")),
        name="out_proj",
    )(a, w, x)


def _router_kernel(x_ref, g_ref, w_ref, b_ref, h_ref, r_ref):
    x = x_ref[...]
    ms = jnp.mean(x * x, axis=-1, keepdims=True)
    h = (x * lax.rsqrt(ms + EPS) * g_ref[...]).astype(BF16)
    h_ref[...] = h
    logits = jnp.dot(h, w_ref[...], preferred_element_type=F32) + b_ref[...]
    lane = lax.broadcasted_iota(jnp.int32, logits.shape, 1)
    big = jnp.int32(1 << 30)
    gl = jnp.where(lane < MOE_GROUPS, logits, NEG)
    gmax = jnp.max(gl, axis=-1, keepdims=True)
    gidx = jnp.min(jnp.where(gl == gmax, lane, big), axis=-1, keepdims=True)
    g_w = 1.0 / jnp.sum(jnp.exp(gl - gmax), axis=-1, keepdims=True)
    eid = lane - MOE_GROUPS
    in_grp = (eid >= 0) & (eid < MOE_GROUPS * MOE_EPG) & ((eid >> 3) == gidx)
    el = jnp.where(in_grp, logits, NEG)
    m1 = jnp.max(el, axis=-1, keepdims=True)
    i1 = jnp.min(jnp.where(el == m1, lane, big), axis=-1, keepdims=True)
    el2 = jnp.where(lane == i1, NEG, el)
    m2 = jnp.max(el2, axis=-1, keepdims=True)
    i2 = jnp.min(jnp.where(el2 == m2, lane, big), axis=-1, keepdims=True)
    p2 = jnp.exp(m2 - m1)
    w1 = g_w / (1.0 + p2)
    w2 = g_w * p2 / (1.0 + p2)
    r_ref[...] = jnp.where(lane == 0, (i1 - MOE_GROUPS).astype(F32),
                           jnp.where(lane == 1, (i2 - MOE_GROUPS).astype(F32),
                                     jnp.where(lane == 2, w1, jnp.where(lane == 3, w2, 0.0))))


def moe_router(x, g, wr, br):
    n, d = x.shape
    tm = _tile(n, 256, BF16_ROWS)
    return pl.pallas_call(
        _router_kernel,
        grid=(n // tm,),
        in_specs=[pl.BlockSpec((tm, d), lambda i: (i, 0)), pl.BlockSpec((1, d), lambda i: (0, 0)),
                  pl.BlockSpec((d, LANES), lambda i: (0, 0)), pl.BlockSpec((1, LANES), lambda i: (0, 0))],
        out_specs=[pl.BlockSpec((tm, d), lambda i: (i, 0)), pl.BlockSpec((tm, LANES), lambda i: (i, 0))],
        out_shape=[jax.ShapeDtypeStruct((n, d), BF16), jax.ShapeDtypeStruct((n, LANES), F32)],
        compiler_params=_params(("parallel",)),
        name="moe_router",
    )(x, g.reshape(1, d), wr, br)


def _expert_kernel(te_ref, nu_ref, x_ref, rw_ref, wg_ref, wu_ref, wd_ref, y_ref, wgb_ref, wub_ref, wdb_ref):
    t = pl.program_id(0)
    fresh = jnp.logical_or(t == 0, te_ref[t] != te_ref[jnp.maximum(t - 1, 0)])

    @pl.when(jnp.logical_and(fresh, t < nu_ref[0]))
    def _():
        wgb_ref[...] = wg_ref[...].astype(BF16)
        wub_ref[...] = wu_ref[...].astype(BF16)
        wdb_ref[...] = wd_ref[...].astype(BF16)

    @pl.when(t < nu_ref[0])
    def _():
        x = x_ref[...]
        a = jnp.dot(x, wgb_ref[...], preferred_element_type=F32)
        u = jnp.dot(x, wub_ref[...], preferred_element_type=F32)
        hid = (_silu(a) * u * rw_ref[...]).astype(BF16)
        y_ref[...] = jnp.dot(hid, wdb_ref[...], preferred_element_type=F32)


def moe_experts(xs, row_w, tile_e, n_used, wg, wu, wd, layer, tme):
    r, d = xs.shape
    ff = wg.shape[3]
    nt = r // tme
    return pl.pallas_call(
        _expert_kernel,
        grid_spec=pltpu.PrefetchScalarGridSpec(
            num_scalar_prefetch=2, grid=(nt,),
            in_specs=[pl.BlockSpec((tme, d), lambda t, te, nu: (t, 0)),
                      pl.BlockSpec((tme, 1), lambda t, te, nu: (t, 0)),
                      pl.BlockSpec((None, None, d, ff), lambda t, te, nu: (layer, te[t], 0, 0)),
                      pl.BlockSpec((None, None, d, ff), lambda t, te, nu: (layer, te[t], 0, 0)),
                      pl.BlockSpec((None, None, ff, d), lambda t, te, nu: (layer, te[t], 0, 0))],
            out_specs=pl.BlockSpec((tme, d), lambda t, te, nu: (t, 0)),
            scratch_shapes=[pltpu.VMEM((d, ff), BF16), pltpu.VMEM((d, ff), BF16), pltpu.VMEM((ff, d), BF16)]),
        out_shape=jax.ShapeDtypeStruct((r, d), F32),
        compiler_params=_params(("arbitrary",)),
        name="moe_experts",
    )(tile_e, n_used, xs, row_w, wg, wu, wd)


def moe_layer(x, g, wrg, brg, wre, bre, wg, wu, wd, layer):
    n, d = x.shape
    n_exp = wg.shape[1]
    wr = jnp.concatenate([wrg, wre.transpose(1, 0, 2).reshape(d, n_exp)], axis=1)
    wr = jnp.pad(wr, ((0, 0), (0, LANES - wr.shape[1])))
    br = jnp.pad(jnp.concatenate([brg, bre.reshape(-1)]), (0, LANES - MOE_GROUPS - n_exp)).reshape(1, LANES)
    h, route = moe_router(x, g, wr.astype(BF16), br)
    tme = 256
    e_flat = route[:, 0:MOE_TOPK].astype(jnp.int32).T.reshape(-1)
    w_flat = route[:, MOE_TOPK:2 * MOE_TOPK].T.reshape(-1)
    na = e_flat.shape[0]
    nt = -(-na // tme) + n_exp
    order = jnp.argsort(e_flat, stable=True)
    counts = jnp.sum(e_flat[:, None] == jnp.arange(n_exp)[None, :], axis=0).astype(jnp.int32)
    pc = ((counts + tme - 1) // tme) * tme
    pend = jnp.cumsum(pc)
    pstart = pend - pc
    ustart = jnp.cumsum(counts) - counts
    se = e_flat[order]
    dest = pstart[se] + jnp.arange(na, dtype=jnp.int32) - ustart[se]
    row_tok = jnp.zeros((nt * tme,), jnp.int32).at[dest].set((order % n).astype(jnp.int32))
    row_w = jnp.zeros((nt * tme,), F32).at[dest].set(w_flat[order])
    dest_flat = jnp.zeros((na,), jnp.int32).at[order].set(dest)
    tile_e = jnp.minimum(jnp.searchsorted(pend, jnp.arange(nt, dtype=jnp.int32) * tme, side="right"), n_exp - 1).astype(jnp.int32)
    n_used = (pend[-1] // tme).astype(jnp.int32).reshape(1)
    xs = jnp.take(h, row_tok, axis=0)
    ys = moe_experts(xs, row_w.reshape(-1, 1), tile_e, n_used, wg, wu, wd, layer, tme)
    return x + jnp.take(ys, dest_flat[:n], axis=0) + jnp.take(ys, dest_flat[n:], axis=0)


def kernel(x_prompt, x_sample, cache_nsa_kv, cache_nsa_win, state_conv_a, state_conv_ssm, state_ssm, page_table,
           rel_bias, norm_mix_g, w_in, conv_a_w, conv_a_b, ln_a_g, ln_a_b, w_a_out, w_nsa_out, conv_s_w, conv_s_b,
           dt_bias, a_log, d_skip, ssm_norm_g, w_ssm_out, w_out, norm_ffn_g, w_router_group, b_router_group,
           w_router_expert, b_router_expert, w_gate, w_up, w_down, final_norm_g):
    bp, tp, d = x_prompt.shape
    bs, ts, _ = x_sample.shape
    depth = w_in.shape[0]
    n_p, n_s = bp * tp, bs * ts
    ch = conv_a_w.shape[2]
    conv_dim = conv_s_w.shape[2]
    heads_ssm = dt_bias.shape[1]
    inner = heads_ssm * SSM_HEAD_DIM
    hd = N_HEADS * HEAD_DIM
    kw = KV_GROUPS * HEAD_DIM
    kvw = KV_KINDS * kw
    n_pages = page_table.shape[1]
    page = cache_nsa_kv.shape[2]
    past = n_pages * page
    wl = cache_nsa_win.shape[2]
    assert page == 2 * BLK and wl == min(WINDOW, past) and tp >= WINDOW
    pps = _tile(n_pages, 16, 1)
    rows_p = GQA * QCHUNK

    splits = (2 * ch, hd, kvw, 3 * N_HEADS, inner, conv_dim, heads_ssm, 3 * d)
    offs = np.concatenate([[0], np.cumsum(splits)])
    o_u, o_q, o_kv, o_gn, o_z, o_x, o_dt, o_gm = [int(v) for v in offs[:-1]]

    near_d = _near_distances()
    tj = np.arange(wl + LANES)[None, :]
    win_d = np.where(tj < wl + ts, wl + np.arange(ts)[:, None] - tj, 0)
    near_f = expand_bias(rel_bias, near_d)
    wtab_f = expand_bias(rel_bias, win_d)
    near_g = near_f.reshape(KV_GROUPS, GQA, 3, QCHUNK, LANES).transpose(0, 2, 1, 3, 4)
    near_p = near_g.reshape(KV_GROUPS, 3, rows_p, LANES)
    near_s = near_g[:, :, :, :ts].reshape(KV_GROUPS, 3, GQA * ts, LANES)
    wtab = wtab_f.reshape(KV_GROUPS, GQA * ts, wl + LANES)
    far = rel_bias[REL_BUCKETS - 1].reshape(KV_GROUPS, GQA, 1)
    cst_p = jnp.repeat(far, QCHUNK, axis=1).reshape(KV_GROUPS, rows_p, 1)
    cst_s = jnp.repeat(far, ts, axis=1).reshape(KV_GROUPS, GQA * ts, 1)
    ii = (np.arange(rows_p) % QCHUNK)[:, None]
    jj = np.arange(LANES)[None, :]
    cst_t = jnp.broadcast_to(cst_p, (KV_GROUPS, rows_p, LANES))
    tabs_p = jnp.stack([jnp.where(jj <= ii, near_p[:, 0], NEG), near_p[:, 1], cst_t,
                        jnp.where(jj >= ii, cst_t, NEG), jnp.full((KV_GROUPS, rows_p, LANES), NEG, F32)], axis=1)

    w_in_t = w_in.transpose(0, 2, 1)
    cache_t = cache_nsa_kv.transpose(0, 1, 3, 4, 5, 2)
    win_t = cache_nsa_win.transpose(0, 1, 3, 4, 5, 2)
    zeros_hist_a = jnp.zeros((bp, HIST_A, ch), F32)
    zeros_hist_s = jnp.zeros((bp, HIST_S, conv_dim), F32)
    zeros_ssm = jnp.zeros((bp, heads_ssm, SSM_HEAD_DIM, SSM_STATE), F32)
    nbl = -(-(past // BLK + 1) // LANES) * LANES

    x = jnp.concatenate([x_prompt.reshape(n_p, d), x_sample.reshape(n_s, d)], axis=0)
    kv_p, kv_s, win_p, win_s, ha_p, ha_s, hs_p, hs_s, st_p, st_s = ([] for _ in range(10))
    for l in range(depth):
        h = rmsnorm(x, norm_mix_g[l], BF16)
        (u_in,) = matmul(h, w_in_t, l, o_u, 2 * ch, [F32])
        (q,) = matmul(h, w_in_t, l, o_q, hd, [BF16], scale=HEAD_DIM ** -0.5)
        kv, kvb = matmul(h, w_in_t, l, o_kv, kvw, [F32, BF16])
        w_small = jnp.concatenate([w_in_t[l, o_dt:o_gm], w_in_t[l, o_gn:o_z],
                                   jnp.zeros((LANES - heads_ssm - 3 * N_HEADS, d), F32)], axis=0)
        (small,) = matmul(h, w_small[None], 0, 0, LANES, [F32])
        (z,) = matmul(h, w_in_t, l, o_z, inner, [F32])
        (xbc,) = matmul(h, w_in_t, l, o_x, conv_dim, [F32])
        (gmix,) = matmul(h, w_in_t, l, o_gm, 3 * d, [F32])

        hist_a = jnp.pad(state_conv_a[l], ((0, 0), (HIST_A - (CONV_A_W - 1), 0), (0, 0)))
        ca_args = (conv_a_w[l], conv_a_b[l], ln_a_g[l], ln_a_b[l])
        c_p, ha_new_p = conv_a_branch(u_in, 0, zeros_hist_a, *ca_args, bp, tp, BF16, n_out=n_p + n_s)
        c_s, ha_new_s = conv_a_branch(u_in, n_p, hist_a, *ca_args, bs, ts, F32)
        hist_s = jnp.pad(state_conv_ssm[l], ((0, 0), (HIST_S - (SSM_CONV_W - 1), 0), (0, 0)))
        ss_args = (conv_s_w[l], conv_s_b[l], dt_bias[l], a_log[l], d_skip[l], ssm_norm_g[l])
        small_tp = small[:n_p].reshape(bp, tp, LANES).transpose(0, 2, 1)
        small_ts = small[n_p:].reshape(bs, ts, LANES).transpose(0, 2, 1)
        y_p, hs_new_p, st_new_p = ssd_branch(xbc, z, small, small_tp, 0, zeros_hist_s, zeros_ssm, *ss_args, bp, tp, BF16,
                                             n_out=n_p + n_s)
        y_s, hs_new_s, st_new_s = ssd_branch(xbc, z, small, small_ts, n_p, hist_s, state_ssm[l], *ss_args, bs, ts, F32)
        kc_p = prompt_block_means(kv, bp, tp)
        kc_p = jnp.pad(kc_p, ((0, 0), (0, LANES - kc_p.shape[1]), (0, 0)))
        o_p = nsa_prompt(q, kvb, kc_p, small, tabs_p, near_p[:, 2], cst_p, bp, tp, n_out=n_p + n_s)
        o_s = nsa_sample(q[n_p:].astype(F32).reshape(bs, ts, hd), kv[n_p:].reshape(bs, ts, kvw),
                         small[n_p:].reshape(bs, ts, LANES), win_t, near_s, wtab, cst_s, cache_t, page_table, l, pps, nbl)

        c_all = lax.dynamic_update_slice(c_p, c_s.astype(BF16), (n_p, 0))
        o_all = lax.dynamic_update_slice(o_p, o_s.reshape(n_s, hd).astype(BF16), (n_p, 0))
        y_all = lax.dynamic_update_slice(y_p, y_s.astype(BF16), (n_p, 0))
        mixed = merge_branches(c_all, o_all, y_all, gmix, w_a_out[l].astype(BF16), w_nsa_out[l].astype(BF16),
                               w_ssm_out[l].astype(BF16))
        x = out_projection(mixed, w_out[l].astype(BF16), x)
        x = moe_layer(x, norm_ffn_g[l], w_router_group[l], b_router_group[l], w_router_expert[l], b_router_expert[l],
                      w_gate, w_up, w_down, l)

        kvr = kv[:, :4 * kw]
        kv_p.append(kvr[:n_p].reshape(bp, tp, 4, KV_GROUPS, HEAD_DIM))
        kv_s.append(kvr[n_p:].reshape(bs, ts, 4, KV_GROUPS, HEAD_DIM))
        wr = kv[:, 4 * kw:]
        win_p.append(wr[:n_p].reshape(bp, tp, 2, KV_GROUPS, HEAD_DIM)[:, tp - WINDOW:])
        win_all = jnp.concatenate([cache_nsa_win[l], wr[n_p:].reshape(bs, ts, 2, KV_GROUPS, HEAD_DIM)], axis=1)
        keep = min(WINDOW, wl + ts)
        win_s.append(win_all[:, wl + ts - keep:])
        ha_p.append(ha_new_p[:, HIST_A - (CONV_A_W - 1):])
        ha_s.append(ha_new_s[:, HIST_A - (CONV_A_W - 1):])
        hs_p.append(hs_new_p[:, HIST_S - (SSM_CONV_W - 1):])
        hs_s.append(hs_new_s[:, HIST_S - (SSM_CONV_W - 1):])
        st_p.append(st_new_p)
        st_s.append(st_new_s)

    y = rmsnorm(x, final_norm_g, F32)
    return (y[:n_p].reshape(bp, tp, d), y[n_p:].reshape(bs, ts, d),
            jnp.stack(kv_p, axis=1), jnp.stack(kv_s, axis=1), jnp.stack(win_p), jnp.stack(win_s),
            jnp.stack(ha_p), jnp.stack(ha_s), jnp.stack(hs_p), jnp.stack(hs_s), jnp.stack(st_p), jnp.stack(st_s))
```

```python
import functools
import math

import numpy as np
import jax
import jax.numpy as jnp
from jax import lax
from jax.experimental import pallas as pl
from jax.experimental.pallas import tpu as pltpu

HEAD_DIM = 64
KV_GROUPS = 4
GQA = 4
N_HEADS = KV_GROUPS * GQA
KV_KINDS = 6
BLK = 64
TOPK = 8
WINDOW = 512
QCHUNK = 128
REL_BUCKETS = 32
REL_MAX_DIST = 128
CONV_A_W = 31
SSM_CONV_W = 4
SSM_HEAD_DIM = 64
SSM_GROUPS = 4
SSM_STATE = 128
SSM_CHUNK = 128
MOE_GROUPS = 4
MOE_EPG = 8
MOE_TOPK = 2
EPS = 1e-6
NEG = -1e30
FORCE = 1e4

LANES = 128
SUBLANES = 8
BF16_ROWS = 16
VMEM_LIMIT = 56 * 1024 * 1024

DT_COLS = 32
GATE_COL0 = DT_COLS
SUPER = 4

HIGHEST = lax.Precision.HIGHEST
F32 = jnp.float32
BF16 = jnp.bfloat16


def _params(sem, limit=VMEM_LIMIT):
    return pltpu.CompilerParams(dimension_semantics=sem, vmem_limit_bytes=limit)


def _tile(n, target, mult):
    best = None
    for t in range(mult, min(n, target) + 1, mult):
        if n % t == 0:
            best = t
    return best if best is not None else n


def _dot_nt(a, b):
    return lax.dot_general(a, b, (((1,), (1,)), ((), ())), preferred_element_type=F32)


def _dot_tn(a, b):
    return lax.dot_general(a, b, (((0,), (0,)), ((), ())), preferred_element_type=F32)


def _silu(x):
    return x * jax.nn.sigmoid(x)


def _softplus(x):
    return jnp.maximum(x, 0.0) + jnp.log(1.0 + jnp.exp(-jnp.abs(x)))


def _rms_kernel(x_ref, g_ref, o_ref):
    x = x_ref[...]
    ms = jnp.mean(x * x, axis=-1, keepdims=True)
    o_ref[...] = (x * lax.rsqrt(ms + EPS) * g_ref[...]).astype(o_ref.dtype)


def rmsnorm(x, g, out_dtype):
    n, d = x.shape
    tm = _tile(n, 1024, BF16_ROWS)
    return pl.pallas_call(
        _rms_kernel,
        grid=(n // tm,),
        in_specs=[pl.BlockSpec((tm, d), lambda i: (i, 0)), pl.BlockSpec((1, d), lambda i: (0, 0))],
        out_specs=pl.BlockSpec((tm, d), lambda i: (i, 0)),
        out_shape=jax.ShapeDtypeStruct((n, d), out_dtype),
        compiler_params=_params(("parallel",)),
        name="rmsnorm",
    )(x, g.reshape(1, d))


def _mm_kernel(a_ref, w_ref, *refs, scale):
    o_refs, wb_ref = refs[:-1], refs[-1]

    @pl.when(pl.program_id(1) == 0)
    def _():
        w = w_ref[0]
        wb_ref[...] = (w if scale == 1.0 else w * scale).astype(BF16)

    acc = _dot_nt(a_ref[...], wb_ref[...])
    for o in o_refs:
        o[...] = acc.astype(o.dtype)


def matmul(a, wt, layer, row0, c, out_dtypes, scale=1.0):
    n, k = a.shape
    assert row0 % SUBLANES == 0
    tm = _tile(n, 768, BF16_ROWS)
    tn = _tile(c, 1024, SUBLANES)
    outs = pl.pallas_call(
        functools.partial(_mm_kernel, scale=scale),
        grid=(c // tn, n // tm),
        in_specs=[pl.BlockSpec((tm, k), lambda j, i: (i, 0)),
                  pl.BlockSpec((pl.Element(1), pl.Element(tn), pl.Element(k)),
                               lambda j, i: (layer, pl.multiple_of(row0 + j * tn, SUBLANES), 0))],
        out_specs=[pl.BlockSpec((tm, tn), lambda j, i: (i, j)) for _ in out_dtypes],
        out_shape=[jax.ShapeDtypeStruct((n, c), dt) for dt in out_dtypes],
        scratch_shapes=[pltpu.VMEM((tn, k), BF16)],
        compiler_params=_params(("parallel", "arbitrary")),
        name="proj",
    )(a, wt)
    return outs


HIST_A = 32
HIST_S = 8


def _round_bf16(x):
    return x.astype(BF16).astype(F32)


def _push_tail(tail_ref, new, n_new, keep):
    if n_new >= keep:
        tail_ref[...] = new[n_new - keep:n_new]
    else:
        tail_ref[...] = jnp.concatenate([tail_ref[n_new:keep], new], axis=0)


def _conva_kernel(u_ref, hist_ref, w_ref, b_ref, lg_ref, lb_ref, c_ref, hout_ref, ext_ref, conv_ref, tail_ref, *, tt, ch):
    t = pl.program_id(1)

    @pl.when(t == 0)
    def _():
        tail_ref[...] = hist_ref[...]
        ext_ref[0:HIST_A, :] = _round_bf16(hist_ref[...])

    uin = u_ref[...]
    u = uin[:, :ch] * jax.nn.sigmoid(uin[:, ch:])
    ext_ref[HIST_A:HIST_A + tt, :] = _round_bf16(u)
    _push_tail(tail_ref, u, tt, HIST_A)
    hout_ref[...] = tail_ref[...]
    off = HIST_A - (CONV_A_W - 1)
    rb = min(tt, 128)

    def lane_block(cb, carry):
        lo = pl.multiple_of(cb * LANES, LANES)
        for r in range(tt // rb):
            acc = jnp.broadcast_to(b_ref[:, pl.ds(lo, LANES)], (rb, LANES))
            for k in range(CONV_A_W):
                acc = acc + w_ref[k:k + 1, pl.ds(lo, LANES)] * ext_ref[r * rb + off + k:r * rb + off + k + rb, pl.ds(lo, LANES)]
            conv_ref[r * rb:(r + 1) * rb, pl.ds(lo, LANES)] = acc
        return carry

    lax.fori_loop(0, ch // LANES, lane_block, 0)
    cv = conv_ref[...]
    mu = jnp.mean(cv, axis=-1, keepdims=True)
    xc = cv - mu
    var = jnp.mean(xc * xc, axis=-1, keepdims=True)
    y = xc * lax.rsqrt(var + EPS) * lg_ref[...] + lb_ref[...]
    c_ref[...] = _silu(y).astype(c_ref.dtype)
    ext_ref[0:HIST_A, :] = ext_ref[tt:tt + HIST_A, :]


def conv_a_branch(u_in, row_off, hist, w, b, lg, lb, batch, t_len, out_dtype, n_out=None):
    ch = w.shape[1]
    tt = _tile(t_len, 256, SUBLANES)
    nt = t_len // tt
    assert row_off % tt == 0
    r0 = row_off // tt
    kern = functools.partial(_conva_kernel, tt=tt, ch=ch)
    vec = lambda: pl.BlockSpec((1, ch), lambda bi, ti: (0, 0))
    return pl.pallas_call(
        kern,
        grid=(batch, nt),
        in_specs=[pl.BlockSpec((tt, 2 * ch), lambda bi, ti: (r0 + bi * nt + ti, 0)),
                  pl.BlockSpec((None, HIST_A, ch), lambda bi, ti: (bi, 0, 0)),
                  pl.BlockSpec((CONV_A_W, ch), lambda bi, ti: (0, 0)), vec(), vec(), vec()],
        out_specs=[pl.BlockSpec((tt, ch), lambda bi, ti: (bi * nt + ti, 0)),
                   pl.BlockSpec((None, HIST_A, ch), lambda bi, ti: (bi, 0, 0))],
        out_shape=[jax.ShapeDtypeStruct((n_out or batch * t_len, ch), out_dtype),
                   jax.ShapeDtypeStruct((batch, HIST_A, ch), F32)],
        scratch_shapes=[pltpu.VMEM((tt + HIST_A, ch), F32), pltpu.VMEM((tt, ch), F32), pltpu.VMEM((HIST_A, ch), F32)],
        compiler_params=_params(("parallel", "arbitrary")),
        name="conv_a",
    )(u_in, hist, _round_bf16(w), b.reshape(1, ch), lg.reshape(1, ch), lb.reshape(1, ch))


def _ssd_kernel(xbc_ref, z_ref, dt_ref, dtt_ref, hist_ref, h0_ref, cw_ref, cb_ref, dtb_ref, dtbt_ref,
                a_ref, at_ref, dvec_ref, gn_ref, y_ref, hout_ref, sout_ref, ext_ref, h_ref, yacc_ref, tail_ref,
                *, q, nc, inner, heads):
    c = pl.program_id(1)

    @pl.when(c == 0)
    def _():
        tail_ref[...] = hist_ref[...]
        ext_ref[0:HIST_S, :] = _round_bf16(hist_ref[...])
        h_ref[...] = h0_ref[...]

    xbc = xbc_ref[...]
    ext_ref[HIST_S:HIST_S + q, :] = _round_bf16(xbc)
    _push_tail(tail_ref, xbc, q, HIST_S)
    hout_ref[...] = tail_ref[...]
    off = HIST_S - (SSM_CONV_W - 1)
    acc = jnp.broadcast_to(cb_ref[...], (q, cb_ref.shape[1]))
    for k in range(SSM_CONV_W):
        acc = acc + cw_ref[k:k + 1, :] * ext_ref[off + k:off + k + q, :]
    xc = _silu(acc)
    ext_ref[0:HIST_S, :] = ext_ref[q:q + HIST_S, :]

    gw = SSM_GROUPS * SSM_STATE
    xs = xc[:, :inner]
    bm = xc[:, inner:inner + gw]
    cm = xc[:, inner + gw:inner + 2 * gw]

    dt = _softplus(dt_ref[...] + dtb_ref[...])
    dtt = _softplus(dtt_ref[...] + dtbt_ref[...])
    da = dt * a_ref[...]
    dat = dtt * at_ref[...]
    ri = lax.broadcasted_iota(jnp.int32, (q, q), 0)
    ci = lax.broadcasted_iota(jnp.int32, (q, q), 1)
    causal = ri >= ci
    tri = causal.astype(F32)
    trit = (ri <= ci).astype(F32)
    acum = jnp.dot(tri, da, precision=HIGHEST, preferred_element_type=F32)
    acumt = jnp.dot(dat, trit, precision=HIGHEST, preferred_element_type=F32)
    alast = acum[q - 1:q, :]
    hpg = heads // SSM_GROUPS
    for g in range(SSM_GROUPS):
        bg = bm[:, g * SSM_STATE:(g + 1) * SSM_STATE].astype(BF16)
        cg = cm[:, g * SSM_STATE:(g + 1) * SSM_STATE].astype(BF16)
        cbm = _dot_nt(cg, bg)
        for e in range(hpg):
            hh = g * hpg + e
            a_col = acum[:, hh:hh + 1]
            a_row = acumt[hh:hh + 1, :]
            decay = jnp.exp(jnp.where(causal, a_col - a_row, NEG))
            wm = cbm * decay * dtt[hh:hh + 1, :]
            xh = xs[:, hh * SSM_HEAD_DIM:(hh + 1) * SSM_HEAD_DIM]
            hprev = h_ref[hh]
            y_h = (jnp.dot(wm.astype(BF16), xh.astype(BF16), preferred_element_type=F32)
                   + _dot_nt(cg, hprev.astype(BF16)) * jnp.exp(a_col))
            yacc_ref[:, hh * SSM_HEAD_DIM:(hh + 1) * SSM_HEAD_DIM] = y_h
            al = alast[:, hh:hh + 1]
            w_end = jnp.exp(al - a_col) * dt[:, hh:hh + 1]
            h_ref[hh] = jnp.exp(al) * hprev + _dot_tn((xh * w_end).astype(BF16), bg)

    y = (yacc_ref[...] + xs * dvec_ref[...]) * _silu(z_ref[...])
    gsz = inner // SSM_GROUPS
    for g in range(SSM_GROUPS):
        v = y[:, g * gsz:(g + 1) * gsz]
        ms = jnp.mean(v * v, axis=-1, keepdims=True)
        y_ref[:, g * gsz:(g + 1) * gsz] = (v * lax.rsqrt(ms + EPS) * gn_ref[:, g * gsz:(g + 1) * gsz]).astype(y_ref.dtype)

    @pl.when(c == nc - 1)
    def _():
        sout_ref[...] = h_ref[...]


def ssd_branch(xbc, z, small, small_t, row_off, hist, h0, cw, cb, dt_bias, a_log, d_skip, norm_g, batch, t_len, out_dtype,
               n_out=None):
    conv_dim = xbc.shape[1]
    inner = z.shape[1]
    heads = dt_bias.shape[0]
    assert heads == DT_COLS
    q = math.gcd(t_len, SSM_CHUNK)
    nc = t_len // q
    assert row_off % q == 0
    r0 = row_off // q
    pad = LANES - heads
    dtb = jnp.pad(dt_bias, (0, pad)).reshape(1, LANES)
    a_neg = jnp.pad(-jnp.exp(a_log), (0, pad)).reshape(1, LANES)
    dvec = jnp.repeat(d_skip, SSM_HEAD_DIM).reshape(1, inner)
    kern = functools.partial(_ssd_kernel, q=q, nc=nc, inner=inner, heads=heads)
    row = lambda w: pl.BlockSpec((1, w), lambda bi, ci: (0, 0))
    col = lambda: pl.BlockSpec((LANES, 1), lambda bi, ci: (0, 0))
    tok = lambda w: pl.BlockSpec((q, w), lambda bi, ci: (r0 + bi * nc + ci, 0))
    return pl.pallas_call(
        kern,
        grid=(batch, nc),
        in_specs=[tok(conv_dim), tok(inner), tok(LANES),
                  pl.BlockSpec((None, LANES, q), lambda bi, ci: (bi, 0, ci)),
                  pl.BlockSpec((None, HIST_S, conv_dim), lambda bi, ci: (bi, 0, 0)),
                  pl.BlockSpec((None, heads, SSM_HEAD_DIM, SSM_STATE), lambda bi, ci: (bi, 0, 0, 0)),
                  pl.BlockSpec((SSM_CONV_W, conv_dim), lambda bi, ci: (0, 0)), row(conv_dim),
                  row(LANES), col(), row(LANES), col(), row(inner), row(inner)],
        out_specs=[pl.BlockSpec((q, inner), lambda bi, ci: (bi * nc + ci, 0)),
                   pl.BlockSpec((None, HIST_S, conv_dim), lambda bi, ci: (bi, 0, 0)),
                   pl.BlockSpec((None, heads, SSM_HEAD_DIM, SSM_STATE), lambda bi, ci: (bi, 0, 0, 0))],
        out_shape=[jax.ShapeDtypeStruct((n_out or batch * t_len, inner), out_dtype),
                   jax.ShapeDtypeStruct((batch, HIST_S, conv_dim), F32),
                   jax.ShapeDtypeStruct((batch, heads, SSM_HEAD_DIM, SSM_STATE), F32)],
        scratch_shapes=[pltpu.VMEM((q + HIST_S, conv_dim), F32),
                        pltpu.VMEM((heads, SSM_HEAD_DIM, SSM_STATE), F32),
                        pltpu.VMEM((q, inner), F32), pltpu.VMEM((HIST_S, conv_dim), F32)],
        compiler_params=_params(("parallel", "arbitrary")),
        name="ssd",
    )(xbc, z, small, small_t, hist, h0, _round_bf16(cw), cb.reshape(1, conv_dim), dtb, dtb.reshape(LANES, 1),
      a_neg, a_neg.reshape(LANES, 1), dvec, norm_g.reshape(1, inner))


def _bucket_np(d):
    n = np.maximum(d, 0)
    exact = REL_BUCKETS // 2
    nf = np.maximum(n, 1).astype(np.float32)
    large = exact + (np.log(nf / np.float32(exact)) / np.float32(math.log(REL_MAX_DIST / exact))
                     * np.float32(REL_BUCKETS - exact)).astype(np.int32)
    return np.where(n < exact, n, np.minimum(large, REL_BUCKETS - 1))


def _near_distances():
    i = np.arange(QCHUNK)[:, None]
    j = np.arange(LANES)[None, :]
    diag = i - j
    prev = QCHUNK + i - j
    band = np.where(j < 4, i + (BLK + 1) - BLK * j, 0)
    return np.stack([diag, prev, band])


def _expand_kernel(tab_ref, oh_ref, o_ref):
    o_ref[...] = jnp.dot(tab_ref[...], oh_ref[...], precision=HIGHEST, preferred_element_type=F32)


def expand_bias(rel_bias, dist):
    flat = dist.reshape(-1)
    m = flat.shape[0]
    assert m % LANES == 0
    onehot = (np.arange(REL_BUCKETS)[:, None] == _bucket_np(flat)[None, :]).astype(np.float32)
    tm = _tile(m, 8192, LANES)
    heads = rel_bias.shape[1]
    out = pl.pallas_call(
        _expand_kernel,
        grid=(m // tm,),
        in_specs=[pl.BlockSpec((heads, REL_BUCKETS), lambda i: (0, 0)), pl.BlockSpec((REL_BUCKETS, tm), lambda i: (0, i))],
        out_specs=pl.BlockSpec((heads, tm), lambda i: (0, i)),
        out_shape=jax.ShapeDtypeStruct((heads, m), F32),
        compiler_params=_params(("parallel",)),
        name="bias_expand",
    )(rel_bias.T, jnp.asarray(onehot))
    return out.reshape((heads,) + dist.shape)


def _select_blocks(imp, lane, axis=-1):
    sel = jnp.zeros(imp.shape, F32)
    big = jnp.int32(1 << 30)
    v = imp
    for _ in range(TOPK):
        mx = jnp.max(v, axis=axis, keepdims=True)
        idx = jnp.min(jnp.where(v == mx, lane, big), axis=axis, keepdims=True)
        hit = lane == idx
        sel = jnp.where(hit, 1.0, sel)
        v = jnp.where(hit, -3e38, v)
    return sel


def _rep4(x):
    return jnp.concatenate([x, x, x, x], axis=0)


def _qgroup(q_ref, g):
    return jnp.concatenate([q_ref[:, (g * GQA + r) * HEAD_DIM:(g * GQA + r + 1) * HEAD_DIM] for r in range(GQA)], axis=0)


def _gate_col(sg, br, g):
    c0 = GATE_COL0 + br * N_HEADS + g * GQA
    return jnp.concatenate([sg[:, c0 + r:c0 + r + 1] for r in range(GQA)], axis=0)


def _rows_means_kernel(x_ref, o_ref):
    nb, w = o_ref.shape
    o_ref[...] = jnp.sum(x_ref[...].reshape(nb, BLK, w), axis=1) * (1.0 / BLK)


def prompt_block_means(kv, batch, t_len):
    w = 2 * KV_GROUPS * HEAD_DIM
    nb = t_len // BLK
    return pl.pallas_call(
        _rows_means_kernel,
        grid=(batch,),
        in_specs=[pl.BlockSpec((t_len, w), lambda b: (b, 0))],
        out_specs=pl.BlockSpec((None, nb, w), lambda b: (b, 0, 0)),
        out_shape=jax.ShapeDtypeStruct((batch, nb, w), F32),
        compiler_params=_params(("parallel",)),
        name="cmp_means_prompt",
    )(kv)


def _page_specs(pps, layer):
    def spec(k):
        return pl.BlockSpec((None, None, 4, KV_GROUPS, HEAD_DIM, 2 * BLK),
                            lambda b, j, pt: (pt[b, j * pps + k], layer, 0, 0, 0, 0))
    return [spec(k) for k in range(pps)]


def _nsa_prompt_kernel(q_ref, ksel_ref, vsel_ref, kwin_ref, vwin_ref, kc_ref, gn_ref, tab_ref, band_ref, cst_ref, e_ref, o_ref,
                       s_ref, sel_ref, oc_ref, mrun_ref, acc_ref, *, nb):
    c = pl.program_id(1)
    qc = QCHUNK
    rows = GQA * qc
    kw = KV_GROUPS * HEAD_DIM
    span = SUPER * qc
    blk = lax.broadcasted_iota(jnp.int32, (nb, rows), 0)
    qpos = c * qc + (lax.broadcasted_iota(jnp.int32, (nb, rows), 1) & (qc - 1))
    okc = (qpos - BLK * blk - (BLK - 1)) >= 0
    blk1 = lax.broadcasted_iota(jnp.int32, (nb, qc), 0)
    qi1 = lax.broadcasted_iota(jnp.int32, (nb, qc), 1)
    okc1 = (c * qc + qi1 - BLK * blk1 - (BLK - 1)) >= 0
    cur = 2 * c + (qi1 >= BLK).astype(jnp.int32)
    for g in range(KV_GROUPS):
        qg = _qgroup(q_ref, g)
        kcg = kc_ref[:, g * HEAD_DIM:(g + 1) * HEAD_DIM].astype(BF16)
        vcg = kc_ref[:, kw + g * HEAD_DIM:kw + (g + 1) * HEAD_DIM].astype(BF16)
        bias_c = jnp.broadcast_to(cst_ref[g], (nb, rows))
        for u in range(4):
            bias_c = jnp.where(blk == 2 * c - 2 + u, band_ref[g, u:u + 1, :], bias_c)
        s = jnp.where(okc, _dot_nt(kcg, qg) + bias_c, NEG)
        e = jnp.exp(s - jnp.max(s, axis=0, keepdims=True))
        p = jnp.where(okc, e / jnp.sum(e, axis=0, keepdims=True), 0.0)
        oc_ref[g] = _dot_tn(p.astype(BF16), vcg)
        psum = p[:, 0:qc] + p[:, qc:2 * qc] + p[:, 2 * qc:3 * qc] + p[:, 3 * qc:4 * qc]
        imp = jnp.where(blk1 == cur, FORCE, jnp.where(okc1, psum, -1.0))
        selm = jnp.where(blk1 <= cur, _select_blocks(imp, blk1, axis=0), 0.0)
        sel_ref[g] = (_dot_tn(selm.astype(BF16), e_ref[...]) - 1.0) * (-NEG)

    sg = jax.nn.sigmoid(gn_ref[...])
    n_super = c // SUPER + 1
    w_tiles = WINDOW // qc + 1
    st0 = jnp.maximum(c - (w_tiles - 1), 0)
    for g in range(KV_GROUPS):
        qg = _qgroup(q_ref, g)
        mrun_ref[...] = jnp.full((rows, LANES), NEG, F32)

        def scores(st, carry):
            k = ksel_ref[pl.ds(pl.multiple_of(st * span, span), span), g * HEAD_DIM:(g + 1) * HEAD_DIM]
            s = _dot_nt(qg, k)
            mrun = mrun_ref[...]
            for j in range(SUPER):
                kt = st * SUPER + j
                idx = jnp.where(kt == c, 0, jnp.where(kt == c - 1, 1, 2))
                off = pl.multiple_of(kt * qc, qc)
                sj = s[:, j * qc:(j + 1) * qc] + tab_ref[g, idx] + _rep4(sel_ref[g, :, pl.ds(off, qc)])
                s_ref[:, pl.ds(off, qc)] = sj
                mrun = jnp.maximum(mrun, sj)
            mrun_ref[...] = mrun
            return carry

        lax.fori_loop(0, n_super, scores, 0)
        m = jnp.max(mrun_ref[...], axis=-1, keepdims=True)
        mrun_ref[...] = jnp.zeros((rows, LANES), F32)

        def exps(st, carry):
            off = pl.multiple_of(st * span, span)
            e = jnp.exp(s_ref[:, pl.ds(off, span)] - m)
            s_ref[:, pl.ds(off, span)] = e
            lrun = mrun_ref[...]
            for j in range(SUPER):
                lrun = lrun + e[:, j * qc:(j + 1) * qc]
            mrun_ref[...] = lrun
            return carry

        lax.fori_loop(0, n_super, exps, 0)
        denom = jnp.sum(mrun_ref[...], axis=-1, keepdims=True)
        acc_ref[...] = jnp.zeros((rows, HEAD_DIM), F32)

        def values(st, carry):
            off = pl.multiple_of(st * span, span)
            p = (s_ref[:, pl.ds(off, span)] / denom).astype(BF16)
            acc_ref[...] += jnp.dot(p, vsel_ref[pl.ds(off, span), g * HEAD_DIM:(g + 1) * HEAD_DIM], preferred_element_type=F32)
            return carry

        lax.fori_loop(0, n_super, values, 0)
        o_s = acc_ref[...]
        r0 = pl.multiple_of(st0 * qc, qc)
        s = _dot_nt(qg, kwin_ref[pl.ds(r0, w_tiles * qc), g * HEAD_DIM:(g + 1) * HEAD_DIM])
        parts = []
        for w in range(w_tiles):
            kt = st0 + w
            idx = jnp.where(kt > c, 4, jnp.where(kt == c, 0, jnp.where(kt == c - 1, 1,
                                                                         jnp.where(kt == c - (w_tiles - 1), 3, 2))))
            parts.append(s[:, w * qc:(w + 1) * qc] + tab_ref[g, idx])
        sw = jnp.concatenate(parts, axis=1)
        ew = jnp.exp(sw - jnp.max(sw, axis=-1, keepdims=True))
        pw = (ew / jnp.sum(ew, axis=-1, keepdims=True)).astype(BF16)
        o_w = jnp.dot(pw, vwin_ref[pl.ds(r0, w_tiles * qc), g * HEAD_DIM:(g + 1) * HEAD_DIM],
                      preferred_element_type=F32)
        tot = _gate_col(sg, 0, g) * oc_ref[g] + _gate_col(sg, 1, g) * o_s + _gate_col(sg, 2, g) * o_w
        for r in range(GQA):
            o_ref[:, (g * GQA + r) * HEAD_DIM:(g * GQA + r + 1) * HEAD_DIM] = tot[r * qc:(r + 1) * qc].astype(o_ref.dtype)


def nsa_prompt(q, kvb, kc, small, tabs, band, cst, batch, t_len, n_out=None):
    assert t_len % (SUPER * QCHUNK) == 0 and t_len >= WINDOW + QCHUNK
    nch = t_len // QCHUNK
    nb = t_len // BLK
    assert nb % SUBLANES == 0
    rows = GQA * QCHUNK
    hd = N_HEADS * HEAD_DIM
    emat = (np.arange(nb)[:, None] == (np.arange(t_len)[None, :] // BLK)).astype(np.float32)
    kern = functools.partial(_nsa_prompt_kernel, nb=nb)
    full = lambda shp: pl.BlockSpec(shp, lambda b, c: (0,) * len(shp))
    kind = lambda k: pl.BlockSpec((t_len, KV_GROUPS * HEAD_DIM), lambda b, c: (b, k))
    return pl.pallas_call(
        kern,
        grid=(batch, nch),
        in_specs=[pl.BlockSpec((QCHUNK, hd), lambda b, c: (b * nch + c, 0)),
                  kind(2), kind(3), kind(4), kind(5),
                  pl.BlockSpec((None, nb, kc.shape[2]), lambda b, c: (b, 0, 0)),
                  pl.BlockSpec((QCHUNK, LANES), lambda b, c: (b * nch + c, 0)),
                  full((KV_GROUPS, 5, rows, LANES)), full((KV_GROUPS, SUBLANES, rows)), full((KV_GROUPS, 1, rows)),
                  full((nb, t_len))],
        out_specs=pl.BlockSpec((QCHUNK, hd), lambda b, c: (b * nch + c, 0)),
        out_shape=jax.ShapeDtypeStruct((n_out or batch * t_len, hd), BF16),
        scratch_shapes=[pltpu.VMEM((rows, t_len), F32),
                        pltpu.VMEM((KV_GROUPS, QCHUNK, t_len), F32),
                        pltpu.VMEM((KV_GROUPS, rows, HEAD_DIM), F32),
                        pltpu.VMEM((rows, LANES), F32), pltpu.VMEM((rows, HEAD_DIM), F32)],
        compiler_params=_params(("parallel", "arbitrary")),
        name="nsa_prompt",
    )(q, kvb, kvb, kvb, kvb, kc, small, tabs, band, cst, jnp.asarray(emat, BF16))


def _nsa_sample_kernel(pt_ref, q_ref, kvn_ref, gn_ref, avg_ref, win_ref, near_ref, wtab_ref, cst_ref, e_ref,
                       *refs, pps, n_steps, past, ts, nbl):
    pages = refs[:pps]
    o_ref = refs[pps]
    s_ref, kc_ref, vbuf_ref, kn_ref = refs[pps + 1:]
    j = pl.program_id(1)
    rows = GQA * ts
    kw = KV_GROUPS * HEAD_DIM
    nb_past = past // BLK
    pg = 2 * BLK
    lane = lax.broadcasted_iota(jnp.int32, (rows, LANES), 1)
    ti = lax.broadcasted_iota(jnp.int32, (rows, LANES), 0) & (ts - 1)

    def qgroup(g):
        return _qgroup(q_ref, g).astype(BF16)

    @pl.when(j == 0)
    def _():
        kn_ref[...] = jnp.zeros(kn_ref.shape, F32)
        kn_ref[0:ts, :] = kvn_ref[:, 2 * kw:6 * kw]
        kc_ref[...] = jnp.zeros(kc_ref.shape, F32)

    is_last = j == n_steps - 1
    base = j * (pps * pg)
    qgs = [qgroup(g) for g in range(KV_GROUPS)]
    avg = avg_ref[...]
    blk0 = pl.multiple_of(j * (2 * pps), 2 * pps)
    for kind in range(2):
        for g in range(KV_GROUPS):
            cat = jnp.concatenate([pgr[kind, g] for pgr in pages], axis=1)
            hi = cat.astype(BF16)
            lo = (cat - hi.astype(F32)).astype(BF16)
            kc_ref[kind, g, pl.ds(blk0, 2 * pps), :] = _dot_nt(avg, hi) + _dot_nt(avg, lo)
    for k in range(pps):
        off = pl.multiple_of(base + k * pg, pg)
        for g in range(KV_GROUPS):
            vbuf_ref[g, :, pl.ds(off, pg)] = pages[k][3, g].astype(BF16)
            sk = jnp.dot(qgs[g], pages[k][2, g].astype(BF16), preferred_element_type=F32)
            bias = jnp.where(is_last, near_ref[g, 1], cst_ref[g]) if k == pps - 1 else cst_ref[g]
            s_ref[g, :, pl.ds(off, pg)] = sk + bias

    @pl.when(is_last)
    def _():
        sg = jax.nn.sigmoid(gn_ref[...])
        wl = win_ref.shape[-1]
        lane_w = lax.broadcasted_iota(jnp.int32, (rows, wl + LANES), 1)
        t_w = lax.broadcasted_iota(jnp.int32, (rows, wl + LANES), 0) & (ts - 1)
        okw = jnp.where(lane_w < wl, lane_w - t_w, t_w - (lane_w - wl)) >= 0
        lane_b = lax.broadcasted_iota(jnp.int32, (rows, nbl), 1)
        t_b = lax.broadcasted_iota(jnp.int32, (rows, nbl), 0) & (ts - 1)
        okc = (past + t_b - BLK * lane_b - (BLK - 1)) >= 0
        lane1 = lax.broadcasted_iota(jnp.int32, (ts, nbl), 1)
        okc1 = (past + lax.broadcasted_iota(jnp.int32, (ts, nbl), 0) - BLK * lane1 - (BLK - 1)) >= 0
        for g in range(KV_GROUPS):
            qg = qgroup(g)
            cst = cst_ref[g]
            band = near_ref[g, 2]
            kcg = kc_ref[0, g].astype(BF16)
            vcg = kc_ref[1, g].astype(BF16)
            bias_c = jnp.where(lane_b == nb_past - 2, band[:, 0:1], jnp.where(lane_b == nb_past - 1, band[:, 1:2], cst))
            s = jnp.where(okc, _dot_nt(qg, kcg) + bias_c, NEG)
            e = jnp.exp(s - jnp.max(s, axis=-1, keepdims=True))
            p = jnp.where(okc, e / jnp.sum(e, axis=-1, keepdims=True), 0.0)
            o_c = jnp.dot(p.astype(BF16), vcg, preferred_element_type=F32)
            psum = p[0:ts] + p[ts:2 * ts] + p[2 * ts:3 * ts] + p[3 * ts:4 * ts]
            imp = jnp.where(lane1 == nb_past, FORCE, jnp.where(okc1, psum, -1.0))
            imp = jnp.where(lane1 <= nb_past, imp, -3e38)
            selm = _select_blocks(imp, lane1)
            selm16 = jnp.concatenate([selm, jnp.zeros_like(selm)], axis=0).astype(BF16)
            madd = (jnp.dot(selm16, e_ref[...], preferred_element_type=F32)[0:ts] - 1.0) * (-NEG)
            kn = kn_ref[:, g * HEAD_DIM:(g + 1) * HEAD_DIM].astype(BF16)
            vn = kn_ref[:, kw + g * HEAD_DIM:kw + (g + 1) * HEAD_DIM].astype(BF16)
            s_new = jnp.where(lane <= ti, _dot_nt(qg, kn) + near_ref[g, 0], NEG)
            s = jnp.concatenate([s_ref[g] + _rep4(madd), s_new], axis=1)
            e = jnp.exp(s - jnp.max(s, axis=-1, keepdims=True))
            p = e / jnp.sum(e, axis=-1, keepdims=True)
            o_s = (_dot_nt(p[:, :past].astype(BF16), vbuf_ref[g])
                   + jnp.dot(p[:, past:].astype(BF16), vn, preferred_element_type=F32))
            kwin = win_ref[0, g].astype(BF16)
            vwin = win_ref[1, g].astype(BF16)
            knw = kn_ref[:, 2 * kw + g * HEAD_DIM:2 * kw + (g + 1) * HEAD_DIM].astype(BF16)
            vnw = kn_ref[:, 3 * kw + g * HEAD_DIM:3 * kw + (g + 1) * HEAD_DIM].astype(BF16)
            s = jnp.concatenate([jnp.dot(qg, kwin, preferred_element_type=F32), _dot_nt(qg, knw)], axis=1) + wtab_ref[g]
            s = jnp.where(okw, s, NEG)
            e = jnp.exp(s - jnp.max(s, axis=-1, keepdims=True))
            p = e / jnp.sum(e, axis=-1, keepdims=True)
            o_w = (_dot_nt(p[:, :wl].astype(BF16), vwin)
                   + jnp.dot(p[:, wl:].astype(BF16), vnw, preferred_element_type=F32))
            tot = _gate_col(sg, 0, g) * o_c + _gate_col(sg, 1, g) * o_s + _gate_col(sg, 2, g) * o_w
            for r in range(GQA):
                o_ref[:, (g * GQA + r) * HEAD_DIM:(g * GQA + r + 1) * HEAD_DIM] = tot[r * ts:(r + 1) * ts]


def nsa_sample(q3, kvn3, small3, win_t, near_s, wtab, cst_s, cache_t, page_table, layer, pps, nbl):
    batch, ts, hd = q3.shape
    n_pages = page_table.shape[1]
    n_steps = n_pages // pps
    past = n_pages * 2 * BLK
    wl = win_t.shape[-1]
    avg = np.where(np.arange(2 * pps)[:, None] == np.arange(pps * 2 * BLK)[None, :] // BLK, 1.0 / BLK, 0.0)
    rows = GQA * ts
    kw = KV_GROUPS * HEAD_DIM
    assert ts == SUBLANES and past % QCHUNK == 0 and wl % LANES == 0
    emat = (np.arange(nbl)[:, None] == (np.arange(past)[None, :] // BLK)).astype(np.float32)
    kern = functools.partial(_nsa_sample_kernel, pps=pps, n_steps=n_steps, past=past, ts=ts, nbl=nbl)
    c3 = lambda shp: pl.BlockSpec(shp, lambda b, j, pt: (0,) * len(shp))
    return pl.pallas_call(
        kern,
        grid_spec=pltpu.PrefetchScalarGridSpec(
            num_scalar_prefetch=1, grid=(batch, n_steps),
            in_specs=[pl.BlockSpec((None, ts, hd), lambda b, j, pt: (b, 0, 0)),
                      pl.BlockSpec((None, ts, kvn3.shape[2]), lambda b, j, pt: (b, 0, 0)),
                      pl.BlockSpec((None, ts, LANES), lambda b, j, pt: (b, 0, 0)),
                      c3((2 * pps, pps * 2 * BLK)),
                      pl.BlockSpec((None, None, 2, KV_GROUPS, HEAD_DIM, wl), lambda b, j, pt: (layer, b, 0, 0, 0, 0)),
                      c3((KV_GROUPS, 3, rows, LANES)), c3((KV_GROUPS, rows, wl + LANES)), c3((KV_GROUPS, rows, 1)),
                      c3((nbl, past))] + _page_specs(pps, layer),
            out_specs=pl.BlockSpec((None, ts, hd), lambda b, j, pt: (b, 0, 0)),
            scratch_shapes=[pltpu.VMEM((KV_GROUPS, rows, past), F32),
                            pltpu.VMEM((2, KV_GROUPS, nbl, HEAD_DIM), F32),
                            pltpu.VMEM((KV_GROUPS, HEAD_DIM, past), BF16),
                            pltpu.VMEM((LANES, 4 * kw), F32)]),
        out_shape=jax.ShapeDtypeStruct((batch, ts, hd), F32),
        compiler_params=_params(("parallel", "arbitrary")),
        name="nsa_sample",
    )(page_table, q3, kvn3, small3, jnp.asarray(avg, BF16), win_t, near_s, wtab, cst_s, jnp.asarray(emat, BF16),
      *([cache_t] * pps))


def _merge_kernel(c_ref, o_ref, y_ref, g0_ref, g1_ref, g2_ref, wa_ref, wn_ref, ws_ref, out_ref):
    oa = jnp.dot(c_ref[...], wa_ref[...], preferred_element_type=F32)
    ob = jnp.dot(o_ref[...], wn_ref[...], preferred_element_type=F32)
    oc = jnp.dot(y_ref[...], ws_ref[...], preferred_element_type=F32)
    mixed = (jax.nn.sigmoid(g0_ref[...]) * oa + jax.nn.sigmoid(g1_ref[...]) * ob + jax.nn.sigmoid(g2_ref[...]) * oc)
    out_ref[...] = mixed.astype(out_ref.dtype)


def merge_branches(c, o, y, gmix, wa, wn, ws):
    n = c.shape[0]
    d = wa.shape[1]
    tm = _tile(n, 768, BF16_ROWS)
    tn = _tile(d, 512, LANES)
    nj = d // tn
    act = lambda a: pl.BlockSpec((tm, a.shape[1]), lambda i, j: (i, 0))
    wsp = lambda w: pl.BlockSpec((w.shape[0], tn), lambda i, j: (0, j))
    gsp = lambda br: pl.BlockSpec((tm, tn), lambda i, j: (i, br * nj + j))
    return pl.pallas_call(
        _merge_kernel,
        grid=(n // tm, nj),
        in_specs=[act(c), act(o), act(y), gsp(0), gsp(1), gsp(2), wsp(wa), wsp(wn), wsp(ws)],
        out_specs=pl.BlockSpec((tm, tn), lambda i, j: (i, j)),
        out_shape=jax.ShapeDtypeStruct((n, d), BF16),
        compiler_params=_params(("parallel", "parallel")),
        name="merge",
    )(c, o, y, gmix, gmix, gmix, wa, wn, ws)


def _outproj_kernel(a_ref, w_ref, x_ref, o_ref):
    o_ref[...] = x_ref[...] + jnp.dot(a_ref[...], w_ref[...], preferred_element_type=F32)


def out_projection(a, w, x):
    n, k = a.shape
    d = w.shape[1]
    tm = _tile(n, 768, BF16_ROWS)
    tn = _tile(d, 1024, LANES)
    return pl.pallas_call(
        _outproj_kernel,
        grid=(n // tm, d // tn),
        in_specs=[pl.BlockSpec((tm, k), lambda i, j: (i, 0)), pl.BlockSpec((k, tn), lambda i, j: (0, j)),
                  pl.BlockSpec((tm, tn), lambda i, j: (i, j))],
        out_specs=pl.BlockSpec((tm, tn), lambda i, j: (i, j)),
        out_shape=jax.ShapeDtypeStruct((n, d), F32),
        compiler_params=_params(("parallel", "parallel")),
        name="out_proj",
    )(a, w, x)


def _router_kernel(x_ref, g_ref, w_ref, b_ref, h_ref, r_ref):
    x = x_ref[...]
    ms = jnp.mean(x * x, axis=-1, keepdims=True)
    h = (x * lax.rsqrt(ms + EPS) * g_ref[...]).astype(BF16)
    h_ref[...] = h
    logits = jnp.dot(h, w_ref[...], preferred_element_type=F32) + b_ref[...]
    lane = lax.broadcasted_iota(jnp.int32, logits.shape, 1)
    big = jnp.int32(1 << 30)
    gl = jnp.where(lane < MOE_GROUPS, logits, NEG)
    gmax = jnp.max(gl, axis=-1, keepdims=True)
    gidx = jnp.min(jnp.where(gl == gmax, lane, big), axis=-1, keepdims=True)
    g_w = 1.0 / jnp.sum(jnp.exp(gl - gmax), axis=-1, keepdims=True)
    eid = lane - MOE_GROUPS
    in_grp = (eid >= 0) & (eid < MOE_GROUPS * MOE_EPG) & ((eid >> 3) == gidx)
    el = jnp.where(in_grp, logits, NEG)
    m1 = jnp.max(el, axis=-1, keepdims=True)
    i1 = jnp.min(jnp.where(el == m1, lane, big), axis=-1, keepdims=True)
    el2 = jnp.where(lane == i1, NEG, el)
    m2 = jnp.max(el2, axis=-1, keepdims=True)
    i2 = jnp.min(jnp.where(el2 == m2, lane, big), axis=-1, keepdims=True)
    p2 = jnp.exp(m2 - m1)
    w1 = g_w / (1.0 + p2)
    w2 = g_w * p2 / (1.0 + p2)
    r_ref[...] = jnp.where(lane == 0, (i1 - MOE_GROUPS).astype(F32),
                           jnp.where(lane == 1, (i2 - MOE_GROUPS).astype(F32),
                                     jnp.where(lane == 2, w1, jnp.where(lane == 3, w2, 0.0))))


def moe_router(x, g, wr, br):
    n, d = x.shape
    tm = _tile(n, 256, BF16_ROWS)
    return pl.pallas_call(
        _router_kernel,
        grid=(n // tm,),
        in_specs=[pl.BlockSpec((tm, d), lambda i: (i, 0)), pl.BlockSpec((1, d), lambda i: (0, 0)),
                  pl.BlockSpec((d, LANES), lambda i: (0, 0)), pl.BlockSpec((1, LANES), lambda i: (0, 0))],
        out_specs=[pl.BlockSpec((tm, d), lambda i: (i, 0)), pl.BlockSpec((tm, LANES), lambda i: (i, 0))],
        out_shape=[jax.ShapeDtypeStruct((n, d), BF16), jax.ShapeDtypeStruct((n, LANES), F32)],
        compiler_params=_params(("parallel",)),
        name="moe_router",
    )(x, g.reshape(1, d), wr, br)


def _expert_kernel(te_ref, nu_ref, x_ref, rw_ref, wg_ref, wu_ref, wd_ref, y_ref, wgb_ref, wub_ref, wdb_ref):
    t = pl.program_id(0)
    fresh = jnp.logical_or(t == 0, te_ref[t] != te_ref[jnp.maximum(t - 1, 0)])

    @pl.when(jnp.logical_and(fresh, t < nu_ref[0]))
    def _():
        wgb_ref[...] = wg_ref[...].astype(BF16)
        wub_ref[...] = wu_ref[...].astype(BF16)
        wdb_ref[...] = wd_ref[...].astype(BF16)

    @pl.when(t < nu_ref[0])
    def _():
        x = x_ref[...]
        a = jnp.dot(x, wgb_ref[...], preferred_element_type=F32)
        u = jnp.dot(x, wub_ref[...], preferred_element_type=F32)
        hid = (_silu(a) * u * rw_ref[...]).astype(BF16)
        y_ref[...] = jnp.dot(hid, wdb_ref[...], preferred_element_type=F32)


def moe_experts(xs, row_w, tile_e, n_used, wg, wu, wd, layer, tme):
    r, d = xs.shape
    ff = wg.shape[3]
    nt = r // tme
    return pl.pallas_call(
        _expert_kernel,
        grid_spec=pltpu.PrefetchScalarGridSpec(
            num_scalar_prefetch=2, grid=(nt,),
            in_specs=[pl.BlockSpec((tme, d), lambda t, te, nu: (t, 0)),
                      pl.BlockSpec((tme, 1), lambda t, te, nu: (t, 0)),
                      pl.BlockSpec((None, None, d, ff), lambda t, te, nu: (layer, te[t], 0, 0)),
                      pl.BlockSpec((None, None, d, ff), lambda t, te, nu: (layer, te[t], 0, 0)),
                      pl.BlockSpec((None, None, ff, d), lambda t, te, nu: (layer, te[t], 0, 0))],
            out_specs=pl.BlockSpec((tme, d), lambda t, te, nu: (t, 0)),
            scratch_shapes=[pltpu.VMEM((d, ff), BF16), pltpu.VMEM((d, ff), BF16), pltpu.VMEM((ff, d), BF16)]),
        out_shape=jax.ShapeDtypeStruct((r, d), F32),
        compiler_params=_params(("arbitrary",)),
        name="moe_experts",
    )(tile_e, n_used, xs, row_w, wg, wu, wd)


def moe_layer(x, g, wrg, brg, wre, bre, wg, wu, wd, layer):
    n, d = x.shape
    n_exp = wg.shape[1]
    wr = jnp.concatenate([wrg, wre.transpose(1, 0, 2).reshape(d, n_exp)], axis=1)
    wr = jnp.pad(wr, ((0, 0), (0, LANES - wr.shape[1])))
    br = jnp.pad(jnp.concatenate([brg, bre.reshape(-1)]), (0, LANES - MOE_GROUPS - n_exp)).reshape(1, LANES)
    h, route = moe_router(x, g, wr.astype(BF16), br)
    tme = 256
    e_flat = route[:, 0:MOE_TOPK].astype(jnp.int32).T.reshape(-1)
    w_flat = route[:, MOE_TOPK:2 * MOE_TOPK].T.reshape(-1)
    na = e_flat.shape[0]
    nt = -(-na // tme) + n_exp
    order = jnp.argsort(e_flat, stable=True)
    counts = jnp.sum(e_flat[:, None] == jnp.arange(n_exp)[None, :], axis=0).astype(jnp.int32)
    pc = ((counts + tme - 1) // tme) * tme
    pend = jnp.cumsum(pc)
    pstart = pend - pc
    ustart = jnp.cumsum(counts) - counts
    se = e_flat[order]
    dest = pstart[se] + jnp.arange(na, dtype=jnp.int32) - ustart[se]
    row_tok = jnp.zeros((nt * tme,), jnp.int32).at[dest].set((order % n).astype(jnp.int32))
    row_w = jnp.zeros((nt * tme,), F32).at[dest].set(w_flat[order])
    dest_flat = jnp.zeros((na,), jnp.int32).at[order].set(dest)
    tile_e = jnp.minimum(jnp.searchsorted(pend, jnp.arange(nt, dtype=jnp.int32) * tme, side="right"), n_exp - 1).astype(jnp.int32)
    n_used = (pend[-1] // tme).astype(jnp.int32).reshape(1)
    xs = jnp.take(h, row_tok, axis=0)
    ys = moe_experts(xs, row_w.reshape(-1, 1), tile_e, n_used, wg, wu, wd, layer, tme)
    return x + jnp.take(ys, dest_flat[:n], axis=0) + jnp.take(ys, dest_flat[n:], axis=0)


def kernel(x_prompt, x_sample, cache_nsa_kv, cache_nsa_win, state_conv_a, state_conv_ssm, state_ssm, page_table,
           rel_bias, norm_mix_g, w_in, conv_a_w, conv_a_b, ln_a_g, ln_a_b, w_a_out, w_nsa_out, conv_s_w, conv_s_b,
           dt_bias, a_log, d_skip, ssm_norm_g, w_ssm_out, w_out, norm_ffn_g, w_router_group, b_router_group,
           w_router_expert, b_router_expert, w_gate, w_up, w_down, final_norm_g):
    bp, tp, d = x_prompt.shape
    bs, ts, _ = x_sample.shape
    depth = w_in.shape[0]
    n_p, n_s = bp * tp, bs * ts
    ch = conv_a_w.shape[2]
    conv_dim = conv_s_w.shape[2]
    heads_ssm = dt_bias.shape[1]
    inner = heads_ssm * SSM_HEAD_DIM
    hd = N_HEADS * HEAD_DIM
    kw = KV_GROUPS * HEAD_DIM
    kvw = KV_KINDS * kw
    n_pages = page_table.shape[1]
    page = cache_nsa_kv.shape[2]
    past = n_pages * page
    wl = cache_nsa_win.shape[2]
    assert page == 2 * BLK and wl == min(WINDOW, past) and tp >= WINDOW
    pps = _tile(n_pages, 16, 1)
    rows_p = GQA * QCHUNK

    splits = (2 * ch, hd, kvw, 3 * N_HEADS, inner, conv_dim, heads_ssm, 3 * d)
    offs = np.concatenate([[0], np.cumsum(splits)])
    o_u, o_q, o_kv, o_gn, o_z, o_x, o_dt, o_gm = [int(v) for v in offs[:-1]]

    near_d = _near_distances()
    tj = np.arange(wl + LANES)[None, :]
    win_d = np.where(tj < wl + ts, wl + np.arange(ts)[:, None] - tj, 0)
    near_f = expand_bias(rel_bias, near_d)
    wtab_f = expand_bias(rel_bias, win_d)
    near_g = near_f.reshape(KV_GROUPS, GQA, 3, QCHUNK, LANES).transpose(0, 2, 1, 3, 4)
    near_p = near_g.reshape(KV_GROUPS, 3, rows_p, LANES)
    near_s = near_g[:, :, :, :ts].reshape(KV_GROUPS, 3, GQA * ts, LANES)
    wtab = wtab_f.reshape(KV_GROUPS, GQA * ts, wl + LANES)
    far = rel_bias[REL_BUCKETS - 1].reshape(KV_GROUPS, GQA, 1)
    cst_p = jnp.repeat(far, QCHUNK, axis=1).reshape(KV_GROUPS, rows_p, 1)
    cst_s = jnp.repeat(far, ts, axis=1).reshape(KV_GROUPS, GQA * ts, 1)
    ii = (np.arange(rows_p) % QCHUNK)[:, None]
    jj = np.arange(LANES)[None, :]
    cst_t = jnp.broadcast_to(cst_p, (KV_GROUPS, rows_p, LANES))
    band_t = jnp.pad(near_p[:, 2, :, 0:4].transpose(0, 2, 1), ((0, 0), (0, SUBLANES - 4), (0, 0)))
    tabs_p = jnp.stack([jnp.where(jj <= ii, near_p[:, 0], NEG), near_p[:, 1], cst_t,
                        jnp.where(jj >= ii, cst_t, NEG), jnp.full((KV_GROUPS, rows_p, LANES), NEG, F32)], axis=1)

    w_in_t = w_in.transpose(0, 2, 1)
    cache_t = cache_nsa_kv.transpose(0, 1, 3, 4, 5, 2)
    win_t = cache_nsa_win.transpose(0, 1, 3, 4, 5, 2)
    zeros_hist_a = jnp.zeros((bp, HIST_A, ch), F32)
    zeros_hist_s = jnp.zeros((bp, HIST_S, conv_dim), F32)
    zeros_ssm = jnp.zeros((bp, heads_ssm, SSM_HEAD_DIM, SSM_STATE), F32)
    nbl = -(-(past // BLK + 1) // LANES) * LANES

    x = jnp.concatenate([x_prompt.reshape(n_p, d), x_sample.reshape(n_s, d)], axis=0)
    kv_p, kv_s, win_p, win_s, ha_p, ha_s, hs_p, hs_s, st_p, st_s = ([] for _ in range(10))
    for l in range(depth):
        h = rmsnorm(x, norm_mix_g[l], BF16)
        (u_in,) = matmul(h, w_in_t, l, o_u, 2 * ch, [F32])
        (q,) = matmul(h, w_in_t, l, o_q, hd, [BF16], scale=HEAD_DIM ** -0.5)
        kv, kvb = matmul(h, w_in_t, l, o_kv, kvw, [F32, BF16])
        w_small = jnp.concatenate([w_in_t[l, o_dt:o_gm], w_in_t[l, o_gn:o_z],
                                   jnp.zeros((LANES - heads_ssm - 3 * N_HEADS, d), F32)], axis=0)
        (small,) = matmul(h, w_small[None], 0, 0, LANES, [F32])
        (z,) = matmul(h, w_in_t, l, o_z, inner, [F32])
        (xbc,) = matmul(h, w_in_t, l, o_x, conv_dim, [F32])
        (gmix,) = matmul(h, w_in_t, l, o_gm, 3 * d, [F32])

        hist_a = jnp.pad(state_conv_a[l], ((0, 0), (HIST_A - (CONV_A_W - 1), 0), (0, 0)))
        ca_args = (conv_a_w[l], conv_a_b[l], ln_a_g[l], ln_a_b[l])
        c_p, ha_new_p = conv_a_branch(u_in, 0, zeros_hist_a, *ca_args, bp, tp, BF16, n_out=n_p + n_s)
        c_s, ha_new_s = conv_a_branch(u_in, n_p, hist_a, *ca_args, bs, ts, F32)
        hist_s = jnp.pad(state_conv_ssm[l], ((0, 0), (HIST_S - (SSM_CONV_W - 1), 0), (0, 0)))
        ss_args = (conv_s_w[l], conv_s_b[l], dt_bias[l], a_log[l], d_skip[l], ssm_norm_g[l])
        small_tp = small[:n_p].reshape(bp, tp, LANES).transpose(0, 2, 1)
        small_ts = small[n_p:].reshape(bs, ts, LANES).transpose(0, 2, 1)
        y_p, hs_new_p, st_new_p = ssd_branch(xbc, z, small, small_tp, 0, zeros_hist_s, zeros_ssm, *ss_args, bp, tp, BF16,
                                             n_out=n_p + n_s)
        y_s, hs_new_s, st_new_s = ssd_branch(xbc, z, small, small_ts, n_p, hist_s, state_ssm[l], *ss_args, bs, ts, F32)
        kc_p = prompt_block_means(kv, bp, tp)
        o_p = nsa_prompt(q, kvb, kc_p, small, tabs_p, band_t, cst_p.reshape(KV_GROUPS, 1, rows_p), bp, tp, n_out=n_p + n_s)
        o_s = nsa_sample(q[n_p:].astype(F32).reshape(bs, ts, hd), kv[n_p:].reshape(bs, ts, kvw),
                         small[n_p:].reshape(bs, ts, LANES), win_t, near_s, wtab, cst_s, cache_t, page_table, l, pps, nbl)

        c_all = lax.dynamic_update_slice(c_p, c_s.astype(BF16), (n_p, 0))
        o_all = lax.dynamic_update_slice(o_p, o_s.reshape(n_s, hd).astype(BF16), (n_p, 0))
        y_all = lax.dynamic_update_slice(y_p, y_s.astype(BF16), (n_p, 0))
        mixed = merge_branches(c_all, o_all, y_all, gmix, w_a_out[l].astype(BF16), w_nsa_out[l].astype(BF16),
                               w_ssm_out[l].astype(BF16))
        x = out_projection(mixed, w_out[l].astype(BF16), x)
        x = moe_layer(x, norm_ffn_g[l], w_router_group[l], b_router_group[l], w_router_expert[l], b_router_expert[l],
                      w_gate, w_up, w_down, l)

        kvr = kv[:, :4 * kw]
        kv_p.append(kvr[:n_p].reshape(bp, tp, 4, KV_GROUPS, HEAD_DIM))
        kv_s.append(kvr[n_p:].reshape(bs, ts, 4, KV_GROUPS, HEAD_DIM))
        wr = kv[:, 4 * kw:]
        win_p.append(wr[:n_p].reshape(bp, tp, 2, KV_GROUPS, HEAD_DIM)[:, tp - WINDOW:])
        win_all = jnp.concatenate([cache_nsa_win[l], wr[n_p:].reshape(bs, ts, 2, KV_GROUPS, HEAD_DIM)], axis=1)
        keep = min(WINDOW, wl + ts)
        win_s.append(win_all[:, wl + ts - keep:])
        ha_p.append(ha_new_p[:, HIST_A - (CONV_A_W - 1):])
        ha_s.append(ha_new_s[:, HIST_A - (CONV_A_W - 1):])
        hs_p.append(hs_new_p[:, HIST_S - (SSM_CONV_W - 1):])
        hs_s.append(hs_new_s[:, HIST_S - (SSM_CONV_W - 1):])
        st_p.append(st_new_p)
        st_s.append(st_new_s)

    y = rmsnorm(x, final_norm_g, F32)
    return (y[:n_p].reshape(bp, tp, d), y[n_p:].reshape(bs, ts, d),
            jnp.stack(kv_p, axis=1), jnp.stack(kv_s, axis=1), jnp.stack(win_p), jnp.stack(win_s),
            jnp.stack(ha_p), jnp.stack(ha_s), jnp.stack(hs_p), jnp.stack(hs_s), jnp.stack(st_p), jnp.stack(st_s))
```

```python
import functools
import math

import numpy as np
import jax
import jax.numpy as jnp
from jax import lax
from jax.experimental import pallas as pl
from jax.experimental.pallas import tpu as pltpu

HEAD_DIM = 64
KV_GROUPS = 4
GQA = 4
N_HEADS = KV_GROUPS * GQA
KV_KINDS = 6
BLK = 64
TOPK = 8
WINDOW = 512
QCHUNK = 128
REL_BUCKETS = 32
REL_MAX_DIST = 128
CONV_A_W = 31
SSM_CONV_W = 4
SSM_HEAD_DIM = 64
SSM_GROUPS = 4
SSM_STATE = 128
SSM_CHUNK = 128
MOE_GROUPS = 4
MOE_EPG = 8
MOE_TOPK = 2
EPS = 1e-6
NEG = -1e30
FORCE = 1e4

LANES = 128
SUBLANES = 8
BF16_ROWS = 16
VMEM_LIMIT = 56 * 1024 * 1024

DT_COLS = 32
GATE_COL0 = DT_COLS
SUPER = 4

HIGHEST = lax.Precision.HIGHEST
F32 = jnp.float32
BF16 = jnp.bfloat16


def _params(sem, limit=VMEM_LIMIT):
    return pltpu.CompilerParams(dimension_semantics=sem, vmem_limit_bytes=limit)


def _tile(n, target, mult):
    best = None
    for t in range(mult, min(n, target) + 1, mult):
        if n % t == 0:
            best = t
    return best if best is not None else n


def _dot_nt(a, b):
    return lax.dot_general(a, b, (((1,), (1,)), ((), ())), preferred_element_type=F32)


def _dot_tn(a, b):
    return lax.dot_general(a, b, (((0,), (0,)), ((), ())), preferred_element_type=F32)


def _silu(x):
    return x * jax.nn.sigmoid(x)


def _softplus(x):
    return jnp.maximum(x, 0.0) + jnp.log(1.0 + jnp.exp(-jnp.abs(x)))


def _rms_kernel(x_ref, g_ref, o_ref):
    x = x_ref[...]
    ms = jnp.mean(x * x, axis=-1, keepdims=True)
    o_ref[...] = (x * lax.rsqrt(ms + EPS) * g_ref[...]).astype(o_ref.dtype)


def rmsnorm(x, g, out_dtype):
    n, d = x.shape
    tm = _tile(n, 1024, BF16_ROWS)
    return pl.pallas_call(
        _rms_kernel,
        grid=(n // tm,),
        in_specs=[pl.BlockSpec((tm, d), lambda i: (i, 0)), pl.BlockSpec((1, d), lambda i: (0, 0))],
        out_specs=pl.BlockSpec((tm, d), lambda i: (i, 0)),
        out_shape=jax.ShapeDtypeStruct((n, d), out_dtype),
        compiler_params=_params(("parallel",)),
        name="rmsnorm",
    )(x, g.reshape(1, d))


def _mm_kernel(a_ref, w_ref, *refs, scale):
    o_refs, wb_ref = refs[:-1], refs[-1]

    @pl.when(pl.program_id(1) == 0)
    def _():
        w = w_ref[0]
        wb_ref[...] = (w if scale == 1.0 else w * scale).astype(BF16)

    acc = _dot_nt(a_ref[...], wb_ref[...])
    for o in o_refs:
        o[...] = acc.astype(o.dtype)


def matmul(a, wt, layer, row0, c, out_dtypes, scale=1.0):
    n, k = a.shape
    assert row0 % SUBLANES == 0
    tm = _tile(n, 768, BF16_ROWS)
    tn = _tile(c, 1024, SUBLANES)
    outs = pl.pallas_call(
        functools.partial(_mm_kernel, scale=scale),
        grid=(c // tn, n // tm),
        in_specs=[pl.BlockSpec((tm, k), lambda j, i: (i, 0)),
                  pl.BlockSpec((pl.Element(1), pl.Element(tn), pl.Element(k)),
                               lambda j, i: (layer, pl.multiple_of(row0 + j * tn, SUBLANES), 0))],
        out_specs=[pl.BlockSpec((tm, tn), lambda j, i: (i, j)) for _ in out_dtypes],
        out_shape=[jax.ShapeDtypeStruct((n, c), dt) for dt in out_dtypes],
        scratch_shapes=[pltpu.VMEM((tn, k), BF16)],
        compiler_params=_params(("parallel", "arbitrary")),
        name="proj",
    )(a, wt)
    return outs


HIST_A = 32
HIST_S = 8


def _round_bf16(x):
    return x.astype(BF16).astype(F32)


def _push_tail(tail_ref, new, n_new, keep):
    if n_new >= keep:
        tail_ref[...] = new[n_new - keep:n_new]
    else:
        tail_ref[...] = jnp.concatenate([tail_ref[n_new:keep], new], axis=0)


def _conva_kernel(u_ref, hist_ref, w_ref, b_ref, lg_ref, lb_ref, c_ref, hout_ref, ext_ref, conv_ref, tail_ref, *, tt, ch):
    t = pl.program_id(1)

    @pl.when(t == 0)
    def _():
        tail_ref[...] = hist_ref[...]
        ext_ref[0:HIST_A, :] = _round_bf16(hist_ref[...])

    uin = u_ref[...]
    u = uin[:, :ch] * jax.nn.sigmoid(uin[:, ch:])
    ext_ref[HIST_A:HIST_A + tt, :] = _round_bf16(u)
    _push_tail(tail_ref, u, tt, HIST_A)
    hout_ref[...] = tail_ref[...]
    off = HIST_A - (CONV_A_W - 1)
    rb = min(tt, 128)

    def lane_block(cb, carry):
        lo = pl.multiple_of(cb * LANES, LANES)
        for r in range(tt // rb):
            acc = jnp.broadcast_to(b_ref[:, pl.ds(lo, LANES)], (rb, LANES))
            for k in range(CONV_A_W):
                acc = acc + w_ref[k:k + 1, pl.ds(lo, LANES)] * ext_ref[r * rb + off + k:r * rb + off + k + rb, pl.ds(lo, LANES)]
            conv_ref[r * rb:(r + 1) * rb, pl.ds(lo, LANES)] = acc
        return carry

    lax.fori_loop(0, ch // LANES, lane_block, 0)
    cv = conv_ref[...]
    mu = jnp.mean(cv, axis=-1, keepdims=True)
    xc = cv - mu
    var = jnp.mean(xc * xc, axis=-1, keepdims=True)
    y = xc * lax.rsqrt(var + EPS) * lg_ref[...] + lb_ref[...]
    c_ref[...] = _silu(y).astype(c_ref.dtype)
    ext_ref[0:HIST_A, :] = ext_ref[tt:tt + HIST_A, :]


def conv_a_branch(u_in, row_off, hist, w, b, lg, lb, batch, t_len, out_dtype, n_out=None):
    ch = w.shape[1]
    tt = _tile(t_len, 256, SUBLANES)
    nt = t_len // tt
    assert row_off % tt == 0
    r0 = row_off // tt
    kern = functools.partial(_conva_kernel, tt=tt, ch=ch)
    vec = lambda: pl.BlockSpec((1, ch), lambda bi, ti: (0, 0))
    return pl.pallas_call(
        kern,
        grid=(batch, nt),
        in_specs=[pl.BlockSpec((tt, 2 * ch), lambda bi, ti: (r0 + bi * nt + ti, 0)),
                  pl.BlockSpec((None, HIST_A, ch), lambda bi, ti: (bi, 0, 0)),
                  pl.BlockSpec((CONV_A_W, ch), lambda bi, ti: (0, 0)), vec(), vec(), vec()],
        out_specs=[pl.BlockSpec((tt, ch), lambda bi, ti: (bi * nt + ti, 0)),
                   pl.BlockSpec((None, HIST_A, ch), lambda bi, ti: (bi, 0, 0))],
        out_shape=[jax.ShapeDtypeStruct((n_out or batch * t_len, ch), out_dtype),
                   jax.ShapeDtypeStruct((batch, HIST_A, ch), F32)],
        scratch_shapes=[pltpu.VMEM((tt + HIST_A, ch), F32), pltpu.VMEM((tt, ch), F32), pltpu.VMEM((HIST_A, ch), F32)],
        compiler_params=_params(("parallel", "arbitrary")),
        name="conv_a",
    )(u_in, hist, _round_bf16(w), b.reshape(1, ch), lg.reshape(1, ch), lb.reshape(1, ch))


def _ssd_kernel(xbc_ref, z_ref, dt_ref, dtt_ref, hist_ref, h0_ref, cw_ref, cb_ref, dtb_ref, dtbt_ref,
                a_ref, at_ref, dvec_ref, gn_ref, y_ref, hout_ref, sout_ref, ext_ref, h_ref, yacc_ref, tail_ref,
                *, q, nc, inner, heads):
    c = pl.program_id(1)

    @pl.when(c == 0)
    def _():
        tail_ref[...] = hist_ref[...]
        ext_ref[0:HIST_S, :] = _round_bf16(hist_ref[...])
        h_ref[...] = h0_ref[...]

    xbc = xbc_ref[...]
    ext_ref[HIST_S:HIST_S + q, :] = _round_bf16(xbc)
    _push_tail(tail_ref, xbc, q, HIST_S)
    hout_ref[...] = tail_ref[...]
    off = HIST_S - (SSM_CONV_W - 1)
    acc = jnp.broadcast_to(cb_ref[...], (q, cb_ref.shape[1]))
    for k in range(SSM_CONV_W):
        acc = acc + cw_ref[k:k + 1, :] * ext_ref[off + k:off + k + q, :]
    xc = _silu(acc)
    ext_ref[0:HIST_S, :] = ext_ref[q:q + HIST_S, :]

    gw = SSM_GROUPS * SSM_STATE
    xs = xc[:, :inner]
    bm = xc[:, inner:inner + gw]
    cm = xc[:, inner + gw:inner + 2 * gw]

    dt = _softplus(dt_ref[...] + dtb_ref[...])
    dtt = _softplus(dtt_ref[...] + dtbt_ref[...])
    da = dt * a_ref[...]
    dat = dtt * at_ref[...]
    ri = lax.broadcasted_iota(jnp.int32, (q, q), 0)
    ci = lax.broadcasted_iota(jnp.int32, (q, q), 1)
    causal = ri >= ci
    tri = causal.astype(F32)
    trit = (ri <= ci).astype(F32)
    acum = jnp.dot(tri, da, precision=HIGHEST, preferred_element_type=F32)
    acumt = jnp.dot(dat, trit, precision=HIGHEST, preferred_element_type=F32)
    alast = acum[q - 1:q, :]
    hpg = heads // SSM_GROUPS
    for g in range(SSM_GROUPS):
        bg = bm[:, g * SSM_STATE:(g + 1) * SSM_STATE].astype(BF16)
        cg = cm[:, g * SSM_STATE:(g + 1) * SSM_STATE].astype(BF16)
        cbm = _dot_nt(cg, bg)
        for e in range(hpg):
            hh = g * hpg + e
            a_col = acum[:, hh:hh + 1]
            a_row = acumt[hh:hh + 1, :]
            decay = jnp.exp(jnp.where(causal, a_col - a_row, NEG))
            wm = cbm * decay * dtt[hh:hh + 1, :]
            xh = xs[:, hh * SSM_HEAD_DIM:(hh + 1) * SSM_HEAD_DIM]
            hprev = h_ref[hh]
            y_h = (jnp.dot(wm.astype(BF16), xh.astype(BF16), preferred_element_type=F32)
                   + _dot_nt(cg, hprev.astype(BF16)) * jnp.exp(a_col))
            yacc_ref[:, hh * SSM_HEAD_DIM:(hh + 1) * SSM_HEAD_DIM] = y_h
            al = alast[:, hh:hh + 1]
            w_end = jnp.exp(al - a_col) * dt[:, hh:hh + 1]
            h_ref[hh] = jnp.exp(al) * hprev + _dot_tn((xh * w_end).astype(BF16), bg)

    y = (yacc_ref[...] + xs * dvec_ref[...]) * _silu(z_ref[...])
    gsz = inner // SSM_GROUPS
    for g in range(SSM_GROUPS):
        v = y[:, g * gsz:(g + 1) * gsz]
        ms = jnp.mean(v * v, axis=-1, keepdims=True)
        y_ref[:, g * gsz:(g + 1) * gsz] = (v * lax.rsqrt(ms + EPS) * gn_ref[:, g * gsz:(g + 1) * gsz]).astype(y_ref.dtype)

    @pl.when(c == nc - 1)
    def _():
        sout_ref[...] = h_ref[...]


def ssd_branch(xbc, z, small, small_t, row_off, hist, h0, cw, cb, dt_bias, a_log, d_skip, norm_g, batch, t_len, out_dtype,
               n_out=None):
    conv_dim = xbc.shape[1]
    inner = z.shape[1]
    heads = dt_bias.shape[0]
    assert heads == DT_COLS
    q = math.gcd(t_len, SSM_CHUNK)
    nc = t_len // q
    assert row_off % q == 0
    r0 = row_off // q
    pad = LANES - heads
    dtb = jnp.pad(dt_bias, (0, pad)).reshape(1, LANES)
    a_neg = jnp.pad(-jnp.exp(a_log), (0, pad)).reshape(1, LANES)
    dvec = jnp.repeat(d_skip, SSM_HEAD_DIM).reshape(1, inner)
    kern = functools.partial(_ssd_kernel, q=q, nc=nc, inner=inner, heads=heads)
    row = lambda w: pl.BlockSpec((1, w), lambda bi, ci: (0, 0))
    col = lambda: pl.BlockSpec((LANES, 1), lambda bi, ci: (0, 0))
    tok = lambda w: pl.BlockSpec((q, w), lambda bi, ci: (r0 + bi * nc + ci, 0))
    return pl.pallas_call(
        kern,
        grid=(batch, nc),
        in_specs=[tok(conv_dim), tok(inner), tok(LANES),
                  pl.BlockSpec((None, LANES, q), lambda bi, ci: (bi, 0, ci)),
                  pl.BlockSpec((None, HIST_S, conv_dim), lambda bi, ci: (bi, 0, 0)),
                  pl.BlockSpec((None, heads, SSM_HEAD_DIM, SSM_STATE), lambda bi, ci: (bi, 0, 0, 0)),
                  pl.BlockSpec((SSM_CONV_W, conv_dim), lambda bi, ci: (0, 0)), row(conv_dim),
                  row(LANES), col(), row(LANES), col(), row(inner), row(inner)],
        out_specs=[pl.BlockSpec((q, inner), lambda bi, ci: (bi * nc + ci, 0)),
                   pl.BlockSpec((None, HIST_S, conv_dim), lambda bi, ci: (bi, 0, 0)),
                   pl.BlockSpec((None, heads, SSM_HEAD_DIM, SSM_STATE), lambda bi, ci: (bi, 0, 0, 0))],
        out_shape=[jax.ShapeDtypeStruct((n_out or batch * t_len, inner), out_dtype),
                   jax.ShapeDtypeStruct((batch, HIST_S, conv_dim), F32),
                   jax.ShapeDtypeStruct((batch, heads, SSM_HEAD_DIM, SSM_STATE), F32)],
        scratch_shapes=[pltpu.VMEM((q + HIST_S, conv_dim), F32),
                        pltpu.VMEM((heads, SSM_HEAD_DIM, SSM_STATE), F32),
                        pltpu.VMEM((q, inner), F32), pltpu.VMEM((HIST_S, conv_dim), F32)],
        compiler_params=_params(("parallel", "arbitrary")),
        name="ssd",
    )(xbc, z, small, small_t, hist, h0, _round_bf16(cw), cb.reshape(1, conv_dim), dtb, dtb.reshape(LANES, 1),
      a_neg, a_neg.reshape(LANES, 1), dvec, norm_g.reshape(1, inner))


def _bucket_np(d):
    n = np.maximum(d, 0)
    exact = REL_BUCKETS // 2
    nf = np.maximum(n, 1).astype(np.float32)
    large = exact + (np.log(nf / np.float32(exact)) / np.float32(math.log(REL_MAX_DIST / exact))
                     * np.float32(REL_BUCKETS - exact)).astype(np.int32)
    return np.where(n < exact, n, np.minimum(large, REL_BUCKETS - 1))


def _near_distances():
    i = np.arange(QCHUNK)[:, None]
    j = np.arange(LANES)[None, :]
    diag = i - j
    prev = QCHUNK + i - j
    band = np.where(j < 4, i + (BLK + 1) - BLK * j, 0)
    return np.stack([diag, prev, band])


def _expand_kernel(tab_ref, oh_ref, o_ref):
    o_ref[...] = jnp.dot(tab_ref[...], oh_ref[...], precision=HIGHEST, preferred_element_type=F32)


def expand_bias(rel_bias, dist):
    flat = dist.reshape(-1)
    m = flat.shape[0]
    assert m % LANES == 0
    onehot = (np.arange(REL_BUCKETS)[:, None] == _bucket_np(flat)[None, :]).astype(np.float32)
    tm = _tile(m, 8192, LANES)
    heads = rel_bias.shape[1]
    out = pl.pallas_call(
        _expand_kernel,
        grid=(m // tm,),
        in_specs=[pl.BlockSpec((heads, REL_BUCKETS), lambda i: (0, 0)), pl.BlockSpec((REL_BUCKETS, tm), lambda i: (0, i))],
        out_specs=pl.BlockSpec((heads, tm), lambda i: (0, i)),
        out_shape=jax.ShapeDtypeStruct((heads, m), F32),
        compiler_params=_params(("parallel",)),
        name="bias_expand",
    )(rel_bias.T, jnp.asarray(onehot))
    return out.reshape((heads,) + dist.shape)


def _select_blocks(imp, lane, axis=-1):
    sel = jnp.zeros(imp.shape, F32)
    big = jnp.int32(1 << 30)
    v = imp
    for _ in range(TOPK):
        mx = jnp.max(v, axis=axis, keepdims=True)
        idx = jnp.min(jnp.where(v == mx, lane, big), axis=axis, keepdims=True)
        hit = lane == idx
        sel = jnp.where(hit, 1.0, sel)
        v = jnp.where(hit, -3e38, v)
    return sel


def _rep4(x):
    return jnp.concatenate([x, x, x, x], axis=0)


def _qgroup(q_ref, g):
    return jnp.concatenate([q_ref[:, (g * GQA + r) * HEAD_DIM:(g * GQA + r + 1) * HEAD_DIM] for r in range(GQA)], axis=0)


def _gate_col(sg, br, g):
    c0 = GATE_COL0 + br * N_HEADS + g * GQA
    return jnp.concatenate([sg[:, c0 + r:c0 + r + 1] for r in range(GQA)], axis=0)


def _rows_means_kernel(x_ref, o_ref):
    nb, w = o_ref.shape
    o_ref[...] = jnp.sum(x_ref[...].reshape(nb, BLK, w), axis=1) * (1.0 / BLK)


def prompt_block_means(kv, batch, t_len):
    w = 2 * KV_GROUPS * HEAD_DIM
    nb = t_len // BLK
    return pl.pallas_call(
        _rows_means_kernel,
        grid=(batch,),
        in_specs=[pl.BlockSpec((t_len, w), lambda b: (b, 0))],
        out_specs=pl.BlockSpec((None, nb, w), lambda b: (b, 0, 0)),
        out_shape=jax.ShapeDtypeStruct((batch, nb, w), F32),
        compiler_params=_params(("parallel",)),
        name="cmp_means_prompt",
    )(kv)


def _page_specs(pps, layer):
    def spec(k):
        return pl.BlockSpec((None, None, 4, KV_GROUPS, HEAD_DIM, 2 * BLK),
                            lambda b, j, pt: (pt[b, j * pps + k], layer, 0, 0, 0, 0))
    return [spec(k) for k in range(pps)]


def _nsa_prompt_kernel(q_ref, ksel_ref, vsel_ref, kwin_ref, vwin_ref, kc_ref, gn_ref, tab_ref, band_ref, cst_ref, e_ref, o_ref,
                       s_ref, sel_ref, oc_ref, mrun_ref, acc_ref, *, nb):
    c = pl.program_id(1)
    qc = QCHUNK
    rows = GQA * qc
    kw = KV_GROUPS * HEAD_DIM
    span = SUPER * qc
    blk = lax.broadcasted_iota(jnp.int32, (nb, rows), 0)
    qpos = c * qc + (lax.broadcasted_iota(jnp.int32, (nb, rows), 1) & (qc - 1))
    okc = (qpos - BLK * blk - (BLK - 1)) >= 0
    blk1 = lax.broadcasted_iota(jnp.int32, (nb, qc), 0)
    qi1 = lax.broadcasted_iota(jnp.int32, (nb, qc), 1)
    okc1 = (c * qc + qi1 - BLK * blk1 - (BLK - 1)) >= 0
    cur = 2 * c + (qi1 >= BLK).astype(jnp.int32)
    for g in range(KV_GROUPS):
        qg = _qgroup(q_ref, g)
        kcg = kc_ref[:, g * HEAD_DIM:(g + 1) * HEAD_DIM].astype(BF16)
        vcg = kc_ref[:, kw + g * HEAD_DIM:kw + (g + 1) * HEAD_DIM].astype(BF16)
        bias_c = jnp.broadcast_to(cst_ref[g], (nb, rows))
        for u in range(4):
            bias_c = jnp.where(blk == 2 * c - 2 + u, band_ref[g, u:u + 1, :], bias_c)
        s = jnp.where(okc, _dot_nt(kcg, qg) + bias_c, NEG)
        e = jnp.exp(s - jnp.max(s, axis=0, keepdims=True))
        p = jnp.where(okc, e / jnp.sum(e, axis=0, keepdims=True), 0.0)
        oc_ref[g] = _dot_tn(p.astype(BF16), vcg)
        psum = p[:, 0:qc] + p[:, qc:2 * qc] + p[:, 2 * qc:3 * qc] + p[:, 3 * qc:4 * qc]
        imp = jnp.where(blk1 == cur, FORCE, jnp.where(okc1, psum, -1.0))
        selm = jnp.where(blk1 <= cur, _select_blocks(imp, blk1, axis=0), 0.0)
        sel_ref[g] = (_dot_tn(selm.astype(BF16), e_ref[...]) - 1.0) * (-NEG)

    sg = jax.nn.sigmoid(gn_ref[...])
    n_super = c // SUPER + 1
    w_tiles = WINDOW // qc + 1
    st0 = jnp.maximum(c - (w_tiles - 1), 0)
    for g in range(KV_GROUPS):
        qg = _qgroup(q_ref, g)
        mrun_ref[...] = jnp.full((rows, LANES), NEG, F32)

        def scores(st, carry):
            k = ksel_ref[pl.ds(pl.multiple_of(st * span, span), span), g * HEAD_DIM:(g + 1) * HEAD_DIM]
            s = _dot_nt(qg, k)
            mrun = mrun_ref[...]
            for j in range(SUPER):
                kt = st * SUPER + j
                idx = jnp.where(kt == c, 0, jnp.where(kt == c - 1, 1, 2))
                off = pl.multiple_of(kt * qc, qc)
                sj = s[:, j * qc:(j + 1) * qc] + tab_ref[g, idx] + _rep4(sel_ref[g, :, pl.ds(off, qc)])
                s_ref[:, pl.ds(off, qc)] = sj
                mrun = jnp.maximum(mrun, sj)
            mrun_ref[...] = mrun
            return carry

        lax.fori_loop(0, n_super, scores, 0)
        m = jnp.max(mrun_ref[...], axis=-1, keepdims=True)
        mrun_ref[...] = jnp.zeros((rows, LANES), F32)

        def exps(st, carry):
            off = pl.multiple_of(st * span, span)
            e = jnp.exp(s_ref[:, pl.ds(off, span)] - m)
            s_ref[:, pl.ds(off, span)] = e
            lrun = mrun_ref[...]
            for j in range(SUPER):
                lrun = lrun + e[:, j * qc:(j + 1) * qc]
            mrun_ref[...] = lrun
            return carry

        lax.fori_loop(0, n_super, exps, 0)
        denom = jnp.sum(mrun_ref[...], axis=-1, keepdims=True)
        acc_ref[...] = jnp.zeros((rows, HEAD_DIM), F32)

        def values(st, carry):
            off = pl.multiple_of(st * span, span)
            p = (s_ref[:, pl.ds(off, span)] / denom).astype(BF16)
            acc_ref[...] += jnp.dot(p, vsel_ref[pl.ds(off, span), g * HEAD_DIM:(g + 1) * HEAD_DIM], preferred_element_type=F32)
            return carry

        lax.fori_loop(0, n_super, values, 0)
        o_s = acc_ref[...]
        r0 = pl.multiple_of(st0 * qc, qc)
        s = _dot_nt(qg, kwin_ref[pl.ds(r0, w_tiles * qc), g * HEAD_DIM:(g + 1) * HEAD_DIM])
        parts = []
        for w in range(w_tiles):
            kt = st0 + w
            idx = jnp.where(kt > c, 4, jnp.where(kt == c, 0, jnp.where(kt == c - 1, 1,
                                                                         jnp.where(kt == c - (w_tiles - 1), 3, 2))))
            parts.append(s[:, w * qc:(w + 1) * qc] + tab_ref[g, idx])
        sw = jnp.concatenate(parts, axis=1)
        ew = jnp.exp(sw - jnp.max(sw, axis=-1, keepdims=True))
        pw = (ew / jnp.sum(ew, axis=-1, keepdims=True)).astype(BF16)
        o_w = jnp.dot(pw, vwin_ref[pl.ds(r0, w_tiles * qc), g * HEAD_DIM:(g + 1) * HEAD_DIM],
                      preferred_element_type=F32)
        tot = _gate_col(sg, 0, g) * oc_ref[g] + _gate_col(sg, 1, g) * o_s + _gate_col(sg, 2, g) * o_w
        for r in range(GQA):
            o_ref[:, (g * GQA + r) * HEAD_DIM:(g * GQA + r + 1) * HEAD_DIM] = tot[r * qc:(r + 1) * qc].astype(o_ref.dtype)


def nsa_prompt(q, kvb, kc, small, tabs, band, cst, batch, t_len, n_out=None):
    assert t_len % (SUPER * QCHUNK) == 0 and t_len >= WINDOW + QCHUNK
    nch = t_len // QCHUNK
    nb = t_len // BLK
    assert nb % SUBLANES == 0
    rows = GQA * QCHUNK
    hd = N_HEADS * HEAD_DIM
    emat = (np.arange(nb)[:, None] == (np.arange(t_len)[None, :] // BLK)).astype(np.float32)
    kern = functools.partial(_nsa_prompt_kernel, nb=nb)
    full = lambda shp: pl.BlockSpec(shp, lambda b, c: (0,) * len(shp))
    kind = lambda k: pl.BlockSpec((t_len, KV_GROUPS * HEAD_DIM), lambda b, c: (b, k))
    return pl.pallas_call(
        kern,
        grid=(batch, nch),
        in_specs=[pl.BlockSpec((QCHUNK, hd), lambda b, c: (b * nch + c, 0)),
                  kind(2), kind(3), kind(4), kind(5),
                  pl.BlockSpec((None, nb, kc.shape[2]), lambda b, c: (b, 0, 0)),
                  pl.BlockSpec((QCHUNK, LANES), lambda b, c: (b * nch + c, 0)),
                  full((KV_GROUPS, 5, rows, LANES)), full((KV_GROUPS, SUBLANES, rows)), full((KV_GROUPS, 1, rows)),
                  full((nb, t_len))],
        out_specs=pl.BlockSpec((QCHUNK, hd), lambda b, c: (b * nch + c, 0)),
        out_shape=jax.ShapeDtypeStruct((n_out or batch * t_len, hd), BF16),
        scratch_shapes=[pltpu.VMEM((rows, t_len), F32),
                        pltpu.VMEM((KV_GROUPS, QCHUNK, t_len), F32),
                        pltpu.VMEM((KV_GROUPS, rows, HEAD_DIM), F32),
                        pltpu.VMEM((rows, LANES), F32), pltpu.VMEM((rows, HEAD_DIM), F32)],
        compiler_params=_params(("parallel", "arbitrary")),
        name="nsa_prompt",
    )(q, kvb, kvb, kvb, kvb, kc, small, tabs, band, cst, jnp.asarray(emat, BF16))


def _nsa_sample_kernel(pt_ref, q_ref, kvn_ref, gn_ref, avg_ref, win_ref, near_ref, wtab_ref, cst_ref, e_ref,
                       *refs, pps, n_steps, past, ts, nbl):
    pages = refs[:pps]
    o_ref = refs[pps]
    s_ref, kc_ref, vbuf_ref, kn_ref = refs[pps + 1:]
    j = pl.program_id(1)
    rows = GQA * ts
    kw = KV_GROUPS * HEAD_DIM
    nb_past = past // BLK
    pg = 2 * BLK
    lane = lax.broadcasted_iota(jnp.int32, (rows, LANES), 1)
    ti = lax.broadcasted_iota(jnp.int32, (rows, LANES), 0) & (ts - 1)

    def qgroup(g):
        return _qgroup(q_ref, g).astype(BF16)

    @pl.when(j == 0)
    def _():
        kn_ref[...] = jnp.zeros(kn_ref.shape, F32)
        kn_ref[0:ts, :] = kvn_ref[:, 2 * kw:6 * kw]
        kc_ref[...] = jnp.zeros(kc_ref.shape, F32)

    is_last = j == n_steps - 1
    base = j * (pps * pg)
    qgs = [qgroup(g) for g in range(KV_GROUPS)]
    avg = avg_ref[...]
    blk0 = pl.multiple_of(j * (2 * pps), 2 * pps)
    for kind in range(2):
        for g in range(KV_GROUPS):
            cat = jnp.concatenate([pgr[kind, g] for pgr in pages], axis=1)
            hi = cat.astype(BF16)
            lo = (cat - hi.astype(F32)).astype(BF16)
            kc_ref[kind, g, pl.ds(blk0, 2 * pps), :] = _dot_nt(avg, hi) + _dot_nt(avg, lo)
    for k in range(pps):
        off = pl.multiple_of(base + k * pg, pg)
        for g in range(KV_GROUPS):
            vbuf_ref[g, :, pl.ds(off, pg)] = pages[k][3, g].astype(BF16)
            sk = jnp.dot(qgs[g], pages[k][2, g].astype(BF16), preferred_element_type=F32)
            bias = jnp.where(is_last, near_ref[g, 1], cst_ref[g]) if k == pps - 1 else cst_ref[g]
            s_ref[g, :, pl.ds(off, pg)] = sk + bias

    @pl.when(is_last)
    def _():
        sg = jax.nn.sigmoid(gn_ref[...])
        wl = win_ref.shape[-1]
        lane_w = lax.broadcasted_iota(jnp.int32, (rows, wl + LANES), 1)
        t_w = lax.broadcasted_iota(jnp.int32, (rows, wl + LANES), 0) & (ts - 1)
        okw = jnp.where(lane_w < wl, lane_w - t_w, t_w - (lane_w - wl)) >= 0
        lane_b = lax.broadcasted_iota(jnp.int32, (rows, nbl), 1)
        t_b = lax.broadcasted_iota(jnp.int32, (rows, nbl), 0) & (ts - 1)
        okc = (past + t_b - BLK * lane_b - (BLK - 1)) >= 0
        lane1 = lax.broadcasted_iota(jnp.int32, (ts, nbl), 1)
        okc1 = (past + lax.broadcasted_iota(jnp.int32, (ts, nbl), 0) - BLK * lane1 - (BLK - 1)) >= 0
        o_cs, madds = [], []
        for g in range(KV_GROUPS):
            qg = qgs[g]
            cst = cst_ref[g]
            band = near_ref[g, 2]
            kcg = kc_ref[0, g].astype(BF16)
            vcg = kc_ref[1, g].astype(BF16)
            bias_c = jnp.where(lane_b == nb_past - 2, band[:, 0:1], jnp.where(lane_b == nb_past - 1, band[:, 1:2], cst))
            s = jnp.where(okc, _dot_nt(qg, kcg) + bias_c, NEG)
            e = jnp.exp(s - jnp.max(s, axis=-1, keepdims=True))
            p = jnp.where(okc, e / jnp.sum(e, axis=-1, keepdims=True), 0.0)
            o_cs.append(jnp.dot(p.astype(BF16), vcg, preferred_element_type=F32))
            psum = p[0:ts] + p[ts:2 * ts] + p[2 * ts:3 * ts] + p[3 * ts:4 * ts]
            imp = jnp.where(lane1 == nb_past, FORCE, jnp.where(okc1, psum, -1.0))
            imp = jnp.where(lane1 <= nb_past, imp, -3e38)
            selm = _select_blocks(imp, lane1)
            selm16 = jnp.concatenate([selm, jnp.zeros_like(selm)], axis=0).astype(BF16)
            madds.append((jnp.dot(selm16, e_ref[...], preferred_element_type=F32)[0:ts] - 1.0) * (-NEG))
        for g in range(KV_GROUPS):
            qg = qgs[g]
            o_c, madd = o_cs[g], madds[g]
            kn = kn_ref[:, g * HEAD_DIM:(g + 1) * HEAD_DIM].astype(BF16)
            vn = kn_ref[:, kw + g * HEAD_DIM:kw + (g + 1) * HEAD_DIM].astype(BF16)
            s_new = jnp.where(lane <= ti, _dot_nt(qg, kn) + near_ref[g, 0], NEG)
            s = jnp.concatenate([s_ref[g] + _rep4(madd), s_new], axis=1)
            e = jnp.exp(s - jnp.max(s, axis=-1, keepdims=True))
            p = e / jnp.sum(e, axis=-1, keepdims=True)
            o_s = (_dot_nt(p[:, :past].astype(BF16), vbuf_ref[g])
                   + jnp.dot(p[:, past:].astype(BF16), vn, preferred_element_type=F32))
            kwin = win_ref[0, g].astype(BF16)
            vwin = win_ref[1, g].astype(BF16)
            knw = kn_ref[:, 2 * kw + g * HEAD_DIM:2 * kw + (g + 1) * HEAD_DIM].astype(BF16)
            vnw = kn_ref[:, 3 * kw + g * HEAD_DIM:3 * kw + (g + 1) * HEAD_DIM].astype(BF16)
            s = jnp.concatenate([jnp.dot(qg, kwin, preferred_element_type=F32), _dot_nt(qg, knw)], axis=1) + wtab_ref[g]
            s = jnp.where(okw, s, NEG)
            e = jnp.exp(s - jnp.max(s, axis=-1, keepdims=True))
            p = e / jnp.sum(e, axis=-1, keepdims=True)
            o_w = (_dot_nt(p[:, :wl].astype(BF16), vwin)
                   + jnp.dot(p[:, wl:].astype(BF16), vnw, preferred_element_type=F32))
            tot = _gate_col(sg, 0, g) * o_c + _gate_col(sg, 1, g) * o_s + _gate_col(sg, 2, g) * o_w
            for r in range(GQA):
                o_ref[:, (g * GQA + r) * HEAD_DIM:(g * GQA + r + 1) * HEAD_DIM] = tot[r * ts:(r + 1) * ts]


def nsa_sample(q3, kvn3, small3, win_t, near_s, wtab, cst_s, cache_t, page_table, layer, pps, nbl):
    batch, ts, hd = q3.shape
    n_pages = page_table.shape[1]
    n_steps = n_pages // pps
    past = n_pages * 2 * BLK
    wl = win_t.shape[-1]
    avg = np.where(np.arange(2 * pps)[:, None] == np.arange(pps * 2 * BLK)[None, :] // BLK, 1.0 / BLK, 0.0)
    rows = GQA * ts
    kw = KV_GROUPS * HEAD_DIM
    assert ts == SUBLANES and past % QCHUNK == 0 and wl % LANES == 0
    emat = (np.arange(nbl)[:, None] == (np.arange(past)[None, :] // BLK)).astype(np.float32)
    kern = functools.partial(_nsa_sample_kernel, pps=pps, n_steps=n_steps, past=past, ts=ts, nbl=nbl)
    c3 = lambda shp: pl.BlockSpec(shp, lambda b, j, pt: (0,) * len(shp))
    return pl.pallas_call(
        kern,
        grid_spec=pltpu.PrefetchScalarGridSpec(
            num_scalar_prefetch=1, grid=(batch, n_steps),
            in_specs=[pl.BlockSpec((None, ts, hd), lambda b, j, pt: (b, 0, 0)),
                      pl.BlockSpec((None, ts, kvn3.shape[2]), lambda b, j, pt: (b, 0, 0)),
                      pl.BlockSpec((None, ts, LANES), lambda b, j, pt: (b, 0, 0)),
                      c3((2 * pps, pps * 2 * BLK)),
                      pl.BlockSpec((None, None, 2, KV_GROUPS, HEAD_DIM, wl), lambda b, j, pt: (layer, b, 0, 0, 0, 0)),
                      c3((KV_GROUPS, 3, rows, LANES)), c3((KV_GROUPS, rows, wl + LANES)), c3((KV_GROUPS, rows, 1)),
                      c3((nbl, past))] + _page_specs(pps, layer),
            out_specs=pl.BlockSpec((None, ts, hd), lambda b, j, pt: (b, 0, 0)),
            scratch_shapes=[pltpu.VMEM((KV_GROUPS, rows, past), F32),
                            pltpu.VMEM((2, KV_GROUPS, nbl, HEAD_DIM), F32),
                            pltpu.VMEM((KV_GROUPS, HEAD_DIM, past), BF16),
                            pltpu.VMEM((LANES, 4 * kw), F32)]),
        out_shape=jax.ShapeDtypeStruct((batch, ts, hd), F32),
        compiler_params=_params(("parallel", "arbitrary")),
        name="nsa_sample",
    )(page_table, q3, kvn3, small3, jnp.asarray(avg, BF16), win_t, near_s, wtab, cst_s, jnp.asarray(emat, BF16),
      *([cache_t] * pps))


def _merge_kernel(c_ref, o_ref, y_ref, g0_ref, g1_ref, g2_ref, wa_ref, wn_ref, ws_ref, out_ref):
    oa = jnp.dot(c_ref[...], wa_ref[...], preferred_element_type=F32)
    ob = jnp.dot(o_ref[...], wn_ref[...], preferred_element_type=F32)
    oc = jnp.dot(y_ref[...], ws_ref[...], preferred_element_type=F32)
    mixed = (jax.nn.sigmoid(g0_ref[...]) * oa + jax.nn.sigmoid(g1_ref[...]) * ob + jax.nn.sigmoid(g2_ref[...]) * oc)
    out_ref[...] = mixed.astype(out_ref.dtype)


def merge_branches(c, o, y, gmix, wa, wn, ws):
    n = c.shape[0]
    d = wa.shape[1]
    tm = _tile(n, 768, BF16_ROWS)
    tn = _tile(d, 512, LANES)
    nj = d // tn
    act = lambda a: pl.BlockSpec((tm, a.shape[1]), lambda i, j: (i, 0))
    wsp = lambda w: pl.BlockSpec((w.shape[0], tn), lambda i, j: (0, j))
    gsp = lambda br: pl.BlockSpec((tm, tn), lambda i, j: (i, br * nj + j))
    return pl.pallas_call(
        _merge_kernel,
        grid=(n // tm, nj),
        in_specs=[act(c), act(o), act(y), gsp(0), gsp(1), gsp(2), wsp(wa), wsp(wn), wsp(ws)],
        out_specs=pl.BlockSpec((tm, tn), lambda i, j: (i, j)),
        out_shape=jax.ShapeDtypeStruct((n, d), BF16),
        compiler_params=_params(("parallel", "parallel")),
        name="merge",
    )(c, o, y, gmix, gmix, gmix, wa, wn, ws)


def _outproj_kernel(a_ref, w_ref, x_ref, o_ref):
    o_ref[...] = x_ref[...] + jnp.dot(a_ref[...], w_ref[...], preferred_element_type=F32)


def out_projection(a, w, x):
    n, k = a.shape
    d = w.shape[1]
    tm = _tile(n, 768, BF16_ROWS)
    tn = _tile(d, 1024, LANES)
    return pl.pallas_call(
        _outproj_kernel,
        grid=(n // tm, d // tn),
        in_specs=[pl.BlockSpec((tm, k), lambda i, j: (i, 0)), pl.BlockSpec((k, tn), lambda i, j: (0, j)),
                  pl.BlockSpec((tm, tn), lambda i, j: (i, j))],
        out_specs=pl.BlockSpec((tm, tn), lambda i, j: (i, j)),
        out_shape=jax.ShapeDtypeStruct((n, d), F32),
        compiler_params=_params(("parallel", "parallel")),
        name="out_proj",
    )(a, w, x)


def _router_kernel(x_ref, g_ref, w_ref, b_ref, h_ref, r_ref):
    x = x_ref[...]
    ms = jnp.mean(x * x, axis=-1, keepdims=True)
    h = (x * lax.rsqrt(ms + EPS) * g_ref[...]).astype(BF16)
    h_ref[...] = h
    logits = jnp.dot(h, w_ref[...], preferred_element_type=F32) + b_ref[...]
    lane = lax.broadcasted_iota(jnp.int32, logits.shape, 1)
    big = jnp.int32(1 << 30)
    gl = jnp.where(lane < MOE_GROUPS, logits, NEG)
    gmax = jnp.max(gl, axis=-1, keepdims=True)
    gidx = jnp.min(jnp.where(gl == gmax, lane, big), axis=-1, keepdims=True)
    g_w = 1.0 / jnp.sum(jnp.exp(gl - gmax), axis=-1, keepdims=True)
    eid = lane - MOE_GROUPS
    in_grp = (eid >= 0) & (eid < MOE_GROUPS * MOE_EPG) & ((eid >> 3) == gidx)
    el = jnp.where(in_grp, logits, NEG)
    m1 = jnp.max(el, axis=-1, keepdims=True)
    i1 = jnp.min(jnp.where(el == m1, lane, big), axis=-1, keepdims=True)
    el2 = jnp.where(lane == i1, NEG, el)
    m2 = jnp.max(el2, axis=-1, keepdims=True)
    i2 = jnp.min(jnp.where(el2 == m2, lane, big), axis=-1, keepdims=True)
    p2 = jnp.exp(m2 - m1)
    w1 = g_w / (1.0 + p2)
    w2 = g_w * p2 / (1.0 + p2)
    r_ref[...] = jnp.where(lane == 0, (i1 - MOE_GROUPS).astype(F32),
                           jnp.where(lane == 1, (i2 - MOE_GROUPS).astype(F32),
                                     jnp.where(lane == 2, w1, jnp.where(lane == 3, w2, 0.0))))


def moe_router(x, g, wr, br):
    n, d = x.shape
    tm = _tile(n, 256, BF16_ROWS)
    return pl.pallas_call(
        _router_kernel,
        grid=(n // tm,),
        in_specs=[pl.BlockSpec((tm, d), lambda i: (i, 0)), pl.BlockSpec((1, d), lambda i: (0, 0)),
                  pl.BlockSpec((d, LANES), lambda i: (0, 0)), pl.BlockSpec((1, LANES), lambda i: (0, 0))],
        out_specs=[pl.BlockSpec((tm, d), lambda i: (i, 0)), pl.BlockSpec((tm, LANES), lambda i: (i, 0))],
        out_shape=[jax.ShapeDtypeStruct((n, d), BF16), jax.ShapeDtypeStruct((n, LANES), F32)],
        compiler_params=_params(("parallel",)),
        name="moe_router",
    )(x, g.reshape(1, d), wr, br)


def _expert_kernel(te_ref, nu_ref, x_ref, rw_ref, wg_ref, wu_ref, wd_ref, y_ref, wgb_ref, wub_ref, wdb_ref):
    t = pl.program_id(0)
    fresh = jnp.logical_or(t == 0, te_ref[t] != te_ref[jnp.maximum(t - 1, 0)])

    @pl.when(jnp.logical_and(fresh, t < nu_ref[0]))
    def _():
        wgb_ref[...] = wg_ref[...].astype(BF16)
        wub_ref[...] = wu_ref[...].astype(BF16)
        wdb_ref[...] = wd_ref[...].astype(BF16)

    @pl.when(t < nu_ref[0])
    def _():
        x = x_ref[...]
        a = jnp.dot(x, wgb_ref[...], preferred_element_type=F32)
        u = jnp.dot(x, wub_ref[...], preferred_element_type=F32)
        rw = rw_ref[...]
        hid = (_silu(a) * u * jnp.concatenate([rw] * (a.shape[1] // LANES), axis=1)).astype(BF16)
        y_ref[...] = jnp.dot(hid, wdb_ref[...], preferred_element_type=F32)


def moe_experts(xs, row_w, tile_e, n_used, wg, wu, wd, layer, tme):
    r, d = xs.shape
    ff = wg.shape[3]
    nt = r // tme
    return pl.pallas_call(
        _expert_kernel,
        grid_spec=pltpu.PrefetchScalarGridSpec(
            num_scalar_prefetch=2, grid=(nt,),
            in_specs=[pl.BlockSpec((tme, d), lambda t, te, nu: (t, 0)),
                      pl.BlockSpec((tme, LANES), lambda t, te, nu: (t, 0)),
                      pl.BlockSpec((None, None, d, ff), lambda t, te, nu: (layer, te[t], 0, 0)),
                      pl.BlockSpec((None, None, d, ff), lambda t, te, nu: (layer, te[t], 0, 0)),
                      pl.BlockSpec((None, None, ff, d), lambda t, te, nu: (layer, te[t], 0, 0))],
            out_specs=pl.BlockSpec((tme, d), lambda t, te, nu: (t, 0)),
            scratch_shapes=[pltpu.VMEM((d, ff), BF16), pltpu.VMEM((d, ff), BF16), pltpu.VMEM((ff, d), BF16)]),
        out_shape=jax.ShapeDtypeStruct((r, d), F32),
        compiler_params=_params(("arbitrary",)),
        name="moe_experts",
    )(tile_e, n_used, xs, row_w, wg, wu, wd)


def moe_layer(x, g, wrg, brg, wre, bre, wg, wu, wd, layer):
    n, d = x.shape
    n_exp = wg.shape[1]
    wr = jnp.concatenate([wrg, wre.transpose(1, 0, 2).reshape(d, n_exp)], axis=1)
    wr = jnp.pad(wr, ((0, 0), (0, LANES - wr.shape[1])))
    br = jnp.pad(jnp.concatenate([brg, bre.reshape(-1)]), (0, LANES - MOE_GROUPS - n_exp)).reshape(1, LANES)
    h, route = moe_router(x, g, wr.astype(BF16), br)
    tme = 256
    e_flat = route[:, 0:MOE_TOPK].astype(jnp.int32).T.reshape(-1)
    w_flat = route[:, MOE_TOPK:2 * MOE_TOPK].T.reshape(-1)
    na = e_flat.shape[0]
    nt = -(-na // tme) + n_exp
    order = jnp.argsort(e_flat, stable=True)
    counts = jnp.sum(e_flat[:, None] == jnp.arange(n_exp)[None, :], axis=0).astype(jnp.int32)
    pc = ((counts + tme - 1) // tme) * tme
    pend = jnp.cumsum(pc)
    pstart = pend - pc
    ustart = jnp.cumsum(counts) - counts
    tile_e = jnp.minimum(jnp.searchsorted(pend, jnp.arange(nt, dtype=jnp.int32) * tme, side="right"), n_exp - 1).astype(jnp.int32)
    n_used = (pend[-1] // tme).astype(jnp.int32).reshape(1)
    row = jnp.arange(nt * tme, dtype=jnp.int32)
    row_e = jnp.repeat(tile_e, tme)
    idx = row - pstart[row_e]
    valid = idx < counts[row_e]
    src = order[jnp.minimum(ustart[row_e] + idx, na - 1)]
    row_tok = jnp.where(valid, src % n, 0).astype(jnp.int32)
    row_w = jnp.where(valid, w_flat[src], 0.0)
    se = e_flat[order]
    dest = pstart[se] + jnp.arange(na, dtype=jnp.int32) - ustart[se]
    dest_flat = dest[jnp.argsort(order)]
    xs = jnp.take(h, row_tok, axis=0)
    ys = moe_experts(xs, jnp.broadcast_to(row_w[:, None], (nt * tme, LANES)), tile_e, n_used, wg, wu, wd, layer, tme)
    return x + jnp.take(ys, dest_flat[:n], axis=0) + jnp.take(ys, dest_flat[n:], axis=0)


def kernel(x_prompt, x_sample, cache_nsa_kv, cache_nsa_win, state_conv_a, state_conv_ssm, state_ssm, page_table,
           rel_bias, norm_mix_g, w_in, conv_a_w, conv_a_b, ln_a_g, ln_a_b, w_a_out, w_nsa_out, conv_s_w, conv_s_b,
           dt_bias, a_log, d_skip, ssm_norm_g, w_ssm_out, w_out, norm_ffn_g, w_router_group, b_router_group,
           w_router_expert, b_router_expert, w_gate, w_up, w_down, final_norm_g):
    bp, tp, d = x_prompt.shape
    bs, ts, _ = x_sample.shape
    depth = w_in.shape[0]
    n_p, n_s = bp * tp, bs * ts
    ch = conv_a_w.shape[2]
    conv_dim = conv_s_w.shape[2]
    heads_ssm = dt_bias.shape[1]
    inner = heads_ssm * SSM_HEAD_DIM
    hd = N_HEADS * HEAD_DIM
    kw = KV_GROUPS * HEAD_DIM
    kvw = KV_KINDS * kw
    n_pages = page_table.shape[1]
    page = cache_nsa_kv.shape[2]
    past = n_pages * page
    wl = cache_nsa_win.shape[2]
    assert page == 2 * BLK and wl == min(WINDOW, past) and tp >= WINDOW
    pps = _tile(n_pages, 16, 1)
    rows_p = GQA * QCHUNK

    splits = (2 * ch, hd, kvw, 3 * N_HEADS, inner, conv_dim, heads_ssm, 3 * d)
    offs = np.concatenate([[0], np.cumsum(splits)])
    o_u, o_q, o_kv, o_gn, o_z, o_x, o_dt, o_gm = [int(v) for v in offs[:-1]]

    near_d = _near_distances()
    tj = np.arange(wl + LANES)[None, :]
    win_d = np.where(tj < wl + ts, wl + np.arange(ts)[:, None] - tj, 0)
    near_f = expand_bias(rel_bias, near_d)
    wtab_f = expand_bias(rel_bias, win_d)
    near_g = near_f.reshape(KV_GROUPS, GQA, 3, QCHUNK, LANES).transpose(0, 2, 1, 3, 4)
    near_p = near_g.reshape(KV_GROUPS, 3, rows_p, LANES)
    near_s = near_g[:, :, :, :ts].reshape(KV_GROUPS, 3, GQA * ts, LANES)
    wtab = wtab_f.reshape(KV_GROUPS, GQA * ts, wl + LANES)
    far = rel_bias[REL_BUCKETS - 1].reshape(KV_GROUPS, GQA, 1)
    cst_p = jnp.repeat(far, QCHUNK, axis=1).reshape(KV_GROUPS, rows_p, 1)
    cst_s = jnp.repeat(far, ts, axis=1).reshape(KV_GROUPS, GQA * ts, 1)
    ii = (np.arange(rows_p) % QCHUNK)[:, None]
    jj = np.arange(LANES)[None, :]
    cst_t = jnp.broadcast_to(cst_p, (KV_GROUPS, rows_p, LANES))
    band_t = jnp.pad(near_p[:, 2, :, 0:4].transpose(0, 2, 1), ((0, 0), (0, SUBLANES - 4), (0, 0)))
    tabs_p = jnp.stack([jnp.where(jj <= ii, near_p[:, 0], NEG), near_p[:, 1], cst_t,
                        jnp.where(jj >= ii, cst_t, NEG), jnp.full((KV_GROUPS, rows_p, LANES), NEG, F32)], axis=1)

    w_in_t = w_in.transpose(0, 2, 1)
    cache_t = cache_nsa_kv.transpose(0, 1, 3, 4, 5, 2)
    win_t = cache_nsa_win.transpose(0, 1, 3, 4, 5, 2)
    zeros_hist_a = jnp.zeros((bp, HIST_A, ch), F32)
    zeros_hist_s = jnp.zeros((bp, HIST_S, conv_dim), F32)
    zeros_ssm = jnp.zeros((bp, heads_ssm, SSM_HEAD_DIM, SSM_STATE), F32)
    nbl = -(-(past // BLK + 1) // LANES) * LANES

    x = jnp.concatenate([x_prompt.reshape(n_p, d), x_sample.reshape(n_s, d)], axis=0)
    kv_p, kv_s, win_p, win_s, ha_p, ha_s, hs_p, hs_s, st_p, st_s = ([] for _ in range(10))
    for l in range(depth):
        h = rmsnorm(x, norm_mix_g[l], BF16)
        (u_in,) = matmul(h, w_in_t, l, o_u, 2 * ch, [F32])
        (q,) = matmul(h, w_in_t, l, o_q, hd, [BF16], scale=HEAD_DIM ** -0.5)
        kv, kvb = matmul(h, w_in_t, l, o_kv, kvw, [F32, BF16])
        w_small = jnp.concatenate([w_in_t[l, o_dt:o_gm], w_in_t[l, o_gn:o_z],
                                   jnp.zeros((LANES - heads_ssm - 3 * N_HEADS, d), F32)], axis=0)
        (small,) = matmul(h, w_small[None], 0, 0, LANES, [F32])
        (z,) = matmul(h, w_in_t, l, o_z, inner, [F32])
        (xbc,) = matmul(h, w_in_t, l, o_x, conv_dim, [F32])
        (gmix,) = matmul(h, w_in_t, l, o_gm, 3 * d, [F32])

        hist_a = jnp.pad(state_conv_a[l], ((0, 0), (HIST_A - (CONV_A_W - 1), 0), (0, 0)))
        ca_args = (conv_a_w[l], conv_a_b[l], ln_a_g[l], ln_a_b[l])
        c_p, ha_new_p = conv_a_branch(u_in, 0, zeros_hist_a, *ca_args, bp, tp, BF16, n_out=n_p + n_s)
        c_s, ha_new_s = conv_a_branch(u_in, n_p, hist_a, *ca_args, bs, ts, F32)
        hist_s = jnp.pad(state_conv_ssm[l], ((0, 0), (HIST_S - (SSM_CONV_W - 1), 0), (0, 0)))
        ss_args = (conv_s_w[l], conv_s_b[l], dt_bias[l], a_log[l], d_skip[l], ssm_norm_g[l])
        small_tp = small[:n_p].reshape(bp, tp, LANES).transpose(0, 2, 1)
        small_ts = small[n_p:].reshape(bs, ts, LANES).transpose(0, 2, 1)
        y_p, hs_new_p, st_new_p = ssd_branch(xbc, z, small, small_tp, 0, zeros_hist_s, zeros_ssm, *ss_args, bp, tp, BF16,
                                             n_out=n_p + n_s)
        y_s, hs_new_s, st_new_s = ssd_branch(xbc, z, small, small_ts, n_p, hist_s, state_ssm[l], *ss_args, bs, ts, F32)
        kc_p = prompt_block_means(kv, bp, tp)
        o_p = nsa_prompt(q, kvb, kc_p, small, tabs_p, band_t, cst_p.reshape(KV_GROUPS, 1, rows_p), bp, tp, n_out=n_p + n_s)
        o_s = nsa_sample(q[n_p:].astype(F32).reshape(bs, ts, hd), kv[n_p:].reshape(bs, ts, kvw),
                         small[n_p:].reshape(bs, ts, LANES), win_t, near_s, wtab, cst_s, cache_t, page_table, l, pps, nbl)

        c_all = lax.dynamic_update_slice(c_p, c_s.astype(BF16), (n_p, 0))
        o_all = lax.dynamic_update_slice(o_p, o_s.reshape(n_s, hd).astype(BF16), (n_p, 0))
        y_all = lax.dynamic_update_slice(y_p, y_s.astype(BF16), (n_p, 0))
        mixed = merge_branches(c_all, o_all, y_all, gmix, w_a_out[l].astype(BF16), w_nsa_out[l].astype(BF16),
                               w_ssm_out[l].astype(BF16))
        x = out_projection(mixed, w_out[l].astype(BF16), x)
        x = moe_layer(x, norm_ffn_g[l], w_router_group[l], b_router_group[l], w_router_expert[l], b_router_expert[l],
                      w_gate, w_up, w_down, l)

        kvr = kv[:, :4 * kw]
        kv_p.append(kvr[:n_p].reshape(bp, tp, 4, KV_GROUPS, HEAD_DIM))
        kv_s.append(kvr[n_p:].reshape(bs, ts, 4, KV_GROUPS, HEAD_DIM))
        wr = kv[:, 4 * kw:]
        win_p.append(wr[:n_p].reshape(bp, tp, 2, KV_GROUPS, HEAD_DIM)[:, tp - WINDOW:])
        win_all = jnp.concatenate([cache_nsa_win[l], wr[n_p:].reshape(bs, ts, 2, KV_GROUPS, HEAD_DIM)], axis=1)
        keep = min(WINDOW, wl + ts)
        win_s.append(win_all[:, wl + ts - keep:])
        ha_p.append(ha_new_p[:, HIST_A - (CONV_A_W - 1):])
        ha_s.append(ha_new_s[:, HIST_A - (CONV_A_W - 1):])
        hs_p.append(hs_new_p[:, HIST_S - (SSM_CONV_W - 1):])
        hs_s.append(hs_new_s[:, HIST_S - (SSM_CONV_W - 1):])
        st_p.append(st_new_p)
        st_s.append(st_new_s)

    y = rmsnorm(x, final_norm_g, F32)
    return (y[:n_p].reshape(bp, tp, d), y[n_p:].reshape(bs, ts, d),
            jnp.stack(kv_p, axis=1), jnp.stack(kv_s, axis=1), jnp.stack(win_p), jnp.stack(win_s),
            jnp.stack(ha_p), jnp.stack(ha_s), jnp.stack(hs_p), jnp.stack(hs_s), jnp.stack(st_p), jnp.stack(st_s))
```

```python
import functools
import math

import numpy as np
import jax
import jax.numpy as jnp
from jax import lax
from jax.experimental import pallas as pl
from jax.experimental.pallas import tpu as pltpu
from jax.experimental.pallas import tpu_sc as plsc

HEAD_DIM = 64
KV_GROUPS = 4
GQA = 4
N_HEADS = KV_GROUPS * GQA
KV_KINDS = 6
BLK = 64
TOPK = 8
WINDOW = 512
QCHUNK = 128
REL_BUCKETS = 32
REL_MAX_DIST = 128
CONV_A_W = 31
SSM_CONV_W = 4
SSM_HEAD_DIM = 64
SSM_GROUPS = 4
SSM_STATE = 128
SSM_CHUNK = 128
MOE_GROUPS = 4
MOE_EPG = 8
MOE_TOPK = 2
EPS = 1e-6
NEG = -1e30
FORCE = 1e4

LANES = 128
SUBLANES = 8
BF16_ROWS = 16
VMEM_LIMIT = 56 * 1024 * 1024
SC_CORES = 2
SC_SUBCORES = 16
SC_TILE_BYTES = 240 * 1024

DT_COLS = 32
GATE_COL0 = DT_COLS
SUPER = 4

HIGHEST = lax.Precision.HIGHEST
F32 = jnp.float32
BF16 = jnp.bfloat16


def _params(sem, limit=VMEM_LIMIT):
    return pltpu.CompilerParams(dimension_semantics=sem, vmem_limit_bytes=limit)


def _tile(n, target, mult):
    best = None
    for t in range(mult, min(n, target) + 1, mult):
        if n % t == 0:
            best = t
    return best if best is not None else n


def _prefilled(kern, in_specs, args, n_out, width, dtype):
    if n_out is None:
        return kern, in_specs, args, {}
    n_in = len(in_specs)

    def body(*refs):
        return kern(*refs[:n_in], *refs[n_in + 1:])

    return (body, in_specs + [pl.BlockSpec(memory_space=pl.ANY)], args + (jnp.zeros((n_out, width), dtype),), {n_in: 0})


def _dot_nt(a, b):
    return lax.dot_general(a, b, (((1,), (1,)), ((), ())), preferred_element_type=F32)


def _dot_tn(a, b):
    return lax.dot_general(a, b, (((0,), (0,)), ((), ())), preferred_element_type=F32)


def _silu(x):
    return x * jax.nn.sigmoid(x)


def _softplus(x):
    return jnp.maximum(x, 0.0) + jnp.log(1.0 + jnp.exp(-jnp.abs(x)))


def _rms_kernel(x_ref, g_ref, o_ref):
    x = x_ref[...]
    ms = jnp.mean(x * x, axis=-1, keepdims=True)
    o_ref[...] = (x * lax.rsqrt(ms + EPS) * g_ref[...]).astype(o_ref.dtype)


def rmsnorm(x, g, out_dtype):
    n, d = x.shape
    tm = _tile(n, 1024, BF16_ROWS)
    return pl.pallas_call(
        _rms_kernel,
        grid=(n // tm,),
        in_specs=[pl.BlockSpec((tm, d), lambda i: (i, 0)), pl.BlockSpec((1, d), lambda i: (0, 0))],
        out_specs=pl.BlockSpec((tm, d), lambda i: (i, 0)),
        out_shape=jax.ShapeDtypeStruct((n, d), out_dtype),
        compiler_params=_params(("parallel",)),
        name="rmsnorm",
    )(x, g.reshape(1, d))


def _mm_kernel(a_ref, w_ref, *refs, scale):
    o_refs, wb_ref = refs[:-1], refs[-1]

    @pl.when(pl.program_id(1) == 0)
    def _():
        w = w_ref[0]
        wb_ref[...] = (w if scale == 1.0 else w * scale).astype(BF16)

    acc = _dot_nt(a_ref[...], wb_ref[...])
    for o in o_refs:
        o[...] = acc.astype(o.dtype)


def matmul(a, wt, layer, row0, c, out_dtypes, scale=1.0):
    n, k = a.shape
    assert row0 % SUBLANES == 0
    tm = _tile(n, 768, BF16_ROWS)
    tn = _tile(c, 1024, SUBLANES)
    outs = pl.pallas_call(
        functools.partial(_mm_kernel, scale=scale),
        grid=(c // tn, n // tm),
        in_specs=[pl.BlockSpec((tm, k), lambda j, i: (i, 0)),
                  pl.BlockSpec((pl.Element(1), pl.Element(tn), pl.Element(k)),
                               lambda j, i: (layer, pl.multiple_of(row0 + j * tn, SUBLANES), 0))],
        out_specs=[pl.BlockSpec((tm, tn), lambda j, i: (i, j)) for _ in out_dtypes],
        out_shape=[jax.ShapeDtypeStruct((n, c), dt) for dt in out_dtypes],
        scratch_shapes=[pltpu.VMEM((tn, k), BF16)],
        compiler_params=_params(("parallel", "arbitrary")),
        name="proj",
    )(a, wt)
    return outs


HIST_A = 32
HIST_S = 8


def _round_bf16(x):
    return x.astype(BF16).astype(F32)


def _push_tail(tail_ref, new, n_new, keep):
    if n_new >= keep:
        tail_ref[...] = new[n_new - keep:n_new]
    else:
        tail_ref[...] = jnp.concatenate([tail_ref[n_new:keep], new], axis=0)


def _conva_kernel(u_ref, hist_ref, w_ref, b_ref, lg_ref, lb_ref, c_ref, hout_ref, ext_ref, conv_ref, tail_ref, *, tt, ch):
    t = pl.program_id(1)

    @pl.when(t == 0)
    def _():
        tail_ref[...] = hist_ref[...]
        ext_ref[0:HIST_A, :] = _round_bf16(hist_ref[...])

    uin = u_ref[...]
    u = uin[:, :ch] * jax.nn.sigmoid(uin[:, ch:])
    ext_ref[HIST_A:HIST_A + tt, :] = _round_bf16(u)
    _push_tail(tail_ref, u, tt, HIST_A)
    hout_ref[...] = tail_ref[...]
    off = HIST_A - (CONV_A_W - 1)
    rb = min(tt, 128)

    def lane_block(cb, carry):
        lo = pl.multiple_of(cb * LANES, LANES)
        for r in range(tt // rb):
            acc = jnp.broadcast_to(b_ref[:, pl.ds(lo, LANES)], (rb, LANES))
            for k in range(CONV_A_W):
                acc = acc + w_ref[k:k + 1, pl.ds(lo, LANES)] * ext_ref[r * rb + off + k:r * rb + off + k + rb, pl.ds(lo, LANES)]
            conv_ref[r * rb:(r + 1) * rb, pl.ds(lo, LANES)] = acc
        return carry

    lax.fori_loop(0, ch // LANES, lane_block, 0)
    cv = conv_ref[...]
    mu = jnp.mean(cv, axis=-1, keepdims=True)
    xc = cv - mu
    var = jnp.mean(xc * xc, axis=-1, keepdims=True)
    y = xc * lax.rsqrt(var + EPS) * lg_ref[...] + lb_ref[...]
    c_ref[...] = _silu(y).astype(c_ref.dtype)
    ext_ref[0:HIST_A, :] = ext_ref[tt:tt + HIST_A, :]


def conv_a_branch(u_in, row_off, hist, w, b, lg, lb, batch, t_len, out_dtype, n_out=None):
    ch = w.shape[1]
    tt = _tile(t_len, 256, SUBLANES)
    nt = t_len // tt
    assert row_off % tt == 0
    r0 = row_off // tt
    kern = functools.partial(_conva_kernel, tt=tt, ch=ch)
    vec = lambda: pl.BlockSpec((1, ch), lambda bi, ti: (0, 0))
    in_specs = [pl.BlockSpec((tt, 2 * ch), lambda bi, ti: (r0 + bi * nt + ti, 0)),
                pl.BlockSpec((None, HIST_A, ch), lambda bi, ti: (bi, 0, 0)),
                pl.BlockSpec((CONV_A_W, ch), lambda bi, ti: (0, 0)), vec(), vec(), vec()]
    args = (u_in, hist, _round_bf16(w), b.reshape(1, ch), lg.reshape(1, ch), lb.reshape(1, ch))
    kern, in_specs, args, alias = _prefilled(kern, in_specs, args, n_out, ch, out_dtype)
    return pl.pallas_call(
        kern,
        grid=(batch, nt),
        in_specs=in_specs,
        out_specs=[pl.BlockSpec((tt, ch), lambda bi, ti: (bi * nt + ti, 0)),
                   pl.BlockSpec((None, HIST_A, ch), lambda bi, ti: (bi, 0, 0))],
        out_shape=[jax.ShapeDtypeStruct((n_out or batch * t_len, ch), out_dtype),
                   jax.ShapeDtypeStruct((batch, HIST_A, ch), F32)],
        scratch_shapes=[pltpu.VMEM((tt + HIST_A, ch), F32), pltpu.VMEM((tt, ch), F32), pltpu.VMEM((HIST_A, ch), F32)],
        input_output_aliases=alias,
        compiler_params=_params(("parallel", "arbitrary")),
        name="conv_a",
    )(*args)


def _ssd_kernel(xbc_ref, z_ref, dt_ref, dtt_ref, hist_ref, h0_ref, cw_ref, cb_ref, dtb_ref, dtbt_ref,
                a_ref, at_ref, dvec_ref, gn_ref, y_ref, hout_ref, sout_ref, ext_ref, h_ref, yacc_ref, tail_ref,
                *, q, nc, inner, heads):
    c = pl.program_id(1)

    @pl.when(c == 0)
    def _():
        tail_ref[...] = hist_ref[...]
        ext_ref[0:HIST_S, :] = _round_bf16(hist_ref[...])
        h_ref[...] = h0_ref[...]

    xbc = xbc_ref[...]
    ext_ref[HIST_S:HIST_S + q, :] = _round_bf16(xbc)
    _push_tail(tail_ref, xbc, q, HIST_S)
    hout_ref[...] = tail_ref[...]
    off = HIST_S - (SSM_CONV_W - 1)
    acc = jnp.broadcast_to(cb_ref[...], (q, cb_ref.shape[1]))
    for k in range(SSM_CONV_W):
        acc = acc + cw_ref[k:k + 1, :] * ext_ref[off + k:off + k + q, :]
    xc = _silu(acc)
    ext_ref[0:HIST_S, :] = ext_ref[q:q + HIST_S, :]

    gw = SSM_GROUPS * SSM_STATE
    xs = xc[:, :inner]
    bm = xc[:, inner:inner + gw]
    cm = xc[:, inner + gw:inner + 2 * gw]

    dt = _softplus(dt_ref[...] + dtb_ref[...])
    dtt = _softplus(dtt_ref[...] + dtbt_ref[...])
    da = dt * a_ref[...]
    dat = dtt * at_ref[...]
    ri = lax.broadcasted_iota(jnp.int32, (q, q), 0)
    ci = lax.broadcasted_iota(jnp.int32, (q, q), 1)
    causal = ri >= ci
    tri = causal.astype(F32)
    trit = (ri <= ci).astype(F32)
    acum = jnp.dot(tri, da, precision=HIGHEST, preferred_element_type=F32)
    acumt = jnp.dot(dat, trit, precision=HIGHEST, preferred_element_type=F32)
    alast = acum[q - 1:q, :]
    hpg = heads // SSM_GROUPS
    for g in range(SSM_GROUPS):
        bg = bm[:, g * SSM_STATE:(g + 1) * SSM_STATE].astype(BF16)
        cg = cm[:, g * SSM_STATE:(g + 1) * SSM_STATE].astype(BF16)
        cbm = _dot_nt(cg, bg)
        for e in range(hpg):
            hh = g * hpg + e
            a_col = acum[:, hh:hh + 1]
            a_row = acumt[hh:hh + 1, :]
            decay = jnp.exp(jnp.where(causal, a_col - a_row, NEG))
            wm = cbm * decay * dtt[hh:hh + 1, :]
            xh = xs[:, hh * SSM_HEAD_DIM:(hh + 1) * SSM_HEAD_DIM]
            hprev = h_ref[hh]
            y_h = (jnp.dot(wm.astype(BF16), xh.astype(BF16), preferred_element_type=F32)
                   + _dot_nt(cg, hprev.astype(BF16)) * jnp.exp(a_col))
            yacc_ref[:, hh * SSM_HEAD_DIM:(hh + 1) * SSM_HEAD_DIM] = y_h
            al = alast[:, hh:hh + 1]
            w_end = jnp.exp(al - a_col) * dt[:, hh:hh + 1]
            h_ref[hh] = jnp.exp(al) * hprev + _dot_tn((xh * w_end).astype(BF16), bg)

    y = (yacc_ref[...] + xs * dvec_ref[...]) * _silu(z_ref[...])
    gsz = inner // SSM_GROUPS
    for g in range(SSM_GROUPS):
        v = y[:, g * gsz:(g + 1) * gsz]
        ms = jnp.mean(v * v, axis=-1, keepdims=True)
        y_ref[:, g * gsz:(g + 1) * gsz] = (v * lax.rsqrt(ms + EPS) * gn_ref[:, g * gsz:(g + 1) * gsz]).astype(y_ref.dtype)

    @pl.when(c == nc - 1)
    def _():
        sout_ref[...] = h_ref[...]


def ssd_branch(xbc, z, small, small_t, row_off, hist, h0, cw, cb, dt_bias, a_log, d_skip, norm_g, batch, t_len, out_dtype,
               n_out=None):
    conv_dim = xbc.shape[1]
    inner = z.shape[1]
    heads = dt_bias.shape[0]
    assert heads == DT_COLS
    q = math.gcd(t_len, SSM_CHUNK)
    nc = t_len // q
    assert row_off % q == 0
    r0 = row_off // q
    pad = LANES - heads
    dtb = jnp.pad(dt_bias, (0, pad)).reshape(1, LANES)
    a_neg = jnp.pad(-jnp.exp(a_log), (0, pad)).reshape(1, LANES)
    dvec = jnp.repeat(d_skip, SSM_HEAD_DIM).reshape(1, inner)
    kern = functools.partial(_ssd_kernel, q=q, nc=nc, inner=inner, heads=heads)
    row = lambda w: pl.BlockSpec((1, w), lambda bi, ci: (0, 0))
    col = lambda: pl.BlockSpec((LANES, 1), lambda bi, ci: (0, 0))
    tok = lambda w: pl.BlockSpec((q, w), lambda bi, ci: (r0 + bi * nc + ci, 0))
    in_specs = [tok(conv_dim), tok(inner), tok(LANES),
                pl.BlockSpec((None, LANES, q), lambda bi, ci: (bi, 0, ci)),
                pl.BlockSpec((None, HIST_S, conv_dim), lambda bi, ci: (bi, 0, 0)),
                pl.BlockSpec((None, heads, SSM_HEAD_DIM, SSM_STATE), lambda bi, ci: (bi, 0, 0, 0)),
                pl.BlockSpec((SSM_CONV_W, conv_dim), lambda bi, ci: (0, 0)), row(conv_dim),
                row(LANES), col(), row(LANES), col(), row(inner), row(inner)]
    args = (xbc, z, small, small_t, hist, h0, _round_bf16(cw), cb.reshape(1, conv_dim), dtb, dtb.reshape(LANES, 1),
            a_neg, a_neg.reshape(LANES, 1), dvec, norm_g.reshape(1, inner))
    kern, in_specs, args, alias = _prefilled(kern, in_specs, args, n_out, inner, out_dtype)
    return pl.pallas_call(
        kern,
        grid=(batch, nc),
        in_specs=in_specs,
        input_output_aliases=alias,
        out_specs=[pl.BlockSpec((q, inner), lambda bi, ci: (bi * nc + ci, 0)),
                   pl.BlockSpec((None, HIST_S, conv_dim), lambda bi, ci: (bi, 0, 0)),
                   pl.BlockSpec((None, heads, SSM_HEAD_DIM, SSM_STATE), lambda bi, ci: (bi, 0, 0, 0))],
        out_shape=[jax.ShapeDtypeStruct((n_out or batch * t_len, inner), out_dtype),
                   jax.ShapeDtypeStruct((batch, HIST_S, conv_dim), F32),
                   jax.ShapeDtypeStruct((batch, heads, SSM_HEAD_DIM, SSM_STATE), F32)],
        scratch_shapes=[pltpu.VMEM((q + HIST_S, conv_dim), F32),
                        pltpu.VMEM((heads, SSM_HEAD_DIM, SSM_STATE), F32),
                        pltpu.VMEM((q, inner), F32), pltpu.VMEM((HIST_S, conv_dim), F32)],
        compiler_params=_params(("parallel", "arbitrary")),
        name="ssd",
    )(*args)


def _bucket_np(d):
    n = np.maximum(d, 0)
    exact = REL_BUCKETS // 2
    nf = np.maximum(n, 1).astype(np.float32)
    large = exact + (np.log(nf / np.float32(exact)) / np.float32(math.log(REL_MAX_DIST / exact))
                     * np.float32(REL_BUCKETS - exact)).astype(np.int32)
    return np.where(n < exact, n, np.minimum(large, REL_BUCKETS - 1))


def _near_distances():
    i = np.arange(QCHUNK)[:, None]
    j = np.arange(LANES)[None, :]
    diag = i - j
    prev = QCHUNK + i - j
    band = np.where(j < 4, i + (BLK + 1) - BLK * j, 0)
    return np.stack([diag, prev, band])


def _expand_kernel(tab_ref, oh_ref, o_ref):
    o_ref[...] = jnp.dot(tab_ref[...], oh_ref[...], precision=HIGHEST, preferred_element_type=F32)


def expand_bias(rel_bias, dist):
    flat = dist.reshape(-1)
    m = flat.shape[0]
    assert m % LANES == 0
    onehot = (np.arange(REL_BUCKETS)[:, None] == _bucket_np(flat)[None, :]).astype(np.float32)
    tm = _tile(m, 8192, LANES)
    heads = rel_bias.shape[1]
    out = pl.pallas_call(
        _expand_kernel,
        grid=(m // tm,),
        in_specs=[pl.BlockSpec((heads, REL_BUCKETS), lambda i: (0, 0)), pl.BlockSpec((REL_BUCKETS, tm), lambda i: (0, i))],
        out_specs=pl.BlockSpec((heads, tm), lambda i: (0, i)),
        out_shape=jax.ShapeDtypeStruct((heads, m), F32),
        compiler_params=_params(("parallel",)),
        name="bias_expand",
    )(rel_bias.T, jnp.asarray(onehot))
    return out.reshape((heads,) + dist.shape)


def _select_blocks(imp, lane, axis=-1):
    sel = jnp.zeros(imp.shape, F32)
    big = jnp.int32(1 << 30)
    v = imp
    for _ in range(TOPK):
        mx = jnp.max(v, axis=axis, keepdims=True)
        idx = jnp.min(jnp.where(v == mx, lane, big), axis=axis, keepdims=True)
        hit = lane == idx
        sel = jnp.where(hit, 1.0, sel)
        v = jnp.where(hit, -3e38, v)
    return sel


def _rep4(x):
    return jnp.concatenate([x, x, x, x], axis=0)


def _qgroup(q_ref, g):
    return jnp.concatenate([q_ref[:, (g * GQA + r) * HEAD_DIM:(g * GQA + r + 1) * HEAD_DIM] for r in range(GQA)], axis=0)


def _gate_col(sg, br, g):
    c0 = GATE_COL0 + br * N_HEADS + g * GQA
    return jnp.concatenate([sg[:, c0 + r:c0 + r + 1] for r in range(GQA)], axis=0)


def _rows_means_kernel(x_ref, o_ref):
    nb, w = o_ref.shape
    o_ref[...] = jnp.sum(x_ref[...].reshape(nb, BLK, w), axis=1) * (1.0 / BLK)


def prompt_block_means(kv, batch, t_len):
    w = 2 * KV_GROUPS * HEAD_DIM
    nb = t_len // BLK
    return pl.pallas_call(
        _rows_means_kernel,
        grid=(batch,),
        in_specs=[pl.BlockSpec((t_len, w), lambda b: (b, 0))],
        out_specs=pl.BlockSpec((None, nb, w), lambda b: (b, 0, 0)),
        out_shape=jax.ShapeDtypeStruct((batch, nb, w), F32),
        compiler_params=_params(("parallel",)),
        name="cmp_means_prompt",
    )(kv)


def _page_specs(pps, layer):
    def spec(k):
        return pl.BlockSpec((None, None, 4, KV_GROUPS, HEAD_DIM, 2 * BLK),
                            lambda b, j, pt: (pt[b, j * pps + k], layer, 0, 0, 0, 0))
    return [spec(k) for k in range(pps)]


def _nsa_prompt_kernel(q_ref, ksel_ref, vsel_ref, kwin_ref, vwin_ref, kc_ref, gn_ref, tab_ref, band_ref, cst_ref, e_ref, o_ref,
                       s_ref, sel_ref, oc_ref, mrun_ref, acc_ref, *, nb):
    c = pl.program_id(1)
    qc = QCHUNK
    rows = GQA * qc
    kw = KV_GROUPS * HEAD_DIM
    span = SUPER * qc
    blk = lax.broadcasted_iota(jnp.int32, (nb, rows), 0)
    qpos = c * qc + (lax.broadcasted_iota(jnp.int32, (nb, rows), 1) & (qc - 1))
    okc = (qpos - BLK * blk - (BLK - 1)) >= 0
    blk1 = lax.broadcasted_iota(jnp.int32, (nb, qc), 0)
    qi1 = lax.broadcasted_iota(jnp.int32, (nb, qc), 1)
    okc1 = (c * qc + qi1 - BLK * blk1 - (BLK - 1)) >= 0
    cur = 2 * c + (qi1 >= BLK).astype(jnp.int32)
    for g in range(KV_GROUPS):
        qg = _qgroup(q_ref, g)
        kcg = kc_ref[:, g * HEAD_DIM:(g + 1) * HEAD_DIM].astype(BF16)
        vcg = kc_ref[:, kw + g * HEAD_DIM:kw + (g + 1) * HEAD_DIM].astype(BF16)
        bias_c = jnp.broadcast_to(cst_ref[g], (nb, rows))
        for u in range(4):
            bias_c = jnp.where(blk == 2 * c - 2 + u, band_ref[g, u:u + 1, :], bias_c)
        s = jnp.where(okc, _dot_nt(kcg, qg) + bias_c, NEG)
        e = jnp.exp(s - jnp.max(s, axis=0, keepdims=True))
        p = jnp.where(okc, e / jnp.sum(e, axis=0, keepdims=True), 0.0)
        oc_ref[g] = _dot_tn(p.astype(BF16), vcg)
        psum = p[:, 0:qc] + p[:, qc:2 * qc] + p[:, 2 * qc:3 * qc] + p[:, 3 * qc:4 * qc]
        imp = jnp.where(blk1 == cur, FORCE, jnp.where(okc1, psum, -1.0))
        selm = jnp.where(blk1 <= cur, _select_blocks(imp, blk1, axis=0), 0.0)
        sel_ref[g] = (_dot_tn(selm.astype(BF16), e_ref[...]) - 1.0) * (-NEG)

    sg = jax.nn.sigmoid(gn_ref[...])
    n_super = c // SUPER + 1
    w_tiles = WINDOW // qc + 1
    st0 = jnp.maximum(c - (w_tiles - 1), 0)
    for g in range(KV_GROUPS):
        qg = _qgroup(q_ref, g)
        mrun_ref[...] = jnp.full((rows, LANES), NEG, F32)

        def scores(st, carry):
            k = ksel_ref[pl.ds(pl.multiple_of(st * span, span), span), g * HEAD_DIM:(g + 1) * HEAD_DIM]
            s = _dot_nt(qg, k)
            mrun = mrun_ref[...]
            for j in range(SUPER):
                kt = st * SUPER + j
                idx = jnp.where(kt == c, 0, jnp.where(kt == c - 1, 1, 2))
                off = pl.multiple_of(kt * qc, qc)
                sj = s[:, j * qc:(j + 1) * qc] + tab_ref[g, idx] + _rep4(sel_ref[g, :, pl.ds(off, qc)])
                s_ref[:, pl.ds(off, qc)] = sj
                mrun = jnp.maximum(mrun, sj)
            mrun_ref[...] = mrun
            return carry

        lax.fori_loop(0, n_super, scores, 0)
        m = jnp.max(mrun_ref[...], axis=-1, keepdims=True)
        mrun_ref[...] = jnp.zeros((rows, LANES), F32)

        def exps(st, carry):
            off = pl.multiple_of(st * span, span)
            e = jnp.exp(s_ref[:, pl.ds(off, span)] - m)
            s_ref[:, pl.ds(off, span)] = e
            lrun = mrun_ref[...]
            for j in range(SUPER):
                lrun = lrun + e[:, j * qc:(j + 1) * qc]
            mrun_ref[...] = lrun
            return carry

        lax.fori_loop(0, n_super, exps, 0)
        denom = jnp.sum(mrun_ref[...], axis=-1, keepdims=True)
        acc_ref[...] = jnp.zeros((rows, HEAD_DIM), F32)

        def values(st, carry):
            off = pl.multiple_of(st * span, span)
            p = (s_ref[:, pl.ds(off, span)] / denom).astype(BF16)
            acc_ref[...] += jnp.dot(p, vsel_ref[pl.ds(off, span), g * HEAD_DIM:(g + 1) * HEAD_DIM], preferred_element_type=F32)
            return carry

        lax.fori_loop(0, n_super, values, 0)
        o_s = acc_ref[...]
        r0 = pl.multiple_of(st0 * qc, qc)
        s = _dot_nt(qg, kwin_ref[pl.ds(r0, w_tiles * qc), g * HEAD_DIM:(g + 1) * HEAD_DIM])
        parts = []
        for w in range(w_tiles):
            kt = st0 + w
            idx = jnp.where(kt > c, 4, jnp.where(kt == c, 0, jnp.where(kt == c - 1, 1,
                                                                         jnp.where(kt == c - (w_tiles - 1), 3, 2))))
            parts.append(s[:, w * qc:(w + 1) * qc] + tab_ref[g, idx])
        sw = jnp.concatenate(parts, axis=1)
        ew = jnp.exp(sw - jnp.max(sw, axis=-1, keepdims=True))
        pw = (ew / jnp.sum(ew, axis=-1, keepdims=True)).astype(BF16)
        o_w = jnp.dot(pw, vwin_ref[pl.ds(r0, w_tiles * qc), g * HEAD_DIM:(g + 1) * HEAD_DIM],
                      preferred_element_type=F32)
        tot = _gate_col(sg, 0, g) * oc_ref[g] + _gate_col(sg, 1, g) * o_s + _gate_col(sg, 2, g) * o_w
        for r in range(GQA):
            o_ref[:, (g * GQA + r) * HEAD_DIM:(g * GQA + r + 1) * HEAD_DIM] = tot[r * qc:(r + 1) * qc].astype(o_ref.dtype)


def nsa_prompt(q, kvb, kc, small, tabs, band, cst, batch, t_len, n_out=None):
    assert t_len % (SUPER * QCHUNK) == 0 and t_len >= WINDOW + QCHUNK
    nch = t_len // QCHUNK
    nb = t_len // BLK
    assert nb % SUBLANES == 0
    rows = GQA * QCHUNK
    hd = N_HEADS * HEAD_DIM
    emat = (np.arange(nb)[:, None] == (np.arange(t_len)[None, :] // BLK)).astype(np.float32)
    kern = functools.partial(_nsa_prompt_kernel, nb=nb)
    full = lambda shp: pl.BlockSpec(shp, lambda b, c: (0,) * len(shp))
    kind = lambda k: pl.BlockSpec((t_len, KV_GROUPS * HEAD_DIM), lambda b, c: (b, k))
    in_specs = [pl.BlockSpec((QCHUNK, hd), lambda b, c: (b * nch + c, 0)),
                kind(2), kind(3), kind(4), kind(5),
                pl.BlockSpec((None, nb, kc.shape[2]), lambda b, c: (b, 0, 0)),
                pl.BlockSpec((QCHUNK, LANES), lambda b, c: (b * nch + c, 0)),
                full((KV_GROUPS, 5, rows, LANES)), full((KV_GROUPS, SUBLANES, rows)), full((KV_GROUPS, 1, rows)),
                full((nb, t_len))]
    args = (q, kvb, kvb, kvb, kvb, kc, small, tabs, band, cst, jnp.asarray(emat, BF16))
    kern, in_specs, args, alias = _prefilled(kern, in_specs, args, n_out, hd, BF16)
    return pl.pallas_call(
        kern,
        grid=(batch, nch),
        in_specs=in_specs,
        input_output_aliases=alias,
        out_specs=pl.BlockSpec((QCHUNK, hd), lambda b, c: (b * nch + c, 0)),
        out_shape=jax.ShapeDtypeStruct((n_out or batch * t_len, hd), BF16),
        scratch_shapes=[pltpu.VMEM((rows, t_len), F32),
                        pltpu.VMEM((KV_GROUPS, QCHUNK, t_len), F32),
                        pltpu.VMEM((KV_GROUPS, rows, HEAD_DIM), F32),
                        pltpu.VMEM((rows, LANES), F32), pltpu.VMEM((rows, HEAD_DIM), F32)],
        compiler_params=_params(("parallel", "arbitrary")),
        name="nsa_prompt",
    )(*args)


def _nsa_sample_kernel(pt_ref, q_ref, kvn_ref, gn_ref, avg_ref, win_ref, near_ref, wtab_ref, cst_ref, e_ref,
                       *refs, pps, n_steps, past, ts, nbl):
    pages = refs[:pps]
    o_ref = refs[pps]
    s_ref, kc_ref, vbuf_ref, kn_ref = refs[pps + 1:]
    j = pl.program_id(1)
    rows = GQA * ts
    kw = KV_GROUPS * HEAD_DIM
    nb_past = past // BLK
    pg = 2 * BLK
    lane = lax.broadcasted_iota(jnp.int32, (rows, LANES), 1)
    ti = lax.broadcasted_iota(jnp.int32, (rows, LANES), 0) & (ts - 1)

    def qgroup(g):
        return _qgroup(q_ref, g).astype(BF16)

    @pl.when(j == 0)
    def _():
        kn_ref[...] = jnp.zeros(kn_ref.shape, F32)
        kn_ref[0:ts, :] = kvn_ref[:, 2 * kw:6 * kw]
        kc_ref[...] = jnp.zeros(kc_ref.shape, F32)

    is_last = j == n_steps - 1
    base = j * (pps * pg)
    qgs = [qgroup(g) for g in range(KV_GROUPS)]
    avg = avg_ref[...]
    blk0 = pl.multiple_of(j * (2 * pps), 2 * pps)
    for kind in range(2):
        for g in range(KV_GROUPS):
            cat = jnp.concatenate([pgr[kind, g] for pgr in pages], axis=1)
            hi = cat.astype(BF16)
            lo = (cat - hi.astype(F32)).astype(BF16)
            kc_ref[kind, g, pl.ds(blk0, 2 * pps), :] = _dot_nt(avg, hi) + _dot_nt(avg, lo)
    for k in range(pps):
        off = pl.multiple_of(base + k * pg, pg)
        for g in range(KV_GROUPS):
            vbuf_ref[g, :, pl.ds(off, pg)] = pages[k][3, g].astype(BF16)
            sk = jnp.dot(qgs[g], pages[k][2, g].astype(BF16), preferred_element_type=F32)
            bias = jnp.where(is_last, near_ref[g, 1], cst_ref[g]) if k == pps - 1 else cst_ref[g]
            s_ref[g, :, pl.ds(off, pg)] = sk + bias

    @pl.when(is_last)
    def _():
        sg = jax.nn.sigmoid(gn_ref[...])
        wl = win_ref.shape[-1]
        lane_w = lax.broadcasted_iota(jnp.int32, (rows, wl + LANES), 1)
        t_w = lax.broadcasted_iota(jnp.int32, (rows, wl + LANES), 0) & (ts - 1)
        okw = jnp.where(lane_w < wl, lane_w - t_w, t_w - (lane_w - wl)) >= 0
        lane_b = lax.broadcasted_iota(jnp.int32, (rows, nbl), 1)
        t_b = lax.broadcasted_iota(jnp.int32, (rows, nbl), 0) & (ts - 1)
        okc = (past + t_b - BLK * lane_b - (BLK - 1)) >= 0
        lane1 = lax.broadcasted_iota(jnp.int32, (ts, nbl), 1)
        okc1 = (past + lax.broadcasted_iota(jnp.int32, (ts, nbl), 0) - BLK * lane1 - (BLK - 1)) >= 0
        o_cs, madds = [], []
        for g in range(KV_GROUPS):
            qg = qgs[g]
            cst = cst_ref[g]
            band = near_ref[g, 2]
            kcg = kc_ref[0, g].astype(BF16)
            vcg = kc_ref[1, g].astype(BF16)
            bias_c = jnp.where(lane_b == nb_past - 2, band[:, 0:1], jnp.where(lane_b == nb_past - 1, band[:, 1:2], cst))
            s = jnp.where(okc, _dot_nt(qg, kcg) + bias_c, NEG)
            e = jnp.exp(s - jnp.max(s, axis=-1, keepdims=True))
            p = jnp.where(okc, e / jnp.sum(e, axis=-1, keepdims=True), 0.0)
            o_cs.append(jnp.dot(p.astype(BF16), vcg, preferred_element_type=F32))
            psum = p[0:ts] + p[ts:2 * ts] + p[2 * ts:3 * ts] + p[3 * ts:4 * ts]
            imp = jnp.where(lane1 == nb_past, FORCE, jnp.where(okc1, psum, -1.0))
            imp = jnp.where(lane1 <= nb_past, imp, -3e38)
            selm = _select_blocks(imp, lane1)
            selm16 = jnp.concatenate([selm, jnp.zeros_like(selm)], axis=0).astype(BF16)
            madds.append((jnp.dot(selm16, e_ref[...], preferred_element_type=F32)[0:ts] - 1.0) * (-NEG))
        for g in range(KV_GROUPS):
            qg = qgs[g]
            o_c, madd = o_cs[g], madds[g]
            kn = kn_ref[:, g * HEAD_DIM:(g + 1) * HEAD_DIM].astype(BF16)
            vn = kn_ref[:, kw + g * HEAD_DIM:kw + (g + 1) * HEAD_DIM].astype(BF16)
            s_new = jnp.where(lane <= ti, _dot_nt(qg, kn) + near_ref[g, 0], NEG)
            s = jnp.concatenate([s_ref[g] + _rep4(madd), s_new], axis=1)
            e = jnp.exp(s - jnp.max(s, axis=-1, keepdims=True))
            p = e / jnp.sum(e, axis=-1, keepdims=True)
            o_s = (_dot_nt(p[:, :past].astype(BF16), vbuf_ref[g])
                   + jnp.dot(p[:, past:].astype(BF16), vn, preferred_element_type=F32))
            kwin = win_ref[0, g].astype(BF16)
            vwin = win_ref[1, g].astype(BF16)
            knw = kn_ref[:, 2 * kw + g * HEAD_DIM:2 * kw + (g + 1) * HEAD_DIM].astype(BF16)
            vnw = kn_ref[:, 3 * kw + g * HEAD_DIM:3 * kw + (g + 1) * HEAD_DIM].astype(BF16)
            s = jnp.concatenate([jnp.dot(qg, kwin, preferred_element_type=F32), _dot_nt(qg, knw)], axis=1) + wtab_ref[g]
            s = jnp.where(okw, s, NEG)
            e = jnp.exp(s - jnp.max(s, axis=-1, keepdims=True))
            p = e / jnp.sum(e, axis=-1, keepdims=True)
            o_w = (_dot_nt(p[:, :wl].astype(BF16), vwin)
                   + jnp.dot(p[:, wl:].astype(BF16), vnw, preferred_element_type=F32))
            tot = _gate_col(sg, 0, g) * o_c + _gate_col(sg, 1, g) * o_s + _gate_col(sg, 2, g) * o_w
            for r in range(GQA):
                o_ref[:, (g * GQA + r) * HEAD_DIM:(g * GQA + r + 1) * HEAD_DIM] = tot[r * ts:(r + 1) * ts]


def nsa_sample(q3, kvn3, small3, win_t, near_s, wtab, cst_s, cache_t, page_table, layer, pps, nbl):
    batch, ts, hd = q3.shape
    n_pages = page_table.shape[1]
    n_steps = n_pages // pps
    past = n_pages * 2 * BLK
    wl = win_t.shape[-1]
    avg = np.where(np.arange(2 * pps)[:, None] == np.arange(pps * 2 * BLK)[None, :] // BLK, 1.0 / BLK, 0.0)
    rows = GQA * ts
    kw = KV_GROUPS * HEAD_DIM
    assert ts == SUBLANES and past % QCHUNK == 0 and wl % LANES == 0
    emat = (np.arange(nbl)[:, None] == (np.arange(past)[None, :] // BLK)).astype(np.float32)
    kern = functools.partial(_nsa_sample_kernel, pps=pps, n_steps=n_steps, past=past, ts=ts, nbl=nbl)
    c3 = lambda shp: pl.BlockSpec(shp, lambda b, j, pt: (0,) * len(shp))
    return pl.pallas_call(
        kern,
        grid_spec=pltpu.PrefetchScalarGridSpec(
            num_scalar_prefetch=1, grid=(batch, n_steps),
            in_specs=[pl.BlockSpec((None, ts, hd), lambda b, j, pt: (b, 0, 0)),
                      pl.BlockSpec((None, ts, kvn3.shape[2]), lambda b, j, pt: (b, 0, 0)),
                      pl.BlockSpec((None, ts, LANES), lambda b, j, pt: (b, 0, 0)),
                      c3((2 * pps, pps * 2 * BLK)),
                      pl.BlockSpec((None, None, 2, KV_GROUPS, HEAD_DIM, wl), lambda b, j, pt: (layer, b, 0, 0, 0, 0)),
                      c3((KV_GROUPS, 3, rows, LANES)), c3((KV_GROUPS, rows, wl + LANES)), c3((KV_GROUPS, rows, 1)),
                      c3((nbl, past))] + _page_specs(pps, layer),
            out_specs=pl.BlockSpec((None, ts, hd), lambda b, j, pt: (b, 0, 0)),
            scratch_shapes=[pltpu.VMEM((KV_GROUPS, rows, past), F32),
                            pltpu.VMEM((2, KV_GROUPS, nbl, HEAD_DIM), F32),
                            pltpu.VMEM((KV_GROUPS, HEAD_DIM, past), BF16),
                            pltpu.VMEM((LANES, 4 * kw), F32)]),
        out_shape=jax.ShapeDtypeStruct((batch, ts, hd), F32),
        compiler_params=_params(("parallel", "arbitrary")),
        name="nsa_sample",
    )(page_table, q3, kvn3, small3, jnp.asarray(avg, BF16), win_t, near_s, wtab, cst_s, jnp.asarray(emat, BF16),
      *([cache_t] * pps))


def _merge_kernel(c_ref, o_ref, y_ref, g0_ref, g1_ref, g2_ref, wa_ref, wn_ref, ws_ref, out_ref):
    oa = jnp.dot(c_ref[...], wa_ref[...], preferred_element_type=F32)
    ob = jnp.dot(o_ref[...], wn_ref[...], preferred_element_type=F32)
    oc = jnp.dot(y_ref[...], ws_ref[...], preferred_element_type=F32)
    mixed = (jax.nn.sigmoid(g0_ref[...]) * oa + jax.nn.sigmoid(g1_ref[...]) * ob + jax.nn.sigmoid(g2_ref[...]) * oc)
    out_ref[...] = mixed.astype(out_ref.dtype)


def merge_branches(c, o, y, gmix, wa, wn, ws):
    n = c.shape[0]
    d = wa.shape[1]
    tm = _tile(n, 768, BF16_ROWS)
    tn = _tile(d, 512, LANES)
    nj = d // tn
    act = lambda a: pl.BlockSpec((tm, a.shape[1]), lambda i, j: (i, 0))
    wsp = lambda w: pl.BlockSpec((w.shape[0], tn), lambda i, j: (0, j))
    gsp = lambda br: pl.BlockSpec((tm, tn), lambda i, j: (i, br * nj + j))
    return pl.pallas_call(
        _merge_kernel,
        grid=(n // tm, nj),
        in_specs=[act(c), act(o), act(y), gsp(0), gsp(1), gsp(2), wsp(wa), wsp(wn), wsp(ws)],
        out_specs=pl.BlockSpec((tm, tn), lambda i, j: (i, j)),
        out_shape=jax.ShapeDtypeStruct((n, d), BF16),
        compiler_params=_params(("parallel", "parallel")),
        name="merge",
    )(c, o, y, gmix, gmix, gmix, wa, wn, ws)


def _outproj_kernel(a_ref, w_ref, x_ref, o_ref):
    o_ref[...] = x_ref[...] + jnp.dot(a_ref[...], w_ref[...], preferred_element_type=F32)


def out_projection(a, w, x):
    n, k = a.shape
    d = w.shape[1]
    tm = _tile(n, 768, BF16_ROWS)
    tn = _tile(d, 1024, LANES)
    return pl.pallas_call(
        _outproj_kernel,
        grid=(n // tm, d // tn),
        in_specs=[pl.BlockSpec((tm, k), lambda i, j: (i, 0)), pl.BlockSpec((k, tn), lambda i, j: (0, j)),
                  pl.BlockSpec((tm, tn), lambda i, j: (i, j))],
        out_specs=pl.BlockSpec((tm, tn), lambda i, j: (i, j)),
        out_shape=jax.ShapeDtypeStruct((n, d), F32),
        compiler_params=_params(("parallel", "parallel")),
        name="out_proj",
    )(a, w, x)


def _router_kernel(x_ref, g_ref, w_ref, b_ref, h_ref, r_ref):
    x = x_ref[...]
    ms = jnp.mean(x * x, axis=-1, keepdims=True)
    h = (x * lax.rsqrt(ms + EPS) * g_ref[...]).astype(BF16)
    h_ref[...] = h
    logits = jnp.dot(h, w_ref[...], preferred_element_type=F32) + b_ref[...]
    lane = lax.broadcasted_iota(jnp.int32, logits.shape, 1)
    big = jnp.int32(1 << 30)
    gl = jnp.where(lane < MOE_GROUPS, logits, NEG)
    gmax = jnp.max(gl, axis=-1, keepdims=True)
    gidx = jnp.min(jnp.where(gl == gmax, lane, big), axis=-1, keepdims=True)
    g_w = 1.0 / jnp.sum(jnp.exp(gl - gmax), axis=-1, keepdims=True)
    eid = lane - MOE_GROUPS
    in_grp = (eid >= 0) & (eid < MOE_GROUPS * MOE_EPG) & ((eid >> 3) == gidx)
    el = jnp.where(in_grp, logits, NEG)
    m1 = jnp.max(el, axis=-1, keepdims=True)
    i1 = jnp.min(jnp.where(el == m1, lane, big), axis=-1, keepdims=True)
    el2 = jnp.where(lane == i1, NEG, el)
    m2 = jnp.max(el2, axis=-1, keepdims=True)
    i2 = jnp.min(jnp.where(el2 == m2, lane, big), axis=-1, keepdims=True)
    p2 = jnp.exp(m2 - m1)
    w1 = g_w / (1.0 + p2)
    w2 = g_w * p2 / (1.0 + p2)
    r_ref[...] = jnp.where(lane == 0, (i1 - MOE_GROUPS).astype(F32),
                           jnp.where(lane == 1, (i2 - MOE_GROUPS).astype(F32),
                                     jnp.where(lane == 2, w1, jnp.where(lane == 3, w2, 0.0))))


def moe_router(x, g, wr, br):
    n, d = x.shape
    tm = _tile(n, 256, BF16_ROWS)
    return pl.pallas_call(
        _router_kernel,
        grid=(n // tm,),
        in_specs=[pl.BlockSpec((tm, d), lambda i: (i, 0)), pl.BlockSpec((1, d), lambda i: (0, 0)),
                  pl.BlockSpec((d, LANES), lambda i: (0, 0)), pl.BlockSpec((1, LANES), lambda i: (0, 0))],
        out_specs=[pl.BlockSpec((tm, d), lambda i: (i, 0)), pl.BlockSpec((tm, LANES), lambda i: (i, 0))],
        out_shape=[jax.ShapeDtypeStruct((n, d), BF16), jax.ShapeDtypeStruct((n, LANES), F32)],
        compiler_params=_params(("parallel",)),
        name="moe_router",
    )(x, g.reshape(1, d), wr, br)


def _expert_kernel(te_ref, nu_ref, x_ref, rw_ref, wg_ref, wu_ref, wd_ref, y_ref, wgb_ref, wub_ref, wdb_ref):
    t = pl.program_id(0)
    fresh = jnp.logical_or(t == 0, te_ref[t] != te_ref[jnp.maximum(t - 1, 0)])

    @pl.when(jnp.logical_and(fresh, t < nu_ref[0]))
    def _():
        wgb_ref[...] = wg_ref[...].astype(BF16)
        wub_ref[...] = wu_ref[...].astype(BF16)
        wdb_ref[...] = wd_ref[...].astype(BF16)

    @pl.when(t < nu_ref[0])
    def _():
        x = x_ref[...]
        a = jnp.dot(x, wgb_ref[...], preferred_element_type=F32)
        u = jnp.dot(x, wub_ref[...], preferred_element_type=F32)
        rw = rw_ref[...]
        hid = (_silu(a) * u * jnp.concatenate([rw] * (a.shape[1] // LANES), axis=1)).astype(BF16)
        y_ref[...] = jnp.dot(hid, wdb_ref[...], preferred_element_type=F32)


def moe_experts(xs, row_w, tile_e, n_used, wg, wu, wd, layer, tme):
    r, d = xs.shape
    ff = wg.shape[3]
    nt = r // tme
    return pl.pallas_call(
        _expert_kernel,
        grid_spec=pltpu.PrefetchScalarGridSpec(
            num_scalar_prefetch=2, grid=(nt,),
            in_specs=[pl.BlockSpec((tme, d), lambda t, te, nu: (t, 0)),
                      pl.BlockSpec((tme, LANES), lambda t, te, nu: (t, 0)),
                      pl.BlockSpec((None, None, d, ff), lambda t, te, nu: (layer, te[t], 0, 0)),
                      pl.BlockSpec((None, None, d, ff), lambda t, te, nu: (layer, te[t], 0, 0)),
                      pl.BlockSpec((None, None, ff, d), lambda t, te, nu: (layer, te[t], 0, 0))],
            out_specs=pl.BlockSpec((tme, d), lambda t, te, nu: (t, 0)),
            scratch_shapes=[pltpu.VMEM((d, ff), BF16), pltpu.VMEM((d, ff), BF16), pltpu.VMEM((ff, d), BF16)]),
        out_shape=jax.ShapeDtypeStruct((r, d), F32),
        compiler_params=_params(("arbitrary",)),
        name="moe_experts",
    )(tile_e, n_used, xs, row_w, wg, wu, wd)


def sc_gather_rows(table, idx):
    n, w = table.shape
    r = idx.shape[0]
    nw = SC_CORES * SC_SUBCORES
    assert r % nw == 0
    per_w = r // nw
    ch = _tile(per_w, SC_TILE_BYTES // (w * 4), SUBLANES)
    assert per_w % ch == 0 and ch % SUBLANES == 0 and ch * w * 4 <= SC_TILE_BYTES
    mesh = plsc.VectorSubcoreMesh(core_axis_name="c", subcore_axis_name="s")

    @functools.partial(pl.kernel, mesh=mesh, out_type=jax.ShapeDtypeStruct((r, w), jnp.int32),
                       scratch_types=[pltpu.VMEM((ch,), jnp.int32), pltpu.VMEM((ch, w), jnp.int32), pltpu.SemaphoreType.DMA])
    def gather(table_hbm, idx_hbm, out_hbm, idx_v, rows_v, sem):
        wid = lax.axis_index("s") * SC_CORES + lax.axis_index("c")

        def chunk(i, carry):
            base = pl.multiple_of(wid * per_w + i * ch, SUBLANES)
            pltpu.sync_copy(idx_hbm.at[pl.ds(base, ch)], idx_v)
            pltpu.async_copy(table_hbm.at[idx_v], rows_v, sem).wait()
            pltpu.sync_copy(rows_v, out_hbm.at[pl.ds(base, ch)])
            return carry

        lax.fori_loop(0, per_w // ch, chunk, 0)

    return gather(table, idx)


def moe_layer(x, g, wrg, brg, wre, bre, wg, wu, wd, layer):
    n, d = x.shape
    n_exp = wg.shape[1]
    wr = jnp.concatenate([wrg, wre.transpose(1, 0, 2).reshape(d, n_exp)], axis=1)
    wr = jnp.pad(wr, ((0, 0), (0, LANES - wr.shape[1])))
    br = jnp.pad(jnp.concatenate([brg, bre.reshape(-1)]), (0, LANES - MOE_GROUPS - n_exp)).reshape(1, LANES)
    h, route = moe_router(x, g, wr.astype(BF16), br)
    tme = 256
    e_flat = route[:, 0:MOE_TOPK].astype(jnp.int32).T.reshape(-1)
    w_flat = route[:, MOE_TOPK:2 * MOE_TOPK].T.reshape(-1)
    na = e_flat.shape[0]
    nt = -(-na // tme) + n_exp
    order = jnp.argsort(e_flat, stable=True)
    counts = jnp.sum(e_flat[:, None] == jnp.arange(n_exp)[None, :], axis=0).astype(jnp.int32)
    pc = ((counts + tme - 1) // tme) * tme
    pend = jnp.cumsum(pc)
    pstart = pend - pc
    ustart = jnp.cumsum(counts) - counts
    tile_e = jnp.minimum(jnp.searchsorted(pend, jnp.arange(nt, dtype=jnp.int32) * tme, side="right"), n_exp - 1).astype(jnp.int32)
    n_used = (pend[-1] // tme).astype(jnp.int32).reshape(1)
    row = jnp.arange(nt * tme, dtype=jnp.int32)
    row_e = jnp.repeat(tile_e, tme)
    idx = row - pstart[row_e]
    valid = idx < counts[row_e]
    src = order[jnp.minimum(ustart[row_e] + idx, na - 1)]
    row_tok = jnp.where(valid, src % n, 0).astype(jnp.int32)
    row_w = jnp.where(valid, w_flat[src], 0.0)
    se = e_flat[order]
    dest = pstart[se] + jnp.arange(na, dtype=jnp.int32) - ustart[se]
    dest_flat = dest[jnp.argsort(order)]
    h32 = lax.bitcast_convert_type(h.reshape(n, d // 2, 2), jnp.int32)
    xs = lax.bitcast_convert_type(sc_gather_rows(h32, row_tok), BF16).reshape(nt * tme, d)
    ys = moe_experts(xs, jnp.broadcast_to(row_w[:, None], (nt * tme, LANES)), tile_e, n_used, wg, wu, wd, layer, tme)
    return x + jnp.take(ys, dest_flat[:n], axis=0) + jnp.take(ys, dest_flat[n:], axis=0)


def kernel(x_prompt, x_sample, cache_nsa_kv, cache_nsa_win, state_conv_a, state_conv_ssm, state_ssm, page_table,
           rel_bias, norm_mix_g, w_in, conv_a_w, conv_a_b, ln_a_g, ln_a_b, w_a_out, w_nsa_out, conv_s_w, conv_s_b,
           dt_bias, a_log, d_skip, ssm_norm_g, w_ssm_out, w_out, norm_ffn_g, w_router_group, b_router_group,
           w_router_expert, b_router_expert, w_gate, w_up, w_down, final_norm_g):
    bp, tp, d = x_prompt.shape
    bs, ts, _ = x_sample.shape
    depth = w_in.shape[0]
    n_p, n_s = bp * tp, bs * ts
    ch = conv_a_w.shape[2]
    conv_dim = conv_s_w.shape[2]
    heads_ssm = dt_bias.shape[1]
    inner = heads_ssm * SSM_HEAD_DIM
    hd = N_HEADS * HEAD_DIM
    kw = KV_GROUPS * HEAD_DIM
    kvw = KV_KINDS * kw
    n_pages = page_table.shape[1]
    page = cache_nsa_kv.shape[2]
    past = n_pages * page
    wl = cache_nsa_win.shape[2]
    assert page == 2 * BLK and wl == min(WINDOW, past) and tp >= WINDOW
    pps = _tile(n_pages, 16, 1)
    rows_p = GQA * QCHUNK

    splits = (2 * ch, hd, kvw, 3 * N_HEADS, inner, conv_dim, heads_ssm, 3 * d)
    offs = np.concatenate([[0], np.cumsum(splits)])
    o_u, o_q, o_kv, o_gn, o_z, o_x, o_dt, o_gm = [int(v) for v in offs[:-1]]

    near_d = _near_distances()
    tj = np.arange(wl + LANES)[None, :]
    win_d = np.where(tj < wl + ts, wl + np.arange(ts)[:, None] - tj, 0)
    near_f = expand_bias(rel_bias, near_d)
    wtab_f = expand_bias(rel_bias, win_d)
    near_g = near_f.reshape(KV_GROUPS, GQA, 3, QCHUNK, LANES).transpose(0, 2, 1, 3, 4)
    near_p = near_g.reshape(KV_GROUPS, 3, rows_p, LANES)
    near_s = near_g[:, :, :, :ts].reshape(KV_GROUPS, 3, GQA * ts, LANES)
    wtab = wtab_f.reshape(KV_GROUPS, GQA * ts, wl + LANES)
    far = rel_bias[REL_BUCKETS - 1].reshape(KV_GROUPS, GQA, 1)
    cst_p = jnp.repeat(far, QCHUNK, axis=1).reshape(KV_GROUPS, rows_p, 1)
    cst_s = jnp.repeat(far, ts, axis=1).reshape(KV_GROUPS, GQA * ts, 1)
    ii = (np.arange(rows_p) % QCHUNK)[:, None]
    jj = np.arange(LANES)[None, :]
    cst_t = jnp.broadcast_to(cst_p, (KV_GROUPS, rows_p, LANES))
    band_t = jnp.pad(near_p[:, 2, :, 0:4].transpose(0, 2, 1), ((0, 0), (0, SUBLANES - 4), (0, 0)))
    tabs_p = jnp.stack([jnp.where(jj <= ii, near_p[:, 0], NEG), near_p[:, 1], cst_t,
                        jnp.where(jj >= ii, cst_t, NEG), jnp.full((KV_GROUPS, rows_p, LANES), NEG, F32)], axis=1)

    w_in_t = w_in.transpose(0, 2, 1)
    cache_t = cache_nsa_kv.transpose(0, 1, 3, 4, 5, 2)
    win_t = cache_nsa_win.transpose(0, 1, 3, 4, 5, 2)
    zeros_hist_a = jnp.zeros((bp, HIST_A, ch), F32)
    zeros_hist_s = jnp.zeros((bp, HIST_S, conv_dim), F32)
    zeros_ssm = jnp.zeros((bp, heads_ssm, SSM_HEAD_DIM, SSM_STATE), F32)
    nbl = -(-(past // BLK + 1) // LANES) * LANES

    x = jnp.concatenate([x_prompt.reshape(n_p, d), x_sample.reshape(n_s, d)], axis=0)
    kv_p, kv_s, win_p, win_s, ha_p, ha_s, hs_p, hs_s, st_p, st_s = ([] for _ in range(10))
    for l in range(depth):
        h = rmsnorm(x, norm_mix_g[l], BF16)
        (u_in,) = matmul(h, w_in_t, l, o_u, 2 * ch, [F32])
        (q,) = matmul(h, w_in_t, l, o_q, hd, [BF16], scale=HEAD_DIM ** -0.5)
        kv, kvb = matmul(h, w_in_t, l, o_kv, kvw, [F32, BF16])
        w_small = jnp.concatenate([w_in_t[l, o_dt:o_gm], w_in_t[l, o_gn:o_z],
                                   jnp.zeros((LANES - heads_ssm - 3 * N_HEADS, d), F32)], axis=0)
        (small,) = matmul(h, w_small[None], 0, 0, LANES, [F32])
        (z,) = matmul(h, w_in_t, l, o_z, inner, [F32])
        (xbc,) = matmul(h, w_in_t, l, o_x, conv_dim, [F32])
        (gmix,) = matmul(h, w_in_t, l, o_gm, 3 * d, [F32])

        hist_a = jnp.pad(state_conv_a[l], ((0, 0), (HIST_A - (CONV_A_W - 1), 0), (0, 0)))
        ca_args = (conv_a_w[l], conv_a_b[l], ln_a_g[l], ln_a_b[l])
        c_p, ha_new_p = conv_a_branch(u_in, 0, zeros_hist_a, *ca_args, bp, tp, BF16, n_out=n_p + n_s)
        c_s, ha_new_s = conv_a_branch(u_in, n_p, hist_a, *ca_args, bs, ts, F32)
        hist_s = jnp.pad(state_conv_ssm[l], ((0, 0), (HIST_S - (SSM_CONV_W - 1), 0), (0, 0)))
        ss_args = (conv_s_w[l], conv_s_b[l], dt_bias[l], a_log[l], d_skip[l], ssm_norm_g[l])
        small_tp = small[:n_p].reshape(bp, tp, LANES).transpose(0, 2, 1)
        small_ts = small[n_p:].reshape(bs, ts, LANES).transpose(0, 2, 1)
        y_p, hs_new_p, st_new_p = ssd_branch(xbc, z, small, small_tp, 0, zeros_hist_s, zeros_ssm, *ss_args, bp, tp, BF16,
                                             n_out=n_p + n_s)
        y_s, hs_new_s, st_new_s = ssd_branch(xbc, z, small, small_ts, n_p, hist_s, state_ssm[l], *ss_args, bs, ts, F32)
        kc_p = prompt_block_means(kv, bp, tp)
        o_p = nsa_prompt(q, kvb, kc_p, small, tabs_p, band_t, cst_p.reshape(KV_GROUPS, 1, rows_p), bp, tp, n_out=n_p + n_s)
        o_s = nsa_sample(q[n_p:].astype(F32).reshape(bs, ts, hd), kv[n_p:].reshape(bs, ts, kvw),
                         small[n_p:].reshape(bs, ts, LANES), win_t, near_s, wtab, cst_s, cache_t, page_table, l, pps, nbl)

        c_all = lax.dynamic_update_slice(c_p, c_s.astype(BF16), (n_p, 0))
        o_all = lax.dynamic_update_slice(o_p, o_s.reshape(n_s, hd).astype(BF16), (n_p, 0))
        y_all = lax.dynamic_update_slice(y_p, y_s.astype(BF16), (n_p, 0))
        mixed = merge_branches(c_all, o_all, y_all, gmix, w_a_out[l].astype(BF16), w_nsa_out[l].astype(BF16),
                               w_ssm_out[l].astype(BF16))
        x = out_projection(mixed, w_out[l].astype(BF16), x)
        x = moe_layer(x, norm_ffn_g[l], w_router_group[l], b_router_group[l], w_router_expert[l], b_router_expert[l],
                      w_gate, w_up, w_down, l)

        kvr = kv[:, :4 * kw]
        kv_p.append(kvr[:n_p].reshape(bp, tp, 4, KV_GROUPS, HEAD_DIM))
        kv_s.append(kvr[n_p:].reshape(bs, ts, 4, KV_GROUPS, HEAD_DIM))
        wr = kv[:, 4 * kw:]
        win_p.append(wr[:n_p].reshape(bp, tp, 2, KV_GROUPS, HEAD_DIM)[:, tp - WINDOW:])
        win_all = jnp.concatenate([cache_nsa_win[l], wr[n_p:].reshape(bs, ts, 2, KV_GROUPS, HEAD_DIM)], axis=1)
        keep = min(WINDOW, wl + ts)
        win_s.append(win_all[:, wl + ts - keep:])
        ha_p.append(ha_new_p[:, HIST_A - (CONV_A_W - 1):])
        ha_s.append(ha_new_s[:, HIST_A - (CONV_A_W - 1):])
        hs_p.append(hs_new_p[:, HIST_S - (SSM_CONV_W - 1):])
        hs_s.append(hs_new_s[:, HIST_S - (SSM_CONV_W - 1):])
        st_p.append(st_new_p)
        st_s.append(st_new_s)

    y = rmsnorm(x, final_norm_g, F32)
    return (y[:n_p].reshape(bp, tp, d), y[n_p:].reshape(bs, ts, d),
            jnp.stack(kv_p, axis=1), jnp.stack(kv_s, axis=1), jnp.stack(win_p), jnp.stack(win_s),
            jnp.stack(ha_p), jnp.stack(ha_s), jnp.stack(hs_p), jnp.stack(hs_s), jnp.stack(st_p), jnp.stack(st_s))
```

```python
import functools
import math

import numpy as np
import jax
import jax.numpy as jnp
from jax import lax
from jax.experimental import pallas as pl
from jax.experimental.pallas import tpu as pltpu

HEAD_DIM = 64
KV_GROUPS = 4
GQA = 4
N_HEADS = KV_GROUPS * GQA
KV_KINDS = 6
BLK = 64
TOPK = 8
WINDOW = 512
QCHUNK = 128
REL_BUCKETS = 32
REL_MAX_DIST = 128
CONV_A_W = 31
SSM_CONV_W = 4
SSM_HEAD_DIM = 64
SSM_GROUPS = 4
SSM_STATE = 128
SSM_CHUNK = 128
MOE_GROUPS = 4
MOE_EPG = 8
MOE_TOPK = 2
EPS = 1e-6
NEG = -1e30
FORCE = 1e4

LANES = 128
SUBLANES = 8
BF16_ROWS = 16
VMEM_LIMIT = 56 * 1024 * 1024
DT_COLS = 32
GATE_COL0 = DT_COLS
SUPER = 4

HIGHEST = lax.Precision.HIGHEST
F32 = jnp.float32
BF16 = jnp.bfloat16


def _params(sem, limit=VMEM_LIMIT):
    return pltpu.CompilerParams(dimension_semantics=sem, vmem_limit_bytes=limit)


def _tile(n, target, mult):
    best = None
    for t in range(mult, min(n, target) + 1, mult):
        if n % t == 0:
            best = t
    return best if best is not None else n


def _prefilled(kern, in_specs, args, n_out, width, dtype):
    if n_out is None:
        return kern, in_specs, args, {}
    n_in = len(in_specs)

    def body(*refs):
        return kern(*refs[:n_in], *refs[n_in + 1:])

    return (body, in_specs + [pl.BlockSpec(memory_space=pl.ANY)], args + (jnp.zeros((n_out, width), dtype),), {n_in: 0})


def _dot_nt(a, b):
    return lax.dot_general(a, b, (((1,), (1,)), ((), ())), preferred_element_type=F32)


def _dot_tn(a, b):
    return lax.dot_general(a, b, (((0,), (0,)), ((), ())), preferred_element_type=F32)


def _silu(x):
    return x * jax.nn.sigmoid(x)


def _softplus(x):
    return jnp.maximum(x, 0.0) + jnp.log(1.0 + jnp.exp(-jnp.abs(x)))


def _rms_kernel(x_ref, g_ref, o_ref):
    x = x_ref[...]
    ms = jnp.mean(x * x, axis=-1, keepdims=True)
    o_ref[...] = (x * lax.rsqrt(ms + EPS) * g_ref[...]).astype(o_ref.dtype)


def rmsnorm(x, g, out_dtype):
    n, d = x.shape
    tm = _tile(n, 1024, BF16_ROWS)
    return pl.pallas_call(
        _rms_kernel,
        grid=(n // tm,),
        in_specs=[pl.BlockSpec((tm, d), lambda i: (i, 0)), pl.BlockSpec((1, d), lambda i: (0, 0))],
        out_specs=pl.BlockSpec((tm, d), lambda i: (i, 0)),
        out_shape=jax.ShapeDtypeStruct((n, d), out_dtype),
        compiler_params=_params(("parallel",)),
        name="rmsnorm",
    )(x, g.reshape(1, d))


def _mm_kernel(a_ref, w_ref, *refs, scale):
    o_refs, wb_ref = refs[:-1], refs[-1]

    @pl.when(pl.program_id(1) == 0)
    def _():
        w = w_ref[0]
        wb_ref[...] = (w if scale == 1.0 else w * scale).astype(BF16)

    acc = _dot_nt(a_ref[...], wb_ref[...])
    for o in o_refs:
        o[...] = acc.astype(o.dtype)


def matmul(a, wt, layer, row0, c, out_dtypes, scale=1.0):
    n, k = a.shape
    assert row0 % SUBLANES == 0
    tm = _tile(n, 768, BF16_ROWS)
    tn = _tile(c, 1024, SUBLANES)
    outs = pl.pallas_call(
        functools.partial(_mm_kernel, scale=scale),
        grid=(c // tn, n // tm),
        in_specs=[pl.BlockSpec((tm, k), lambda j, i: (i, 0)),
                  pl.BlockSpec((pl.Element(1), pl.Element(tn), pl.Element(k)),
                               lambda j, i: (layer, pl.multiple_of(row0 + j * tn, SUBLANES), 0))],
        out_specs=[pl.BlockSpec((tm, tn), lambda j, i: (i, j)) for _ in out_dtypes],
        out_shape=[jax.ShapeDtypeStruct((n, c), dt) for dt in out_dtypes],
        scratch_shapes=[pltpu.VMEM((tn, k), BF16)],
        compiler_params=_params(("parallel", "arbitrary")),
        name="proj",
    )(a, wt)
    return outs


HIST_A = 32
HIST_S = 8


def _round_bf16(x):
    return x.astype(BF16).astype(F32)


def _push_tail(tail_ref, new, n_new, keep):
    if n_new >= keep:
        tail_ref[...] = new[n_new - keep:n_new]
    else:
        tail_ref[...] = jnp.concatenate([tail_ref[n_new:keep], new], axis=0)


def _conva_kernel(u_ref, hist_ref, w_ref, b_ref, lg_ref, lb_ref, c_ref, hout_ref, ext_ref, conv_ref, tail_ref, *, tt, ch):
    t = pl.program_id(1)

    @pl.when(t == 0)
    def _():
        tail_ref[...] = hist_ref[...]
        ext_ref[0:HIST_A, :] = _round_bf16(hist_ref[...])

    uin = u_ref[...]
    u = uin[:, :ch] * jax.nn.sigmoid(uin[:, ch:])
    ext_ref[HIST_A:HIST_A + tt, :] = _round_bf16(u)
    _push_tail(tail_ref, u, tt, HIST_A)
    hout_ref[...] = tail_ref[...]
    off = HIST_A - (CONV_A_W - 1)
    rb = min(tt, 128)

    def lane_block(cb, carry):
        lo = pl.multiple_of(cb * LANES, LANES)
        for r in range(tt // rb):
            acc = jnp.broadcast_to(b_ref[:, pl.ds(lo, LANES)], (rb, LANES))
            for k in range(CONV_A_W):
                acc = acc + w_ref[k:k + 1, pl.ds(lo, LANES)] * ext_ref[r * rb + off + k:r * rb + off + k + rb, pl.ds(lo, LANES)]
            conv_ref[r * rb:(r + 1) * rb, pl.ds(lo, LANES)] = acc
        return carry

    lax.fori_loop(0, ch // LANES, lane_block, 0)
    cv = conv_ref[...]
    mu = jnp.mean(cv, axis=-1, keepdims=True)
    xc = cv - mu
    var = jnp.mean(xc * xc, axis=-1, keepdims=True)
    y = xc * lax.rsqrt(var + EPS) * lg_ref[...] + lb_ref[...]
    c_ref[...] = _silu(y).astype(c_ref.dtype)
    ext_ref[0:HIST_A, :] = ext_ref[tt:tt + HIST_A, :]


def conv_a_branch(u_in, row_off, hist, w, b, lg, lb, batch, t_len, out_dtype, n_out=None):
    ch = w.shape[1]
    tt = _tile(t_len, 256, SUBLANES)
    nt = t_len // tt
    assert row_off % tt == 0
    r0 = row_off // tt
    kern = functools.partial(_conva_kernel, tt=tt, ch=ch)
    vec = lambda: pl.BlockSpec((1, ch), lambda bi, ti: (0, 0))
    in_specs = [pl.BlockSpec((tt, 2 * ch), lambda bi, ti: (r0 + bi * nt + ti, 0)),
                pl.BlockSpec((None, HIST_A, ch), lambda bi, ti: (bi, 0, 0)),
                pl.BlockSpec((CONV_A_W, ch), lambda bi, ti: (0, 0)), vec(), vec(), vec()]
    args = (u_in, hist, _round_bf16(w), b.reshape(1, ch), lg.reshape(1, ch), lb.reshape(1, ch))
    kern, in_specs, args, alias = _prefilled(kern, in_specs, args, n_out, ch, out_dtype)
    return pl.pallas_call(
        kern,
        grid=(batch, nt),
        in_specs=in_specs,
        out_specs=[pl.BlockSpec((tt, ch), lambda bi, ti: (bi * nt + ti, 0)),
                   pl.BlockSpec((None, HIST_A, ch), lambda bi, ti: (bi, 0, 0))],
        out_shape=[jax.ShapeDtypeStruct((n_out or batch * t_len, ch), out_dtype),
                   jax.ShapeDtypeStruct((batch, HIST_A, ch), F32)],
        scratch_shapes=[pltpu.VMEM((tt + HIST_A, ch), F32), pltpu.VMEM((tt, ch), F32), pltpu.VMEM((HIST_A, ch), F32)],
        input_output_aliases=alias,
        compiler_params=_params(("parallel", "arbitrary")),
        name="conv_a",
    )(*args)


def _ssd_kernel(xbc_ref, z_ref, dt_ref, dtt_ref, hist_ref, h0_ref, cw_ref, cb_ref, dtb_ref, dtbt_ref,
                a_ref, at_ref, dvec_ref, gn_ref, y_ref, hout_ref, sout_ref, ext_ref, h_ref, yacc_ref, tail_ref,
                *, q, nc, inner, heads):
    c = pl.program_id(1)

    @pl.when(c == 0)
    def _():
        tail_ref[...] = hist_ref[...]
        ext_ref[0:HIST_S, :] = _round_bf16(hist_ref[...])
        h_ref[...] = h0_ref[...]

    xbc = xbc_ref[...]
    ext_ref[HIST_S:HIST_S + q, :] = _round_bf16(xbc)
    _push_tail(tail_ref, xbc, q, HIST_S)
    hout_ref[...] = tail_ref[...]
    off = HIST_S - (SSM_CONV_W - 1)
    acc = jnp.broadcast_to(cb_ref[...], (q, cb_ref.shape[1]))
    for k in range(SSM_CONV_W):
        acc = acc + cw_ref[k:k + 1, :] * ext_ref[off + k:off + k + q, :]
    xc = _silu(acc)
    ext_ref[0:HIST_S, :] = ext_ref[q:q + HIST_S, :]

    gw = SSM_GROUPS * SSM_STATE
    xs = xc[:, :inner]
    bm = xc[:, inner:inner + gw]
    cm = xc[:, inner + gw:inner + 2 * gw]

    dt = _softplus(dt_ref[...] + dtb_ref[...])
    dtt = _softplus(dtt_ref[...] + dtbt_ref[...])
    da = dt * a_ref[...]
    dat = dtt * at_ref[...]
    ri = lax.broadcasted_iota(jnp.int32, (q, q), 0)
    ci = lax.broadcasted_iota(jnp.int32, (q, q), 1)
    causal = ri >= ci
    tri = causal.astype(F32)
    trit = (ri <= ci).astype(F32)
    acum = jnp.dot(tri, da, precision=HIGHEST, preferred_element_type=F32)
    acumt = jnp.dot(dat, trit, precision=HIGHEST, preferred_element_type=F32)
    alast = acum[q - 1:q, :]
    hpg = heads // SSM_GROUPS
    for g in range(SSM_GROUPS):
        bg = bm[:, g * SSM_STATE:(g + 1) * SSM_STATE].astype(BF16)
        cg = cm[:, g * SSM_STATE:(g + 1) * SSM_STATE].astype(BF16)
        cbm = _dot_nt(cg, bg)
        for e in range(hpg):
            hh = g * hpg + e
            a_col = acum[:, hh:hh + 1]
            a_row = acumt[hh:hh + 1, :]
            decay = jnp.exp(jnp.where(causal, a_col - a_row, NEG))
            wm = cbm * decay * dtt[hh:hh + 1, :]
            xh = xs[:, hh * SSM_HEAD_DIM:(hh + 1) * SSM_HEAD_DIM]
            hprev = h_ref[hh]
            y_h = (jnp.dot(wm.astype(BF16), xh.astype(BF16), preferred_element_type=F32)
                   + _dot_nt(cg, hprev.astype(BF16)) * jnp.exp(a_col))
            yacc_ref[:, hh * SSM_HEAD_DIM:(hh + 1) * SSM_HEAD_DIM] = y_h
            al = alast[:, hh:hh + 1]
            w_end = jnp.exp(al - a_col) * dt[:, hh:hh + 1]
            h_ref[hh] = jnp.exp(al) * hprev + _dot_tn((xh * w_end).astype(BF16), bg)

    y = (yacc_ref[...] + xs * dvec_ref[...]) * _silu(z_ref[...])
    gsz = inner // SSM_GROUPS
    for g in range(SSM_GROUPS):
        v = y[:, g * gsz:(g + 1) * gsz]
        ms = jnp.mean(v * v, axis=-1, keepdims=True)
        y_ref[:, g * gsz:(g + 1) * gsz] = (v * lax.rsqrt(ms + EPS) * gn_ref[:, g * gsz:(g + 1) * gsz]).astype(y_ref.dtype)

    @pl.when(c == nc - 1)
    def _():
        sout_ref[...] = h_ref[...]


def ssd_branch(xbc, z, small, small_t, row_off, hist, h0, cw, cb, dt_bias, a_log, d_skip, norm_g, batch, t_len, out_dtype,
               n_out=None):
    conv_dim = xbc.shape[1]
    inner = z.shape[1]
    heads = dt_bias.shape[0]
    assert heads == DT_COLS
    q = math.gcd(t_len, SSM_CHUNK)
    nc = t_len // q
    assert row_off % q == 0
    r0 = row_off // q
    pad = LANES - heads
    dtb = jnp.pad(dt_bias, (0, pad)).reshape(1, LANES)
    a_neg = jnp.pad(-jnp.exp(a_log), (0, pad)).reshape(1, LANES)
    dvec = jnp.repeat(d_skip, SSM_HEAD_DIM).reshape(1, inner)
    kern = functools.partial(_ssd_kernel, q=q, nc=nc, inner=inner, heads=heads)
    row = lambda w: pl.BlockSpec((1, w), lambda bi, ci: (0, 0))
    col = lambda: pl.BlockSpec((LANES, 1), lambda bi, ci: (0, 0))
    tok = lambda w: pl.BlockSpec((q, w), lambda bi, ci: (r0 + bi * nc + ci, 0))
    in_specs = [tok(conv_dim), tok(inner), tok(LANES),
                pl.BlockSpec((None, LANES, q), lambda bi, ci: (bi, 0, ci)),
                pl.BlockSpec((None, HIST_S, conv_dim), lambda bi, ci: (bi, 0, 0)),
                pl.BlockSpec((None, heads, SSM_HEAD_DIM, SSM_STATE), lambda bi, ci: (bi, 0, 0, 0)),
                pl.BlockSpec((SSM_CONV_W, conv_dim), lambda bi, ci: (0, 0)), row(conv_dim),
                row(LANES), col(), row(LANES), col(), row(inner), row(inner)]
    args = (xbc, z, small, small_t, hist, h0, _round_bf16(cw), cb.reshape(1, conv_dim), dtb, dtb.reshape(LANES, 1),
            a_neg, a_neg.reshape(LANES, 1), dvec, norm_g.reshape(1, inner))
    kern, in_specs, args, alias = _prefilled(kern, in_specs, args, n_out, inner, out_dtype)
    return pl.pallas_call(
        kern,
        grid=(batch, nc),
        in_specs=in_specs,
        input_output_aliases=alias,
        out_specs=[pl.BlockSpec((q, inner), lambda bi, ci: (bi * nc + ci, 0)),
                   pl.BlockSpec((None, HIST_S, conv_dim), lambda bi, ci: (bi, 0, 0)),
                   pl.BlockSpec((None, heads, SSM_HEAD_DIM, SSM_STATE), lambda bi, ci: (bi, 0, 0, 0))],
        out_shape=[jax.ShapeDtypeStruct((n_out or batch * t_len, inner), out_dtype),
                   jax.ShapeDtypeStruct((batch, HIST_S, conv_dim), F32),
                   jax.ShapeDtypeStruct((batch, heads, SSM_HEAD_DIM, SSM_STATE), F32)],
        scratch_shapes=[pltpu.VMEM((q + HIST_S, conv_dim), F32),
                        pltpu.VMEM((heads, SSM_HEAD_DIM, SSM_STATE), F32),
                        pltpu.VMEM((q, inner), F32), pltpu.VMEM((HIST_S, conv_dim), F32)],
        compiler_params=_params(("parallel", "arbitrary")),
        name="ssd",
    )(*args)


def _bucket_np(d):
    n = np.maximum(d, 0)
    exact = REL_BUCKETS // 2
    nf = np.maximum(n, 1).astype(np.float32)
    large = exact + (np.log(nf / np.float32(exact)) / np.float32(math.log(REL_MAX_DIST / exact))
                     * np.float32(REL_BUCKETS - exact)).astype(np.int32)
    return np.where(n < exact, n, np.minimum(large, REL_BUCKETS - 1))


def _near_distances():
    i = np.arange(QCHUNK)[:, None]
    j = np.arange(LANES)[None, :]
    diag = i - j
    prev = QCHUNK + i - j
    band = np.where(j < 4, i + (BLK + 1) - BLK * j, 0)
    return np.stack([diag, prev, band])


def _expand_kernel(tab_ref, oh_ref, o_ref):
    o_ref[...] = jnp.dot(tab_ref[...], oh_ref[...], precision=HIGHEST, preferred_element_type=F32)


def expand_bias(rel_bias, dist):
    flat = dist.reshape(-1)
    m = flat.shape[0]
    assert m % LANES == 0
    onehot = (np.arange(REL_BUCKETS)[:, None] == _bucket_np(flat)[None, :]).astype(np.float32)
    tm = _tile(m, 8192, LANES)
    heads = rel_bias.shape[1]
    out = pl.pallas_call(
        _expand_kernel,
        grid=(m // tm,),
        in_specs=[pl.BlockSpec((heads, REL_BUCKETS), lambda i: (0, 0)), pl.BlockSpec((REL_BUCKETS, tm), lambda i: (0, i))],
        out_specs=pl.BlockSpec((heads, tm), lambda i: (0, i)),
        out_shape=jax.ShapeDtypeStruct((heads, m), F32),
        compiler_params=_params(("parallel",)),
        name="bias_expand",
    )(rel_bias.T, jnp.asarray(onehot))
    return out.reshape((heads,) + dist.shape)


def _select_blocks(imp, lane, axis=-1):
    sel = jnp.zeros(imp.shape, F32)
    big = jnp.int32(1 << 30)
    v = imp
    for _ in range(TOPK):
        mx = jnp.max(v, axis=axis, keepdims=True)
        idx = jnp.min(jnp.where(v == mx, lane, big), axis=axis, keepdims=True)
        hit = lane == idx
        sel = jnp.where(hit, 1.0, sel)
        v = jnp.where(hit, -3e38, v)
    return sel


def _rep4(x):
    return jnp.concatenate([x, x, x, x], axis=0)


def _qgroup(q_ref, g):
    return jnp.concatenate([q_ref[:, (g * GQA + r) * HEAD_DIM:(g * GQA + r + 1) * HEAD_DIM] for r in range(GQA)], axis=0)


def _gate_col(sg, br, g):
    c0 = GATE_COL0 + br * N_HEADS + g * GQA
    return jnp.concatenate([sg[:, c0 + r:c0 + r + 1] for r in range(GQA)], axis=0)


def _rows_means_kernel(x_ref, o_ref):
    nb, w = o_ref.shape
    o_ref[...] = jnp.sum(x_ref[...].reshape(nb, BLK, w), axis=1) * (1.0 / BLK)


def prompt_block_means(kv, batch, t_len):
    w = 2 * KV_GROUPS * HEAD_DIM
    nb = t_len // BLK
    return pl.pallas_call(
        _rows_means_kernel,
        grid=(batch,),
        in_specs=[pl.BlockSpec((t_len, w), lambda b: (b, 0))],
        out_specs=pl.BlockSpec((None, nb, w), lambda b: (b, 0, 0)),
        out_shape=jax.ShapeDtypeStruct((batch, nb, w), F32),
        compiler_params=_params(("parallel",)),
        name="cmp_means_prompt",
    )(kv)


def _page_specs(pps, layer):
    def spec(k):
        return pl.BlockSpec((None, None, 4, KV_GROUPS, HEAD_DIM, 2 * BLK),
                            lambda b, j, pt: (pt[b, j * pps + k], layer, 0, 0, 0, 0))
    return [spec(k) for k in range(pps)]


def _nsa_prompt_kernel(q_ref, ksel_ref, vsel_ref, kwin_ref, vwin_ref, kc_ref, gn_ref, tab_ref, band_ref, cst_ref, e_ref, o_ref,
                       s_ref, sel_ref, oc_ref, mrun_ref, acc_ref, *, nb):
    c = pl.program_id(1)
    qc = QCHUNK
    rows = GQA * qc
    kw = KV_GROUPS * HEAD_DIM
    span = SUPER * qc
    blk = lax.broadcasted_iota(jnp.int32, (nb, rows), 0)
    qpos = c * qc + (lax.broadcasted_iota(jnp.int32, (nb, rows), 1) & (qc - 1))
    okc = (qpos - BLK * blk - (BLK - 1)) >= 0
    blk1 = lax.broadcasted_iota(jnp.int32, (nb, qc), 0)
    qi1 = lax.broadcasted_iota(jnp.int32, (nb, qc), 1)
    okc1 = (c * qc + qi1 - BLK * blk1 - (BLK - 1)) >= 0
    cur = 2 * c + (qi1 >= BLK).astype(jnp.int32)
    for g in range(KV_GROUPS):
        qg = _qgroup(q_ref, g)
        kcg = kc_ref[:, g * HEAD_DIM:(g + 1) * HEAD_DIM].astype(BF16)
        vcg = kc_ref[:, kw + g * HEAD_DIM:kw + (g + 1) * HEAD_DIM].astype(BF16)
        bias_c = jnp.broadcast_to(cst_ref[g], (nb, rows))
        for u in range(4):
            bias_c = jnp.where(blk == 2 * c - 2 + u, band_ref[g, u:u + 1, :], bias_c)
        s = jnp.where(okc, _dot_nt(kcg, qg) + bias_c, NEG)
        e = jnp.exp(s - jnp.max(s, axis=0, keepdims=True))
        p = jnp.where(okc, e / jnp.sum(e, axis=0, keepdims=True), 0.0)
        oc_ref[g] = _dot_tn(p.astype(BF16), vcg)
        psum = p[:, 0:qc] + p[:, qc:2 * qc] + p[:, 2 * qc:3 * qc] + p[:, 3 * qc:4 * qc]
        imp = jnp.where(blk1 == cur, FORCE, jnp.where(okc1, psum, -1.0))
        selm = jnp.where(blk1 <= cur, _select_blocks(imp, blk1, axis=0), 0.0)
        sel_ref[g] = (_dot_tn(selm.astype(BF16), e_ref[...]) - 1.0) * (-NEG)

    sg = jax.nn.sigmoid(gn_ref[...])
    n_super = c // SUPER + 1
    w_tiles = WINDOW // qc + 1
    st0 = jnp.maximum(c - (w_tiles - 1), 0)
    for g in range(KV_GROUPS):
        qg = _qgroup(q_ref, g)
        mrun_ref[...] = jnp.full((rows, LANES), NEG, F32)

        def scores(st, carry):
            k = ksel_ref[pl.ds(pl.multiple_of(st * span, span), span), g * HEAD_DIM:(g + 1) * HEAD_DIM]
            s = _dot_nt(qg, k)
            mrun = mrun_ref[...]
            for j in range(SUPER):
                kt = st * SUPER + j
                idx = jnp.where(kt == c, 0, jnp.where(kt == c - 1, 1, 2))
                off = pl.multiple_of(kt * qc, qc)
                sj = s[:, j * qc:(j + 1) * qc] + tab_ref[g, idx] + _rep4(sel_ref[g, :, pl.ds(off, qc)])
                s_ref[:, pl.ds(off, qc)] = sj
                mrun = jnp.maximum(mrun, sj)
            mrun_ref[...] = mrun
            return carry

        lax.fori_loop(0, n_super, scores, 0)
        m = jnp.max(mrun_ref[...], axis=-1, keepdims=True)
        mrun_ref[...] = jnp.zeros((rows, LANES), F32)

        def exps(st, carry):
            off = pl.multiple_of(st * span, span)
            e = jnp.exp(s_ref[:, pl.ds(off, span)] - m)
            s_ref[:, pl.ds(off, span)] = e
            lrun = mrun_ref[...]
            for j in range(SUPER):
                lrun = lrun + e[:, j * qc:(j + 1) * qc]
            mrun_ref[...] = lrun
            return carry

        lax.fori_loop(0, n_super, exps, 0)
        denom = jnp.sum(mrun_ref[...], axis=-1, keepdims=True)
        acc_ref[...] = jnp.zeros((rows, HEAD_DIM), F32)

        def values(st, carry):
            off = pl.multiple_of(st * span, span)
            p = (s_ref[:, pl.ds(off, span)] / denom).astype(BF16)
            acc_ref[...] += jnp.dot(p, vsel_ref[pl.ds(off, span), g * HEAD_DIM:(g + 1) * HEAD_DIM], preferred_element_type=F32)
            return carry

        lax.fori_loop(0, n_super, values, 0)
        o_s = acc_ref[...]
        r0 = pl.multiple_of(st0 * qc, qc)
        s = _dot_nt(qg, kwin_ref[pl.ds(r0, w_tiles * qc), g * HEAD_DIM:(g + 1) * HEAD_DIM])
        parts = []
        for w in range(w_tiles):
            kt = st0 + w
            idx = jnp.where(kt > c, 4, jnp.where(kt == c, 0, jnp.where(kt == c - 1, 1,
                                                                         jnp.where(kt == c - (w_tiles - 1), 3, 2))))
            parts.append(s[:, w * qc:(w + 1) * qc] + tab_ref[g, idx])
        sw = jnp.concatenate(parts, axis=1)
        ew = jnp.exp(sw - jnp.max(sw, axis=-1, keepdims=True))
        pw = (ew / jnp.sum(ew, axis=-1, keepdims=True)).astype(BF16)
        o_w = jnp.dot(pw, vwin_ref[pl.ds(r0, w_tiles * qc), g * HEAD_DIM:(g + 1) * HEAD_DIM],
                      preferred_element_type=F32)
        tot = _gate_col(sg, 0, g) * oc_ref[g] + _gate_col(sg, 1, g) * o_s + _gate_col(sg, 2, g) * o_w
        for r in range(GQA):
            o_ref[:, (g * GQA + r) * HEAD_DIM:(g * GQA + r + 1) * HEAD_DIM] = tot[r * qc:(r + 1) * qc].astype(o_ref.dtype)


def nsa_prompt(q, kvb, kc, small, tabs, band, cst, batch, t_len, n_out=None):
    assert t_len % (SUPER * QCHUNK) == 0 and t_len >= WINDOW + QCHUNK
    nch = t_len // QCHUNK
    nb = t_len // BLK
    assert nb % SUBLANES == 0
    rows = GQA * QCHUNK
    hd = N_HEADS * HEAD_DIM
    emat = (np.arange(nb)[:, None] == (np.arange(t_len)[None, :] // BLK)).astype(np.float32)
    kern = functools.partial(_nsa_prompt_kernel, nb=nb)
    full = lambda shp: pl.BlockSpec(shp, lambda b, c: (0,) * len(shp))
    kind = lambda k: pl.BlockSpec((t_len, KV_GROUPS * HEAD_DIM), lambda b, c: (b, k))
    in_specs = [pl.BlockSpec((QCHUNK, hd), lambda b, c: (b * nch + c, 0)),
                kind(2), kind(3), kind(4), kind(5),
                pl.BlockSpec((None, nb, kc.shape[2]), lambda b, c: (b, 0, 0)),
                pl.BlockSpec((QCHUNK, LANES), lambda b, c: (b * nch + c, 0)),
                full((KV_GROUPS, 5, rows, LANES)), full((KV_GROUPS, SUBLANES, rows)), full((KV_GROUPS, 1, rows)),
                full((nb, t_len))]
    args = (q, kvb, kvb, kvb, kvb, kc, small, tabs, band, cst, jnp.asarray(emat, BF16))
    kern, in_specs, args, alias = _prefilled(kern, in_specs, args, n_out, hd, BF16)
    return pl.pallas_call(
        kern,
        grid=(batch, nch),
        in_specs=in_specs,
        input_output_aliases=alias,
        out_specs=pl.BlockSpec((QCHUNK, hd), lambda b, c: (b * nch + c, 0)),
        out_shape=jax.ShapeDtypeStruct((n_out or batch * t_len, hd), BF16),
        scratch_shapes=[pltpu.VMEM((rows, t_len), F32),
                        pltpu.VMEM((KV_GROUPS, QCHUNK, t_len), F32),
                        pltpu.VMEM((KV_GROUPS, rows, HEAD_DIM), F32),
                        pltpu.VMEM((rows, LANES), F32), pltpu.VMEM((rows, HEAD_DIM), F32)],
        compiler_params=_params(("parallel", "arbitrary")),
        name="nsa_prompt",
    )(*args)


def _nsa_sample_kernel(pt_ref, q_ref, kvn_ref, gn_ref, avg_ref, win_ref, near_ref, wtab_ref, cst_ref, e_ref,
                       *refs, pps, n_steps, past, ts, nbl):
    pages = refs[:pps]
    o_ref = refs[pps]
    s_ref, kc_ref, vbuf_ref, kn_ref = refs[pps + 1:]
    j = pl.program_id(1)
    rows = GQA * ts
    kw = KV_GROUPS * HEAD_DIM
    nb_past = past // BLK
    pg = 2 * BLK
    lane = lax.broadcasted_iota(jnp.int32, (rows, LANES), 1)
    ti = lax.broadcasted_iota(jnp.int32, (rows, LANES), 0) & (ts - 1)

    def qgroup(g):
        return _qgroup(q_ref, g).astype(BF16)

    @pl.when(j == 0)
    def _():
        kn_ref[...] = jnp.zeros(kn_ref.shape, F32)
        kn_ref[0:ts, :] = kvn_ref[:, 2 * kw:6 * kw]
        kc_ref[...] = jnp.zeros(kc_ref.shape, F32)

    is_last = j == n_steps - 1
    base = j * (pps * pg)
    qgs = [qgroup(g) for g in range(KV_GROUPS)]
    avg = avg_ref[...]
    blk0 = pl.multiple_of(j * (2 * pps), 2 * pps)
    for kind in range(2):
        for g in range(KV_GROUPS):
            cat = jnp.concatenate([pgr[kind, g] for pgr in pages], axis=1)
            hi = cat.astype(BF16)
            lo = (cat - hi.astype(F32)).astype(BF16)
            kc_ref[kind, g, pl.ds(blk0, 2 * pps), :] = _dot_nt(avg, hi) + _dot_nt(avg, lo)
    for k in range(pps):
        off = pl.multiple_of(base + k * pg, pg)
        for g in range(KV_GROUPS):
            vbuf_ref[g, :, pl.ds(off, pg)] = pages[k][3, g].astype(BF16)
            sk = jnp.dot(qgs[g], pages[k][2, g].astype(BF16), preferred_element_type=F32)
            bias = jnp.where(is_last, near_ref[g, 1], cst_ref[g]) if k == pps - 1 else cst_ref[g]
            s_ref[g, :, pl.ds(off, pg)] = sk + bias

    @pl.when(is_last)
    def _():
        sg = jax.nn.sigmoid(gn_ref[...])
        wl = win_ref.shape[-1]
        lane_w = lax.broadcasted_iota(jnp.int32, (rows, wl + LANES), 1)
        t_w = lax.broadcasted_iota(jnp.int32, (rows, wl + LANES), 0) & (ts - 1)
        okw = jnp.where(lane_w < wl, lane_w - t_w, t_w - (lane_w - wl)) >= 0
        lane_b = lax.broadcasted_iota(jnp.int32, (rows, nbl), 1)
        t_b = lax.broadcasted_iota(jnp.int32, (rows, nbl), 0) & (ts - 1)
        okc = (past + t_b - BLK * lane_b - (BLK - 1)) >= 0
        lane1 = lax.broadcasted_iota(jnp.int32, (ts, nbl), 1)
        okc1 = (past + lax.broadcasted_iota(jnp.int32, (ts, nbl), 0) - BLK * lane1 - (BLK - 1)) >= 0
        o_cs, madds = [], []
        for g in range(KV_GROUPS):
            qg = qgs[g]
            cst = cst_ref[g]
            band = near_ref[g, 2]
            kcg = kc_ref[0, g].astype(BF16)
            vcg = kc_ref[1, g].astype(BF16)
            bias_c = jnp.where(lane_b == nb_past - 2, band[:, 0:1], jnp.where(lane_b == nb_past - 1, band[:, 1:2], cst))
            s = jnp.where(okc, _dot_nt(qg, kcg) + bias_c, NEG)
            e = jnp.exp(s - jnp.max(s, axis=-1, keepdims=True))
            p = jnp.where(okc, e / jnp.sum(e, axis=-1, keepdims=True), 0.0)
            o_cs.append(jnp.dot(p.astype(BF16), vcg, preferred_element_type=F32))
            psum = p[0:ts] + p[ts:2 * ts] + p[2 * ts:3 * ts] + p[3 * ts:4 * ts]
            imp = jnp.where(lane1 == nb_past, FORCE, jnp.where(okc1, psum, -1.0))
            imp = jnp.where(lane1 <= nb_past, imp, -3e38)
            selm = _select_blocks(imp, lane1)
            selm16 = jnp.concatenate([selm, jnp.zeros_like(selm)], axis=0).astype(BF16)
            madds.append((jnp.dot(selm16, e_ref[...], preferred_element_type=F32)[0:ts] - 1.0) * (-NEG))
        for g in range(KV_GROUPS):
            qg = qgs[g]
            o_c, madd = o_cs[g], madds[g]
            kn = kn_ref[:, g * HEAD_DIM:(g + 1) * HEAD_DIM].astype(BF16)
            vn = kn_ref[:, kw + g * HEAD_DIM:kw + (g + 1) * HEAD_DIM].astype(BF16)
            s_new = jnp.where(lane <= ti, _dot_nt(qg, kn) + near_ref[g, 0], NEG)
            s = jnp.concatenate([s_ref[g] + _rep4(madd), s_new], axis=1)
            e = jnp.exp(s - jnp.max(s, axis=-1, keepdims=True))
            p = e / jnp.sum(e, axis=-1, keepdims=True)
            o_s = (_dot_nt(p[:, :past].astype(BF16), vbuf_ref[g])
                   + jnp.dot(p[:, past:].astype(BF16), vn, preferred_element_type=F32))
            kwin = win_ref[0, g].astype(BF16)
            vwin = win_ref[1, g].astype(BF16)
            knw = kn_ref[:, 2 * kw + g * HEAD_DIM:2 * kw + (g + 1) * HEAD_DIM].astype(BF16)
            vnw = kn_ref[:, 3 * kw + g * HEAD_DIM:3 * kw + (g + 1) * HEAD_DIM].astype(BF16)
            s = jnp.concatenate([jnp.dot(qg, kwin, preferred_element_type=F32), _dot_nt(qg, knw)], axis=1) + wtab_ref[g]
            s = jnp.where(okw, s, NEG)
            e = jnp.exp(s - jnp.max(s, axis=-1, keepdims=True))
            p = e / jnp.sum(e, axis=-1, keepdims=True)
            o_w = (_dot_nt(p[:, :wl].astype(BF16), vwin)
                   + jnp.dot(p[:, wl:].astype(BF16), vnw, preferred_element_type=F32))
            tot = _gate_col(sg, 0, g) * o_c + _gate_col(sg, 1, g) * o_s + _gate_col(sg, 2, g) * o_w
            for r in range(GQA):
                o_ref[:, (g * GQA + r) * HEAD_DIM:(g * GQA + r + 1) * HEAD_DIM] = tot[r * ts:(r + 1) * ts]


def nsa_sample(q3, kvn3, small3, win_t, near_s, wtab, cst_s, cache_t, page_table, layer, pps, nbl):
    batch, ts, hd = q3.shape
    n_pages = page_table.shape[1]
    n_steps = n_pages // pps
    past = n_pages * 2 * BLK
    wl = win_t.shape[-1]
    avg = np.where(np.arange(2 * pps)[:, None] == np.arange(pps * 2 * BLK)[None, :] // BLK, 1.0 / BLK, 0.0)
    rows = GQA * ts
    kw = KV_GROUPS * HEAD_DIM
    assert ts == SUBLANES and past % QCHUNK == 0 and wl % LANES == 0
    emat = (np.arange(nbl)[:, None] == (np.arange(past)[None, :] // BLK)).astype(np.float32)
    kern = functools.partial(_nsa_sample_kernel, pps=pps, n_steps=n_steps, past=past, ts=ts, nbl=nbl)
    c3 = lambda shp: pl.BlockSpec(shp, lambda b, j, pt: (0,) * len(shp))
    return pl.pallas_call(
        kern,
        grid_spec=pltpu.PrefetchScalarGridSpec(
            num_scalar_prefetch=1, grid=(batch, n_steps),
            in_specs=[pl.BlockSpec((None, ts, hd), lambda b, j, pt: (b, 0, 0)),
                      pl.BlockSpec((None, ts, kvn3.shape[2]), lambda b, j, pt: (b, 0, 0)),
                      pl.BlockSpec((None, ts, LANES), lambda b, j, pt: (b, 0, 0)),
                      c3((2 * pps, pps * 2 * BLK)),
                      pl.BlockSpec((None, None, 2, KV_GROUPS, HEAD_DIM, wl), lambda b, j, pt: (layer, b, 0, 0, 0, 0)),
                      c3((KV_GROUPS, 3, rows, LANES)), c3((KV_GROUPS, rows, wl + LANES)), c3((KV_GROUPS, rows, 1)),
                      c3((nbl, past))] + _page_specs(pps, layer),
            out_specs=pl.BlockSpec((None, ts, hd), lambda b, j, pt: (b, 0, 0)),
            scratch_shapes=[pltpu.VMEM((KV_GROUPS, rows, past), F32),
                            pltpu.VMEM((2, KV_GROUPS, nbl, HEAD_DIM), F32),
                            pltpu.VMEM((KV_GROUPS, HEAD_DIM, past), BF16),
                            pltpu.VMEM((LANES, 4 * kw), F32)]),
        out_shape=jax.ShapeDtypeStruct((batch, ts, hd), F32),
        compiler_params=_params(("parallel", "arbitrary")),
        name="nsa_sample",
    )(page_table, q3, kvn3, small3, jnp.asarray(avg, BF16), win_t, near_s, wtab, cst_s, jnp.asarray(emat, BF16),
      *([cache_t] * pps))


def _merge_kernel(c_ref, o_ref, y_ref, g0_ref, g1_ref, g2_ref, wa_ref, wn_ref, ws_ref, out_ref):
    oa = jnp.dot(c_ref[...], wa_ref[...], preferred_element_type=F32)
    ob = jnp.dot(o_ref[...], wn_ref[...], preferred_element_type=F32)
    oc = jnp.dot(y_ref[...], ws_ref[...], preferred_element_type=F32)
    mixed = (jax.nn.sigmoid(g0_ref[...]) * oa + jax.nn.sigmoid(g1_ref[...]) * ob + jax.nn.sigmoid(g2_ref[...]) * oc)
    out_ref[...] = mixed.astype(out_ref.dtype)


def merge_branches(c, o, y, gmix, wa, wn, ws):
    n = c.shape[0]
    d = wa.shape[1]
    tm = _tile(n, 768, BF16_ROWS)
    tn = _tile(d, 512, LANES)
    nj = d // tn
    act = lambda a: pl.BlockSpec((tm, a.shape[1]), lambda i, j: (i, 0))
    wsp = lambda w: pl.BlockSpec((w.shape[0], tn), lambda i, j: (0, j))
    gsp = lambda br: pl.BlockSpec((tm, tn), lambda i, j: (i, br * nj + j))
    return pl.pallas_call(
        _merge_kernel,
        grid=(n // tm, nj),
        in_specs=[act(c), act(o), act(y), gsp(0), gsp(1), gsp(2), wsp(wa), wsp(wn), wsp(ws)],
        out_specs=pl.BlockSpec((tm, tn), lambda i, j: (i, j)),
        out_shape=jax.ShapeDtypeStruct((n, d), BF16),
        compiler_params=_params(("parallel", "parallel")),
        name="merge",
    )(c, o, y, gmix, gmix, gmix, wa, wn, ws)


def _outproj_kernel(a_ref, w_ref, x_ref, o_ref):
    o_ref[...] = x_ref[...] + jnp.dot(a_ref[...], w_ref[...], preferred_element_type=F32)


def out_projection(a, w, x):
    n, k = a.shape
    d = w.shape[1]
    tm = _tile(n, 768, BF16_ROWS)
    tn = _tile(d, 1024, LANES)
    return pl.pallas_call(
        _outproj_kernel,
        grid=(n // tm, d // tn),
        in_specs=[pl.BlockSpec((tm, k), lambda i, j: (i, 0)), pl.BlockSpec((k, tn), lambda i, j: (0, j)),
                  pl.BlockSpec((tm, tn), lambda i, j: (i, j))],
        out_specs=pl.BlockSpec((tm, tn), lambda i, j: (i, j)),
        out_shape=jax.ShapeDtypeStruct((n, d), F32),
        compiler_params=_params(("parallel", "parallel")),
        name="out_proj",
    )(a, w, x)


def _router_kernel(x_ref, g_ref, w_ref, b_ref, h_ref, r_ref):
    x = x_ref[...]
    ms = jnp.mean(x * x, axis=-1, keepdims=True)
    h = (x * lax.rsqrt(ms + EPS) * g_ref[...]).astype(BF16)
    h_ref[...] = h
    logits = jnp.dot(h, w_ref[...], preferred_element_type=F32) + b_ref[...]
    lane = lax.broadcasted_iota(jnp.int32, logits.shape, 1)
    big = jnp.int32(1 << 30)
    gl = jnp.where(lane < MOE_GROUPS, logits, NEG)
    gmax = jnp.max(gl, axis=-1, keepdims=True)
    gidx = jnp.min(jnp.where(gl == gmax, lane, big), axis=-1, keepdims=True)
    g_w = 1.0 / jnp.sum(jnp.exp(gl - gmax), axis=-1, keepdims=True)
    eid = lane - MOE_GROUPS
    in_grp = (eid >= 0) & (eid < MOE_GROUPS * MOE_EPG) & ((eid >> 3) == gidx)
    el = jnp.where(in_grp, logits, NEG)
    m1 = jnp.max(el, axis=-1, keepdims=True)
    i1 = jnp.min(jnp.where(el == m1, lane, big), axis=-1, keepdims=True)
    el2 = jnp.where(lane == i1, NEG, el)
    m2 = jnp.max(el2, axis=-1, keepdims=True)
    i2 = jnp.min(jnp.where(el2 == m2, lane, big), axis=-1, keepdims=True)
    p2 = jnp.exp(m2 - m1)
    w1 = g_w / (1.0 + p2)
    w2 = g_w * p2 / (1.0 + p2)
    r_ref[...] = jnp.where(lane == 0, (i1 - MOE_GROUPS).astype(F32),
                           jnp.where(lane == 1, (i2 - MOE_GROUPS).astype(F32),
                                     jnp.where(lane == 2, w1, jnp.where(lane == 3, w2, 0.0))))


def moe_router(x, g, wr, br):
    n, d = x.shape
    tm = _tile(n, 256, BF16_ROWS)
    return pl.pallas_call(
        _router_kernel,
        grid=(n // tm,),
        in_specs=[pl.BlockSpec((tm, d), lambda i: (i, 0)), pl.BlockSpec((1, d), lambda i: (0, 0)),
                  pl.BlockSpec((d, LANES), lambda i: (0, 0)), pl.BlockSpec((1, LANES), lambda i: (0, 0))],
        out_specs=[pl.BlockSpec((tm, d), lambda i: (i, 0)), pl.BlockSpec((tm, LANES), lambda i: (i, 0))],
        out_shape=[jax.ShapeDtypeStruct((n, d), BF16), jax.ShapeDtypeStruct((n, LANES), F32)],
        compiler_params=_params(("parallel",)),
        name="moe_router",
    )(x, g.reshape(1, d), wr, br)


def _expert_kernel(te_ref, nu_ref, x_ref, rw_ref, wg_ref, wu_ref, wd_ref, y_ref, wgb_ref, wub_ref, wdb_ref):
    t = pl.program_id(0)
    fresh = jnp.logical_or(t == 0, te_ref[t] != te_ref[jnp.maximum(t - 1, 0)])

    @pl.when(jnp.logical_and(fresh, t < nu_ref[0]))
    def _():
        wgb_ref[...] = wg_ref[...].astype(BF16)
        wub_ref[...] = wu_ref[...].astype(BF16)
        wdb_ref[...] = wd_ref[...].astype(BF16)

    @pl.when(t < nu_ref[0])
    def _():
        x = x_ref[...]
        a = jnp.dot(x, wgb_ref[...], preferred_element_type=F32)
        u = jnp.dot(x, wub_ref[...], preferred_element_type=F32)
        rw = rw_ref[...]
        hid = (_silu(a) * u * jnp.concatenate([rw] * (a.shape[1] // LANES), axis=1)).astype(BF16)
        y_ref[...] = jnp.dot(hid, wdb_ref[...], preferred_element_type=F32)


def moe_experts(xs, row_w, tile_e, n_used, wg, wu, wd, layer, tme):
    r, d = xs.shape
    ff = wg.shape[3]
    nt = r // tme
    return pl.pallas_call(
        _expert_kernel,
        grid_spec=pltpu.PrefetchScalarGridSpec(
            num_scalar_prefetch=2, grid=(nt,),
            in_specs=[pl.BlockSpec((tme, d), lambda t, te, nu: (t, 0)),
                      pl.BlockSpec((tme, LANES), lambda t, te, nu: (t, 0)),
                      pl.BlockSpec((None, None, d, ff), lambda t, te, nu: (layer, te[t], 0, 0)),
                      pl.BlockSpec((None, None, d, ff), lambda t, te, nu: (layer, te[t], 0, 0)),
                      pl.BlockSpec((None, None, ff, d), lambda t, te, nu: (layer, te[t], 0, 0))],
            out_specs=pl.BlockSpec((tme, d), lambda t, te, nu: (t, 0)),
            scratch_shapes=[pltpu.VMEM((d, ff), BF16), pltpu.VMEM((d, ff), BF16), pltpu.VMEM((ff, d), BF16)]),
        out_shape=jax.ShapeDtypeStruct((r, d), F32),
        compiler_params=_params(("arbitrary",)),
        name="moe_experts",
    )(tile_e, n_used, xs, row_w, wg, wu, wd)


def moe_layer(x, g, wrg, brg, wre, bre, wg, wu, wd, layer):
    n, d = x.shape
    n_exp = wg.shape[1]
    wr = jnp.concatenate([wrg, wre.transpose(1, 0, 2).reshape(d, n_exp)], axis=1)
    wr = jnp.pad(wr, ((0, 0), (0, LANES - wr.shape[1])))
    br = jnp.pad(jnp.concatenate([brg, bre.reshape(-1)]), (0, LANES - MOE_GROUPS - n_exp)).reshape(1, LANES)
    h, route = moe_router(x, g, wr.astype(BF16), br)
    tme = 256
    e_flat = route[:, 0:MOE_TOPK].astype(jnp.int32).T.reshape(-1)
    w_flat = route[:, MOE_TOPK:2 * MOE_TOPK].T.reshape(-1)
    na = e_flat.shape[0]
    nt = -(-na // tme) + n_exp
    order = jnp.argsort(e_flat, stable=True)
    counts = jnp.sum(e_flat[:, None] == jnp.arange(n_exp)[None, :], axis=0).astype(jnp.int32)
    pc = ((counts + tme - 1) // tme) * tme
    pend = jnp.cumsum(pc)
    pstart = pend - pc
    ustart = jnp.cumsum(counts) - counts
    tile_e = jnp.minimum(jnp.searchsorted(pend, jnp.arange(nt, dtype=jnp.int32) * tme, side="right"), n_exp - 1).astype(jnp.int32)
    n_used = (pend[-1] // tme).astype(jnp.int32).reshape(1)
    row = jnp.arange(nt * tme, dtype=jnp.int32)
    row_e = jnp.repeat(tile_e, tme)
    idx = row - pstart[row_e]
    valid = idx < counts[row_e]
    src = order[jnp.minimum(ustart[row_e] + idx, na - 1)]
    row_tok = jnp.where(valid, src % n, 0).astype(jnp.int32)
    row_w = jnp.where(valid, w_flat[src], 0.0)
    se = e_flat[order]
    dest = pstart[se] + jnp.arange(na, dtype=jnp.int32) - ustart[se]
    dest_flat = dest[jnp.argsort(order)]
    xs = jnp.take(h, row_tok, axis=0)
    ys = moe_experts(xs, jnp.broadcast_to(row_w[:, None], (nt * tme, LANES)), tile_e, n_used, wg, wu, wd, layer, tme)
    return x + jnp.take(ys, dest_flat[:n], axis=0) + jnp.take(ys, dest_flat[n:], axis=0)


def kernel(x_prompt, x_sample, cache_nsa_kv, cache_nsa_win, state_conv_a, state_conv_ssm, state_ssm, page_table,
           rel_bias, norm_mix_g, w_in, conv_a_w, conv_a_b, ln_a_g, ln_a_b, w_a_out, w_nsa_out, conv_s_w, conv_s_b,
           dt_bias, a_log, d_skip, ssm_norm_g, w_ssm_out, w_out, norm_ffn_g, w_router_group, b_router_group,
           w_router_expert, b_router_expert, w_gate, w_up, w_down, final_norm_g):
    bp, tp, d = x_prompt.shape
    bs, ts, _ = x_sample.shape
    depth = w_in.shape[0]
    n_p, n_s = bp * tp, bs * ts
    ch = conv_a_w.shape[2]
    conv_dim = conv_s_w.shape[2]
    heads_ssm = dt_bias.shape[1]
    inner = heads_ssm * SSM_HEAD_DIM
    hd = N_HEADS * HEAD_DIM
    kw = KV_GROUPS * HEAD_DIM
    kvw = KV_KINDS * kw
    n_pages = page_table.shape[1]
    page = cache_nsa_kv.shape[2]
    past = n_pages * page
    wl = cache_nsa_win.shape[2]
    assert page == 2 * BLK and wl == min(WINDOW, past) and tp >= WINDOW
    pps = _tile(n_pages, 16, 1)
    rows_p = GQA * QCHUNK

    splits = (2 * ch, hd, kvw, 3 * N_HEADS, inner, conv_dim, heads_ssm, 3 * d)
    offs = np.concatenate([[0], np.cumsum(splits)])
    o_u, o_q, o_kv, o_gn, o_z, o_x, o_dt, o_gm = [int(v) for v in offs[:-1]]

    near_d = _near_distances()
    tj = np.arange(wl + LANES)[None, :]
    win_d = np.where(tj < wl + ts, wl + np.arange(ts)[:, None] - tj, 0)
    near_f = expand_bias(rel_bias, near_d)
    wtab_f = expand_bias(rel_bias, win_d)
    near_g = near_f.reshape(KV_GROUPS, GQA, 3, QCHUNK, LANES).transpose(0, 2, 1, 3, 4)
    near_p = near_g.reshape(KV_GROUPS, 3, rows_p, LANES)
    near_s = near_g[:, :, :, :ts].reshape(KV_GROUPS, 3, GQA * ts, LANES)
    wtab = wtab_f.reshape(KV_GROUPS, GQA * ts, wl + LANES)
    far = rel_bias[REL_BUCKETS - 1].reshape(KV_GROUPS, GQA, 1)
    cst_p = jnp.repeat(far, QCHUNK, axis=1).reshape(KV_GROUPS, rows_p, 1)
    cst_s = jnp.repeat(far, ts, axis=1).reshape(KV_GROUPS, GQA * ts, 1)
    ii = (np.arange(rows_p) % QCHUNK)[:, None]
    jj = np.arange(LANES)[None, :]
    cst_t = jnp.broadcast_to(cst_p, (KV_GROUPS, rows_p, LANES))
    band_t = jnp.pad(near_p[:, 2, :, 0:4].transpose(0, 2, 1), ((0, 0), (0, SUBLANES - 4), (0, 0)))
    tabs_p = jnp.stack([jnp.where(jj <= ii, near_p[:, 0], NEG), near_p[:, 1], cst_t,
                        jnp.where(jj >= ii, cst_t, NEG), jnp.full((KV_GROUPS, rows_p, LANES), NEG, F32)], axis=1)

    w_in_t = w_in.transpose(0, 2, 1)
    cache_t = cache_nsa_kv.transpose(0, 1, 3, 4, 5, 2)
    win_t = cache_nsa_win.transpose(0, 1, 3, 4, 5, 2)
    zeros_hist_a = jnp.zeros((bp, HIST_A, ch), F32)
    zeros_hist_s = jnp.zeros((bp, HIST_S, conv_dim), F32)
    zeros_ssm = jnp.zeros((bp, heads_ssm, SSM_HEAD_DIM, SSM_STATE), F32)
    nbl = -(-(past // BLK + 1) // LANES) * LANES

    x = jnp.concatenate([x_prompt.reshape(n_p, d), x_sample.reshape(n_s, d)], axis=0)
    kv_p, kv_s, win_p, win_s, ha_p, ha_s, hs_p, hs_s, st_p, st_s = ([] for _ in range(10))
    for l in range(depth):
        h = rmsnorm(x, norm_mix_g[l], BF16)
        (u_in,) = matmul(h, w_in_t, l, o_u, 2 * ch, [F32])
        (q,) = matmul(h, w_in_t, l, o_q, hd, [BF16], scale=HEAD_DIM ** -0.5)
        kv, kvb = matmul(h, w_in_t, l, o_kv, kvw, [F32, BF16])
        w_small = jnp.concatenate([w_in_t[l, o_dt:o_gm], w_in_t[l, o_gn:o_z],
                                   jnp.zeros((LANES - heads_ssm - 3 * N_HEADS, d), F32)], axis=0)
        (small,) = matmul(h, w_small[None], 0, 0, LANES, [F32])
        (z,) = matmul(h, w_in_t, l, o_z, inner, [F32])
        (xbc,) = matmul(h, w_in_t, l, o_x, conv_dim, [F32])
        (gmix,) = matmul(h, w_in_t, l, o_gm, 3 * d, [F32])

        hist_a = jnp.pad(state_conv_a[l], ((0, 0), (HIST_A - (CONV_A_W - 1), 0), (0, 0)))
        ca_args = (conv_a_w[l], conv_a_b[l], ln_a_g[l], ln_a_b[l])
        c_p, ha_new_p = conv_a_branch(u_in, 0, zeros_hist_a, *ca_args, bp, tp, BF16, n_out=n_p + n_s)
        c_s, ha_new_s = conv_a_branch(u_in, n_p, hist_a, *ca_args, bs, ts, F32)
        hist_s = jnp.pad(state_conv_ssm[l], ((0, 0), (HIST_S - (SSM_CONV_W - 1), 0), (0, 0)))
        ss_args = (conv_s_w[l], conv_s_b[l], dt_bias[l], a_log[l], d_skip[l], ssm_norm_g[l])
        small_tp = small[:n_p].reshape(bp, tp, LANES).transpose(0, 2, 1)
        small_ts = small[n_p:].reshape(bs, ts, LANES).transpose(0, 2, 1)
        y_p, hs_new_p, st_new_p = ssd_branch(xbc, z, small, small_tp, 0, zeros_hist_s, zeros_ssm, *ss_args, bp, tp, BF16,
                                             n_out=n_p + n_s)
        y_s, hs_new_s, st_new_s = ssd_branch(xbc, z, small, small_ts, n_p, hist_s, state_ssm[l], *ss_args, bs, ts, F32)
        kc_p = prompt_block_means(kv, bp, tp)
        o_p = nsa_prompt(q, kvb, kc_p, small, tabs_p, band_t, cst_p.reshape(KV_GROUPS, 1, rows_p), bp, tp, n_out=n_p + n_s)
        o_s = nsa_sample(q[n_p:].astype(F32).reshape(bs, ts, hd), kv[n_p:].reshape(bs, ts, kvw),
                         small[n_p:].reshape(bs, ts, LANES), win_t, near_s, wtab, cst_s, cache_t, page_table, l, pps, nbl)

        c_all = lax.dynamic_update_slice(c_p, c_s.astype(BF16), (n_p, 0))
        o_all = lax.dynamic_update_slice(o_p, o_s.reshape(n_s, hd).astype(BF16), (n_p, 0))
        y_all = lax.dynamic_update_slice(y_p, y_s.astype(BF16), (n_p, 0))
        mixed = merge_branches(c_all, o_all, y_all, gmix, w_a_out[l].astype(BF16), w_nsa_out[l].astype(BF16),
                               w_ssm_out[l].astype(BF16))
        x = out_projection(mixed, w_out[l].astype(BF16), x)
        x = moe_layer(x, norm_ffn_g[l], w_router_group[l], b_router_group[l], w_router_expert[l], b_router_expert[l],
                      w_gate, w_up, w_down, l)

        kvr = kv[:, :4 * kw]
        kv_p.append(kvr[:n_p].reshape(bp, tp, 4, KV_GROUPS, HEAD_DIM))
        kv_s.append(kvr[n_p:].reshape(bs, ts, 4, KV_GROUPS, HEAD_DIM))
        wr = kv[:, 4 * kw:]
        win_p.append(wr[:n_p].reshape(bp, tp, 2, KV_GROUPS, HEAD_DIM)[:, tp - WINDOW:])
        win_all = jnp.concatenate([cache_nsa_win[l], wr[n_p:].reshape(bs, ts, 2, KV_GROUPS, HEAD_DIM)], axis=1)
        keep = min(WINDOW, wl + ts)
        win_s.append(win_all[:, wl + ts - keep:])
        ha_p.append(ha_new_p[:, HIST_A - (CONV_A_W - 1):])
        ha_s.append(ha_new_s[:, HIST_A - (CONV_A_W - 1):])
        hs_p.append(hs_new_p[:, HIST_S - (SSM_CONV_W - 1):])
        hs_s.append(hs_new_s[:, HIST_S - (SSM_CONV_W - 1):])
        st_p.append(st_new_p)
        st_s.append(st_new_s)

    y = rmsnorm(x, final_norm_g, F32)
    return (y[:n_p].reshape(bp, tp, d), y[n_p:].reshape(bs, ts, d),
            jnp.stack(kv_p, axis=1), jnp.stack(kv_s, axis=1), jnp.stack(win_p), jnp.stack(win_s),
            jnp.stack(ha_p), jnp.stack(ha_s), jnp.stack(hs_p), jnp.stack(hs_s), jnp.stack(st_p), jnp.stack(st_s))
```

```python
import functools
import math

import numpy as np
import jax
import jax.numpy as jnp
from jax import lax
from jax.experimental import pallas as pl
from jax.experimental.pallas import tpu as pltpu

HEAD_DIM = 64
KV_GROUPS = 4
GQA = 4
N_HEADS = KV_GROUPS * GQA
KV_KINDS = 6
BLK = 64
TOPK = 8
WINDOW = 512
QCHUNK = 128
REL_BUCKETS = 32
REL_MAX_DIST = 128
CONV_A_W = 31
SSM_CONV_W = 4
SSM_HEAD_DIM = 64
SSM_GROUPS = 4
SSM_STATE = 128
SSM_CHUNK = 128
MOE_GROUPS = 4
MOE_EPG = 8
MOE_TOPK = 2
EPS = 1e-6
NEG = -1e30
FORCE = 1e4

LANES = 128
SUBLANES = 8
BF16_ROWS = 16
VMEM_LIMIT = 56 * 1024 * 1024
DT_COLS = 32
GATE_COL0 = DT_COLS
SUPER = 4

HIGHEST = lax.Precision.HIGHEST
F32 = jnp.float32
BF16 = jnp.bfloat16


def _params(sem, limit=VMEM_LIMIT):
    return pltpu.CompilerParams(dimension_semantics=sem, vmem_limit_bytes=limit)


def _tile(n, target, mult):
    best = None
    for t in range(mult, min(n, target) + 1, mult):
        if n % t == 0:
            best = t
    return best if best is not None else n


def _prefilled(kern, in_specs, args, n_out, width, dtype):
    if n_out is None:
        return kern, in_specs, args, {}
    n_in = len(in_specs)

    def body(*refs):
        return kern(*refs[:n_in], *refs[n_in + 1:])

    return (body, in_specs + [pl.BlockSpec(memory_space=pl.ANY)], args + (jnp.zeros((n_out, width), dtype),), {n_in: 0})


def _dot_nt(a, b):
    return lax.dot_general(a, b, (((1,), (1,)), ((), ())), preferred_element_type=F32)


def _dot_tn(a, b):
    return lax.dot_general(a, b, (((0,), (0,)), ((), ())), preferred_element_type=F32)


def _silu(x):
    return x * jax.nn.sigmoid(x)


def _softplus(x):
    return jnp.maximum(x, 0.0) + jnp.log(1.0 + jnp.exp(-jnp.abs(x)))


def _rms_kernel(x_ref, g_ref, o_ref):
    x = x_ref[...]
    ms = jnp.mean(x * x, axis=-1, keepdims=True)
    o_ref[...] = (x * lax.rsqrt(ms + EPS) * g_ref[...]).astype(o_ref.dtype)


def rmsnorm(x, g, out_dtype):
    n, d = x.shape
    tm = _tile(n, 1024, BF16_ROWS)
    return pl.pallas_call(
        _rms_kernel,
        grid=(n // tm,),
        in_specs=[pl.BlockSpec((tm, d), lambda i: (i, 0)), pl.BlockSpec((1, d), lambda i: (0, 0))],
        out_specs=pl.BlockSpec((tm, d), lambda i: (i, 0)),
        out_shape=jax.ShapeDtypeStruct((n, d), out_dtype),
        compiler_params=_params(("parallel",)),
        name="rmsnorm",
    )(x, g.reshape(1, d))


def _mm_kernel(a_ref, w_ref, *refs, scale):
    o_refs, wb_ref = refs[:-1], refs[-1]

    @pl.when(pl.program_id(1) == 0)
    def _():
        w = w_ref[0]
        wb_ref[...] = (w if scale == 1.0 else w * scale).astype(BF16)

    acc = _dot_nt(a_ref[...], wb_ref[...])
    for o in o_refs:
        o[...] = acc.astype(o.dtype)


def matmul(a, wt, layer, row0, c, out_dtypes, scale=1.0):
    n, k = a.shape
    assert row0 % SUBLANES == 0
    tm = _tile(n, 768, BF16_ROWS)
    tn = _tile(c, 1024, SUBLANES)
    outs = pl.pallas_call(
        functools.partial(_mm_kernel, scale=scale),
        grid=(c // tn, n // tm),
        in_specs=[pl.BlockSpec((tm, k), lambda j, i: (i, 0)),
                  pl.BlockSpec((pl.Element(1), pl.Element(tn), pl.Element(k)),
                               lambda j, i: (layer, pl.multiple_of(row0 + j * tn, SUBLANES), 0))],
        out_specs=[pl.BlockSpec((tm, tn), lambda j, i: (i, j)) for _ in out_dtypes],
        out_shape=[jax.ShapeDtypeStruct((n, c), dt) for dt in out_dtypes],
        scratch_shapes=[pltpu.VMEM((tn, k), BF16)],
        compiler_params=_params(("parallel", "arbitrary")),
        name="proj",
    )(a, wt)
    return outs


HIST_A = 32
HIST_S = 8


def _round_bf16(x):
    return x.astype(BF16).astype(F32)


def _push_tail(tail_ref, new, n_new, keep):
    if n_new >= keep:
        tail_ref[...] = new[n_new - keep:n_new]
    else:
        tail_ref[...] = jnp.concatenate([tail_ref[n_new:keep], new], axis=0)


def _conva_kernel(u_ref, hist_ref, w_ref, b_ref, lg_ref, lb_ref, c_ref, hout_ref, ext_ref, conv_ref, tail_ref, *, tt, ch):
    t = pl.program_id(1)

    @pl.when(t == 0)
    def _():
        tail_ref[...] = hist_ref[...]
        ext_ref[0:HIST_A, :] = _round_bf16(hist_ref[...])

    uin = u_ref[...]
    u = uin[:, :ch] * jax.nn.sigmoid(uin[:, ch:])
    ext_ref[HIST_A:HIST_A + tt, :] = _round_bf16(u)
    _push_tail(tail_ref, u, tt, HIST_A)
    hout_ref[...] = tail_ref[...]
    off = HIST_A - (CONV_A_W - 1)
    rb = min(tt, 128)

    def lane_block(cb, carry):
        lo = pl.multiple_of(cb * LANES, LANES)
        for r in range(tt // rb):
            acc = jnp.broadcast_to(b_ref[:, pl.ds(lo, LANES)], (rb, LANES))
            for k in range(CONV_A_W):
                acc = acc + w_ref[k:k + 1, pl.ds(lo, LANES)] * ext_ref[r * rb + off + k:r * rb + off + k + rb, pl.ds(lo, LANES)]
            conv_ref[r * rb:(r + 1) * rb, pl.ds(lo, LANES)] = acc
        return carry

    lax.fori_loop(0, ch // LANES, lane_block, 0)
    cv = conv_ref[...]
    mu = jnp.mean(cv, axis=-1, keepdims=True)
    xc = cv - mu
    var = jnp.mean(xc * xc, axis=-1, keepdims=True)
    y = xc * lax.rsqrt(var + EPS) * lg_ref[...] + lb_ref[...]
    c_ref[...] = _silu(y).astype(c_ref.dtype)
    ext_ref[0:HIST_A, :] = ext_ref[tt:tt + HIST_A, :]


def conv_a_branch(u_in, row_off, hist, w, b, lg, lb, batch, t_len, out_dtype, n_out=None):
    ch = w.shape[1]
    tt = _tile(t_len, 256, SUBLANES)
    nt = t_len // tt
    assert row_off % tt == 0
    r0 = row_off // tt
    kern = functools.partial(_conva_kernel, tt=tt, ch=ch)
    vec = lambda: pl.BlockSpec((1, ch), lambda bi, ti: (0, 0))
    in_specs = [pl.BlockSpec((tt, 2 * ch), lambda bi, ti: (r0 + bi * nt + ti, 0)),
                pl.BlockSpec((None, HIST_A, ch), lambda bi, ti: (bi, 0, 0)),
                pl.BlockSpec((CONV_A_W, ch), lambda bi, ti: (0, 0)), vec(), vec(), vec()]
    args = (u_in, hist, _round_bf16(w), b.reshape(1, ch), lg.reshape(1, ch), lb.reshape(1, ch))
    kern, in_specs, args, alias = _prefilled(kern, in_specs, args, n_out, ch, out_dtype)
    return pl.pallas_call(
        kern,
        grid=(batch, nt),
        in_specs=in_specs,
        out_specs=[pl.BlockSpec((tt, ch), lambda bi, ti: (bi * nt + ti, 0)),
                   pl.BlockSpec((None, HIST_A, ch), lambda bi, ti: (bi, 0, 0))],
        out_shape=[jax.ShapeDtypeStruct((n_out or batch * t_len, ch), out_dtype),
                   jax.ShapeDtypeStruct((batch, HIST_A, ch), F32)],
        scratch_shapes=[pltpu.VMEM((tt + HIST_A, ch), F32), pltpu.VMEM((tt, ch), F32), pltpu.VMEM((HIST_A, ch), F32)],
        input_output_aliases=alias,
        compiler_params=_params(("parallel", "arbitrary")),
        name="conv_a",
    )(*args)


def _ssd_kernel(xbc_ref, z_ref, dt_ref, dtt_ref, hist_ref, h0_ref, cw_ref, cb_ref, dtb_ref, dtbt_ref,
                a_ref, at_ref, dvec_ref, gn_ref, y_ref, hout_ref, sout_ref, ext_ref, h_ref, yacc_ref, tail_ref,
                *, q, nc, inner, heads):
    c = pl.program_id(1)

    @pl.when(c == 0)
    def _():
        tail_ref[...] = hist_ref[...]
        ext_ref[0:HIST_S, :] = _round_bf16(hist_ref[...])
        h_ref[...] = h0_ref[...]

    xbc = xbc_ref[...]
    ext_ref[HIST_S:HIST_S + q, :] = _round_bf16(xbc)
    _push_tail(tail_ref, xbc, q, HIST_S)
    hout_ref[...] = tail_ref[...]
    off = HIST_S - (SSM_CONV_W - 1)
    acc = jnp.broadcast_to(cb_ref[...], (q, cb_ref.shape[1]))
    for k in range(SSM_CONV_W):
        acc = acc + cw_ref[k:k + 1, :] * ext_ref[off + k:off + k + q, :]
    xc = _silu(acc)
    ext_ref[0:HIST_S, :] = ext_ref[q:q + HIST_S, :]

    gw = SSM_GROUPS * SSM_STATE
    xs = xc[:, :inner]
    bm = xc[:, inner:inner + gw]
    cm = xc[:, inner + gw:inner + 2 * gw]

    dt = _softplus(dt_ref[...] + dtb_ref[...])
    dtt = _softplus(dtt_ref[...] + dtbt_ref[...])
    da = dt * a_ref[...]
    dat = dtt * at_ref[...]
    ri = lax.broadcasted_iota(jnp.int32, (q, q), 0)
    ci = lax.broadcasted_iota(jnp.int32, (q, q), 1)
    causal = ri >= ci
    tri = causal.astype(F32)
    trit = (ri <= ci).astype(F32)
    acum = jnp.dot(tri, da, precision=HIGHEST, preferred_element_type=F32)
    acumt = jnp.dot(dat, trit, precision=HIGHEST, preferred_element_type=F32)
    alast = acum[q - 1:q, :]
    hpg = heads // SSM_GROUPS
    for g in range(SSM_GROUPS):
        bg = bm[:, g * SSM_STATE:(g + 1) * SSM_STATE].astype(BF16)
        cg = cm[:, g * SSM_STATE:(g + 1) * SSM_STATE].astype(BF16)
        cbm = _dot_nt(cg, bg)
        for e in range(hpg):
            hh = g * hpg + e
            a_col = acum[:, hh:hh + 1]
            a_row = acumt[hh:hh + 1, :]
            decay = jnp.exp(jnp.where(causal, a_col - a_row, NEG))
            wm = cbm * decay * dtt[hh:hh + 1, :]
            xh = xs[:, hh * SSM_HEAD_DIM:(hh + 1) * SSM_HEAD_DIM]
            hprev = h_ref[hh]
            y_h = (jnp.dot(wm.astype(BF16), xh.astype(BF16), preferred_element_type=F32)
                   + _dot_nt(cg, hprev.astype(BF16)) * jnp.exp(a_col))
            yacc_ref[:, hh * SSM_HEAD_DIM:(hh + 1) * SSM_HEAD_DIM] = y_h
            al = alast[:, hh:hh + 1]
            w_end = jnp.exp(al - a_col) * dt[:, hh:hh + 1]
            h_ref[hh] = jnp.exp(al) * hprev + _dot_tn((xh * w_end).astype(BF16), bg)

    y = (yacc_ref[...] + xs * dvec_ref[...]) * _silu(z_ref[...])
    gsz = inner // SSM_GROUPS
    for g in range(SSM_GROUPS):
        v = y[:, g * gsz:(g + 1) * gsz]
        ms = jnp.mean(v * v, axis=-1, keepdims=True)
        y_ref[:, g * gsz:(g + 1) * gsz] = (v * lax.rsqrt(ms + EPS) * gn_ref[:, g * gsz:(g + 1) * gsz]).astype(y_ref.dtype)

    @pl.when(c == nc - 1)
    def _():
        sout_ref[...] = h_ref[...]


def ssd_branch(xbc, z, small, small_t, row_off, hist, h0, cw, cb, dt_bias, a_log, d_skip, norm_g, batch, t_len, out_dtype,
               n_out=None):
    conv_dim = xbc.shape[1]
    inner = z.shape[1]
    heads = dt_bias.shape[0]
    assert heads == DT_COLS
    q = math.gcd(t_len, SSM_CHUNK)
    nc = t_len // q
    assert row_off % q == 0
    r0 = row_off // q
    pad = LANES - heads
    dtb = jnp.pad(dt_bias, (0, pad)).reshape(1, LANES)
    a_neg = jnp.pad(-jnp.exp(a_log), (0, pad)).reshape(1, LANES)
    dvec = jnp.repeat(d_skip, SSM_HEAD_DIM).reshape(1, inner)
    kern = functools.partial(_ssd_kernel, q=q, nc=nc, inner=inner, heads=heads)
    row = lambda w: pl.BlockSpec((1, w), lambda bi, ci: (0, 0))
    col = lambda: pl.BlockSpec((LANES, 1), lambda bi, ci: (0, 0))
    tok = lambda w: pl.BlockSpec((q, w), lambda bi, ci: (r0 + bi * nc + ci, 0))
    in_specs = [tok(conv_dim), tok(inner), tok(LANES),
                pl.BlockSpec((None, LANES, q), lambda bi, ci: (bi, 0, ci)),
                pl.BlockSpec((None, HIST_S, conv_dim), lambda bi, ci: (bi, 0, 0)),
                pl.BlockSpec((None, heads, SSM_HEAD_DIM, SSM_STATE), lambda bi, ci: (bi, 0, 0, 0)),
                pl.BlockSpec((SSM_CONV_W, conv_dim), lambda bi, ci: (0, 0)), row(conv_dim),
                row(LANES), col(), row(LANES), col(), row(inner), row(inner)]
    args = (xbc, z, small, small_t, hist, h0, _round_bf16(cw), cb.reshape(1, conv_dim), dtb, dtb.reshape(LANES, 1),
            a_neg, a_neg.reshape(LANES, 1), dvec, norm_g.reshape(1, inner))
    kern, in_specs, args, alias = _prefilled(kern, in_specs, args, n_out, inner, out_dtype)
    return pl.pallas_call(
        kern,
        grid=(batch, nc),
        in_specs=in_specs,
        input_output_aliases=alias,
        out_specs=[pl.BlockSpec((q, inner), lambda bi, ci: (bi * nc + ci, 0)),
                   pl.BlockSpec((None, HIST_S, conv_dim), lambda bi, ci: (bi, 0, 0)),
                   pl.BlockSpec((None, heads, SSM_HEAD_DIM, SSM_STATE), lambda bi, ci: (bi, 0, 0, 0))],
        out_shape=[jax.ShapeDtypeStruct((n_out or batch * t_len, inner), out_dtype),
                   jax.ShapeDtypeStruct((batch, HIST_S, conv_dim), F32),
                   jax.ShapeDtypeStruct((batch, heads, SSM_HEAD_DIM, SSM_STATE), F32)],
        scratch_shapes=[pltpu.VMEM((q + HIST_S, conv_dim), F32),
                        pltpu.VMEM((heads, SSM_HEAD_DIM, SSM_STATE), F32),
                        pltpu.VMEM((q, inner), F32), pltpu.VMEM((HIST_S, conv_dim), F32)],
        compiler_params=_params(("parallel", "arbitrary")),
        name="ssd",
    )(*args)


def _bucket_np(d):
    n = np.maximum(d, 0)
    exact = REL_BUCKETS // 2
    nf = np.maximum(n, 1).astype(np.float32)
    large = exact + (np.log(nf / np.float32(exact)) / np.float32(math.log(REL_MAX_DIST / exact))
                     * np.float32(REL_BUCKETS - exact)).astype(np.int32)
    return np.where(n < exact, n, np.minimum(large, REL_BUCKETS - 1))


def _near_distances():
    i = np.arange(QCHUNK)[:, None]
    j = np.arange(LANES)[None, :]
    diag = i - j
    prev = QCHUNK + i - j
    band = np.where(j < 4, i + (BLK + 1) - BLK * j, 0)
    return np.stack([diag, prev, band])


def _expand_kernel(tab_ref, oh_ref, o_ref):
    o_ref[...] = jnp.dot(tab_ref[...], oh_ref[...], precision=HIGHEST, preferred_element_type=F32)


def expand_bias(rel_bias, dist):
    flat = dist.reshape(-1)
    m = flat.shape[0]
    assert m % LANES == 0
    onehot = (np.arange(REL_BUCKETS)[:, None] == _bucket_np(flat)[None, :]).astype(np.float32)
    tm = _tile(m, 8192, LANES)
    heads = rel_bias.shape[1]
    out = pl.pallas_call(
        _expand_kernel,
        grid=(m // tm,),
        in_specs=[pl.BlockSpec((heads, REL_BUCKETS), lambda i: (0, 0)), pl.BlockSpec((REL_BUCKETS, tm), lambda i: (0, i))],
        out_specs=pl.BlockSpec((heads, tm), lambda i: (0, i)),
        out_shape=jax.ShapeDtypeStruct((heads, m), F32),
        compiler_params=_params(("parallel",)),
        name="bias_expand",
    )(rel_bias.T, jnp.asarray(onehot))
    return out.reshape((heads,) + dist.shape)


def _select_blocks(imp, lane, axis=-1):
    sel = jnp.zeros(imp.shape, F32)
    big = jnp.int32(1 << 30)
    v = imp
    for _ in range(TOPK):
        mx = jnp.max(v, axis=axis, keepdims=True)
        idx = jnp.min(jnp.where(v == mx, lane, big), axis=axis, keepdims=True)
        hit = lane == idx
        sel = jnp.where(hit, 1.0, sel)
        v = jnp.where(hit, -3e38, v)
    return sel


def _rep4(x):
    return jnp.concatenate([x, x, x, x], axis=0)


def _qgroup(q_ref, g):
    return jnp.concatenate([q_ref[:, (g * GQA + r) * HEAD_DIM:(g * GQA + r + 1) * HEAD_DIM] for r in range(GQA)], axis=0)


def _gate_col(sg, br, g):
    c0 = GATE_COL0 + br * N_HEADS + g * GQA
    return jnp.concatenate([sg[:, c0 + r:c0 + r + 1] for r in range(GQA)], axis=0)


def _rows_means_kernel(x_ref, o_ref):
    nb, w = o_ref.shape
    o_ref[...] = jnp.sum(x_ref[...].reshape(nb, BLK, w), axis=1) * (1.0 / BLK)


def prompt_block_means(kv, batch, t_len):
    w = 2 * KV_GROUPS * HEAD_DIM
    nb = t_len // BLK
    return pl.pallas_call(
        _rows_means_kernel,
        grid=(batch,),
        in_specs=[pl.BlockSpec((t_len, w), lambda b: (b, 0))],
        out_specs=pl.BlockSpec((None, nb, w), lambda b: (b, 0, 0)),
        out_shape=jax.ShapeDtypeStruct((batch, nb, w), F32),
        compiler_params=_params(("parallel",)),
        name="cmp_means_prompt",
    )(kv)


def _page_specs(pps, layer):
    def spec(k):
        return pl.BlockSpec((None, None, 4, KV_GROUPS, HEAD_DIM, 2 * BLK),
                            lambda b, j, pt: (pt[b, j * pps + k], layer, 0, 0, 0, 0))
    return [spec(k) for k in range(pps)]


def _nsa_prompt_kernel(q_ref, ksel_ref, vsel_ref, kwin_ref, vwin_ref, kc_ref, gn_ref, tab_ref, band_ref, cst_ref, e_ref, o_ref,
                       s_ref, sel_ref, oc_ref, mrun_ref, acc_ref, *, nb):
    c = pl.program_id(1)
    qc = QCHUNK
    rows = GQA * qc
    kw = KV_GROUPS * HEAD_DIM
    span = SUPER * qc
    blk = lax.broadcasted_iota(jnp.int32, (nb, rows), 0)
    qpos = c * qc + (lax.broadcasted_iota(jnp.int32, (nb, rows), 1) & (qc - 1))
    okc = (qpos - BLK * blk - (BLK - 1)) >= 0
    blk1 = lax.broadcasted_iota(jnp.int32, (nb, qc), 0)
    qi1 = lax.broadcasted_iota(jnp.int32, (nb, qc), 1)
    okc1 = (c * qc + qi1 - BLK * blk1 - (BLK - 1)) >= 0
    cur = 2 * c + (qi1 >= BLK).astype(jnp.int32)
    for g in range(KV_GROUPS):
        qg = _qgroup(q_ref, g)
        kcg = kc_ref[:, g * HEAD_DIM:(g + 1) * HEAD_DIM].astype(BF16)
        vcg = kc_ref[:, kw + g * HEAD_DIM:kw + (g + 1) * HEAD_DIM].astype(BF16)
        bias_c = jnp.broadcast_to(cst_ref[g], (nb, rows))
        for u in range(4):
            bias_c = jnp.where(blk == 2 * c - 2 + u, band_ref[g, u:u + 1, :], bias_c)
        s = jnp.where(okc, _dot_nt(kcg, qg) + bias_c, NEG)
        e = jnp.exp(s - jnp.max(s, axis=0, keepdims=True))
        p = jnp.where(okc, e / jnp.sum(e, axis=0, keepdims=True), 0.0)
        oc_ref[g] = _dot_tn(p.astype(BF16), vcg)
        psum = p[:, 0:qc] + p[:, qc:2 * qc] + p[:, 2 * qc:3 * qc] + p[:, 3 * qc:4 * qc]
        imp = jnp.where(blk1 == cur, FORCE, jnp.where(okc1, psum, -1.0))
        selm = jnp.where(blk1 <= cur, _select_blocks(imp, blk1, axis=0), 0.0)
        sel_ref[g] = (_dot_tn(selm.astype(BF16), e_ref[...]) - 1.0) * (-NEG)

    sg = jax.nn.sigmoid(gn_ref[...])
    n_super = c // SUPER + 1
    w_tiles = WINDOW // qc + 1
    st0 = jnp.maximum(c - (w_tiles - 1), 0)
    for g in range(KV_GROUPS):
        qg = _qgroup(q_ref, g)
        mrun_ref[...] = jnp.full((rows, LANES), NEG, F32)

        def scores(st, carry):
            k = ksel_ref[pl.ds(pl.multiple_of(st * span, span), span), g * HEAD_DIM:(g + 1) * HEAD_DIM]
            s = _dot_nt(qg, k)
            mrun = mrun_ref[...]
            for j in range(SUPER):
                kt = st * SUPER + j
                idx = jnp.where(kt == c, 0, jnp.where(kt == c - 1, 1, 2))
                off = pl.multiple_of(kt * qc, qc)
                sj = s[:, j * qc:(j + 1) * qc] + tab_ref[g, idx] + _rep4(sel_ref[g, :, pl.ds(off, qc)])
                s_ref[:, pl.ds(off, qc)] = sj
                mrun = jnp.maximum(mrun, sj)
            mrun_ref[...] = mrun
            return carry

        lax.fori_loop(0, n_super, scores, 0)
        m = jnp.max(mrun_ref[...], axis=-1, keepdims=True)
        mrun_ref[...] = jnp.zeros((rows, LANES), F32)

        def exps(st, carry):
            off = pl.multiple_of(st * span, span)
            e = jnp.exp(s_ref[:, pl.ds(off, span)] - m)
            s_ref[:, pl.ds(off, span)] = e
            lrun = mrun_ref[...]
            for j in range(SUPER):
                lrun = lrun + e[:, j * qc:(j + 1) * qc]
            mrun_ref[...] = lrun
            return carry

        lax.fori_loop(0, n_super, exps, 0)
        denom = jnp.sum(mrun_ref[...], axis=-1, keepdims=True)
        acc_ref[...] = jnp.zeros((rows, HEAD_DIM), F32)

        def values(st, carry):
            off = pl.multiple_of(st * span, span)
            p = (s_ref[:, pl.ds(off, span)] / denom).astype(BF16)
            acc_ref[...] += jnp.dot(p, vsel_ref[pl.ds(off, span), g * HEAD_DIM:(g + 1) * HEAD_DIM], preferred_element_type=F32)
            return carry

        lax.fori_loop(0, n_super, values, 0)
        o_s = acc_ref[...]
        r0 = pl.multiple_of(st0 * qc, qc)
        s = _dot_nt(qg, kwin_ref[pl.ds(r0, w_tiles * qc), g * HEAD_DIM:(g + 1) * HEAD_DIM])
        parts = []
        for w in range(w_tiles):
            kt = st0 + w
            idx = jnp.where(kt > c, 4, jnp.where(kt == c, 0, jnp.where(kt == c - 1, 1,
                                                                         jnp.where(kt == c - (w_tiles - 1), 3, 2))))
            parts.append(s[:, w * qc:(w + 1) * qc] + tab_ref[g, idx])
        sw = jnp.concatenate(parts, axis=1)
        ew = jnp.exp(sw - jnp.max(sw, axis=-1, keepdims=True))
        pw = (ew / jnp.sum(ew, axis=-1, keepdims=True)).astype(BF16)
        o_w = jnp.dot(pw, vwin_ref[pl.ds(r0, w_tiles * qc), g * HEAD_DIM:(g + 1) * HEAD_DIM],
                      preferred_element_type=F32)
        tot = _gate_col(sg, 0, g) * oc_ref[g] + _gate_col(sg, 1, g) * o_s + _gate_col(sg, 2, g) * o_w
        for r in range(GQA):
            o_ref[:, (g * GQA + r) * HEAD_DIM:(g * GQA + r + 1) * HEAD_DIM] = tot[r * qc:(r + 1) * qc].astype(o_ref.dtype)


def nsa_prompt(q, kvb, kc, small, tabs, band, cst, batch, t_len, n_out=None):
    assert t_len % (SUPER * QCHUNK) == 0 and t_len >= WINDOW + QCHUNK
    nch = t_len // QCHUNK
    nb = t_len // BLK
    assert nb % SUBLANES == 0
    rows = GQA * QCHUNK
    hd = N_HEADS * HEAD_DIM
    emat = (np.arange(nb)[:, None] == (np.arange(t_len)[None, :] // BLK)).astype(np.float32)
    kern = functools.partial(_nsa_prompt_kernel, nb=nb)
    full = lambda shp: pl.BlockSpec(shp, lambda b, c: (0,) * len(shp))
    kind = lambda k: pl.BlockSpec((t_len, KV_GROUPS * HEAD_DIM), lambda b, c: (b, k))
    in_specs = [pl.BlockSpec((QCHUNK, hd), lambda b, c: (b * nch + c, 0)),
                kind(2), kind(3), kind(4), kind(5),
                pl.BlockSpec((None, nb, kc.shape[2]), lambda b, c: (b, 0, 0)),
                pl.BlockSpec((QCHUNK, LANES), lambda b, c: (b * nch + c, 0)),
                full((KV_GROUPS, 5, rows, LANES)), full((KV_GROUPS, SUBLANES, rows)), full((KV_GROUPS, 1, rows)),
                full((nb, t_len))]
    args = (q, kvb, kvb, kvb, kvb, kc, small, tabs, band, cst, jnp.asarray(emat, BF16))
    kern, in_specs, args, alias = _prefilled(kern, in_specs, args, n_out, hd, BF16)
    return pl.pallas_call(
        kern,
        grid=(batch, nch),
        in_specs=in_specs,
        input_output_aliases=alias,
        out_specs=pl.BlockSpec((QCHUNK, hd), lambda b, c: (b * nch + c, 0)),
        out_shape=jax.ShapeDtypeStruct((n_out or batch * t_len, hd), BF16),
        scratch_shapes=[pltpu.VMEM((rows, t_len), F32),
                        pltpu.VMEM((KV_GROUPS, QCHUNK, t_len), F32),
                        pltpu.VMEM((KV_GROUPS, rows, HEAD_DIM), F32),
                        pltpu.VMEM((rows, LANES), F32), pltpu.VMEM((rows, HEAD_DIM), F32)],
        compiler_params=_params(("parallel", "arbitrary")),
        name="nsa_prompt",
    )(*args)


def _nsa_sample_kernel(pt_ref, q_ref, kvn_ref, gn_ref, avg_ref, win_ref, near_ref, wtab_ref, cst_ref, e_ref,
                       *refs, pps, n_steps, past, ts, nbl):
    pages = refs[:pps]
    o_ref = refs[pps]
    s_ref, kc_ref, vbuf_ref, kn_ref = refs[pps + 1:]
    j = pl.program_id(1)
    rows = GQA * ts
    kw = KV_GROUPS * HEAD_DIM
    nb_past = past // BLK
    pg = 2 * BLK
    lane = lax.broadcasted_iota(jnp.int32, (rows, LANES), 1)
    ti = lax.broadcasted_iota(jnp.int32, (rows, LANES), 0) & (ts - 1)

    def qgroup(g):
        return _qgroup(q_ref, g).astype(BF16)

    @pl.when(j == 0)
    def _():
        kn_ref[...] = jnp.zeros(kn_ref.shape, F32)
        kn_ref[0:ts, :] = kvn_ref[:, 2 * kw:6 * kw]
        kc_ref[...] = jnp.zeros(kc_ref.shape, F32)

    is_last = j == n_steps - 1
    base = j * (pps * pg)
    qgs = [qgroup(g) for g in range(KV_GROUPS)]
    avg = avg_ref[...]
    blk0 = pl.multiple_of(j * (2 * pps), 2 * pps)
    for kind in range(2):
        for g in range(KV_GROUPS):
            cat = jnp.concatenate([pgr[kind, g] for pgr in pages], axis=1)
            hi = cat.astype(BF16)
            lo = (cat - hi.astype(F32)).astype(BF16)
            kc_ref[kind, g, pl.ds(blk0, 2 * pps), :] = _dot_nt(avg, hi) + _dot_nt(avg, lo)
    for k in range(pps):
        off = pl.multiple_of(base + k * pg, pg)
        for g in range(KV_GROUPS):
            vbuf_ref[g, :, pl.ds(off, pg)] = pages[k][3, g].astype(BF16)
            sk = jnp.dot(qgs[g], pages[k][2, g].astype(BF16), preferred_element_type=F32)
            bias = jnp.where(is_last, near_ref[g, 1], cst_ref[g]) if k == pps - 1 else cst_ref[g]
            s_ref[g, :, pl.ds(off, pg)] = sk + bias

    @pl.when(is_last)
    def _():
        sg = jax.nn.sigmoid(gn_ref[...])
        wl = win_ref.shape[-1]
        lane_w = lax.broadcasted_iota(jnp.int32, (rows, wl + LANES), 1)
        t_w = lax.broadcasted_iota(jnp.int32, (rows, wl + LANES), 0) & (ts - 1)
        okw = jnp.where(lane_w < wl, lane_w - t_w, t_w - (lane_w - wl)) >= 0
        lane_b = lax.broadcasted_iota(jnp.int32, (rows, nbl), 1)
        t_b = lax.broadcasted_iota(jnp.int32, (rows, nbl), 0) & (ts - 1)
        okc = (past + t_b - BLK * lane_b - (BLK - 1)) >= 0
        lane1 = lax.broadcasted_iota(jnp.int32, (ts, nbl), 1)
        okc1 = (past + lax.broadcasted_iota(jnp.int32, (ts, nbl), 0) - BLK * lane1 - (BLK - 1)) >= 0
        o_cs, madds = [], []
        for g in range(KV_GROUPS):
            qg = qgs[g]
            cst = cst_ref[g]
            band = near_ref[g, 2]
            kcg = kc_ref[0, g].astype(BF16)
            vcg = kc_ref[1, g].astype(BF16)
            bias_c = jnp.where(lane_b == nb_past - 2, band[:, 0:1], jnp.where(lane_b == nb_past - 1, band[:, 1:2], cst))
            s = jnp.where(okc, _dot_nt(qg, kcg) + bias_c, NEG)
            e = jnp.exp(s - jnp.max(s, axis=-1, keepdims=True))
            p = jnp.where(okc, e / jnp.sum(e, axis=-1, keepdims=True), 0.0)
            o_cs.append(jnp.dot(p.astype(BF16), vcg, preferred_element_type=F32))
            psum = p[0:ts] + p[ts:2 * ts] + p[2 * ts:3 * ts] + p[3 * ts:4 * ts]
            imp = jnp.where(lane1 == nb_past, FORCE, jnp.where(okc1, psum, -1.0))
            imp = jnp.where(lane1 <= nb_past, imp, -3e38)
            selm = _select_blocks(imp, lane1)
            selm16 = jnp.concatenate([selm, jnp.zeros_like(selm)], axis=0).astype(BF16)
            madds.append((jnp.dot(selm16, e_ref[...], preferred_element_type=F32)[0:ts] - 1.0) * (-NEG))
        for g in range(KV_GROUPS):
            qg = qgs[g]
            o_c, madd = o_cs[g], madds[g]
            kn = kn_ref[:, g * HEAD_DIM:(g + 1) * HEAD_DIM].astype(BF16)
            vn = kn_ref[:, kw + g * HEAD_DIM:kw + (g + 1) * HEAD_DIM].astype(BF16)
            s_new = jnp.where(lane <= ti, _dot_nt(qg, kn) + near_ref[g, 0], NEG)
            s = jnp.concatenate([s_ref[g] + _rep4(madd), s_new], axis=1)
            e = jnp.exp(s - jnp.max(s, axis=-1, keepdims=True))
            p = e / jnp.sum(e, axis=-1, keepdims=True)
            o_s = (_dot_nt(p[:, :past].astype(BF16), vbuf_ref[g])
                   + jnp.dot(p[:, past:].astype(BF16), vn, preferred_element_type=F32))
            kwin = win_ref[0, g].astype(BF16)
            vwin = win_ref[1, g].astype(BF16)
            knw = kn_ref[:, 2 * kw + g * HEAD_DIM:2 * kw + (g + 1) * HEAD_DIM].astype(BF16)
            vnw = kn_ref[:, 3 * kw + g * HEAD_DIM:3 * kw + (g + 1) * HEAD_DIM].astype(BF16)
            s = jnp.concatenate([jnp.dot(qg, kwin, preferred_element_type=F32), _dot_nt(qg, knw)], axis=1) + wtab_ref[g]
            s = jnp.where(okw, s, NEG)
            e = jnp.exp(s - jnp.max(s, axis=-1, keepdims=True))
            p = e / jnp.sum(e, axis=-1, keepdims=True)
            o_w = (_dot_nt(p[:, :wl].astype(BF16), vwin)
                   + jnp.dot(p[:, wl:].astype(BF16), vnw, preferred_element_type=F32))
            tot = _gate_col(sg, 0, g) * o_c + _gate_col(sg, 1, g) * o_s + _gate_col(sg, 2, g) * o_w
            for r in range(GQA):
                o_ref[:, (g * GQA + r) * HEAD_DIM:(g * GQA + r + 1) * HEAD_DIM] = tot[r * ts:(r + 1) * ts]


def nsa_sample(q3, kvn3, small3, win_t, near_s, wtab, cst_s, cache_t, page_table, layer, pps, nbl):
    batch, ts, hd = q3.shape
    n_pages = page_table.shape[1]
    n_steps = n_pages // pps
    past = n_pages * 2 * BLK
    wl = win_t.shape[-1]
    avg = np.where(np.arange(2 * pps)[:, None] == np.arange(pps * 2 * BLK)[None, :] // BLK, 1.0 / BLK, 0.0)
    rows = GQA * ts
    kw = KV_GROUPS * HEAD_DIM
    assert ts == SUBLANES and past % QCHUNK == 0 and wl % LANES == 0
    emat = (np.arange(nbl)[:, None] == (np.arange(past)[None, :] // BLK)).astype(np.float32)
    kern = functools.partial(_nsa_sample_kernel, pps=pps, n_steps=n_steps, past=past, ts=ts, nbl=nbl)
    c3 = lambda shp: pl.BlockSpec(shp, lambda b, j, pt: (0,) * len(shp))
    return pl.pallas_call(
        kern,
        grid_spec=pltpu.PrefetchScalarGridSpec(
            num_scalar_prefetch=1, grid=(batch, n_steps),
            in_specs=[pl.BlockSpec((None, ts, hd), lambda b, j, pt: (b, 0, 0)),
                      pl.BlockSpec((None, ts, kvn3.shape[2]), lambda b, j, pt: (b, 0, 0)),
                      pl.BlockSpec((None, ts, LANES), lambda b, j, pt: (b, 0, 0)),
                      c3((2 * pps, pps * 2 * BLK)),
                      pl.BlockSpec((None, None, 2, KV_GROUPS, HEAD_DIM, wl), lambda b, j, pt: (layer, b, 0, 0, 0, 0)),
                      c3((KV_GROUPS, 3, rows, LANES)), c3((KV_GROUPS, rows, wl + LANES)), c3((KV_GROUPS, rows, 1)),
                      c3((nbl, past))] + _page_specs(pps, layer),
            out_specs=pl.BlockSpec((None, ts, hd), lambda b, j, pt: (b, 0, 0)),
            scratch_shapes=[pltpu.VMEM((KV_GROUPS, rows, past), F32),
                            pltpu.VMEM((2, KV_GROUPS, nbl, HEAD_DIM), F32),
                            pltpu.VMEM((KV_GROUPS, HEAD_DIM, past), BF16),
                            pltpu.VMEM((LANES, 4 * kw), F32)]),
        out_shape=jax.ShapeDtypeStruct((batch, ts, hd), F32),
        compiler_params=_params(("parallel", "arbitrary")),
        name="nsa_sample",
    )(page_table, q3, kvn3, small3, jnp.asarray(avg, BF16), win_t, near_s, wtab, cst_s, jnp.asarray(emat, BF16),
      *([cache_t] * pps))


def _merge_kernel(c_ref, o_ref, y_ref, g0_ref, g1_ref, g2_ref, wa_ref, wn_ref, ws_ref, out_ref):
    oa = jnp.dot(c_ref[...], wa_ref[...], preferred_element_type=F32)
    ob = jnp.dot(o_ref[...], wn_ref[...], preferred_element_type=F32)
    oc = jnp.dot(y_ref[...], ws_ref[...], preferred_element_type=F32)
    mixed = (jax.nn.sigmoid(g0_ref[...]) * oa + jax.nn.sigmoid(g1_ref[...]) * ob + jax.nn.sigmoid(g2_ref[...]) * oc)
    out_ref[...] = mixed.astype(out_ref.dtype)


def merge_branches(c, o, y, gmix, wa, wn, ws):
    n = c.shape[0]
    d = wa.shape[1]
    tm = _tile(n, 768, BF16_ROWS)
    tn = _tile(d, 512, LANES)
    nj = d // tn
    act = lambda a: pl.BlockSpec((tm, a.shape[1]), lambda i, j: (i, 0))
    wsp = lambda w: pl.BlockSpec((w.shape[0], tn), lambda i, j: (0, j))
    gsp = lambda br: pl.BlockSpec((tm, tn), lambda i, j: (i, br * nj + j))
    return pl.pallas_call(
        _merge_kernel,
        grid=(n // tm, nj),
        in_specs=[act(c), act(o), act(y), gsp(0), gsp(1), gsp(2), wsp(wa), wsp(wn), wsp(ws)],
        out_specs=pl.BlockSpec((tm, tn), lambda i, j: (i, j)),
        out_shape=jax.ShapeDtypeStruct((n, d), BF16),
        compiler_params=_params(("parallel", "parallel")),
        name="merge",
    )(c, o, y, gmix, gmix, gmix, wa, wn, ws)


def _outproj_kernel(a_ref, w_ref, x_ref, o_ref):
    o_ref[...] = x_ref[...] + jnp.dot(a_ref[...], w_ref[...], preferred_element_type=F32)


def out_projection(a, w, x):
    n, k = a.shape
    d = w.shape[1]
    tm = _tile(n, 768, BF16_ROWS)
    tn = _tile(d, 1024, LANES)
    return pl.pallas_call(
        _outproj_kernel,
        grid=(n // tm, d // tn),
        in_specs=[pl.BlockSpec((tm, k), lambda i, j: (i, 0)), pl.BlockSpec((k, tn), lambda i, j: (0, j)),
                  pl.BlockSpec((tm, tn), lambda i, j: (i, j))],
        out_specs=pl.BlockSpec((tm, tn), lambda i, j: (i, j)),
        out_shape=jax.ShapeDtypeStruct((n, d), F32),
        compiler_params=_params(("parallel", "parallel")),
        name="out_proj",
    )(a, w, x)


def _router_kernel(x_ref, g_ref, w_ref, b_ref, h_ref, r_ref):
    x = x_ref[...]
    ms = jnp.mean(x * x, axis=-1, keepdims=True)
    h = (x * lax.rsqrt(ms + EPS) * g_ref[...]).astype(BF16)
    h_ref[...] = h
    logits = jnp.dot(h, w_ref[...], preferred_element_type=F32) + b_ref[...]
    lane = lax.broadcasted_iota(jnp.int32, logits.shape, 1)
    big = jnp.int32(1 << 30)
    gl = jnp.where(lane < MOE_GROUPS, logits, NEG)
    gmax = jnp.max(gl, axis=-1, keepdims=True)
    gidx = jnp.min(jnp.where(gl == gmax, lane, big), axis=-1, keepdims=True)
    g_w = 1.0 / jnp.sum(jnp.exp(gl - gmax), axis=-1, keepdims=True)
    eid = lane - MOE_GROUPS
    in_grp = (eid >= 0) & (eid < MOE_GROUPS * MOE_EPG) & ((eid >> 3) == gidx)
    el = jnp.where(in_grp, logits, NEG)
    m1 = jnp.max(el, axis=-1, keepdims=True)
    i1 = jnp.min(jnp.where(el == m1, lane, big), axis=-1, keepdims=True)
    el2 = jnp.where(lane == i1, NEG, el)
    m2 = jnp.max(el2, axis=-1, keepdims=True)
    i2 = jnp.min(jnp.where(el2 == m2, lane, big), axis=-1, keepdims=True)
    p2 = jnp.exp(m2 - m1)
    w1 = g_w / (1.0 + p2)
    w2 = g_w * p2 / (1.0 + p2)
    r_ref[...] = jnp.where(lane == 0, (i1 - MOE_GROUPS).astype(F32),
                           jnp.where(lane == 1, (i2 - MOE_GROUPS).astype(F32),
                                     jnp.where(lane == 2, w1, jnp.where(lane == 3, w2, 0.0))))


def moe_router(x, g, wr, br):
    n, d = x.shape
    tm = _tile(n, 256, BF16_ROWS)
    return pl.pallas_call(
        _router_kernel,
        grid=(n // tm,),
        in_specs=[pl.BlockSpec((tm, d), lambda i: (i, 0)), pl.BlockSpec((1, d), lambda i: (0, 0)),
                  pl.BlockSpec((d, LANES), lambda i: (0, 0)), pl.BlockSpec((1, LANES), lambda i: (0, 0))],
        out_specs=[pl.BlockSpec((tm, d), lambda i: (i, 0)), pl.BlockSpec((tm, LANES), lambda i: (i, 0))],
        out_shape=[jax.ShapeDtypeStruct((n, d), BF16), jax.ShapeDtypeStruct((n, LANES), F32)],
        compiler_params=_params(("parallel",)),
        name="moe_router",
    )(x, g.reshape(1, d), wr, br)


def _expert_kernel(te_ref, nu_ref, x_ref, rw_ref, wg_ref, wu_ref, wd_ref, y_ref, wgb_ref, wub_ref, wdb_ref):
    t = pl.program_id(0)
    fresh = jnp.logical_or(t == 0, te_ref[t] != te_ref[jnp.maximum(t - 1, 0)])

    @pl.when(jnp.logical_and(fresh, t < nu_ref[0]))
    def _():
        wgb_ref[...] = wg_ref[...].astype(BF16)
        wub_ref[...] = wu_ref[...].astype(BF16)
        wdb_ref[...] = wd_ref[...].astype(BF16)

    @pl.when(t >= nu_ref[0])
    def _():
        y_ref[...] = jnp.zeros(y_ref.shape, F32)

    @pl.when(t < nu_ref[0])
    def _():
        x = x_ref[...]
        a = jnp.dot(x, wgb_ref[...], preferred_element_type=F32)
        u = jnp.dot(x, wub_ref[...], preferred_element_type=F32)
        rw = rw_ref[...]
        hid = (_silu(a) * u * jnp.concatenate([rw] * (a.shape[1] // LANES), axis=1)).astype(BF16)
        y_ref[...] = jnp.dot(hid, wdb_ref[...], preferred_element_type=F32)


def moe_experts(xs, row_w, tile_e, n_used, wg, wu, wd, layer, tme):
    r, d = xs.shape
    ff = wg.shape[3]
    nt = r // tme
    return pl.pallas_call(
        _expert_kernel,
        grid_spec=pltpu.PrefetchScalarGridSpec(
            num_scalar_prefetch=2, grid=(nt,),
            in_specs=[pl.BlockSpec((tme, d), lambda t, te, nu: (t, 0)),
                      pl.BlockSpec((tme, LANES), lambda t, te, nu: (t, 0)),
                      pl.BlockSpec((None, None, d, ff), lambda t, te, nu: (layer, te[t], 0, 0)),
                      pl.BlockSpec((None, None, d, ff), lambda t, te, nu: (layer, te[t], 0, 0)),
                      pl.BlockSpec((None, None, ff, d), lambda t, te, nu: (layer, te[t], 0, 0))],
            out_specs=pl.BlockSpec((tme, d), lambda t, te, nu: (t, 0)),
            scratch_shapes=[pltpu.VMEM((d, ff), BF16), pltpu.VMEM((d, ff), BF16), pltpu.VMEM((ff, d), BF16)]),
        out_shape=jax.ShapeDtypeStruct((r, d), F32),
        compiler_params=_params(("arbitrary",)),
        name="moe_experts",
    )(tile_e, n_used, xs, row_w, wg, wu, wd)


def moe_layer(x, g, wrg, brg, wre, bre, wg, wu, wd, layer):
    n, d = x.shape
    n_exp = wg.shape[1]
    wr = jnp.concatenate([wrg, wre.transpose(1, 0, 2).reshape(d, n_exp)], axis=1)
    wr = jnp.pad(wr, ((0, 0), (0, LANES - wr.shape[1])))
    br = jnp.pad(jnp.concatenate([brg, bre.reshape(-1)]), (0, LANES - MOE_GROUPS - n_exp)).reshape(1, LANES)
    h, route = moe_router(x, g, wr.astype(BF16), br)
    tme = 256
    e_flat = route[:, 0:MOE_TOPK].astype(jnp.int32).T.reshape(-1)
    w_flat = route[:, MOE_TOPK:2 * MOE_TOPK].T.reshape(-1)
    na = e_flat.shape[0]
    nt = -(-na // tme) + n_exp
    order = jnp.argsort(e_flat, stable=True)
    counts = jnp.sum(e_flat[:, None] == jnp.arange(n_exp)[None, :], axis=0).astype(jnp.int32)
    pc = ((counts + tme - 1) // tme) * tme
    pend = jnp.cumsum(pc)
    pstart = pend - pc
    ustart = jnp.cumsum(counts) - counts
    tile_e = jnp.minimum(jnp.searchsorted(pend, jnp.arange(nt, dtype=jnp.int32) * tme, side="right"), n_exp - 1).astype(jnp.int32)
    n_used = (pend[-1] // tme).astype(jnp.int32).reshape(1)
    row = jnp.arange(nt * tme, dtype=jnp.int32)
    row_e = jnp.repeat(tile_e, tme)
    idx = row - pstart[row_e]
    valid = idx < counts[row_e]
    src = order[jnp.minimum(ustart[row_e] + idx, na - 1)]
    row_tok = jnp.where(valid, src % n, 0).astype(jnp.int32)
    row_w = jnp.where(valid, w_flat[src], 0.0)
    se = e_flat[order]
    dest = pstart[se] + jnp.arange(na, dtype=jnp.int32) - ustart[se]
    dest_flat = dest[jnp.argsort(order)]
    xs = jnp.take(h, row_tok, axis=0)
    ys = moe_experts(xs, jnp.broadcast_to(row_w[:, None], (nt * tme, LANES)), tile_e, n_used, wg, wu, wd, layer, tme)
    return x + jnp.take(ys, dest_flat[:n], axis=0) + jnp.take(ys, dest_flat[n:], axis=0)


def kernel(x_prompt, x_sample, cache_nsa_kv, cache_nsa_win, state_conv_a, state_conv_ssm, state_ssm, page_table,
           rel_bias, norm_mix_g, w_in, conv_a_w, conv_a_b, ln_a_g, ln_a_b, w_a_out, w_nsa_out, conv_s_w, conv_s_b,
           dt_bias, a_log, d_skip, ssm_norm_g, w_ssm_out, w_out, norm_ffn_g, w_router_group, b_router_group,
           w_router_expert, b_router_expert, w_gate, w_up, w_down, final_norm_g):
    bp, tp, d = x_prompt.shape
    bs, ts, _ = x_sample.shape
    depth = w_in.shape[0]
    n_p, n_s = bp * tp, bs * ts
    ch = conv_a_w.shape[2]
    conv_dim = conv_s_w.shape[2]
    heads_ssm = dt_bias.shape[1]
    inner = heads_ssm * SSM_HEAD_DIM
    hd = N_HEADS * HEAD_DIM
    kw = KV_GROUPS * HEAD_DIM
    kvw = KV_KINDS * kw
    n_pages = page_table.shape[1]
    page = cache_nsa_kv.shape[2]
    past = n_pages * page
    wl = cache_nsa_win.shape[2]
    assert page == 2 * BLK and wl == min(WINDOW, past) and tp >= WINDOW
    pps = _tile(n_pages, 16, 1)
    rows_p = GQA * QCHUNK

    splits = (2 * ch, hd, kvw, 3 * N_HEADS, inner, conv_dim, heads_ssm, 3 * d)
    offs = np.concatenate([[0], np.cumsum(splits)])
    o_u, o_q, o_kv, o_gn, o_z, o_x, o_dt, o_gm = [int(v) for v in offs[:-1]]

    near_d = _near_distances()
    tj = np.arange(wl + LANES)[None, :]
    win_d = np.where(tj < wl + ts, wl + np.arange(ts)[:, None] - tj, 0)
    near_f = expand_bias(rel_bias, near_d)
    wtab_f = expand_bias(rel_bias, win_d)
    near_g = near_f.reshape(KV_GROUPS, GQA, 3, QCHUNK, LANES).transpose(0, 2, 1, 3, 4)
    near_p = near_g.reshape(KV_GROUPS, 3, rows_p, LANES)
    near_s = near_g[:, :, :, :ts].reshape(KV_GROUPS, 3, GQA * ts, LANES)
    wtab = wtab_f.reshape(KV_GROUPS, GQA * ts, wl + LANES)
    far = rel_bias[REL_BUCKETS - 1].reshape(KV_GROUPS, GQA, 1)
    cst_p = jnp.repeat(far, QCHUNK, axis=1).reshape(KV_GROUPS, rows_p, 1)
    cst_s = jnp.repeat(far, ts, axis=1).reshape(KV_GROUPS, GQA * ts, 1)
    ii = (np.arange(rows_p) % QCHUNK)[:, None]
    jj = np.arange(LANES)[None, :]
    cst_t = jnp.broadcast_to(cst_p, (KV_GROUPS, rows_p, LANES))
    band_t = jnp.pad(near_p[:, 2, :, 0:4].transpose(0, 2, 1), ((0, 0), (0, SUBLANES - 4), (0, 0)))
    tabs_p = jnp.stack([jnp.where(jj <= ii, near_p[:, 0], NEG), near_p[:, 1], cst_t,
                        jnp.where(jj >= ii, cst_t, NEG), jnp.full((KV_GROUPS, rows_p, LANES), NEG, F32)], axis=1)

    w_in_t = w_in.transpose(0, 2, 1)
    cache_t = cache_nsa_kv.transpose(0, 1, 3, 4, 5, 2)
    win_t = cache_nsa_win.transpose(0, 1, 3, 4, 5, 2)
    zeros_hist_a = jnp.zeros((bp, HIST_A, ch), F32)
    zeros_hist_s = jnp.zeros((bp, HIST_S, conv_dim), F32)
    zeros_ssm = jnp.zeros((bp, heads_ssm, SSM_HEAD_DIM, SSM_STATE), F32)
    nbl = -(-(past // BLK + 1) // LANES) * LANES

    x = jnp.concatenate([x_prompt.reshape(n_p, d), x_sample.reshape(n_s, d)], axis=0)
    kv_p, kv_s, win_p, win_s, ha_p, ha_s, hs_p, hs_s, st_p, st_s = ([] for _ in range(10))
    for l in range(depth):
        h = rmsnorm(x, norm_mix_g[l], BF16)
        (u_in,) = matmul(h, w_in_t, l, o_u, 2 * ch, [F32])
        (q,) = matmul(h, w_in_t, l, o_q, hd, [BF16], scale=HEAD_DIM ** -0.5)
        kv, kvb = matmul(h, w_in_t, l, o_kv, kvw, [F32, BF16])
        w_small = jnp.concatenate([w_in_t[l, o_dt:o_gm], w_in_t[l, o_gn:o_z],
                                   jnp.zeros((LANES - heads_ssm - 3 * N_HEADS, d), F32)], axis=0)
        (small,) = matmul(h, w_small[None], 0, 0, LANES, [F32])
        (z,) = matmul(h, w_in_t, l, o_z, inner, [F32])
        (xbc,) = matmul(h, w_in_t, l, o_x, conv_dim, [F32])
        (gmix,) = matmul(h, w_in_t, l, o_gm, 3 * d, [F32])

        hist_a = jnp.pad(state_conv_a[l], ((0, 0), (HIST_A - (CONV_A_W - 1), 0), (0, 0)))
        ca_args = (conv_a_w[l], conv_a_b[l], ln_a_g[l], ln_a_b[l])
        c_p, ha_new_p = conv_a_branch(u_in, 0, zeros_hist_a, *ca_args, bp, tp, BF16, n_out=n_p + n_s)
        c_s, ha_new_s = conv_a_branch(u_in, n_p, hist_a, *ca_args, bs, ts, F32)
        hist_s = jnp.pad(state_conv_ssm[l], ((0, 0), (HIST_S - (SSM_CONV_W - 1), 0), (0, 0)))
        ss_args = (conv_s_w[l], conv_s_b[l], dt_bias[l], a_log[l], d_skip[l], ssm_norm_g[l])
        small_tp = small[:n_p].reshape(bp, tp, LANES).transpose(0, 2, 1)
        small_ts = small[n_p:].reshape(bs, ts, LANES).transpose(0, 2, 1)
        y_p, hs_new_p, st_new_p = ssd_branch(xbc, z, small, small_tp, 0, zeros_hist_s, zeros_ssm, *ss_args, bp, tp, BF16,
                                             n_out=n_p + n_s)
        y_s, hs_new_s, st_new_s = ssd_branch(xbc, z, small, small_ts, n_p, hist_s, state_ssm[l], *ss_args, bs, ts, F32)
        kc_p = prompt_block_means(kv, bp, tp)
        o_p = nsa_prompt(q, kvb, kc_p, small, tabs_p, band_t, cst_p.reshape(KV_GROUPS, 1, rows_p), bp, tp, n_out=n_p + n_s)
        o_s = nsa_sample(q[n_p:].astype(F32).reshape(bs, ts, hd), kv[n_p:].reshape(bs, ts, kvw),
                         small[n_p:].reshape(bs, ts, LANES), win_t, near_s, wtab, cst_s, cache_t, page_table, l, pps, nbl)

        c_all = lax.dynamic_update_slice(c_p, c_s.astype(BF16), (n_p, 0))
        o_all = lax.dynamic_update_slice(o_p, o_s.reshape(n_s, hd).astype(BF16), (n_p, 0))
        y_all = lax.dynamic_update_slice(y_p, y_s.astype(BF16), (n_p, 0))
        mixed = merge_branches(c_all, o_all, y_all, gmix, w_a_out[l].astype(BF16), w_nsa_out[l].astype(BF16),
                               w_ssm_out[l].astype(BF16))
        x = out_projection(mixed, w_out[l].astype(BF16), x)
        x = moe_layer(x, norm_ffn_g[l], w_router_group[l], b_router_group[l], w_router_expert[l], b_router_expert[l],
                      w_gate, w_up, w_down, l)

        kvr = kv[:, :4 * kw]
        kv_p.append(kvr[:n_p].reshape(bp, tp, 4, KV_GROUPS, HEAD_DIM))
        kv_s.append(kvr[n_p:].reshape(bs, ts, 4, KV_GROUPS, HEAD_DIM))
        wr = kv[:, 4 * kw:]
        win_p.append(wr[:n_p].reshape(bp, tp, 2, KV_GROUPS, HEAD_DIM)[:, tp - WINDOW:])
        win_all = jnp.concatenate([cache_nsa_win[l], wr[n_p:].reshape(bs, ts, 2, KV_GROUPS, HEAD_DIM)], axis=1)
        keep = min(WINDOW, wl + ts)
        win_s.append(win_all[:, wl + ts - keep:])
        ha_p.append(ha_new_p[:, HIST_A - (CONV_A_W - 1):])
        ha_s.append(ha_new_s[:, HIST_A - (CONV_A_W - 1):])
        hs_p.append(hs_new_p[:, HIST_S - (SSM_CONV_W - 1):])
        hs_s.append(hs_new_s[:, HIST_S - (SSM_CONV_W - 1):])
        st_p.append(st_new_p)
        st_s.append(st_new_s)

    y = rmsnorm(x, final_norm_g, F32)
    return (y[:n_p].reshape(bp, tp, d), y[n_p:].reshape(bs, ts, d),
            jnp.stack(kv_p, axis=1), jnp.stack(kv_s, axis=1), jnp.stack(win_p), jnp.stack(win_s),
            jnp.stack(ha_p), jnp.stack(ha_s), jnp.stack(hs_p), jnp.stack(hs_s), jnp.stack(st_p), jnp.stack(st_s))
```
